```python
import jax, jax.numpy as jnp
from jax import lax
import numpy as np

D_MODEL = 2048
BATCH = 16
SEQ = 2048
DEPTH = 4

N_MIXERS = 3
N_A = (DEPTH + 2) // 3
N_B = (DEPTH + 1) // 3
N_C = DEPTH // 3
RMS_EPS = 1e-6
FFN_DIM = D_MODEL * 11 // 4
FFN_CONV = 3
A_DIM = D_MODEL
A_CHUNK = 128
A_GROUP_DIM = 128
A_GROUPS = A_DIM // A_GROUP_DIM
B_HEAD_DIM = 128
B_HEADS = D_MODEL // B_HEAD_DIM
B_DIM = B_HEADS * B_HEAD_DIM
B_BLOCK = 128
C_HEAD_DIM = 128
C_K_HEADS = D_MODEL // C_HEAD_DIM
C_V_HEADS = 2 * C_K_HEADS
C_DK = C_K_HEADS * C_HEAD_DIM
C_DV = C_V_HEADS * C_HEAD_DIM
C_CONV = 4
C_CHUNK = 64

kernel_name = 'hybrid_gmlp_fox_gdn_convffn'


def rmsnorm(x, g):
    xf = x.astype(jnp.float32)
    y = xf * lax.rsqrt(jnp.mean(xf * xf, axis=-1, keepdims=True) + RMS_EPS)
    return (y * g.astype(jnp.float32)).astype(x.dtype)


def l2norm(x):
    xf = x.astype(jnp.float32)
    return xf * lax.rsqrt(jnp.sum(xf * xf, axis=-1, keepdims=True) + RMS_EPS)


def causal_dwconv(x, w):
    k, c = w.shape
    return lax.conv_general_dilated(
        x, w.astype(x.dtype)[:, None, :], window_strides=(1,), padding=[(k - 1, 0)],
        dimension_numbers=('NWC', 'WIO', 'NWC'), feature_group_count=c)


def conv_ffn(x, w_gate, w_up, conv_w, conv_b, w_down):
    gate = causal_dwconv(x @ w_gate, conv_w) + conv_b
    return (jax.nn.silu(gate) * (x @ w_up)) @ w_down


def mixer_gmlp(x, w_in, b_in, v_norm, w_s, b_s, w_out):
    bsz, s, _ = x.shape
    h = jax.nn.gelu(x @ w_in + b_in)
    u, v = jnp.split(h, 2, axis=-1)
    v = rmsnorm(v, v_norm).reshape(bsz, s // A_CHUNK, A_CHUNK, A_GROUPS, A_GROUP_DIM)
    tri = jnp.tril(jnp.ones((A_CHUNK, A_CHUNK), dtype=bool))
    w_causal = jnp.where(tri, w_s, 0)
    sv = jnp.einsum('gts,bnsgd->bntgd', w_causal.astype(v.dtype), v) + b_s.T[None, None, :, :, None]
    return (u * sv.reshape(bsz, s, A_DIM)) @ w_out


def mixer_fox(x, w_in, b_f, q_norm, k_norm, w_out):
    bsz, s, _ = x.shape
    proj = x @ w_in
    q, k, v, og, fl = jnp.split(proj, [B_DIM, 2 * B_DIM, 3 * B_DIM, 4 * B_DIM], axis=-1)
    q = rmsnorm(q.reshape(bsz, s, B_HEADS, B_HEAD_DIM), q_norm).transpose(0, 2, 1, 3)
    k = rmsnorm(k.reshape(bsz, s, B_HEADS, B_HEAD_DIM), k_norm).transpose(0, 2, 1, 3)
    v = v.reshape(bsz, s, B_HEADS, B_HEAD_DIM).transpose(0, 2, 1, 3)
    log_f = jax.nn.log_sigmoid((fl + b_f).astype(jnp.float32))
    c = jnp.cumsum(log_f, axis=1).transpose(0, 2, 1)
    scale = B_HEAD_DIM ** -0.5
    tri = jnp.tril(jnp.ones((B_BLOCK, B_BLOCK), dtype=bool))
    outs = []
    for i in range(s // B_BLOCK):
        lo, hi = i * B_BLOCK, (i + 1) * B_BLOCK
        sc = jnp.einsum('bhqd,bhkd->bhqk', q[:, :, lo:hi], k[:, :, :hi]).astype(jnp.float32) * scale
        sc = sc + c[:, :, lo:hi, None] - c[:, :, None, :hi]
        mask = jnp.concatenate([jnp.ones((B_BLOCK, lo), dtype=bool), tri], axis=1)
        sc = jnp.where(mask, sc, -jnp.inf)
        p = jax.nn.softmax(sc, axis=-1).astype(v.dtype)
        outs.append(jnp.einsum('bhqk,bhkd->bhqd', p, v[:, :, :hi]))
    o = jnp.concatenate(outs, axis=2).transpose(0, 2, 1, 3).reshape(bsz, s, B_DIM)
    return (o * jax.nn.sigmoid(og)) @ w_out


def gated_delta_rule_chunked(q, k, v, beta, g):
    bsz, s, h, dk = q.shape
    dv = v.shape[-1]
    n = s // C_CHUNK

    def chunks(t):
        return t.reshape(bsz, n, C_CHUNK, h, -1).transpose(1, 0, 3, 2, 4)

    q, k, v = chunks(q), chunks(k), chunks(v)
    beta = beta.reshape(bsz, n, C_CHUNK, h).transpose(1, 0, 3, 2)
    g = jnp.cumsum(g.reshape(bsz, n, C_CHUNK, h).transpose(1, 0, 3, 2), axis=-1)
    tri = jnp.tril(jnp.ones((C_CHUNK, C_CHUNK), dtype=bool))
    strict = jnp.tril(jnp.ones((C_CHUNK, C_CHUNK), dtype=bool), -1)
    decay = jnp.exp(jnp.where(tri, g[..., :, None] - g[..., None, :], -jnp.inf))
    kb = k * beta[..., None]
    a_mat = jnp.where(strict, jnp.einsum('nbhtd,nbhsd->nbhts', kb, k) * decay, 0.0)
    rhs = jnp.concatenate([v * beta[..., None], kb * jnp.exp(g)[..., None]], axis=-1)
    sol = lax.linalg.triangular_solve(a_mat + jnp.eye(C_CHUNK, dtype=a_mat.dtype), rhs,
                                      left_side=True, lower=True, unit_diagonal=True)
    u, w = sol[..., :dv], sol[..., dv:]
    attn = jnp.einsum('nbhtd,nbhsd->nbhts', q, k) * decay
    q_dec = q * jnp.exp(g)[..., None]
    g_last = g[..., -1]
    k_dec = k * jnp.exp(g_last[..., None] - g)[..., None]

    def step(state, xs):
        u_n, w_n, attn_n, q_n, k_n, gl = xs
        v_new = u_n - jnp.einsum('bhcd,bhde->bhce', w_n, state)
        o = jnp.einsum('bhcd,bhde->bhce', q_n, state) + jnp.einsum('bhts,bhse->bhte', attn_n, v_new)
        state = state * jnp.exp(gl)[..., None, None] + jnp.einsum('bhcd,bhce->bhde', k_n, v_new)
        return state, o

    state0 = jnp.zeros((bsz, h, dk, dv), dtype=jnp.float32)
    _, o = lax.scan(step, state0, (u, w, attn, q_dec, k_dec, g_last))
    return o.transpose(1, 0, 3, 2, 4).reshape(bsz, s, h, dv)


def mixer_gdn(x, w_in, conv_w, a_log, dt_bias, out_norm, w_out):
    bsz, s, _ = x.shape
    proj = x @ w_in
    qkv, z, b, a = jnp.split(proj, [2 * C_DK + C_DV, 2 * C_DK + 2 * C_DV, 2 * C_DK + 2 * C_DV + C_V_HEADS], axis=-1)
    qkv = jax.nn.silu(causal_dwconv(qkv, conv_w))
    q, k, v = jnp.split(qkv, [C_DK, 2 * C_DK], axis=-1)
    rep = C_V_HEADS // C_K_HEADS
    q = jnp.repeat(l2norm(q.reshape(bsz, s, C_K_HEADS, C_HEAD_DIM)), rep, axis=2) * (C_HEAD_DIM ** -0.5)
    k = jnp.repeat(l2norm(k.reshape(bsz, s, C_K_HEADS, C_HEAD_DIM)), rep, axis=2)
    v = v.reshape(bsz, s, C_V_HEADS, C_HEAD_DIM).astype(jnp.float32)
    beta = jax.nn.sigmoid(b.astype(jnp.float32))
    g = -jnp.exp(a_log.astype(jnp.float32)) * jax.nn.softplus(a.astype(jnp.float32) + dt_bias.astype(jnp.float32))
    o = gated_delta_rule_chunked(q, k, v, beta, g)
    o = rmsnorm(o, out_norm) * jax.nn.silu(z.reshape(bsz, s, C_V_HEADS, C_HEAD_DIM).astype(jnp.float32))
    return o.reshape(bsz, s, C_DV).astype(x.dtype) @ w_out


def _fwd_setup_inputs(seed: int = 0) -> dict:
    key = jax.random.key(seed)
    keys = list(jax.random.split(key, 32))
    ctr = [0]

    def nk():
        ctr[0] += 1
        return keys[ctr[0] - 1]

    def nrm(shape, scale):
        return scale * jax.random.normal(nk(), shape, jnp.float32)

    def gain(shape):
        return 1.0 + 0.05 * jax.random.normal(nk(), shape, jnp.float32)

    def unif(shape, lo, hi):
        return jax.random.uniform(nk(), shape, jnp.float32, lo, hi)

    d = D_MODEL
    dt = jnp.exp(unif((N_C, C_V_HEADS), float(np.log(1e-3)), float(np.log(1e-1))))
    return {
        'x': nrm((BATCH, SEQ, d), 1.0),
        'norm_mix': gain((DEPTH, d)),
        'norm_ffn': gain((DEPTH, d)),
        'ffn_w_gate': nrm((DEPTH, d, FFN_DIM), d ** -0.5),
        'ffn_w_up': nrm((DEPTH, d, FFN_DIM), d ** -0.5),
        'ffn_conv_w': nrm((DEPTH, FFN_CONV, FFN_DIM), FFN_CONV ** -0.5),
        'ffn_conv_b': nrm((DEPTH, FFN_DIM), 0.02),
        'ffn_w_down': nrm((DEPTH, FFN_DIM, d), FFN_DIM ** -0.5),
        'a_w_in': nrm((N_A, d, 2 * A_DIM), d ** -0.5),
        'a_b_in': nrm((N_A, 2 * A_DIM), 0.02),
        'a_v_norm': gain((N_A, A_DIM)),
        'a_w_s': nrm((N_A, A_GROUPS, A_CHUNK, A_CHUNK), A_CHUNK ** -0.5),
        'a_b_s': gain((N_A, A_GROUPS, A_CHUNK)),
        'a_w_out': nrm((N_A, A_DIM, d), A_DIM ** -0.5),
        'b_w_in': nrm((N_B, d, 4 * B_DIM + B_HEADS), d ** -0.5),
        'b_b_f': unif((N_B, B_HEADS), 1.0, 4.0),
        'b_q_norm': gain((N_B, B_HEAD_DIM)),
        'b_k_norm': gain((N_B, B_HEAD_DIM)),
        'b_w_out': nrm((N_B, B_DIM, d), B_DIM ** -0.5),
        'c_w_in': nrm((N_C, d, 2 * C_DK + 2 * C_DV + 2 * C_V_HEADS), d ** -0.5),
        'c_conv_w': nrm((N_C, C_CONV, 2 * C_DK + C_DV), C_CONV ** -0.5),
        'c_a_log': jnp.log(unif((N_C, C_V_HEADS), 1.0, 16.0)),
        'c_dt_bias': dt + jnp.log(-jnp.expm1(-dt)),
        'c_out_norm': gain((N_C, C_HEAD_DIM)),
        'c_w_out': nrm((N_C, C_DV, d), C_DV ** -0.5),
    }


def _fwd_reference(x, norm_mix, norm_ffn, ffn_w_gate, ffn_w_up, ffn_conv_w, ffn_conv_b, ffn_w_down,
              a_w_in, a_b_in, a_v_norm, a_w_s, a_b_s, a_w_out,
              b_w_in, b_b_f, b_q_norm, b_k_norm, b_w_out,
              c_w_in, c_conv_w, c_a_log, c_dt_bias, c_out_norm, c_w_out):
    for i in range(DEPTH):
        kind, j = i % N_MIXERS, i // N_MIXERS
        h = rmsnorm(x, norm_mix[i])
        if kind == 0:
            m = mixer_gmlp(h, a_w_in[j], a_b_in[j], a_v_norm[j], a_w_s[j], a_b_s[j], a_w_out[j])
        elif kind == 1:
            m = mixer_fox(h, b_w_in[j], b_b_f[j], b_q_norm[j], b_k_norm[j], b_w_out[j])
        else:
            m = mixer_gdn(h, c_w_in[j], c_conv_w[j], c_a_log[j], c_dt_bias[j], c_out_norm[j], c_w_out[j])
        x = x + m
        x = x + conv_ffn(rmsnorm(x, norm_ffn[i]), ffn_w_gate[i], ffn_w_up[i], ffn_conv_w[i],
                         ffn_conv_b[i], ffn_w_down[i])
    return x


import jax as _jax
import jax.numpy as _jnp

TWIN_FORMAT = 'train_step'
FWD_PARAMS = ['x', 'norm_mix', 'norm_ffn', 'ffn_w_gate', 'ffn_w_up', 'ffn_conv_w', 'ffn_conv_b', 'ffn_w_down', 'a_w_in', 'a_b_in', 'a_v_norm', 'a_w_s', 'a_b_s', 'a_w_out', 'b_w_in', 'b_b_f', 'b_q_norm', 'b_k_norm', 'b_w_out', 'c_w_in', 'c_conv_w', 'c_a_log', 'c_dt_bias', 'c_out_norm', 'c_w_out']
TWIN_WEIGHTS = ['norm_mix', 'norm_ffn', 'ffn_w_gate', 'ffn_w_up', 'ffn_conv_w', 'ffn_conv_b', 'ffn_w_down', 'a_w_in', 'a_b_in', 'a_v_norm', 'a_w_s', 'a_b_s', 'a_w_out', 'b_w_in', 'b_b_f', 'b_q_norm', 'b_k_norm', 'b_w_out', 'c_w_in', 'c_conv_w', 'c_a_log', 'c_dt_bias', 'c_out_norm', 'c_w_out']
TWIN_DIFF_INPUT = 'x'
TWIN_INPUTS = ['x', 'norm_mix', 'norm_ffn', 'ffn_w_gate', 'ffn_w_up', 'ffn_conv_w', 'ffn_conv_b', 'ffn_w_down', 'a_w_in', 'a_b_in', 'a_v_norm', 'a_w_s', 'a_b_s', 'a_w_out', 'b_w_in', 'b_b_f', 'b_q_norm', 'b_k_norm', 'b_w_out', 'c_w_in', 'c_conv_w', 'c_a_log', 'c_dt_bias', 'c_out_norm', 'c_w_out', 'loss_target', 'm_norm_mix', 'm_norm_ffn', 'm_ffn_w_gate', 'm_ffn_w_up', 'm_ffn_conv_w', 'm_ffn_conv_b', 'm_ffn_w_down', 'm_a_w_in', 'm_a_b_in', 'm_a_v_norm', 'm_a_w_s', 'm_a_b_s', 'm_a_w_out', 'm_b_w_in', 'm_b_b_f', 'm_b_q_norm', 'm_b_k_norm', 'm_b_w_out', 'm_c_w_in', 'm_c_conv_w', 'm_c_a_log', 'm_c_dt_bias', 'm_c_out_norm', 'm_c_w_out', 'v_norm_mix', 'v_norm_ffn', 'v_ffn_w_gate', 'v_ffn_w_up', 'v_ffn_conv_w', 'v_ffn_conv_b', 'v_ffn_w_down', 'v_a_w_in', 'v_a_b_in', 'v_a_v_norm', 'v_a_w_s', 'v_a_b_s', 'v_a_w_out', 'v_b_w_in', 'v_b_b_f', 'v_b_q_norm', 'v_b_k_norm', 'v_b_w_out', 'v_c_w_in', 'v_c_conv_w', 'v_c_a_log', 'v_c_dt_bias', 'v_c_out_norm', 'v_c_w_out']
TWIN_OUTPUTS = ['loss', 'grad_x', 'grad_norm_mix', 'grad_norm_ffn', 'grad_ffn_w_gate', 'grad_ffn_w_up', 'grad_ffn_conv_w', 'grad_ffn_conv_b', 'grad_ffn_w_down', 'grad_a_w_in', 'grad_a_b_in', 'grad_a_v_norm', 'grad_a_w_s', 'grad_a_b_s', 'grad_a_w_out', 'grad_b_w_in', 'grad_b_b_f', 'grad_b_q_norm', 'grad_b_k_norm', 'grad_b_w_out', 'grad_c_w_in', 'grad_c_conv_w', 'grad_c_a_log', 'grad_c_dt_bias', 'grad_c_out_norm', 'grad_c_w_out', 'delta_norm_mix', 'delta_norm_ffn', 'delta_ffn_w_gate', 'delta_ffn_w_up', 'delta_ffn_conv_w', 'delta_ffn_conv_b', 'delta_ffn_w_down', 'delta_a_w_in', 'delta_a_b_in', 'delta_a_v_norm', 'delta_a_w_s', 'delta_a_b_s', 'delta_a_w_out', 'delta_b_w_in', 'delta_b_b_f', 'delta_b_q_norm', 'delta_b_k_norm', 'delta_b_w_out', 'delta_c_w_in', 'delta_c_conv_w', 'delta_c_a_log', 'delta_c_dt_bias', 'delta_c_out_norm', 'delta_c_w_out', 'new_m_norm_mix', 'new_m_norm_ffn', 'new_m_ffn_w_gate', 'new_m_ffn_w_up', 'new_m_ffn_conv_w', 'new_m_ffn_conv_b', 'new_m_ffn_w_down', 'new_m_a_w_in', 'new_m_a_b_in', 'new_m_a_v_norm', 'new_m_a_w_s', 'new_m_a_b_s', 'new_m_a_w_out', 'new_m_b_w_in', 'new_m_b_b_f', 'new_m_b_q_norm', 'new_m_b_k_norm', 'new_m_b_w_out', 'new_m_c_w_in', 'new_m_c_conv_w', 'new_m_c_a_log', 'new_m_c_dt_bias', 'new_m_c_out_norm', 'new_m_c_w_out', 'new_v_norm_mix', 'new_v_norm_ffn', 'new_v_ffn_w_gate', 'new_v_ffn_w_up', 'new_v_ffn_conv_w', 'new_v_ffn_conv_b', 'new_v_ffn_w_down', 'new_v_a_w_in', 'new_v_a_b_in', 'new_v_a_v_norm', 'new_v_a_w_s', 'new_v_a_b_s', 'new_v_a_w_out', 'new_v_b_w_in', 'new_v_b_b_f', 'new_v_b_q_norm', 'new_v_b_k_norm', 'new_v_b_w_out', 'new_v_c_w_in', 'new_v_c_conv_w', 'new_v_c_a_log', 'new_v_c_dt_bias', 'new_v_c_out_norm', 'new_v_c_w_out']
TWIN_LEAF_KINDS = {'loss': 'loss', 'grad_x': 'grad_x', 'grad_norm_mix': 'grad_w', 'grad_norm_ffn': 'grad_w', 'grad_ffn_w_gate': 'grad_w', 'grad_ffn_w_up': 'grad_w', 'grad_ffn_conv_w': 'grad_w', 'grad_ffn_conv_b': 'grad_w', 'grad_ffn_w_down': 'grad_w', 'grad_a_w_in': 'grad_w', 'grad_a_b_in': 'grad_w', 'grad_a_v_norm': 'grad_w', 'grad_a_w_s': 'grad_w', 'grad_a_b_s': 'grad_w', 'grad_a_w_out': 'grad_w', 'grad_b_w_in': 'grad_w', 'grad_b_b_f': 'grad_w', 'grad_b_q_norm': 'grad_w', 'grad_b_k_norm': 'grad_w', 'grad_b_w_out': 'grad_w', 'grad_c_w_in': 'grad_w', 'grad_c_conv_w': 'grad_w', 'grad_c_a_log': 'grad_w', 'grad_c_dt_bias': 'grad_w', 'grad_c_out_norm': 'grad_w', 'grad_c_w_out': 'grad_w', 'delta_norm_mix': 'delta_w', 'delta_norm_ffn': 'delta_w', 'delta_ffn_w_gate': 'delta_w', 'delta_ffn_w_up': 'delta_w', 'delta_ffn_conv_w': 'delta_w', 'delta_ffn_conv_b': 'delta_w', 'delta_ffn_w_down': 'delta_w', 'delta_a_w_in': 'delta_w', 'delta_a_b_in': 'delta_w', 'delta_a_v_norm': 'delta_w', 'delta_a_w_s': 'delta_w', 'delta_a_b_s': 'delta_w', 'delta_a_w_out': 'delta_w', 'delta_b_w_in': 'delta_w', 'delta_b_b_f': 'delta_w', 'delta_b_q_norm': 'delta_w', 'delta_b_k_norm': 'delta_w', 'delta_b_w_out': 'delta_w', 'delta_c_w_in': 'delta_w', 'delta_c_conv_w': 'delta_w', 'delta_c_a_log': 'delta_w', 'delta_c_dt_bias': 'delta_w', 'delta_c_out_norm': 'delta_w', 'delta_c_w_out': 'delta_w', 'new_m_norm_mix': 'new_m', 'new_m_norm_ffn': 'new_m', 'new_m_ffn_w_gate': 'new_m', 'new_m_ffn_w_up': 'new_m', 'new_m_ffn_conv_w': 'new_m', 'new_m_ffn_conv_b': 'new_m', 'new_m_ffn_w_down': 'new_m', 'new_m_a_w_in': 'new_m', 'new_m_a_b_in': 'new_m', 'new_m_a_v_norm': 'new_m', 'new_m_a_w_s': 'new_m', 'new_m_a_b_s': 'new_m', 'new_m_a_w_out': 'new_m', 'new_m_b_w_in': 'new_m', 'new_m_b_b_f': 'new_m', 'new_m_b_q_norm': 'new_m', 'new_m_b_k_norm': 'new_m', 'new_m_b_w_out': 'new_m', 'new_m_c_w_in': 'new_m', 'new_m_c_conv_w': 'new_m', 'new_m_c_a_log': 'new_m', 'new_m_c_dt_bias': 'new_m', 'new_m_c_out_norm': 'new_m', 'new_m_c_w_out': 'new_m', 'new_v_norm_mix': 'new_v', 'new_v_norm_ffn': 'new_v', 'new_v_ffn_w_gate': 'new_v', 'new_v_ffn_w_up': 'new_v', 'new_v_ffn_conv_w': 'new_v', 'new_v_ffn_conv_b': 'new_v', 'new_v_ffn_w_down': 'new_v', 'new_v_a_w_in': 'new_v', 'new_v_a_b_in': 'new_v', 'new_v_a_v_norm': 'new_v', 'new_v_a_w_s': 'new_v', 'new_v_a_b_s': 'new_v', 'new_v_a_w_out': 'new_v', 'new_v_b_w_in': 'new_v', 'new_v_b_b_f': 'new_v', 'new_v_b_q_norm': 'new_v', 'new_v_b_k_norm': 'new_v', 'new_v_b_w_out': 'new_v', 'new_v_c_w_in': 'new_v', 'new_v_c_conv_w': 'new_v', 'new_v_c_a_log': 'new_v', 'new_v_c_dt_bias': 'new_v', 'new_v_c_out_norm': 'new_v', 'new_v_c_w_out': 'new_v'}


def _forward(args):
    return _fwd_reference(*[args[k] for k in FWD_PARAMS])


def _output_shape():
    out = _jax.eval_shape(lambda: _forward(_fwd_setup_inputs(0)))
    return out.shape, out.dtype

N_MICROBATCH = 1
ADAM_LR = 0.001
ADAM_B1 = 0.9
ADAM_B2 = 0.999
ADAM_EPS = 1e-08
ADAM_WD = 0.01
ADAM_STEP = 10
PER_EXAMPLE_BATCH_AXIS = {'x': 0, 'loss_target': 0}
SHARED_INPUTS = []
_WEIGHT_DTYPES = {'norm_mix': _jnp.float32, 'norm_ffn': _jnp.float32, 'ffn_w_gate': _jnp.float32, 'ffn_w_up': _jnp.float32, 'ffn_conv_w': _jnp.float32, 'ffn_conv_b': _jnp.float32, 'ffn_w_down': _jnp.float32, 'a_w_in': _jnp.float32, 'a_b_in': _jnp.float32, 'a_v_norm': _jnp.float32, 'a_w_s': _jnp.float32, 'a_b_s': _jnp.float32, 'a_w_out': _jnp.float32, 'b_w_in': _jnp.float32, 'b_b_f': _jnp.float32, 'b_q_norm': _jnp.float32, 'b_k_norm': _jnp.float32, 'b_w_out': _jnp.float32, 'c_w_in': _jnp.float32, 'c_conv_w': _jnp.float32, 'c_a_log': _jnp.float32, 'c_dt_bias': _jnp.float32, 'c_out_norm': _jnp.float32, 'c_w_out': _jnp.float32}
MOMENT_SCALE = {'norm_mix': 8.900413e+00, 'norm_ffn': 1.299549e+01, 'ffn_w_gate': 4.564377e-01, 'ffn_w_up': 2.910544e-01, 'ffn_conv_w': 1.523764e+00, 'ffn_conv_b': 1.888449e+00, 'ffn_w_down': 4.464402e-01, 'a_w_in': 6.987593e-01, 'a_b_in': 7.112379e+00, 'a_v_norm': 3.433509e+00, 'a_w_s': 2.052490e+00, 'a_b_s': 6.987866e+00, 'a_w_out': 2.840249e+00, 'b_w_in': 5.389960e-01, 'b_b_f': 2.460440e+01, 'b_q_norm': 2.944545e+00, 'b_k_norm': 2.934655e+00, 'b_w_out': 1.068047e+00, 'c_w_in': 4.157521e-01, 'c_conv_w': 5.688240e-01, 'c_a_log': 7.907766e+00, 'c_dt_bias': 7.523272e+00, 'c_out_norm': 8.839654e+01, 'c_w_out': 1.276916e+00}


def _to_microbatches(a, axis):
    t = _jnp.moveaxis(a, axis, 0)
    t = t.reshape((N_MICROBATCH, t.shape[0] // N_MICROBATCH) + t.shape[1:])
    return _jnp.moveaxis(t, 1, axis + 1)


def setup_inputs(seed: int = 0) -> dict:
    inp = _fwd_setup_inputs(seed)
    key = _jax.random.fold_in(_jax.random.key(seed), 7919)
    shape, _ = _output_shape()
    out = dict(inp)
    out["loss_target"] = _jax.random.normal(_jax.random.fold_in(key, 0), shape, _jnp.float32)
    for i, name in enumerate(TWIN_WEIGHTS):
        w = inp[name].astype(_jnp.float32)
        if MOMENT_SCALE is None:
            s = _jnp.sqrt(_jnp.mean(_jnp.square(w)) + 1e-30)
        else:
            s = MOMENT_SCALE[name]
        km, kv = _jax.random.split(_jax.random.fold_in(key, i + 1))
        out[name] = w
        out["m_" + name] = s * _jax.random.normal(km, w.shape, _jnp.float32)
        out["v_" + name] = (s * s) * _jax.random.uniform(kv, w.shape, _jnp.float32, 0.5, 1.5)
    if N_MICROBATCH > 1:
        for name, axis in PER_EXAMPLE_BATCH_AXIS.items():
            out[name] = _to_microbatches(out[name], axis)
    return {'x': out['x'], 'norm_mix': out['norm_mix'], 'norm_ffn': out['norm_ffn'], 'ffn_w_gate': out['ffn_w_gate'], 'ffn_w_up': out['ffn_w_up'], 'ffn_conv_w': out['ffn_conv_w'], 'ffn_conv_b': out['ffn_conv_b'], 'ffn_w_down': out['ffn_w_down'], 'a_w_in': out['a_w_in'], 'a_b_in': out['a_b_in'], 'a_v_norm': out['a_v_norm'], 'a_w_s': out['a_w_s'], 'a_b_s': out['a_b_s'], 'a_w_out': out['a_w_out'], 'b_w_in': out['b_w_in'], 'b_b_f': out['b_b_f'], 'b_q_norm': out['b_q_norm'], 'b_k_norm': out['b_k_norm'], 'b_w_out': out['b_w_out'], 'c_w_in': out['c_w_in'], 'c_conv_w': out['c_conv_w'], 'c_a_log': out['c_a_log'], 'c_dt_bias': out['c_dt_bias'], 'c_out_norm': out['c_out_norm'], 'c_w_out': out['c_w_out'], 'loss_target': out['loss_target'], 'm_norm_mix': out['m_norm_mix'], 'm_norm_ffn': out['m_norm_ffn'], 'm_ffn_w_gate': out['m_ffn_w_gate'], 'm_ffn_w_up': out['m_ffn_w_up'], 'm_ffn_conv_w': out['m_ffn_conv_w'], 'm_ffn_conv_b': out['m_ffn_conv_b'], 'm_ffn_w_down': out['m_ffn_w_down'], 'm_a_w_in': out['m_a_w_in'], 'm_a_b_in': out['m_a_b_in'], 'm_a_v_norm': out['m_a_v_norm'], 'm_a_w_s': out['m_a_w_s'], 'm_a_b_s': out['m_a_b_s'], 'm_a_w_out': out['m_a_w_out'], 'm_b_w_in': out['m_b_w_in'], 'm_b_b_f': out['m_b_b_f'], 'm_b_q_norm': out['m_b_q_norm'], 'm_b_k_norm': out['m_b_k_norm'], 'm_b_w_out': out['m_b_w_out'], 'm_c_w_in': out['m_c_w_in'], 'm_c_conv_w': out['m_c_conv_w'], 'm_c_a_log': out['m_c_a_log'], 'm_c_dt_bias': out['m_c_dt_bias'], 'm_c_out_norm': out['m_c_out_norm'], 'm_c_w_out': out['m_c_w_out'], 'v_norm_mix': out['v_norm_mix'], 'v_norm_ffn': out['v_norm_ffn'], 'v_ffn_w_gate': out['v_ffn_w_gate'], 'v_ffn_w_up': out['v_ffn_w_up'], 'v_ffn_conv_w': out['v_ffn_conv_w'], 'v_ffn_conv_b': out['v_ffn_conv_b'], 'v_ffn_w_down': out['v_ffn_w_down'], 'v_a_w_in': out['v_a_w_in'], 'v_a_b_in': out['v_a_b_in'], 'v_a_v_norm': out['v_a_v_norm'], 'v_a_w_s': out['v_a_w_s'], 'v_a_b_s': out['v_a_b_s'], 'v_a_w_out': out['v_a_w_out'], 'v_b_w_in': out['v_b_w_in'], 'v_b_b_f': out['v_b_b_f'], 'v_b_q_norm': out['v_b_q_norm'], 'v_b_k_norm': out['v_b_k_norm'], 'v_b_w_out': out['v_b_w_out'], 'v_c_w_in': out['v_c_w_in'], 'v_c_conv_w': out['v_c_conv_w'], 'v_c_a_log': out['v_c_a_log'], 'v_c_dt_bias': out['v_c_dt_bias'], 'v_c_out_norm': out['v_c_out_norm'], 'v_c_w_out': out['v_c_w_out']}


def _loss(weights, diff, rest, loss_target):
    with _jax.named_scope("forward"):
        args = {**rest, TWIN_DIFF_INPUT: diff, **{k: w.astype(_WEIGHT_DTYPES[k]) for k, w in weights.items()}}
        y = _forward(args)
    with _jax.named_scope("loss_head"):
        err = _jnp.square(y.astype(_jnp.float32) - loss_target)
        return 0.5 * _jnp.sum(_jnp.mean(err, axis=-1)) if err.ndim else 0.5 * err


def _adamw(w, g, m, v):
    m = ADAM_B1 * m + (1.0 - ADAM_B1) * g
    v = ADAM_B2 * v + (1.0 - ADAM_B2) * _jnp.square(g)
    m_hat = m / (1.0 - ADAM_B1 ** ADAM_STEP)
    v_hat = v / (1.0 - ADAM_B2 ** ADAM_STEP)
    delta = -ADAM_LR * (m_hat / (_jnp.sqrt(v_hat) + ADAM_EPS) + ADAM_WD * w)
    return delta, m, v


def reference(x, norm_mix, norm_ffn, ffn_w_gate, ffn_w_up, ffn_conv_w, ffn_conv_b, ffn_w_down, a_w_in, a_b_in, a_v_norm, a_w_s, a_b_s, a_w_out, b_w_in, b_b_f, b_q_norm, b_k_norm, b_w_out, c_w_in, c_conv_w, c_a_log, c_dt_bias, c_out_norm, c_w_out, loss_target, m_norm_mix, m_norm_ffn, m_ffn_w_gate, m_ffn_w_up, m_ffn_conv_w, m_ffn_conv_b, m_ffn_w_down, m_a_w_in, m_a_b_in, m_a_v_norm, m_a_w_s, m_a_b_s, m_a_w_out, m_b_w_in, m_b_b_f, m_b_q_norm, m_b_k_norm, m_b_w_out, m_c_w_in, m_c_conv_w, m_c_a_log, m_c_dt_bias, m_c_out_norm, m_c_w_out, v_norm_mix, v_norm_ffn, v_ffn_w_gate, v_ffn_w_up, v_ffn_conv_w, v_ffn_conv_b, v_ffn_w_down, v_a_w_in, v_a_b_in, v_a_v_norm, v_a_w_s, v_a_b_s, v_a_w_out, v_b_w_in, v_b_b_f, v_b_q_norm, v_b_k_norm, v_b_w_out, v_c_w_in, v_c_conv_w, v_c_a_log, v_c_dt_bias, v_c_out_norm, v_c_w_out):
    given = dict(x=x, norm_mix=norm_mix, norm_ffn=norm_ffn, ffn_w_gate=ffn_w_gate, ffn_w_up=ffn_w_up, ffn_conv_w=ffn_conv_w, ffn_conv_b=ffn_conv_b, ffn_w_down=ffn_w_down, a_w_in=a_w_in, a_b_in=a_b_in, a_v_norm=a_v_norm, a_w_s=a_w_s, a_b_s=a_b_s, a_w_out=a_w_out, b_w_in=b_w_in, b_b_f=b_b_f, b_q_norm=b_q_norm, b_k_norm=b_k_norm, b_w_out=b_w_out, c_w_in=c_w_in, c_conv_w=c_conv_w, c_a_log=c_a_log, c_dt_bias=c_dt_bias, c_out_norm=c_out_norm, c_w_out=c_w_out, loss_target=loss_target, m_norm_mix=m_norm_mix, m_norm_ffn=m_norm_ffn, m_ffn_w_gate=m_ffn_w_gate, m_ffn_w_up=m_ffn_w_up, m_ffn_conv_w=m_ffn_conv_w, m_ffn_conv_b=m_ffn_conv_b, m_ffn_w_down=m_ffn_w_down, m_a_w_in=m_a_w_in, m_a_b_in=m_a_b_in, m_a_v_norm=m_a_v_norm, m_a_w_s=m_a_w_s, m_a_b_s=m_a_b_s, m_a_w_out=m_a_w_out, m_b_w_in=m_b_w_in, m_b_b_f=m_b_b_f, m_b_q_norm=m_b_q_norm, m_b_k_norm=m_b_k_norm, m_b_w_out=m_b_w_out, m_c_w_in=m_c_w_in, m_c_conv_w=m_c_conv_w, m_c_a_log=m_c_a_log, m_c_dt_bias=m_c_dt_bias, m_c_out_norm=m_c_out_norm, m_c_w_out=m_c_w_out, v_norm_mix=v_norm_mix, v_norm_ffn=v_norm_ffn, v_ffn_w_gate=v_ffn_w_gate, v_ffn_w_up=v_ffn_w_up, v_ffn_conv_w=v_ffn_conv_w, v_ffn_conv_b=v_ffn_conv_b, v_ffn_w_down=v_ffn_w_down, v_a_w_in=v_a_w_in, v_a_b_in=v_a_b_in, v_a_v_norm=v_a_v_norm, v_a_w_s=v_a_w_s, v_a_b_s=v_a_b_s, v_a_w_out=v_a_w_out, v_b_w_in=v_b_w_in, v_b_b_f=v_b_b_f, v_b_q_norm=v_b_q_norm, v_b_k_norm=v_b_k_norm, v_b_w_out=v_b_w_out, v_c_w_in=v_c_w_in, v_c_conv_w=v_c_conv_w, v_c_a_log=v_c_a_log, v_c_dt_bias=v_c_dt_bias, v_c_out_norm=v_c_out_norm, v_c_w_out=v_c_w_out)
    weights = {n: given[n] for n in TWIN_WEIGHTS}
    shared = {n: given[n] for n in SHARED_INPUTS}
    per_example = {n: given[n] for n in ['x']}
    grad_fn = _jax.value_and_grad(_loss, argnums=(0, 1))

    def one_microbatch(ex, loss_target):
        ex = dict(ex)
        diff = ex.pop(TWIN_DIFF_INPUT)
        return grad_fn(weights, diff, {**shared, **ex}, loss_target)

    if N_MICROBATCH == 1:
        loss, (grad_w, grad_x) = one_microbatch(per_example, given["loss_target"])
    else:
        def body(carry, xs):
            loss_sum, grad_sum = carry
            l_k, (gw_k, gx_k) = one_microbatch(xs[0], xs[1])
            with _jax.named_scope("update"):
                return (loss_sum + l_k, _jax.tree.map(_jnp.add, grad_sum, gw_k)), gx_k

        init = (_jnp.zeros((), _jnp.float32), _jax.tree.map(_jnp.zeros_like, weights))
        (loss, grad_w), grad_x = _jax.lax.scan(body, init, (per_example, given["loss_target"]))
    with _jax.named_scope("update"):
        delta_w, new_m, new_v = {}, {}, {}
        for n in TWIN_WEIGHTS:
            delta_w[n], new_m[n], new_v[n] = _adamw(weights[n], grad_w[n], given["m_" + n], given["v_" + n])
    return (loss, grad_x, *[grad_w[n] for n in TWIN_WEIGHTS], *[delta_w[n] for n in TWIN_WEIGHTS],
            *[new_m[n] for n in TWIN_WEIGHTS], *[new_v[n] for n in TWIN_WEIGHTS])
```

```python
import functools

import jax
import jax.numpy as jnp
from jax import lax
from jax.experimental import pallas as pl
from jax.experimental.pallas import tpu as pltpu

F32 = jnp.float32
BF16 = jnp.bfloat16
HI = lax.Precision.HIGHEST
MESH = pl.DeviceIdType.MESH

RMS_EPS = 1e-6
ADAM_LR, ADAM_B1, ADAM_B2, ADAM_EPS, ADAM_WD, ADAM_STEP = 0.001, 0.9, 0.999, 1e-08, 0.01, 10
FFN_CONV, C_CONV = 3, 4
A_CHUNK, HEAD, GDN_CHUNK = 128, 128, 64
LANES, SUBLANES = 128, 8
PACK_COLS = 1024
N_CHIPS = 4
VMEM_LIMIT = 56 * 1024 * 1024
ROWMAP_BUDGET = 20 * 1024 * 1024

NN = (((1,), (0,)), ((), ()))
NT = (((1,), (1,)), ((), ()))
TN = (((0,), (0,)), ((), ()))


def _pick(n, cap, mult):
    if n <= cap:
        return n
    best = None
    for d in range(mult, cap + 1, mult):
        if n % d == 0:
            best = d
    if best is None:
        raise ValueError(f"no tile for {n} (cap {cap}, multiple of {mult})")
    return best


def _pad_cols(n):
    j = -(-n // LANES)
    while not (j <= 8 or any(j % d == 0 for d in (4, 5, 6, 7, 8))):
        j += 1
    return j * LANES


def _cparams(sem):
    return pltpu.CompilerParams(dimension_semantics=sem, vmem_limit_bytes=VMEM_LIMIT)


def _matmul(a, b, kind, out_dtype, name):
    if kind == "nn":
        (m, k), (k2, n) = a.shape, b.shape
    elif kind == "nt":
        (m, k), (n, k2) = a.shape, b.shape
    else:
        (k, m), (k2, n) = a.shape, b.shape
    assert k == k2, (name, a.shape, b.shape)
    tm, tn, tk = _pick(m, 1024, LANES), _pick(n, 1024, LANES), _pick(k, 512, LANES)
    nk = k // tk
    dims = {"nn": NN, "nt": NT, "tn": TN}[kind]
    a_spec = pl.BlockSpec((tk, tm), lambda i, j, kk: (kk, i)) if kind == "tn" else pl.BlockSpec((tm, tk), lambda i, j, kk: (i, kk))
    b_spec = pl.BlockSpec((tn, tk), lambda i, j, kk: (j, kk)) if kind == "nt" else pl.BlockSpec((tk, tn), lambda i, j, kk: (kk, j))

    def body(a_ref, b_ref, o_ref, acc_ref):
        kk = pl.program_id(2)

        @pl.when(kk == 0)
        def _():
            acc_ref[...] = jnp.zeros_like(acc_ref)

        acc_ref[...] += lax.dot_general(a_ref[...].astype(BF16), b_ref[...].astype(BF16), dims, preferred_element_type=F32)

        @pl.when(kk == nk - 1)
        def _():
            o_ref[...] = acc_ref[...].astype(o_ref.dtype)

    return pl.pallas_call(
        body, name=name, grid=(m // tm, n // tn, nk),
        in_specs=[a_spec, b_spec], out_specs=pl.BlockSpec((tm, tn), lambda i, j, kk: (i, j)),
        out_shape=jax.ShapeDtypeStruct((m, n), out_dtype),
        scratch_shapes=[pltpu.VMEM((tm, tn), F32)],
        compiler_params=_cparams(("parallel", "parallel", "arbitrary")),
    )(a, b)


def _win(arr, width=None, blk=0):
    return (arr, arr.shape[1] if width is None else width, blk)


def _rowmap(fn, rows, params, name, n_acc=0, tc=None, col_params=()):
    rows = [r if isinstance(r, tuple) else _win(r) for r in rows]
    t = rows[0][0].shape[0]
    widths = [tc if tc is not None else w for (_, w, _) in rows]

    def blocks_for(tr):
        rb = [jax.ShapeDtypeStruct((tr, w), a.dtype) for (a, _, _), w in zip(rows, widths)]
        pb = [jax.ShapeDtypeStruct((p.shape[0], tc) if (i in col_params) else p.shape, p.dtype) for i, p in enumerate(params)]
        return rb, pb

    rb, pb = blocks_for(SUBLANES * 2)
    outs = jax.eval_shape(fn, *rb, *pb)
    outs = list(outs) if isinstance(outs, (tuple, list)) else [outs]
    n_row = len(outs) - n_acc
    row_bytes = sum(w * a.dtype.itemsize for (a, _, _), w in zip(rows, widths)) + sum(o.shape[1] * o.dtype.itemsize for o in outs[:n_row])
    tr = 16
    while tr * 2 <= 512 and t % (tr * 2) == 0 and (tr * 2) * row_bytes * 5 <= ROWMAP_BUDGET:
        tr *= 2
    rb, pb = blocks_for(tr)
    outs = jax.eval_shape(fn, *rb, *pb)
    outs = list(outs) if isinstance(outs, (tuple, list)) else [outs]
    n_in = len(rows) + len(params)

    if tc is None:
        grid = (t // tr,)
        row_axis = 0
        in_specs = [pl.BlockSpec((tr, w), functools.partial(lambda i, b: (i, b), b=blk)) for (_, w, blk) in rows]
        in_specs += [pl.BlockSpec(p.shape, functools.partial(lambda i, nd: (0,) * nd, nd=p.ndim)) for p in params]
        out_specs = [pl.BlockSpec((tr, o.shape[1]), lambda i: (i, 0)) for o in outs[:n_row]]
        out_specs += [pl.BlockSpec(o.shape, functools.partial(lambda i, nd: (0,) * nd, nd=len(o.shape))) for o in outs[n_row:]]
        out_shape = [jax.ShapeDtypeStruct((t, o.shape[1]), o.dtype) for o in outs[:n_row]]
        out_shape += [jax.ShapeDtypeStruct(o.shape, o.dtype) for o in outs[n_row:]]
        sem = ("arbitrary",) if n_acc else ("parallel",)
    else:
        wtot = rows[0][1]
        grid = (wtot // tc, t // tr)
        row_axis = 1
        in_specs = [pl.BlockSpec((tr, tc), functools.partial(lambda j, i, b: (i, j + b), b=blk)) for (_, _, blk) in rows]
        for i, p in enumerate(params):
            if i in col_params:
                in_specs.append(pl.BlockSpec((p.shape[0], tc), lambda j, i: (0, j)))
            else:
                in_specs.append(pl.BlockSpec(p.shape, functools.partial(lambda j, i, nd: (0,) * nd, nd=p.ndim)))
        out_specs = [pl.BlockSpec((tr, tc), lambda j, i: (i, j)) for _ in outs[:n_row]]
        out_specs += [pl.BlockSpec((o.shape[0], tc), lambda j, i: (0, j)) for o in outs[n_row:]]
        out_shape = [jax.ShapeDtypeStruct((t, wtot), o.dtype) for o in outs[:n_row]]
        out_shape += [jax.ShapeDtypeStruct((o.shape[0], wtot), o.dtype) for o in outs[n_row:]]
        sem = ("parallel", "arbitrary") if n_acc else ("parallel", "parallel")

    def body(*refs):
        ins, ors = refs[:n_in], refs[n_in:]
        res = fn(*[r[...] for r in ins])
        res = list(res) if isinstance(res, (tuple, list)) else [res]
        for o, r in zip(ors[:n_row], res[:n_row]):
            o[...] = r.astype(o.dtype)
        if n_acc:
            i = pl.program_id(row_axis)
            for o, r in zip(ors[n_row:], res[n_row:]):
                @pl.when(i == 0)
                def _(o=o, r=r):
                    o[...] = r.astype(o.dtype)

                @pl.when(i > 0)
                def _(o=o, r=r):
                    o[...] += r.astype(o.dtype)

    res = pl.pallas_call(
        body, name=name, grid=grid, in_specs=in_specs, out_specs=out_specs, out_shape=out_shape,
        compiler_params=_cparams(sem),
    )(*[a for (a, _, _) in rows], *params)
    return res


def _elementwise(fn, arrays, n_out, name):
    shape = arrays[0].shape
    cols = shape[-1]
    rws = 1
    for s in shape[:-1]:
        rws *= s
    arrs = [a.reshape(rws, cols) for a in arrays]
    per_row = cols * 4 * (len(arrays) + n_out) * 3
    tr = rws
    if rws * per_row > ROWMAP_BUDGET:
        tr = _pick(rws, max(SUBLANES, ROWMAP_BUDGET // per_row), SUBLANES)

    def body(*refs):
        res = fn(*[r[...] for r in refs[:len(arrs)]])
        for o, r in zip(refs[len(arrs):], res):
            o[...] = r

    spec = pl.BlockSpec((tr, cols), lambda i: (i, 0))
    outs = pl.pallas_call(
        body, name=name, grid=(rws // tr,), in_specs=[spec] * len(arrs), out_specs=[spec] * n_out,
        out_shape=[jax.ShapeDtypeStruct((rws, cols), F32)] * n_out, compiler_params=_cparams(("parallel",)),
    )(*arrs)
    return [o.reshape(shape) for o in outs]


def _rms(x, g):
    return x * lax.rsqrt(jnp.mean(x * x, axis=-1, keepdims=True) + RMS_EPS) * g


def _silu(x):
    return x * jax.nn.sigmoid(x)


def _softplus(x):
    return jnp.maximum(x, 0.0) + jnp.log(1.0 + jnp.exp(-jnp.abs(x)))


def _gelu_tanh(x):
    return 0.5 * x * (1.0 + jnp.tanh(0.7978845608028654 * (x + 0.044715 * (x * x * x))))


def _heads(x):
    return [x[:, h * HEAD:(h + 1) * HEAD] for h in range(x.shape[1] // HEAD)]


def _dot(a, b, dims=NN):
    return lax.dot_general(a, b, dims, precision=HI, preferred_element_type=F32)


def _eye(n):
    return (lax.broadcasted_iota(jnp.int32, (n, n), 0) == lax.broadcasted_iota(jnp.int32, (n, n), 1)).astype(F32)


def _tri(n):
    return lax.broadcasted_iota(jnp.int32, (n, n), 0) >= lax.broadcasted_iota(jnp.int32, (n, n), 1)


def _row_to_col(row):
    return jnp.sum(_eye(row.shape[1]) * row, axis=1, keepdims=True)


def _conv_tiles(t, w, seq):
    tc = _pick(w, 512, LANES)
    tr = _pick(seq, 512, SUBLANES)
    return tr, tc


def _conv_fwd(x, width, w, bias, seq, name):
    t = x.shape[0]
    kk = w.shape[0]
    tr, tc = _conv_tiles(t, width, seq)
    hb = tr // SUBLANES

    def body(*refs):
        if bias is None:
            x_ref, h_ref, w_ref, o_ref = refs
        else:
            x_ref, h_ref, w_ref, b_ref, o_ref = refs
        i = pl.program_id(1)
        first = (i * tr) % seq == 0
        halo = jnp.where(first, 0.0, h_ref[...])
        xe = jnp.concatenate([halo, x_ref[...]], axis=0)
        wv = w_ref[...]
        acc = xe[SUBLANES:, :] * wv[kk - 1:kk, :]
        for s in range(1, kk):
            acc = acc + pltpu.roll(xe, s, 0)[SUBLANES:, :] * wv[kk - 1 - s:kk - s, :]
        if bias is not None:
            acc = acc + b_ref[...]
        o_ref[...] = acc

    in_specs = [pl.BlockSpec((tr, tc), lambda j, i: (i, j)),
                pl.BlockSpec((SUBLANES, tc), lambda j, i: (jnp.maximum(i * hb - 1, 0), j)),
                pl.BlockSpec((kk, tc), lambda j, i: (0, j))]
    ops = [x, x, w]
    if bias is not None:
        in_specs.append(pl.BlockSpec((1, tc), lambda j, i: (0, j)))
        ops.append(bias)
    return pl.pallas_call(
        body, name=name, grid=(width // tc, t // tr), in_specs=in_specs,
        out_specs=pl.BlockSpec((tr, tc), lambda j, i: (i, j)), out_shape=jax.ShapeDtypeStruct((t, width), F32),
        compiler_params=_cparams(("parallel", "parallel")),
    )(*ops)


def _conv_bwd(dy, x, width, w, seq, with_bias, name):
    t = x.shape[0]
    kk = w.shape[0]
    tr, tc = _conv_tiles(t, width, seq)
    hb = tr // SUBLANES
    n_halo_blocks = t // SUBLANES

    def body(dy_ref, dyn_ref, x_ref, xh_ref, w_ref, dx_ref, dw_ref, *rest):
        i = pl.program_id(1)
        first = (i * tr) % seq == 0
        last = ((i + 1) * tr) % seq == 0
        dyc = dy_ref[...]
        dye = jnp.concatenate([dyc, jnp.where(last, 0.0, dyn_ref[...])], axis=0)
        xe = jnp.concatenate([jnp.where(first, 0.0, xh_ref[...]), x_ref[...]], axis=0)
        wv = w_ref[...]
        dx = dyc * wv[kk - 1:kk, :]
        dws = [None] * kk
        dws[kk - 1] = jnp.sum(dyc * xe[SUBLANES:, :], axis=0, keepdims=True)
        for s in range(1, kk):
            dx = dx + pltpu.roll(dye, tr + SUBLANES - s, 0)[:tr, :] * wv[kk - 1 - s:kk - s, :]
            dws[kk - 1 - s] = jnp.sum(dyc * pltpu.roll(xe, s, 0)[SUBLANES:, :], axis=0, keepdims=True)
        dx_ref[...] = dx

        @pl.when(i == 0)
        def _():
            for j in range(kk):
                dw_ref[j:j + 1, :] = dws[j]
            if with_bias:
                rest[0][...] = jnp.sum(dyc, axis=0, keepdims=True)

        @pl.when(i > 0)
        def _():
            for j in range(kk):
                dw_ref[j:j + 1, :] += dws[j]
            if with_bias:
                rest[0][...] += jnp.sum(dyc, axis=0, keepdims=True)

    cur = pl.BlockSpec((tr, tc), lambda j, i: (i, j))
    in_specs = [cur, pl.BlockSpec((SUBLANES, tc), lambda j, i: (jnp.minimum((i + 1) * hb, n_halo_blocks - 1), j)),
                cur, pl.BlockSpec((SUBLANES, tc), lambda j, i: (jnp.maximum(i * hb - 1, 0), j)),
                pl.BlockSpec((kk, tc), lambda j, i: (0, j))]
    out_specs = [cur, pl.BlockSpec((kk, tc), lambda j, i: (0, j))]
    out_shape = [jax.ShapeDtypeStruct((t, width), F32), jax.ShapeDtypeStruct((kk, width), F32)]
    if with_bias:
        out_specs.append(pl.BlockSpec((1, tc), lambda j, i: (0, j)))
        out_shape.append(jax.ShapeDtypeStruct((1, width), F32))
    return pl.pallas_call(
        body, name=name, grid=(width // tc, t // tr), in_specs=in_specs, out_specs=out_specs, out_shape=out_shape,
        compiler_params=_cparams(("parallel", "arbitrary")),
    )(dy, dy, x, x, w)


def _cumsum_rows(x, seq, reverse, name):
    t, w = x.shape
    tb = _pick(seq, 256, SUBLANES)
    nb = seq // tb

    def pos(b, i):
        return (b * nb + (nb - 1 - i if reverse else i), 0)

    def body(x_ref, o_ref, carry):
        i = pl.program_id(1)

        @pl.when(i == 0)
        def _():
            carry[...] = jnp.zeros_like(carry)

        blk = x_ref[...]
        r = lax.broadcasted_iota(jnp.int32, (tb, tb), 0)
        c = lax.broadcasted_iota(jnp.int32, (tb, tb), 1)
        m = ((r <= c) if reverse else (r >= c)).astype(F32)
        o_ref[...] = _dot(m, blk) + carry[...]
        carry[...] += jnp.sum(blk, axis=0, keepdims=True)

    return pl.pallas_call(
        body, name=name, grid=(t // seq, nb), in_specs=[pl.BlockSpec((tb, w), pos)], out_specs=pl.BlockSpec((tb, w), pos),
        out_shape=jax.ShapeDtypeStruct((t, w), F32), scratch_shapes=[pltpu.VMEM((1, w), F32)],
        compiler_params=_cparams(("parallel", "arbitrary")),
    )(x)


def _sgu_fwd(vn, u, w_s, b_col, name):
    t, a = vn.shape
    g = a // HEAD

    def body(v_ref, u_ref, w_ref, b_ref, y_ref):
        tri = _tri(A_CHUNK)
        for gi in range(g):
            sl = slice(gi * HEAD, (gi + 1) * HEAD)
            wc = jnp.where(tri, w_ref[gi], 0.0)
            sv = _dot(wc, v_ref[:, sl]) + b_ref[gi]
            y_ref[:, sl] = (u_ref[:, sl] * sv).astype(y_ref.dtype)

    blk = pl.BlockSpec((A_CHUNK, a), lambda i: (i, 0))
    return pl.pallas_call(
        body, name=name, grid=(t // A_CHUNK,),
        in_specs=[blk, blk, pl.BlockSpec(w_s.shape, lambda i: (0, 0, 0)), pl.BlockSpec(b_col.shape, lambda i: (0, 0, 0))],
        out_specs=blk, out_shape=jax.ShapeDtypeStruct((t, a), BF16), compiler_params=_cparams(("parallel",)),
    )(vn, u, w_s, b_col)


def _sgu_bwd(vn, u, dy, w_s, b_col, name):
    t, a = vn.shape
    g = a // HEAD

    def body(v_ref, u_ref, dy_ref, w_ref, b_ref, dv_ref, du_ref, dw_ref, db_ref):
        i = pl.program_id(0)
        tri = _tri(A_CHUNK)
        for gi in range(g):
            sl = slice(gi * HEAD, (gi + 1) * HEAD)
            wc = jnp.where(tri, w_ref[gi], 0.0)
            v = v_ref[:, sl]
            sv = _dot(wc, v) + b_ref[gi]
            dyb = dy_ref[:, sl]
            du_ref[:, sl] = dyb * sv
            dsv = dyb * u_ref[:, sl]
            dv_ref[:, sl] = _dot(wc, dsv, TN)
            dw = jnp.where(tri, _dot(dsv, v, NT), 0.0)
            db = jnp.sum(dsv, axis=1, keepdims=True)

            @pl.when(i == 0)
            def _(gi=gi, dw=dw, db=db):
                dw_ref[gi] = dw
                db_ref[gi] = db

            @pl.when(i > 0)
            def _(gi=gi, dw=dw, db=db):
                dw_ref[gi] += dw
                db_ref[gi] += db

    blk = pl.BlockSpec((A_CHUNK, a), lambda i: (i, 0))
    wsp = pl.BlockSpec(w_s.shape, lambda i: (0, 0, 0))
    bsp = pl.BlockSpec(b_col.shape, lambda i: (0, 0, 0))
    return pl.pallas_call(
        body, name=name, grid=(t // A_CHUNK,), in_specs=[blk, blk, blk, wsp, bsp], out_specs=[blk, blk, wsp, bsp],
        out_shape=[jax.ShapeDtypeStruct((t, a), F32), jax.ShapeDtypeStruct((t, a), F32),
                   jax.ShapeDtypeStruct(w_s.shape, F32), jax.ShapeDtypeStruct(b_col.shape, F32)],
        compiler_params=_cparams(("arbitrary",)),
    )(vn, u, dy, w_s, b_col)


def _fox_scores(q, k, cq_row, ck_row, diag, scale):
    s = lax.dot_general(q.astype(BF16), k.astype(BF16), NT, preferred_element_type=F32) * scale
    s = s + _row_to_col(cq_row) - ck_row
    mask = jnp.logical_or(jnp.logical_not(diag), _tri(q.shape[0]))
    return s, mask


def _fox_fwd(qn, kn, proj, v_blk0, c_rows, bsz, seq, nh, name):
    t = qn.shape[0]
    tq = _pick(seq, 512, LANES)
    nq = seq // tq
    scale = HEAD ** -0.5

    def body(q_ref, k_ref, v_ref, cq_ref, ck_ref, o_ref, lse_ref, m_s, l_s, acc_s):
        i, j = pl.program_id(2), pl.program_id(3)

        @pl.when(j == 0)
        def _():
            m_s[...] = jnp.full_like(m_s, -jnp.inf)
            l_s[...] = jnp.zeros_like(l_s)
            acc_s[...] = jnp.zeros_like(acc_s)

        @pl.when(j <= i)
        def _():
            s, mask = _fox_scores(q_ref[...], k_ref[...], cq_ref[...], ck_ref[...], j == i, scale)
            s = jnp.where(mask, s, -jnp.inf)
            m_new = jnp.maximum(m_s[...], jnp.max(s, axis=1, keepdims=True))
            p = jnp.exp(s - m_new)
            alpha = jnp.exp(m_s[...] - m_new)
            l_s[...] = alpha * l_s[...] + jnp.sum(p, axis=1, keepdims=True)
            acc_s[...] = alpha * acc_s[...] + lax.dot_general(p.astype(BF16), v_ref[...].astype(BF16), NN, preferred_element_type=F32)
            m_s[...] = m_new

        @pl.when(j == nq - 1)
        def _():
            o_ref[...] = acc_s[...] / l_s[...]
            lse_ref[...] = jnp.broadcast_to(m_s[...] + jnp.log(l_s[...]), lse_ref.shape)

    qspec = pl.BlockSpec((tq, HEAD), lambda b, h, i, j: (b * nq + i, h))
    kspec = pl.BlockSpec((tq, HEAD), lambda b, h, i, j: (b * nq + jnp.minimum(i, j), h))
    vspec = pl.BlockSpec((tq, HEAD), lambda b, h, i, j: (b * nq + jnp.minimum(i, j), v_blk0 + h))
    cq = pl.BlockSpec((None, None, 1, tq), lambda b, h, i, j: (b, h, 0, i))
    ck = pl.BlockSpec((None, None, 1, tq), lambda b, h, i, j: (b, h, 0, jnp.minimum(i, j)))
    return pl.pallas_call(
        body, name=name, grid=(bsz, nh, nq, nq), in_specs=[qspec, kspec, vspec, cq, ck], out_specs=[qspec, qspec],
        out_shape=[jax.ShapeDtypeStruct((t, nh * HEAD), F32)] * 2,
        scratch_shapes=[pltpu.VMEM((tq, 1), F32), pltpu.VMEM((tq, 1), F32), pltpu.VMEM((tq, HEAD), F32)],
        compiler_params=_cparams(("parallel", "parallel", "parallel", "arbitrary")),
    )(qn, kn, proj, c_rows, c_rows)


def _fox_p_dp(q, k, v, do, lse, cq_row, ck_row, diag, scale):
    s, mask = _fox_scores(q, k, cq_row, ck_row, diag, scale)
    p = jnp.where(mask, jnp.exp(s - jnp.max(lse, axis=1, keepdims=True)), 0.0)
    dp = lax.dot_general(do.astype(BF16), v.astype(BF16), NT, preferred_element_type=F32)
    return p, dp


def _fox_bwd_kv(qn, kn, proj, v_blk0, do, lse, delta, c_rows, bsz, seq, nh, name):
    t = qn.shape[0]
    tq = _pick(seq, 512, LANES)
    nq = seq // tq
    scale = HEAD ** -0.5

    def body(q_ref, k_ref, v_ref, do_ref, lse_ref, dl_ref, cq_ref, ck_ref, dk_ref, dv_ref, dc_ref, dk_s, dv_s, dc_s):
        j, i = pl.program_id(2), pl.program_id(3)

        @pl.when(i == 0)
        def _():
            dk_s[...] = jnp.zeros_like(dk_s)
            dv_s[...] = jnp.zeros_like(dv_s)
            dc_s[...] = jnp.zeros_like(dc_s)

        @pl.when(i >= j)
        def _():
            p, dp = _fox_p_dp(q_ref[...], k_ref[...], v_ref[...], do_ref[...], lse_ref[...], cq_ref[...], ck_ref[...], j == i, scale)
            ds = p * (dp - jnp.max(dl_ref[...], axis=1, keepdims=True))
            dv_s[...] += lax.dot_general(p.astype(BF16), do_ref[...].astype(BF16), TN, preferred_element_type=F32)
            dk_s[...] += lax.dot_general(ds.astype(BF16), q_ref[...].astype(BF16), TN, preferred_element_type=F32) * scale
            dc_s[...] -= jnp.sum(ds, axis=0, keepdims=True)

        @pl.when(i == nq - 1)
        def _():
            dk_ref[...] = dk_s[...]
            dv_ref[...] = dv_s[...]
            dc_ref[...] = dc_s[...]

    kspec = pl.BlockSpec((tq, HEAD), lambda b, h, j, i: (b * nq + j, h))
    vspec = pl.BlockSpec((tq, HEAD), lambda b, h, j, i: (b * nq + j, v_blk0 + h))
    qspec = pl.BlockSpec((tq, HEAD), lambda b, h, j, i: (b * nq + jnp.maximum(i, j), h))
    cq = pl.BlockSpec((None, None, 1, tq), lambda b, h, j, i: (b, h, 0, jnp.maximum(i, j)))
    ck = pl.BlockSpec((None, None, 1, tq), lambda b, h, j, i: (b, h, 0, j))
    return pl.pallas_call(
        body, name=name, grid=(bsz, nh, nq, nq), in_specs=[qspec, kspec, vspec, qspec, qspec, qspec, cq, ck],
        out_specs=[kspec, kspec, ck],
        out_shape=[jax.ShapeDtypeStruct((t, nh * HEAD), F32)] * 2 + [jax.ShapeDtypeStruct(c_rows.shape, F32)],
        scratch_shapes=[pltpu.VMEM((tq, HEAD), F32), pltpu.VMEM((tq, HEAD), F32), pltpu.VMEM((1, tq), F32)],
        compiler_params=_cparams(("parallel", "parallel", "parallel", "arbitrary")),
    )(qn, kn, proj, do, lse, delta, c_rows, c_rows)


def _fox_bwd_q(qn, kn, proj, v_blk0, do, lse, c_rows, bsz, seq, nh, name):
    t = qn.shape[0]
    tq = _pick(seq, 512, LANES)
    nq = seq // tq
    scale = HEAD ** -0.5

    def key_block(jj):
        return jnp.where(jj >= nq, jj - nq, jj)

    def body(q_ref, k_ref, v_ref, do_ref, lse_ref, cq_ref, ck_ref, dq_ref, dl_ref, dq_s, dl_s):
        i, jj = pl.program_id(2), pl.program_id(3)
        j = key_block(jj)

        @pl.when(jj == 0)
        def _():
            dq_s[...] = jnp.zeros_like(dq_s)
            dl_s[...] = jnp.zeros_like(dl_s)

        @pl.when(j <= i)
        def _():
            p, dp = _fox_p_dp(q_ref[...], k_ref[...], v_ref[...], do_ref[...], lse_ref[...], cq_ref[...], ck_ref[...], j == i, scale)

            @pl.when(jj < nq)
            def _():
                dl_s[...] += jnp.sum(p * dp, axis=1, keepdims=True)

            @pl.when(jj >= nq)
            def _():
                ds = p * (dp - dl_s[...])
                dq_s[...] += lax.dot_general(ds.astype(BF16), k_ref[...].astype(BF16), NN, preferred_element_type=F32) * scale

        @pl.when(jj == 2 * nq - 1)
        def _():
            dq_ref[...] = dq_s[...]
            dl_ref[...] = jnp.broadcast_to(dl_s[...], dl_ref.shape)

    qspec = pl.BlockSpec((tq, HEAD), lambda b, h, i, jj: (b * nq + i, h))
    kspec = pl.BlockSpec((tq, HEAD), lambda b, h, i, jj: (b * nq + jnp.minimum(i, key_block(jj)), h))
    vspec = pl.BlockSpec((tq, HEAD), lambda b, h, i, jj: (b * nq + jnp.minimum(i, key_block(jj)), v_blk0 + h))
    cq = pl.BlockSpec((None, None, 1, tq), lambda b, h, i, jj: (b, h, 0, i))
    ck = pl.BlockSpec((None, None, 1, tq), lambda b, h, i, jj: (b, h, 0, jnp.minimum(i, key_block(jj))))
    return pl.pallas_call(
        body, name=name, grid=(bsz, nh, nq, 2 * nq), in_specs=[qspec, kspec, vspec, qspec, qspec, cq, ck],
        out_specs=[qspec, qspec], out_shape=[jax.ShapeDtypeStruct((t, nh * HEAD), F32)] * 2,
        scratch_shapes=[pltpu.VMEM((tq, HEAD), F32), pltpu.VMEM((tq, 1), F32)],
        compiler_params=_cparams(("parallel", "parallel", "parallel", "arbitrary")),
    )(qn, kn, proj, do, lse, c_rows, c_rows)


def _gdn_chunk(qp, kp, vp, b_row, a_row, alog, dtb, state):
    c = qp.shape[0]
    qc, kc, vc = _silu(qp), _silu(kp), _silu(vp)
    q = qc * lax.rsqrt(jnp.sum(qc * qc, -1, keepdims=True) + RMS_EPS) * (HEAD ** -0.5)
    k = kc * lax.rsqrt(jnp.sum(kc * kc, -1, keepdims=True) + RMS_EPS)
    beta_row = jax.nn.sigmoid(b_row)
    g_row = -jnp.exp(alog) * _softplus(a_row + dtb)
    ri = lax.broadcasted_iota(jnp.int32, (c, c), 0)
    ci = lax.broadcasted_iota(jnp.int32, (c, c), 1)
    eye = (ri == ci).astype(F32)
    tri = ri >= ci
    beta_col = jnp.sum(eye * beta_row, axis=1, keepdims=True)
    g_col = jnp.sum(eye * g_row, axis=1, keepdims=True)
    gc_col = jnp.sum(tri.astype(F32) * g_row, axis=1, keepdims=True)
    gc_row = jnp.sum(g_col * (ri <= ci).astype(F32), axis=0, keepdims=True)
    decay = jnp.where(tri, jnp.exp(jnp.where(tri, gc_col - gc_row, 0.0)), 0.0)
    kb = k * beta_col
    a_mat = jnp.where(ri > ci, _dot(kb, k, NT) * decay, 0.0)
    egc = jnp.exp(gc_col)
    inv = eye - a_mat
    pw = _dot(a_mat, a_mat)
    n_sq = max(1, (c - 1).bit_length() - 1)
    for it in range(n_sq):
        inv = inv + _dot(inv, pw)
        if it < n_sq - 1:
            pw = _dot(pw, pw)
    u = _dot(inv, vc * beta_col)
    w = _dot(inv, kb * egc)
    attn = _dot(q, k, NT) * decay
    g_last = jnp.sum(g_row, axis=1, keepdims=True)
    v_new = u - _dot(w, state)
    o = _dot(q * egc, state) + _dot(attn, v_new)
    new_state = state * jnp.exp(g_last) + _dot(k * jnp.exp(g_last - gc_col), v_new, TN)
    return o, new_state


def _gdn_specs(nch, nkh, rev):
    def n_of(n):
        return nch - 1 - n if rev else n

    q = pl.BlockSpec((GDN_CHUNK, HEAD), lambda h, b, n: (b * nch + n_of(n), h // 2))
    k = pl.BlockSpec((GDN_CHUNK, HEAD), lambda h, b, n: (b * nch + n_of(n), nkh + h // 2))
    v = pl.BlockSpec((GDN_CHUNK, HEAD), lambda h, b, n: (b * nch + n_of(n), 2 * nkh + h))
    per_v = pl.BlockSpec((GDN_CHUNK, HEAD), lambda h, b, n: (b * nch + n_of(n), h))
    row = pl.BlockSpec((None, None, None, 1, GDN_CHUNK), lambda h, b, n: (h, b, n_of(n), 0, 0))
    sc = pl.BlockSpec((None, 1, 1), lambda h, b, n: (h, 0, 0))
    st = pl.BlockSpec((None, None, None, HEAD, HEAD), lambda h, b, n: (h, b, n_of(n), 0, 0))
    return q, k, v, per_v, row, sc, st


def _gdn_fwd(conv, b_rows, a_rows, alog, dtb, bsz, seq, nvh, name):
    t = conv.shape[0]
    nch = seq // GDN_CHUNK
    nkh = nvh // 2
    q, k, v, per_v, row, sc, st = _gdn_specs(nch, nkh, False)

    def body(q_ref, k_ref, v_ref, b_ref, a_ref, al_ref, dt_ref, o_ref, st_ref, state):
        @pl.when(pl.program_id(2) == 0)
        def _():
            state[...] = jnp.zeros_like(state)

        st_ref[...] = state[...]
        o, new_state = _gdn_chunk(q_ref[...], k_ref[...], v_ref[...], b_ref[...], a_ref[...], al_ref[...], dt_ref[...], state[...])
        o_ref[...] = o
        state[...] = new_state

    return pl.pallas_call(
        body, name=name, grid=(nvh, bsz, nch), in_specs=[q, k, v, row, row, sc, sc], out_specs=[per_v, st],
        out_shape=[jax.ShapeDtypeStruct((t, nvh * HEAD), F32), jax.ShapeDtypeStruct((nvh, bsz, nch, HEAD, HEAD), F32)],
        scratch_shapes=[pltpu.VMEM((HEAD, HEAD), F32)],
        compiler_params=_cparams(("parallel", "parallel", "arbitrary")),
    )(conv, conv, conv, b_rows, a_rows, alog, dtb)


def _gdn_bwd(conv, b_rows, a_rows, alog, dtb, states, do, bsz, seq, nvh, name):
    t = conv.shape[0]
    nch = seq // GDN_CHUNK
    nkh = nvh // 2
    q, k, v, per_v, row, sc, st = _gdn_specs(nch, nkh, True)

    def body(q_ref, k_ref, v_ref, b_ref, a_ref, al_ref, dt_ref, st_ref, do_ref,
             dq_ref, dk_ref, dv_ref, db_ref, da_ref, dal_ref, ddt_ref, dstate):
        b, n = pl.program_id(1), pl.program_id(2)

        @pl.when(n == 0)
        def _():
            dstate[...] = jnp.zeros_like(dstate)

        _, vjp = jax.vjp(_gdn_chunk, q_ref[...], k_ref[...], v_ref[...], b_ref[...], a_ref[...], al_ref[...], dt_ref[...], st_ref[...])
        dq, dk, dv, db, da, dal, ddt, dst = vjp((do_ref[...], dstate[...]))
        dq_ref[...] = dq
        dk_ref[...] = dk
        dv_ref[...] = dv
        db_ref[...] = db
        da_ref[...] = da
        dstate[...] = dst
        start = jnp.logical_and(b == 0, n == 0)

        @pl.when(start)
        def _():
            dal_ref[...] = dal
            ddt_ref[...] = ddt

        @pl.when(jnp.logical_not(start))
        def _():
            dal_ref[...] += dal
            ddt_ref[...] += ddt

    f = lambda *s: jax.ShapeDtypeStruct(s, F32)
    return pl.pallas_call(
        body, name=name, grid=(nvh, bsz, nch), in_specs=[q, k, v, row, row, sc, sc, st, per_v],
        out_specs=[per_v, per_v, per_v, row, row, sc, sc],
        out_shape=[f(t, nvh * HEAD), f(t, nvh * HEAD), f(t, nvh * HEAD), f(*b_rows.shape), f(*a_rows.shape), f(nvh, 1, 1), f(nvh, 1, 1)],
        scratch_shapes=[pltpu.VMEM((HEAD, HEAD), F32)],
        compiler_params=_cparams(("arbitrary", "arbitrary", "arbitrary")),
    )(conv, conv, conv, b_rows, a_rows, alog, dtb, states, do)


ANY = pl.BlockSpec(memory_space=pl.ANY)


def _place():
    x, y, c = lax.axis_index("x"), lax.axis_index("y"), lax.axis_index("c")
    chips = [(1 - x, y), (x, 1 - y), (1 - x, 1 - y)]
    return x, y, c, chips


def _gather_shards(mine, name):
    r, w = mine.shape
    rh = r // 2

    def body(mine_ref, out_ref, send_sems, recv_sems, local_sem):
        x, y, c, chips = _place()
        me = 2 * x + y
        half = pl.ds(pl.multiple_of(c * rh, 16), rh)
        other = pl.ds(pl.multiple_of((1 - c) * rh, 16), rh)

        def copy(k, src, chip, rows, to):
            return pltpu.make_async_remote_copy(src_ref=src, dst_ref=out_ref.at[chip, rows], send_sem=send_sems.at[k],
                                                recv_sem=recv_sems.at[k], device_id=to, device_id_type=MESH)

        local = pltpu.make_async_copy(mine_ref, out_ref.at[me], local_sem)
        local.start()
        sends = [copy(k, mine_ref.at[half], me, half, (cx, cy, c)) for k, (cx, cy) in enumerate(chips)]
        for s in sends:
            s.start()
        passed = []
        for k, (cx, cy) in enumerate(chips):
            chip = 2 * cx + cy
            copy(k, mine_ref.at[half], chip, half, (x, y, c)).wait_recv()
            fwd = copy(3 + k, out_ref.at[chip, half], chip, half, (x, y, 1 - c))
            fwd.start()
            passed.append(fwd)
        for k, (cx, cy) in enumerate(chips):
            copy(3 + k, mine_ref.at[half], 2 * cx + cy, other, (x, y, c)).wait_recv()
        for s in sends + passed:
            s.wait_send()
        local.wait()

    return pl.pallas_call(
        body, name=name, in_specs=[ANY], out_specs=ANY, out_shape=jax.ShapeDtypeStruct((N_CHIPS, r, w), mine.dtype),
        scratch_shapes=[pltpu.SemaphoreType.DMA((6,)), pltpu.SemaphoreType.DMA((6,)), pltpu.SemaphoreType.DMA],
    )(mine)


def _pair_exchange(full, name):
    _, r, w = full.shape
    rh = r // 2

    def body(full_ref, got_ref, send_sem, recv_sem):
        x, y, c, _ = _place()
        other = pl.ds(pl.multiple_of((1 - c) * rh, 8), rh)
        cp = pltpu.make_async_remote_copy(src_ref=full_ref.at[:, other, :], dst_ref=got_ref, send_sem=send_sem, recv_sem=recv_sem,
                                          device_id=(x, y, 1 - c), device_id_type=MESH)
        cp.start()
        cp.wait()

    return pl.pallas_call(
        body, name=name, in_specs=[ANY], out_specs=ANY, out_shape=jax.ShapeDtypeStruct((N_CHIPS, rh, w), full.dtype),
        scratch_shapes=[pltpu.SemaphoreType.DMA, pltpu.SemaphoreType.DMA],
    )(full)


def _pair_add(full, got, name):
    _, r, w = full.shape
    rh = r // 2
    tr = _pick(rh, 256, SUBLANES)
    nb = rh // tr
    c_arr = lax.axis_index("c").astype(jnp.int32).reshape(1)

    def body(c_ref, a_ref, b_ref, o_ref):
        o_ref[...] = a_ref[...] + b_ref[...]

    gs = pltpu.PrefetchScalarGridSpec(
        num_scalar_prefetch=1, grid=(N_CHIPS, nb),
        in_specs=[pl.BlockSpec((None, tr, w), lambda s, i, c_ref: (s, c_ref[0] * nb + i, 0)),
                  pl.BlockSpec((None, tr, w), lambda s, i, c_ref: (s, i, 0))],
        out_specs=pl.BlockSpec((None, tr, w), lambda s, i, c_ref: (s, i, 0)))
    return pl.pallas_call(body, name=name, grid_spec=gs, out_shape=jax.ShapeDtypeStruct((N_CHIPS, rh, w), F32),
                          compiler_params=_cparams(("parallel", "parallel")))(c_arr, full, got)


def _chip_exchange(q, name):
    def body(q_ref, out_ref, send_sems, recv_sems, local_sem):
        x, y, c, chips = _place()
        me = 2 * x + y
        local = pltpu.make_async_copy(q_ref.at[me], out_ref.at[me], local_sem)
        local.start()
        sends = []
        for k, (cx, cy) in enumerate(chips):
            s = pltpu.make_async_remote_copy(src_ref=q_ref.at[2 * cx + cy], dst_ref=out_ref.at[me], send_sem=send_sems.at[k],
                                             recv_sem=recv_sems.at[k], device_id=(cx, cy, c), device_id_type=MESH)
            s.start()
            sends.append(s)
        for k, (cx, cy) in enumerate(chips):
            pltpu.make_async_remote_copy(src_ref=q_ref.at[me], dst_ref=out_ref.at[2 * cx + cy], send_sem=send_sems.at[k],
                                         recv_sem=recv_sems.at[k], device_id=(x, y, c), device_id_type=MESH).wait_recv()
        for s in sends:
            s.wait_send()
        local.wait()

    return pl.pallas_call(
        body, name=name, in_specs=[ANY], out_specs=ANY, out_shape=jax.ShapeDtypeStruct(q.shape, q.dtype),
        scratch_shapes=[pltpu.SemaphoreType.DMA((3,)), pltpu.SemaphoreType.DMA((3,)), pltpu.SemaphoreType.DMA],
    )(q)


def _sum_slots(slots, name):
    n, r, w = slots.shape
    tr = _pick(r, 256, SUBLANES)

    def body(s_ref, o_ref):
        acc = s_ref[0]
        for k in range(1, n):
            acc = acc + s_ref[k]
        o_ref[...] = acc

    return pl.pallas_call(
        body, name=name, grid=(r // tr,), in_specs=[pl.BlockSpec((n, tr, w), lambda i: (0, i, 0))],
        out_specs=pl.BlockSpec((tr, w), lambda i: (i, 0)), out_shape=jax.ShapeDtypeStruct((r, w), F32),
        compiler_params=_cparams(("parallel",)),
    )(slots)


def _pair_join(halfsum, name):
    rh, w = halfsum.shape

    def body(h_ref, out_ref, send_sem, recv_sem, local_sem):
        x, y, c, _ = _place()
        mine = pl.ds(pl.multiple_of(c * rh, 8), rh)
        other = pl.ds(pl.multiple_of((1 - c) * rh, 8), rh)
        local = pltpu.make_async_copy(h_ref, out_ref.at[mine], local_sem)
        local.start()
        cp = pltpu.make_async_remote_copy(src_ref=h_ref, dst_ref=out_ref.at[mine], send_sem=send_sem, recv_sem=recv_sem,
                                          device_id=(x, y, 1 - c), device_id_type=MESH)
        cp.start()
        pltpu.make_async_remote_copy(src_ref=h_ref, dst_ref=out_ref.at[other], send_sem=send_sem, recv_sem=recv_sem,
                                     device_id=(x, y, c), device_id_type=MESH).wait_recv()
        cp.wait_send()
        local.wait()

    return pl.pallas_call(
        body, name=name, in_specs=[ANY], out_specs=ANY, out_shape=jax.ShapeDtypeStruct((2 * rh, w), halfsum.dtype),
        scratch_shapes=[pltpu.SemaphoreType.DMA, pltpu.SemaphoreType.DMA, pltpu.SemaphoreType.DMA],
    )(halfsum)


def _reduce_scatter(full, tag):
    got = _pair_exchange(full, f"rs_pair_{tag}")
    q = _pair_add(full, got, f"rs_add_{tag}")
    parts = _chip_exchange(q, f"rs_chip_{tag}")
    halfsum = _sum_slots(parts, f"rs_sum_{tag}")
    return _pair_join(halfsum, f"rs_join_{tag}")


def _all_reduce_small(flat, name):
    r, w = flat.shape

    def body(f_ref, out_ref, send_sems, recv_sems, local_sem):
        x, y, c, _ = _place()
        me = 4 * x + 2 * y + c
        local = pltpu.make_async_copy(f_ref, out_ref.at[me], local_sem)
        local.start()
        sends = []
        for k in range(1, 8):
            peer = (x ^ (k >> 2), y ^ ((k >> 1) & 1), c ^ (k & 1))
            s = pltpu.make_async_remote_copy(src_ref=f_ref, dst_ref=out_ref.at[me], send_sem=send_sems.at[k - 1],
                                             recv_sem=recv_sems.at[k - 1], device_id=peer, device_id_type=MESH)
            s.start()
            sends.append(s)
        for k in range(1, 8):
            peer_slot = 4 * (x ^ (k >> 2)) + 2 * (y ^ ((k >> 1) & 1)) + (c ^ (k & 1))
            pltpu.make_async_remote_copy(src_ref=f_ref, dst_ref=out_ref.at[peer_slot], send_sem=send_sems.at[k - 1],
                                         recv_sem=recv_sems.at[k - 1], device_id=(x, y, c), device_id_type=MESH).wait_recv()
        for s in sends:
            s.wait_send()
        local.wait()

    slots = pl.pallas_call(
        body, name=name, in_specs=[ANY], out_specs=ANY, out_shape=jax.ShapeDtypeStruct((8, r, w), flat.dtype),
        scratch_shapes=[pltpu.SemaphoreType.DMA((7,)), pltpu.SemaphoreType.DMA((7,)), pltpu.SemaphoreType.DMA],
    )(flat)
    return _sum_slots(slots, name + "_sum")


def _pack(pieces, dtype, lead):
    flat = []
    n_lead = len(lead)
    for p in pieces:
        f = p.astype(dtype).reshape(*lead, -1)
        pad = (-f.shape[-1]) % PACK_COLS
        if pad:
            f = jnp.pad(f, [(0, 0)] * n_lead + [(0, pad)])
        flat.append(f)
    f = jnp.concatenate(flat, axis=-1) if len(flat) > 1 else flat[0]
    pad = (-f.shape[-1]) % (32 * PACK_COLS)
    if pad:
        f = jnp.pad(f, [(0, 0)] * n_lead + [(0, pad)])
    return f.reshape(*lead, -1, PACK_COLS)


def _unpack(buf, shapes, lead):
    f = buf.reshape(*lead, -1)
    out, off = [], 0
    for shp in shapes:
        n = 1
        for s in shp:
            n *= s
        out.append(f[..., off:off + n].reshape(*lead, *shp))
        off += n + ((-n) % PACK_COLS)
    return out


def _join(g, axis):
    g = jnp.moveaxis(g, 0, axis)
    return g.reshape(*g.shape[:axis], g.shape[axis] * g.shape[axis + 1], *g.shape[axis + 2:])


def _split(full, axis):
    shp = full.shape
    g = full.reshape(*shp[:axis], N_CHIPS, shp[axis] // N_CHIPS, *shp[axis + 1:])
    return jnp.moveaxis(g, axis, 0)


def _norm_bwd(xs, dres, dhs, gain, name):
    def fn(x, dr, *rest):
        dh = rest[0]
        for d in rest[1:-1]:
            dh = dh + d
        _, vjp = jax.vjp(_rms, x, rest[-1])
        dx, dg = vjp(dh.astype(F32))
        return dr + dx, dg
    return _rowmap(fn, [xs, dres] + list(dhs), [gain], name, n_acc=1)


def _ffn_fwd(x1, h2, wg, wu, conv_w, conv_b, wd, seq, tag):
    gpre = _matmul(h2, wg, "nn", F32, f"ffn_gate_{tag}")
    up = _matmul(h2, wu, "nn", F32, f"ffn_up_{tag}")
    gate = _conv_fwd(gpre, gpre.shape[1], conv_w, conv_b, seq, f"ffn_conv_{tag}")
    act, = _rowmap(lambda g, u: (_silu(g) * u).astype(BF16), [gate, up], [], f"ffn_act_{tag}", tc=_pick(gate.shape[1], 1024, LANES))
    f = _matmul(act, wd, "nn", F32, f"ffn_down_{tag}")
    return f, (gpre, up, gate, act)


def _ffn_bwd(dx2, h2, saved, wg, wu, conv_w, wd, seq, tag):
    gpre, up, gate, act = saved
    da = _matmul(dx2, wd, "nt", F32, f"ffn_dact_{tag}")
    d_wd = _matmul(act, dx2, "tn", F32, f"ffn_dwd_{tag}")

    def act_bwd(g, u, d):
        _, vjp = jax.vjp(lambda g_, u_: _silu(g_) * u_, g, u)
        return vjp(d)
    dgate, dup = _rowmap(act_bwd, [gate, up, da], [], f"ffn_dactfn_{tag}", tc=_pick(gate.shape[1], 1024, LANES))
    dgpre, d_cw, d_cb = _conv_bwd(dgate, gpre, gpre.shape[1], conv_w, seq, True, f"ffn_dconv_{tag}")
    dh_a = _matmul(dgpre, wg, "nt", F32, f"ffn_dh_gate_{tag}")
    dh_b = _matmul(dup, wu, "nt", F32, f"ffn_dh_up_{tag}")
    d_wg = _matmul(h2, dgpre, "tn", F32, f"ffn_dwg_{tag}")
    d_wu = _matmul(h2, dup, "tn", F32, f"ffn_dwu_{tag}")
    return (dh_a, dh_b), dict(ffn_w_gate=d_wg, ffn_w_up=d_wu, ffn_conv_w=d_cw, ffn_conv_b=d_cb.reshape(-1), ffn_w_down=d_wd)


def _gmlp_fwd(h, w, tag):
    a = w["a_w_out"].shape[0]
    p = _matmul(h, w["a_w_in"], "nn", F32, f"a_in_{tag}")
    b_in, vnorm = w["a_b_in"].reshape(1, -1), w["a_v_norm"].reshape(1, -1)

    def fn(p_, b_, g_):
        hh = _gelu_tanh(p_ + b_)
        return hh[:, :a], _rms(hh[:, a:], g_)
    u, vn = _rowmap(fn, [p], [b_in, vnorm], f"a_gelu_{tag}")
    b_col = w["a_b_s"][:, :, None]
    y = _sgu_fwd(vn, u, w["a_w_s"], b_col, f"a_sgu_{tag}")
    m = _matmul(y, w["a_w_out"], "nn", F32, f"a_out_{tag}")
    return m, (p, u, vn, y, fn, b_in, vnorm, b_col)


def _gmlp_bwd(dm, h, saved, w, tag):
    p, u, vn, y, fn, b_in, vnorm, b_col = saved
    dy = _matmul(dm, w["a_w_out"], "nt", F32, f"a_dy_{tag}")
    d_wout = _matmul(y, dm, "tn", F32, f"a_dwout_{tag}")
    dvn, du, d_ws, d_bcol = _sgu_bwd(vn, u, dy, w["a_w_s"], b_col, f"a_dsgu_{tag}")

    def bwd(p_, du_, dvn_, b_, g_):
        _, vjp = jax.vjp(fn, p_, b_, g_)
        return vjp((du_, dvn_))
    dp, d_bin, d_vnorm = _rowmap(bwd, [p, du, dvn], [b_in, vnorm], f"a_dgelu_{tag}", n_acc=2)
    dh = _matmul(dp, w["a_w_in"], "nt", F32, f"a_dh_{tag}")
    d_win = _matmul(h, dp, "tn", F32, f"a_dwin_{tag}")
    return (dh,), dict(a_w_in=d_win, a_b_in=d_bin.reshape(-1), a_v_norm=d_vnorm.reshape(-1), a_w_s=d_ws, a_b_s=d_bcol[:, :, 0], a_w_out=d_wout)


def _fox_fwd_mixer(h, w, bsz, seq, tag):
    d = h.shape[1]
    nh = d // HEAD
    win = w["b_w_in"]
    wp = win.shape[1]
    proj = _matmul(h, win, "nn", F32, f"b_in_{tag}")
    gq, gk = w["b_q_norm"].reshape(1, HEAD), w["b_k_norm"].reshape(1, HEAD)
    bf = jnp.pad(w["b_b_f"].reshape(1, nh), ((0, 0), (0, LANES - nh)))

    def prep(pq, pk, pfl, gq_, gk_, bf_):
        qn = jnp.concatenate([_rms(x, gq_) for x in _heads(pq)], axis=1)
        kn = jnp.concatenate([_rms(x, gk_) for x in _heads(pk)], axis=1)
        return qn, kn, -_softplus(-(pfl + bf_))
    wins = [_win(proj, d, 0), _win(proj, d, 1), _win(proj, LANES, 4 * d // LANES)]

    def prep_fwd(pq, pk, pfl, gq_, gk_, bf_):
        qn, kn, lf = prep(pq, pk, pfl, gq_, gk_, bf_)
        return qn.astype(BF16), kn.astype(BF16), lf
    qn, kn, lf = _rowmap(prep_fwd, wins, [gq, gk, bf], f"b_prep_{tag}")
    cs = _cumsum_rows(lf, seq, False, f"b_cumsum_{tag}")
    c_rows = cs[:, :nh].reshape(bsz, seq, nh).transpose(0, 2, 1)[:, :, None, :]
    o, lse = _fox_fwd(qn, kn, proj, 2 * nh, c_rows, bsz, seq, nh, f"b_attn_{tag}")
    og = _win(proj, d, 3)
    y, = _rowmap(lambda o_, g_: (o_ * jax.nn.sigmoid(g_)).astype(BF16), [o, og], [], f"b_gate_{tag}")
    m = _matmul(y, w["b_w_out"], "nn", F32, f"b_out_{tag}")
    return m, (proj, qn, kn, c_rows, o, lse, y, prep, wins, (gq, gk, bf), wp)


def _fox_bwd_mixer(dm, h, saved, w, bsz, seq, tag):
    proj, qn, kn, c_rows, o, lse, y, prep, wins, (gq, gk, bf), wp = saved
    d = h.shape[1]
    nh = d // HEAD
    dy = _matmul(dm, w["b_w_out"], "nt", F32, f"b_dy_{tag}")
    d_wout = _matmul(y, dm, "tn", F32, f"b_dwout_{tag}")
    og = _win(proj, d, 3)

    def gate_bwd(o_, g_, dy_):
        _, vjp = jax.vjp(lambda a, b: a * jax.nn.sigmoid(b), o_, g_)
        return vjp(dy_)
    do, dog = _rowmap(gate_bwd, [o, og, dy], [], f"b_dgate_{tag}")
    dqn, delta = _fox_bwd_q(qn, kn, proj, 2 * nh, do, lse, c_rows, bsz, seq, nh, f"b_dattn_q_{tag}")
    dkn, dv, dc_rows = _fox_bwd_kv(qn, kn, proj, 2 * nh, do, lse, delta, c_rows, bsz, seq, nh, f"b_dattn_kv_{tag}")
    dc = dc_rows[:, :, 0, :].transpose(0, 2, 1).reshape(bsz * seq, nh)
    dc = jnp.pad(dc, ((0, 0), (0, LANES - nh)))
    dlf = _cumsum_rows(dc, seq, True, f"b_dcumsum_{tag}")
    extra = wp - (4 * d + LANES)

    def prep_bwd(pq, pk, pfl, dqn_, dkn_, dv_, dog_, dlf_, gq_, gk_, bf_):
        _, vjp = jax.vjp(prep, pq, pk, pfl, gq_, gk_, bf_)
        dpq, dpk, dpfl, dgq, dgk, dbf = vjp((dqn_, dkn_, dlf_))
        parts = [dpq, dpk, dv_, dog_, dpfl]
        if extra:
            parts.append(jnp.zeros((pq.shape[0], extra), F32))
        return jnp.concatenate(parts, axis=1), dgq, dgk, dbf
    dproj, d_gq, d_gk, d_bf = _rowmap(prep_bwd, wins + [dqn, dkn, dv, dog, dlf], [gq, gk, bf], f"b_dprep_{tag}", n_acc=3)
    dh = _matmul(dproj, w["b_w_in"], "nt", F32, f"b_dh_{tag}")
    d_win = _matmul(h, dproj, "tn", F32, f"b_dwin_{tag}")
    return (dh,), dict(b_w_in=d_win[:, :4 * d + nh], b_b_f=d_bf[0, :nh], b_q_norm=d_gq.reshape(-1), b_k_norm=d_gk.reshape(-1), b_w_out=d_wout)


def _gdn_fwd_mixer(h, w, bsz, seq, tag):
    d = h.shape[1]
    nkh = d // HEAD
    nvh = 2 * nkh
    dqkv = (2 * nkh + nvh) * HEAD
    dz = nvh * HEAD
    nch = seq // GDN_CHUNK
    proj = _matmul(h, w["c_w_in"], "nn", F32, f"c_in_{tag}")
    conv = _conv_fwd(proj, dqkv, w["c_conv_w"], None, seq, f"c_conv_{tag}")

    def rows_of(cols):
        return cols.reshape(bsz, nch, GDN_CHUNK, nvh).transpose(3, 0, 1, 2)[:, :, :, None, :]
    b_rows = rows_of(proj[:, dqkv + dz:dqkv + dz + nvh])
    a_rows = rows_of(proj[:, dqkv + dz + nvh:dqkv + dz + 2 * nvh])
    alog, dtb = w["c_a_log"].reshape(nvh, 1, 1), w["c_dt_bias"].reshape(nvh, 1, 1)
    o, states = _gdn_fwd(conv, b_rows, a_rows, alog, dtb, bsz, seq, nvh, f"c_core_{tag}")
    gn = w["c_out_norm"].reshape(1, HEAD)
    zwin = _win(proj, dz, dqkv // dz)

    def outfn(o_, z_, g_):
        return jnp.concatenate([_rms(a, g_) * _silu(b) for a, b in zip(_heads(o_), _heads(z_))], axis=1)
    y, = _rowmap(lambda o_, z_, g_: outfn(o_, z_, g_).astype(BF16), [o, zwin], [gn], f"c_outnorm_{tag}")
    m = _matmul(y, w["c_w_out"], "nn", F32, f"c_out_{tag}")
    return m, (proj, conv, b_rows, a_rows, alog, dtb, o, states, y, gn, zwin, outfn)


def _gdn_bwd_mixer(dm, h, saved, w, bsz, seq, tag):
    proj, conv, b_rows, a_rows, alog, dtb, o, states, y, gn, zwin, outfn = saved
    d = h.shape[1]
    nkh = d // HEAD
    nvh = 2 * nkh
    dqkv = (2 * nkh + nvh) * HEAD
    dz = nvh * HEAD
    wp = proj.shape[1]
    dy = _matmul(dm, w["c_w_out"], "nt", F32, f"c_dy_{tag}")
    d_wout = _matmul(y, dm, "tn", F32, f"c_dwout_{tag}")

    def out_bwd(o_, z_, dy_, g_):
        _, vjp = jax.vjp(outfn, o_, z_, g_)
        return vjp(dy_)
    do, dzz, d_gn = _rowmap(out_bwd, [o, zwin, dy], [gn], f"c_doutnorm_{tag}", n_acc=1)
    dq_v, dk_v, dv, db_rows, da_rows, d_alog, d_dtb = _gdn_bwd(conv, b_rows, a_rows, alog, dtb, states, do, bsz, seq, nvh, f"c_dcore_{tag}")

    def pair_sum(dq_, dk_, dv_):
        hq, hk = _heads(dq_), _heads(dk_)
        return jnp.concatenate([hq[2 * i] + hq[2 * i + 1] for i in range(nkh)] + [hk[2 * i] + hk[2 * i + 1] for i in range(nkh)] + [dv_], axis=1)
    dconv, = _rowmap(pair_sum, [dq_v, dk_v, dv], [], f"c_pairsum_{tag}")
    dqkv_pre, d_cw = _conv_bwd(dconv, proj, dqkv, w["c_conv_w"], seq, False, f"c_dconv_{tag}")

    def cols_of(rows):
        return rows[:, :, :, 0, :].transpose(1, 2, 3, 0).reshape(bsz * seq, nvh)
    dba = jnp.concatenate([cols_of(db_rows), cols_of(da_rows)], axis=1)
    dba = jnp.pad(dba, ((0, 0), (0, wp - dqkv - dz - 2 * nvh)))
    dproj, = _rowmap(lambda a, b, c: jnp.concatenate([a, b, c], axis=1), [dqkv_pre, dzz, dba], [], f"c_dproj_{tag}")
    dh = _matmul(dproj, w["c_w_in"], "nt", F32, f"c_dh_{tag}")
    d_win = _matmul(h, dproj, "tn", F32, f"c_dwin_{tag}")
    return (dh,), dict(c_w_in=d_win[:, :dqkv + dz + 2 * nvh], c_conv_w=d_cw, c_a_log=d_alog.reshape(-1), c_dt_bias=d_dtb.reshape(-1),
                       c_out_norm=d_gn.reshape(-1), c_w_out=d_wout)


_MIXER_FWD = (lambda h, w, bsz, seq, tag: _gmlp_fwd(h, w, tag), _fox_fwd_mixer, _gdn_fwd_mixer)
_MIXER_BWD = (lambda dm, h, s, w, bsz, seq, tag: _gmlp_bwd(dm, h, s, w, tag), _fox_bwd_mixer, _gdn_bwd_mixer)


def _local_step(x, target, layers, bsz, seq):
    t, d = x.shape
    depth = len(layers)
    saved = []
    m_prev = None
    xin = x
    for i, w in enumerate(layers):
        tag = f"l{i}"
        g_mix, g_ffn = w["norm_mix"].reshape(1, d), w["norm_ffn"].reshape(1, d)
        if i == 0:
            h, = _rowmap(lambda x_, g_: _rms(x_, g_).astype(BF16), [xin], [g_mix], f"norm_mix_{tag}")
            xl = xin
        else:
            xl, h = _rowmap(lambda x_, m_, g_: (x_ + m_, _rms(x_ + m_, g_).astype(BF16)), [xin, m_prev], [g_mix], f"norm_mix_{tag}")
        m, msaved = _MIXER_FWD[i % 3](h, w, bsz, seq, tag)
        x1, h2 = _rowmap(lambda x_, m_, g_: (x_ + m_, _rms(x_ + m_, g_).astype(BF16)), [xl, m], [g_ffn], f"norm_ffn_{tag}")
        f, fsaved = _ffn_fwd(x1, h2, w["ffn_w_gate"], w["ffn_w_up"], w["ffn_conv_w"], w["ffn_conv_b"].reshape(1, -1), w["ffn_w_down"], seq, tag)
        saved.append((xl, h, msaved, x1, h2, fsaved))
        xin, m_prev = x1, f

    def loss_fn(x_, f_, tg_):
        e = x_ + f_ - tg_
        return e * (1.0 / d), jnp.full((1, LANES), (0.5 / d) * jnp.sum(e * e), F32)
    dx, loss_acc = _rowmap(loss_fn, [xin, m_prev, target], [], "loss", n_acc=1)
    loss = loss_acc[0, 0]

    grads = [None] * depth
    for i in reversed(range(depth)):
        w = layers[i]
        tag = f"l{i}"
        xl, h, msaved, x1, h2, fsaved = saved[i]
        g_mix, g_ffn = w["norm_mix"].reshape(1, d), w["norm_ffn"].reshape(1, d)
        dhs, gw = _ffn_bwd(dx, h2, fsaved, w["ffn_w_gate"], w["ffn_w_up"], w["ffn_conv_w"], w["ffn_w_down"], seq, tag)
        dx1, d_gffn = _norm_bwd(x1, dx, dhs, g_ffn, f"dnorm_ffn_{tag}")
        dhs, gm = _MIXER_BWD[i % 3](dx1, h, msaved, w, bsz, seq, tag)
        dx, d_gmix = _norm_bwd(xl, dx1, dhs, g_mix, f"dnorm_mix_{tag}")
        gw.update(gm)
        gw["norm_mix"], gw["norm_ffn"] = d_gmix.reshape(-1), d_gffn.reshape(-1)
        grads[i] = gw
    return loss, dx, grads


def _adamw(w, g, m, v, name):
    def fn(w_, g_, m_, v_):
        m_new = ADAM_B1 * m_ + (1.0 - ADAM_B1) * g_
        v_new = ADAM_B2 * v_ + (1.0 - ADAM_B2) * (g_ * g_)
        m_hat = m_new / (1.0 - ADAM_B1 ** ADAM_STEP)
        v_hat = v_new / (1.0 - ADAM_B2 ** ADAM_STEP)
        delta = -ADAM_LR * (m_hat / (jnp.sqrt(v_hat) + ADAM_EPS) + ADAM_WD * w_)
        return delta, m_new, v_new
    shape = w.shape
    if w.ndim == 1:
        w, g, m, v = (a.reshape(1, -1) for a in (w, g, m, v))
    return [o.reshape(shape) for o in _elementwise(fn, [w, g, m, v], 3, name)]


WEIGHTS = ['norm_mix', 'norm_ffn', 'ffn_w_gate', 'ffn_w_up', 'ffn_conv_w', 'ffn_conv_b', 'ffn_w_down', 'a_w_in', 'a_b_in', 'a_v_norm',
           'a_w_s', 'a_b_s', 'a_w_out', 'b_w_in', 'b_b_f', 'b_q_norm', 'b_k_norm', 'b_w_out', 'c_w_in', 'c_conv_w', 'c_a_log',
           'c_dt_bias', 'c_out_norm', 'c_w_out']
BIG = {'ffn_w_gate': 1, 'ffn_w_up': 1, 'ffn_w_down': 0, 'a_w_in': 1, 'a_w_out': 0, 'b_w_in': 1, 'b_w_out': 0, 'c_w_in': 1, 'c_w_out': 0}
SMALL_SHARDED = {'ffn_conv_w': 1, 'a_b_in': 0, 'a_v_norm': 0, 'c_conv_w': 1}
MIXER_NAMES = (('a_w_in', 'a_b_in', 'a_v_norm', 'a_w_s', 'a_b_s', 'a_w_out'), ('b_w_in', 'b_b_f', 'b_q_norm', 'b_k_norm', 'b_w_out'),
               ('c_w_in', 'c_conv_w', 'c_a_log', 'c_dt_bias', 'c_out_norm', 'c_w_out'))
FFN_NAMES = ('norm_mix', 'norm_ffn', 'ffn_w_gate', 'ffn_w_up', 'ffn_conv_w', 'ffn_conv_b', 'ffn_w_down')


def _layer_entries(depth):
    out = []
    for i in range(depth):
        kind, j = i % 3, i // 3
        out.append([(n, i) for n in FFN_NAMES] + [(n, j) for n in MIXER_NAMES[kind]])
    return out


def _train_step(x, target, params, moments_m, moments_v):
    bsz, seq, d = x.shape
    depth = params['norm_mix'].shape[0]
    entries = _layer_entries(depth)
    nh = d // HEAD

    small_list = [(n, j) for n in SMALL_SHARDED for j in range(params[n].shape[0])]
    small_buf = _gather_shards(_pack([params[n][j] for n, j in small_list], F32, ()), "gather_small")
    small_full = {}
    for (n, j), g in zip(small_list, _unpack(small_buf, [params[n][j].shape for n, j in small_list], (N_CHIPS,))):
        small_full[(n, j)] = _join(g, SMALL_SHARDED[n])

    layers = []
    for i, ent in enumerate(entries):
        big = [(n, j) for n, j in ent if n in BIG]
        buf = _gather_shards(_pack([params[n][j] for n, j in big], BF16, ()), f"gather_l{i}")
        w = {}
        for (n, j), g in zip(big, _unpack(buf, [params[n][j].shape for n, j in big], (N_CHIPS,))):
            w[n] = _join(g, BIG[n])
        for n, j in ent:
            if n in SMALL_SHARDED:
                w[n] = small_full[(n, j)]
            elif n not in BIG:
                w[n] = params[n][j]
        for n in ('b_w_in', 'c_w_in'):
            if n in w:
                w[n] = jnp.pad(w[n], ((0, 0), (0, _pad_cols(w[n].shape[1]) - w[n].shape[1])))
        layers.append(w)

    loss_local, dx, grads = _local_step(x.reshape(bsz * seq, d), target.reshape(bsz * seq, d), layers, bsz, seq)
    loss = lax.psum(loss_local, ("x", "y", "c"))

    total = {}
    for i, ent in enumerate(entries):
        big = [(n, j) for n, j in ent if n in BIG]
        red = _reduce_scatter(_pack([_split(grads[i][n], BIG[n]) for n, j in big], F32, (N_CHIPS,)), f"l{i}")
        for (n, j), g in zip(big, _unpack(red, [params[n][j].shape for n, j in big], ())):
            total[(n, j)] = g
    layer_of = {(n, j): i for i, ent in enumerate(entries) for n, j in ent}
    red = _reduce_scatter(_pack([_split(grads[layer_of[(n, j)]][n], SMALL_SHARDED[n]) for n, j in small_list], F32, (N_CHIPS,)), "small")
    for (n, j), g in zip(small_list, _unpack(red, [params[n][j].shape for n, j in small_list], ())):
        total[(n, j)] = g
    repl = [(n, j) for n in WEIGHTS if n not in BIG and n not in SMALL_SHARDED for j in range(params[n].shape[0])]
    flat = jnp.concatenate([grads[layer_of[k]][k[0]].reshape(-1) for k in repl])
    n_flat = flat.shape[0]
    flat = jnp.pad(flat, (0, (-n_flat) % (SUBLANES * LANES))).reshape(-1, LANES)
    flat = _all_reduce_small(flat, "allreduce_small").reshape(-1)
    off = 0
    for k in repl:
        shp = params[k[0]][k[1]].shape
        n = 1
        for s in shp:
            n *= s
        total[k] = flat[off:off + n].reshape(shp)
        off += n

    grad_w, delta_w, new_m, new_v = {}, {}, {}, {}
    for n in WEIGHTS:
        g = jnp.stack([total[(n, j)] for j in range(params[n].shape[0])])
        grad_w[n] = g
        delta_w[n], new_m[n], new_v[n] = _adamw(params[n], g, moments_m[n], moments_v[n], f"adamw_{n}")
    return (loss, dx.reshape(bsz, seq, d), *[grad_w[n] for n in WEIGHTS], *[delta_w[n] for n in WEIGHTS],
            *[new_m[n] for n in WEIGHTS], *[new_v[n] for n in WEIGHTS])


def kernel(x, norm_mix, norm_ffn, ffn_w_gate, ffn_w_up, ffn_conv_w, ffn_conv_b, ffn_w_down, a_w_in, a_b_in, a_v_norm, a_w_s, a_b_s, a_w_out, b_w_in, b_b_f, b_q_norm, b_k_norm, b_w_out, c_w_in, c_conv_w, c_a_log, c_dt_bias, c_out_norm, c_w_out, loss_target, m_norm_mix, m_norm_ffn, m_ffn_w_gate, m_ffn_w_up, m_ffn_conv_w, m_ffn_conv_b, m_ffn_w_down, m_a_w_in, m_a_b_in, m_a_v_norm, m_a_w_s, m_a_b_s, m_a_w_out, m_b_w_in, m_b_b_f, m_b_q_norm, m_b_k_norm, m_b_w_out, m_c_w_in, m_c_conv_w, m_c_a_log, m_c_dt_bias, m_c_out_norm, m_c_w_out, v_norm_mix, v_norm_ffn, v_ffn_w_gate, v_ffn_w_up, v_ffn_conv_w, v_ffn_conv_b, v_ffn_w_down, v_a_w_in, v_a_b_in, v_a_v_norm, v_a_w_s, v_a_b_s, v_a_w_out, v_b_w_in, v_b_b_f, v_b_q_norm, v_b_k_norm, v_b_w_out, v_c_w_in, v_c_conv_w, v_c_a_log, v_c_dt_bias, v_c_out_norm, v_c_w_out):
    given = dict(locals())
    params = {n: given[n] for n in WEIGHTS}
    moments_m = {n: given["m_" + n] for n in WEIGHTS}
    moments_v = {n: given["v_" + n] for n in WEIGHTS}
    return _train_step(x, loss_target, params, moments_m, moments_v)
```

```python
import functools

import jax
import jax.numpy as jnp
from jax import lax
from jax.experimental import pallas as pl
from jax.experimental.pallas import tpu as pltpu

F32 = jnp.float32
BF16 = jnp.bfloat16
HI = lax.Precision.HIGHEST
MESH = pl.DeviceIdType.MESH

RMS_EPS = 1e-6
ADAM_LR, ADAM_B1, ADAM_B2, ADAM_EPS, ADAM_WD, ADAM_STEP = 0.001, 0.9, 0.999, 1e-08, 0.01, 10
A_CHUNK, HEAD, GDN_CHUNK = 128, 128, 64
LANES, SUBLANES = 128, 8
PACK_COLS = 1024
N_CHIPS = 4
VMEM_LIMIT = 56 * 1024 * 1024
ROWMAP_BUDGET = 20 * 1024 * 1024

NN = (((1,), (0,)), ((), ()))
NT = (((1,), (1,)), ((), ()))
TN = (((0,), (0,)), ((), ()))
BNN = (((2,), (1,)), ((0,), (0,)))
BNT = (((2,), (2,)), ((0,), (0,)))
BTN = (((1,), (1,)), ((0,), (0,)))


def _pick(n, cap, mult):
    if n <= cap:
        return n
    best = None
    for d in range(mult, cap + 1, mult):
        if n % d == 0:
            best = d
    if best is None:
        raise ValueError(f"no tile for {n} (cap {cap}, multiple of {mult})")
    return best


def _pad_cols(n):
    j = -(-n // LANES)
    while not (j <= 8 or any(j % d == 0 for d in (4, 5, 6, 7, 8))):
        j += 1
    return j * LANES


def _cparams(sem):
    return pltpu.CompilerParams(dimension_semantics=sem, vmem_limit_bytes=VMEM_LIMIT)


def _matmul(a, b, kind, out_dtype, name):
    if kind == "nn":
        (m, k), (k2, n) = a.shape, b.shape
    elif kind == "nt":
        (m, k), (n, k2) = a.shape, b.shape
    else:
        (k, m), (k2, n) = a.shape, b.shape
    assert k == k2, (name, a.shape, b.shape)
    tm, tn, tk = _pick(m, 1024, LANES), _pick(n, 1024, LANES), _pick(k, 512, LANES)
    nk = k // tk
    dims = {"nn": NN, "nt": NT, "tn": TN}[kind]
    a_spec = pl.BlockSpec((tk, tm), lambda i, j, kk: (kk, i)) if kind == "tn" else pl.BlockSpec((tm, tk), lambda i, j, kk: (i, kk))
    b_spec = pl.BlockSpec((tn, tk), lambda i, j, kk: (j, kk)) if kind == "nt" else pl.BlockSpec((tk, tn), lambda i, j, kk: (kk, j))

    def body(a_ref, b_ref, o_ref, acc_ref):
        kk = pl.program_id(2)

        @pl.when(kk == 0)
        def _():
            acc_ref[...] = jnp.zeros_like(acc_ref)

        acc_ref[...] += lax.dot_general(a_ref[...].astype(BF16), b_ref[...].astype(BF16), dims, preferred_element_type=F32)

        @pl.when(kk == nk - 1)
        def _():
            o_ref[...] = acc_ref[...].astype(o_ref.dtype)

    return pl.pallas_call(
        body, name=name, grid=(m // tm, n // tn, nk),
        in_specs=[a_spec, b_spec], out_specs=pl.BlockSpec((tm, tn), lambda i, j, kk: (i, j)),
        out_shape=jax.ShapeDtypeStruct((m, n), out_dtype),
        scratch_shapes=[pltpu.VMEM((tm, tn), F32)],
        compiler_params=_cparams(("parallel", "parallel", "arbitrary")),
    )(a, b)


def _win(arr, width=None, blk=0):
    return (arr, arr.shape[1] if width is None else width, blk)


def _rowmap(fn, rows, params, name, n_acc=0, tc=None, col_params=()):
    rows = [r if isinstance(r, tuple) else _win(r) for r in rows]
    t = rows[0][0].shape[0]
    widths = [tc if tc is not None else w for (_, w, _) in rows]

    def blocks_for(tr):
        rb = [jax.ShapeDtypeStruct((tr, w), a.dtype) for (a, _, _), w in zip(rows, widths)]
        pb = [jax.ShapeDtypeStruct((p.shape[0], tc) if (i in col_params) else p.shape, p.dtype) for i, p in enumerate(params)]
        return rb, pb

    rb, pb = blocks_for(SUBLANES * 2)
    outs = jax.eval_shape(fn, *rb, *pb)
    outs = list(outs) if isinstance(outs, (tuple, list)) else [outs]
    n_row = len(outs) - n_acc
    row_bytes = sum(w * a.dtype.itemsize for (a, _, _), w in zip(rows, widths)) + sum(o.shape[1] * o.dtype.itemsize for o in outs[:n_row])
    tr = 16
    while tr * 2 <= 512 and t % (tr * 2) == 0 and (tr * 2) * row_bytes * 5 <= ROWMAP_BUDGET:
        tr *= 2
    rb, pb = blocks_for(tr)
    outs = jax.eval_shape(fn, *rb, *pb)
    outs = list(outs) if isinstance(outs, (tuple, list)) else [outs]
    n_in = len(rows) + len(params)

    if tc is None:
        grid = (t // tr,)
        row_axis = 0
        in_specs = [pl.BlockSpec((tr, w), functools.partial(lambda i, b: (i, b), b=blk)) for (_, w, blk) in rows]
        in_specs += [pl.BlockSpec(p.shape, functools.partial(lambda i, nd: (0,) * nd, nd=p.ndim)) for p in params]
        out_specs = [pl.BlockSpec((tr, o.shape[1]), lambda i: (i, 0)) for o in outs[:n_row]]
        out_specs += [pl.BlockSpec(o.shape, functools.partial(lambda i, nd: (0,) * nd, nd=len(o.shape))) for o in outs[n_row:]]
        out_shape = [jax.ShapeDtypeStruct((t, o.shape[1]), o.dtype) for o in outs[:n_row]]
        out_shape += [jax.ShapeDtypeStruct(o.shape, o.dtype) for o in outs[n_row:]]
        sem = ("arbitrary",) if n_acc else ("parallel",)
    else:
        wtot = rows[0][1]
        grid = (wtot // tc, t // tr)
        row_axis = 1
        in_specs = [pl.BlockSpec((tr, tc), functools.partial(lambda j, i, b: (i, j + b), b=blk)) for (_, _, blk) in rows]
        for i, p in enumerate(params):
            if i in col_params:
                in_specs.append(pl.BlockSpec((p.shape[0], tc), lambda j, i: (0, j)))
            else:
                in_specs.append(pl.BlockSpec(p.shape, functools.partial(lambda j, i, nd: (0,) * nd, nd=p.ndim)))
        out_specs = [pl.BlockSpec((tr, tc), lambda j, i: (i, j)) for _ in outs[:n_row]]
        out_specs += [pl.BlockSpec((o.shape[0], tc), lambda j, i: (0, j)) for o in outs[n_row:]]
        out_shape = [jax.ShapeDtypeStruct((t, wtot), o.dtype) for o in outs[:n_row]]
        out_shape += [jax.ShapeDtypeStruct((o.shape[0], wtot), o.dtype) for o in outs[n_row:]]
        sem = ("parallel", "arbitrary") if n_acc else ("parallel", "parallel")

    def body(*refs):
        ins, ors = refs[:n_in], refs[n_in:]
        res = fn(*[r[...] for r in ins])
        res = list(res) if isinstance(res, (tuple, list)) else [res]
        for o, r in zip(ors[:n_row], res[:n_row]):
            o[...] = r.astype(o.dtype)
        if n_acc:
            i = pl.program_id(row_axis)
            for o, r in zip(ors[n_row:], res[n_row:]):
                @pl.when(i == 0)
                def _(o=o, r=r):
                    o[...] = r.astype(o.dtype)

                @pl.when(i > 0)
                def _(o=o, r=r):
                    o[...] += r.astype(o.dtype)

    res = pl.pallas_call(
        body, name=name, grid=grid, in_specs=in_specs, out_specs=out_specs, out_shape=out_shape,
        compiler_params=_cparams(sem),
    )(*[a for (a, _, _) in rows], *params)
    return res


def _elementwise(fn, arrays, n_out, name):
    shape = arrays[0].shape
    cols = shape[-1]
    rws = 1
    for s in shape[:-1]:
        rws *= s
    arrs = [a.reshape(rws, cols) for a in arrays]
    per_row = cols * 4 * (len(arrays) + n_out) * 3
    tr = rws
    if rws * per_row > ROWMAP_BUDGET:
        tr = _pick(rws, max(SUBLANES, ROWMAP_BUDGET // per_row), SUBLANES)

    def body(*refs):
        res = fn(*[r[...] for r in refs[:len(arrs)]])
        for o, r in zip(refs[len(arrs):], res):
            o[...] = r

    spec = pl.BlockSpec((tr, cols), lambda i: (i, 0))
    outs = pl.pallas_call(
        body, name=name, grid=(rws // tr,), in_specs=[spec] * len(arrs), out_specs=[spec] * n_out,
        out_shape=[jax.ShapeDtypeStruct((rws, cols), F32)] * n_out, compiler_params=_cparams(("parallel",)),
    )(*arrs)
    return [o.reshape(shape) for o in outs]


def _rms(x, g):
    return x * lax.rsqrt(jnp.mean(x * x, axis=-1, keepdims=True) + RMS_EPS) * g


def _silu(x):
    return x * jax.nn.sigmoid(x)


def _softplus(x):
    return jnp.maximum(x, 0.0) + jnp.log(1.0 + jnp.exp(-jnp.abs(x)))


def _gelu_tanh(x):
    return 0.5 * x * (1.0 + jnp.tanh(0.7978845608028654 * (x + 0.044715 * (x * x * x))))


def _heads(x):
    return [x[:, h * HEAD:(h + 1) * HEAD] for h in range(x.shape[1] // HEAD)]


def _dot(a, b, dims=NN):
    return lax.dot_general(a, b, dims, precision=HI, preferred_element_type=F32)


def _bdot(a, b, dims):
    return lax.dot_general(a.astype(BF16), b.astype(BF16), dims, preferred_element_type=F32)


def _bdot3(a, b, dims):
    return lax.dot_general(a, b, dims, precision=lax.Precision.HIGH, preferred_element_type=F32)


def _eye(n):
    return (lax.broadcasted_iota(jnp.int32, (n, n), 0) == lax.broadcasted_iota(jnp.int32, (n, n), 1)).astype(F32)


def _tri(n):
    return lax.broadcasted_iota(jnp.int32, (n, n), 0) >= lax.broadcasted_iota(jnp.int32, (n, n), 1)


def _row_to_col(row):
    return jnp.sum(_eye(row.shape[1]) * row, axis=1, keepdims=True)


def _conv_tiles(w, seq):
    return _pick(seq, 512, SUBLANES), _pick(w, 512, LANES)


def _conv_fwd(x, width, w, bias, seq, name):
    t = x.shape[0]
    kk = w.shape[0]
    tr, tc = _conv_tiles(width, seq)
    hb = tr // SUBLANES

    def body(*refs):
        if bias is None:
            x_ref, h_ref, w_ref, o_ref = refs
        else:
            x_ref, h_ref, w_ref, b_ref, o_ref = refs
        i = pl.program_id(1)
        first = (i * tr) % seq == 0
        halo = jnp.where(first, 0.0, h_ref[...])
        xe = jnp.concatenate([halo, x_ref[...]], axis=0)
        wv = w_ref[...]
        acc = xe[SUBLANES:, :] * wv[kk - 1:kk, :]
        for s in range(1, kk):
            acc = acc + pltpu.roll(xe, s, 0)[SUBLANES:, :] * wv[kk - 1 - s:kk - s, :]
        if bias is not None:
            acc = acc + b_ref[...]
        o_ref[...] = acc

    in_specs = [pl.BlockSpec((tr, tc), lambda j, i: (i, j)),
                pl.BlockSpec((SUBLANES, tc), lambda j, i: (jnp.maximum(i * hb - 1, 0), j)),
                pl.BlockSpec((kk, tc), lambda j, i: (0, j))]
    ops = [x, x, w]
    if bias is not None:
        in_specs.append(pl.BlockSpec((1, tc), lambda j, i: (0, j)))
        ops.append(bias)
    return pl.pallas_call(
        body, name=name, grid=(width // tc, t // tr), in_specs=in_specs,
        out_specs=pl.BlockSpec((tr, tc), lambda j, i: (i, j)), out_shape=jax.ShapeDtypeStruct((t, width), F32),
        compiler_params=_cparams(("parallel", "parallel")),
    )(*ops)


def _conv_bwd(dy, x, width, w, seq, with_bias, name, xcol=0):
    t = x.shape[0]
    kk = w.shape[0]
    tr, tc = _conv_tiles(width, seq)
    hb = tr // SUBLANES
    n_halo_blocks = t // SUBLANES
    assert xcol % tc == 0
    xb = xcol // tc

    def body(dy_ref, dyn_ref, x_ref, xh_ref, w_ref, dx_ref, dw_ref, *rest):
        i = pl.program_id(1)
        first = (i * tr) % seq == 0
        last = ((i + 1) * tr) % seq == 0
        dyc = dy_ref[...]
        dye = jnp.concatenate([dyc, jnp.where(last, 0.0, dyn_ref[...])], axis=0)
        xe = jnp.concatenate([jnp.where(first, 0.0, xh_ref[...]), x_ref[...]], axis=0)
        wv = w_ref[...]
        dx = dyc * wv[kk - 1:kk, :]
        dws = [None] * kk
        dws[kk - 1] = jnp.sum(dyc * xe[SUBLANES:, :], axis=0, keepdims=True)
        for s in range(1, kk):
            dx = dx + pltpu.roll(dye, tr + SUBLANES - s, 0)[:tr, :] * wv[kk - 1 - s:kk - s, :]
            dws[kk - 1 - s] = jnp.sum(dyc * pltpu.roll(xe, s, 0)[SUBLANES:, :], axis=0, keepdims=True)
        dx_ref[...] = dx

        @pl.when(i == 0)
        def _():
            for j in range(kk):
                dw_ref[j:j + 1, :] = dws[j]
            if with_bias:
                rest[0][...] = jnp.sum(dyc, axis=0, keepdims=True)

        @pl.when(i > 0)
        def _():
            for j in range(kk):
                dw_ref[j:j + 1, :] += dws[j]
            if with_bias:
                rest[0][...] += jnp.sum(dyc, axis=0, keepdims=True)

    cur = pl.BlockSpec((tr, tc), lambda j, i: (i, j))
    in_specs = [cur, pl.BlockSpec((SUBLANES, tc), lambda j, i: (jnp.minimum((i + 1) * hb, n_halo_blocks - 1), j)),
                pl.BlockSpec((tr, tc), lambda j, i: (i, j + xb)),
                pl.BlockSpec((SUBLANES, tc), lambda j, i: (jnp.maximum(i * hb - 1, 0), j + xb)),
                pl.BlockSpec((kk, tc), lambda j, i: (0, j))]
    out_specs = [cur, pl.BlockSpec((kk, tc), lambda j, i: (0, j))]
    out_shape = [jax.ShapeDtypeStruct((t, width), F32), jax.ShapeDtypeStruct((kk, width), F32)]
    if with_bias:
        out_specs.append(pl.BlockSpec((1, tc), lambda j, i: (0, j)))
        out_shape.append(jax.ShapeDtypeStruct((1, width), F32))
    return pl.pallas_call(
        body, name=name, grid=(width // tc, t // tr), in_specs=in_specs, out_specs=out_specs, out_shape=out_shape,
        compiler_params=_cparams(("parallel", "arbitrary")),
    )(dy, dy, x, x, w)


def _cumsum_rows(x, seq, reverse, name):
    t, w = x.shape
    tb = _pick(seq, 256, SUBLANES)
    nb = seq // tb

    def pos(b, i):
        return (b * nb + (nb - 1 - i if reverse else i), 0)

    def body(x_ref, o_ref, carry):
        i = pl.program_id(1)

        @pl.when(i == 0)
        def _():
            carry[...] = jnp.zeros_like(carry)

        blk = x_ref[...]
        r = lax.broadcasted_iota(jnp.int32, (tb, tb), 0)
        c = lax.broadcasted_iota(jnp.int32, (tb, tb), 1)
        m = ((r <= c) if reverse else (r >= c)).astype(F32)
        o_ref[...] = _dot(m, blk) + carry[...]
        carry[...] += jnp.sum(blk, axis=0, keepdims=True)

    return pl.pallas_call(
        body, name=name, grid=(t // seq, nb), in_specs=[pl.BlockSpec((tb, w), pos)], out_specs=pl.BlockSpec((tb, w), pos),
        out_shape=jax.ShapeDtypeStruct((t, w), F32), scratch_shapes=[pltpu.VMEM((1, w), F32)],
        compiler_params=_cparams(("parallel", "arbitrary")),
    )(x)


def _sgu_fwd(vn, u, w_s, b_col, name):
    t, a = vn.shape
    g = a // HEAD

    def body(v_ref, u_ref, w_ref, b_ref, y_ref):
        tri = _tri(A_CHUNK)
        for gi in range(g):
            sl = slice(gi * HEAD, (gi + 1) * HEAD)
            wc = jnp.where(tri, w_ref[gi], 0.0)
            sv = _dot(wc, v_ref[:, sl]) + b_ref[gi]
            y_ref[:, sl] = (u_ref[:, sl] * sv).astype(y_ref.dtype)

    blk = pl.BlockSpec((A_CHUNK, a), lambda i: (i, 0))
    return pl.pallas_call(
        body, name=name, grid=(t // A_CHUNK,),
        in_specs=[blk, blk, pl.BlockSpec(w_s.shape, lambda i: (0, 0, 0)), pl.BlockSpec(b_col.shape, lambda i: (0, 0, 0))],
        out_specs=blk, out_shape=jax.ShapeDtypeStruct((t, a), BF16), compiler_params=_cparams(("parallel",)),
    )(vn, u, w_s, b_col)


def _sgu_bwd(vn, u, dy, w_s, b_col, name):
    t, a = vn.shape
    g = a // HEAD

    def body(v_ref, u_ref, dy_ref, w_ref, b_ref, dv_ref, du_ref, dw_ref, db_ref):
        i = pl.program_id(0)
        tri = _tri(A_CHUNK)
        for gi in range(g):
            sl = slice(gi * HEAD, (gi + 1) * HEAD)
            wc = jnp.where(tri, w_ref[gi], 0.0)
            v = v_ref[:, sl]
            sv = _dot(wc, v) + b_ref[gi]
            dyb = dy_ref[:, sl]
            du_ref[:, sl] = dyb * sv
            dsv = dyb * u_ref[:, sl]
            dv_ref[:, sl] = _dot(wc, dsv, TN)
            dw = jnp.where(tri, _dot(dsv, v, NT), 0.0)
            db = jnp.sum(dsv, axis=1, keepdims=True)

            @pl.when(i == 0)
            def _(gi=gi, dw=dw, db=db):
                dw_ref[gi] = dw
                db_ref[gi] = db

            @pl.when(i > 0)
            def _(gi=gi, dw=dw, db=db):
                dw_ref[gi] += dw
                db_ref[gi] += db

    blk = pl.BlockSpec((A_CHUNK, a), lambda i: (i, 0))
    wsp = pl.BlockSpec(w_s.shape, lambda i: (0, 0, 0))
    bsp = pl.BlockSpec(b_col.shape, lambda i: (0, 0, 0))
    return pl.pallas_call(
        body, name=name, grid=(t // A_CHUNK,), in_specs=[blk, blk, blk, wsp, bsp], out_specs=[blk, blk, wsp, bsp],
        out_shape=[jax.ShapeDtypeStruct((t, a), F32), jax.ShapeDtypeStruct((t, a), F32),
                   jax.ShapeDtypeStruct(w_s.shape, F32), jax.ShapeDtypeStruct(b_col.shape, F32)],
        compiler_params=_cparams(("arbitrary",)),
    )(vn, u, dy, w_s, b_col)


def _fox_scores(q, k, cq_row, ck_row, diag, scale):
    s = lax.dot_general(q.astype(BF16), k.astype(BF16), NT, preferred_element_type=F32) * scale
    s = s + _row_to_col(cq_row) - ck_row
    mask = jnp.logical_or(jnp.logical_not(diag), _tri(q.shape[0]))
    return s, mask


def _fox_fwd(qn, kn, proj, v_blk0, c_rows, bsz, seq, nh, name):
    t = qn.shape[0]
    tq = _pick(seq, 512, LANES)
    nq = seq // tq
    scale = HEAD ** -0.5

    def body(q_ref, k_ref, v_ref, cq_ref, ck_ref, o_ref, lse_ref, m_s, l_s, acc_s):
        i, j = pl.program_id(2), pl.program_id(3)

        @pl.when(j == 0)
        def _():
            m_s[...] = jnp.full_like(m_s, -jnp.inf)
            l_s[...] = jnp.zeros_like(l_s)
            acc_s[...] = jnp.zeros_like(acc_s)

        @pl.when(j <= i)
        def _():
            s, mask = _fox_scores(q_ref[...], k_ref[...], cq_ref[...], ck_ref[...], j == i, scale)
            s = jnp.where(mask, s, -jnp.inf)
            m_new = jnp.maximum(m_s[...], jnp.max(s, axis=1, keepdims=True))
            p = jnp.exp(s - m_new)
            alpha = jnp.exp(m_s[...] - m_new)
            l_s[...] = alpha * l_s[...] + jnp.sum(p, axis=1, keepdims=True)
            acc_s[...] = alpha * acc_s[...] + lax.dot_general(p.astype(BF16), v_ref[...].astype(BF16), NN, preferred_element_type=F32)
            m_s[...] = m_new

        @pl.when(j == nq - 1)
        def _():
            o_ref[...] = acc_s[...] / l_s[...]
            lse_ref[...] = jnp.broadcast_to(m_s[...] + jnp.log(l_s[...]), lse_ref.shape)

    qspec = pl.BlockSpec((tq, HEAD), lambda b, h, i, j: (b * nq + i, h))
    kspec = pl.BlockSpec((tq, HEAD), lambda b, h, i, j: (b * nq + jnp.minimum(i, j), h))
    vspec = pl.BlockSpec((tq, HEAD), lambda b, h, i, j: (b * nq + jnp.minimum(i, j), v_blk0 + h))
    cq = pl.BlockSpec((None, None, 1, tq), lambda b, h, i, j: (b, h, 0, i))
    ck = pl.BlockSpec((None, None, 1, tq), lambda b, h, i, j: (b, h, 0, jnp.minimum(i, j)))
    return pl.pallas_call(
        body, name=name, grid=(bsz, nh, nq, nq), in_specs=[qspec, kspec, vspec, cq, ck], out_specs=[qspec, qspec],
        out_shape=[jax.ShapeDtypeStruct((t, nh * HEAD), F32)] * 2,
        scratch_shapes=[pltpu.VMEM((tq, 1), F32), pltpu.VMEM((tq, 1), F32), pltpu.VMEM((tq, HEAD), F32)],
        compiler_params=_cparams(("parallel", "parallel", "parallel", "arbitrary")),
    )(qn, kn, proj, c_rows, c_rows)


def _fox_p_dp(q, k, v, do, lse, cq_row, ck_row, diag, scale):
    s, mask = _fox_scores(q, k, cq_row, ck_row, diag, scale)
    p = jnp.where(mask, jnp.exp(s - jnp.max(lse, axis=1, keepdims=True)), 0.0)
    dp = lax.dot_general(do.astype(BF16), v.astype(BF16), NT, preferred_element_type=F32)
    return p, dp


def _fox_bwd_q(qn, kn, proj, v_blk0, do, lse, c_rows, bsz, seq, nh, name):
    t = qn.shape[0]
    tq = _pick(seq, 512, LANES)
    nq = seq // tq
    scale = HEAD ** -0.5

    def key_block(jj):
        return jnp.where(jj >= nq, jj - nq, jj)

    def body(q_ref, k_ref, v_ref, do_ref, lse_ref, cq_ref, ck_ref, dq_ref, dl_ref, dq_s, dl_s):
        i, jj = pl.program_id(2), pl.program_id(3)
        j = key_block(jj)

        @pl.when(jj == 0)
        def _():
            dq_s[...] = jnp.zeros_like(dq_s)
            dl_s[...] = jnp.zeros_like(dl_s)

        @pl.when(j <= i)
        def _():
            p, dp = _fox_p_dp(q_ref[...], k_ref[...], v_ref[...], do_ref[...], lse_ref[...], cq_ref[...], ck_ref[...], j == i, scale)

            @pl.when(jj < nq)
            def _():
                dl_s[...] += jnp.sum(p * dp, axis=1, keepdims=True)

            @pl.when(jj >= nq)
            def _():
                ds = p * (dp - dl_s[...])
                dq_s[...] += lax.dot_general(ds.astype(BF16), k_ref[...].astype(BF16), NN, preferred_element_type=F32) * scale

        @pl.when(jj == 2 * nq - 1)
        def _():
            dq_ref[...] = dq_s[...]
            dl_ref[...] = jnp.broadcast_to(dl_s[...], dl_ref.shape)

    qspec = pl.BlockSpec((tq, HEAD), lambda b, h, i, jj: (b * nq + i, h))
    kspec = pl.BlockSpec((tq, HEAD), lambda b, h, i, jj: (b * nq + jnp.minimum(i, key_block(jj)), h))
    vspec = pl.BlockSpec((tq, HEAD), lambda b, h, i, jj: (b * nq + jnp.minimum(i, key_block(jj)), v_blk0 + h))
    cq = pl.BlockSpec((None, None, 1, tq), lambda b, h, i, jj: (b, h, 0, i))
    ck = pl.BlockSpec((None, None, 1, tq), lambda b, h, i, jj: (b, h, 0, jnp.minimum(i, key_block(jj))))
    return pl.pallas_call(
        body, name=name, grid=(bsz, nh, nq, 2 * nq), in_specs=[qspec, kspec, vspec, qspec, qspec, cq, ck],
        out_specs=[qspec, qspec], out_shape=[jax.ShapeDtypeStruct((t, nh * HEAD), F32)] * 2,
        scratch_shapes=[pltpu.VMEM((tq, HEAD), F32), pltpu.VMEM((tq, 1), F32)],
        compiler_params=_cparams(("parallel", "parallel", "parallel", "arbitrary")),
    )(qn, kn, proj, do, lse, c_rows, c_rows)


def _fox_bwd_kv(qn, kn, proj, v_blk0, do, lse, delta, c_rows, bsz, seq, nh, name):
    t = qn.shape[0]
    tq = _pick(seq, 512, LANES)
    nq = seq // tq
    scale = HEAD ** -0.5

    def body(q_ref, k_ref, v_ref, do_ref, lse_ref, dl_ref, cq_ref, ck_ref, dk_ref, dv_ref, dc_ref, dk_s, dv_s, dc_s):
        j, i = pl.program_id(2), pl.program_id(3)

        @pl.when(i == 0)
        def _():
            dk_s[...] = jnp.zeros_like(dk_s)
            dv_s[...] = jnp.zeros_like(dv_s)
            dc_s[...] = jnp.zeros_like(dc_s)

        @pl.when(i >= j)
        def _():
            p, dp = _fox_p_dp(q_ref[...], k_ref[...], v_ref[...], do_ref[...], lse_ref[...], cq_ref[...], ck_ref[...], j == i, scale)
            ds = p * (dp - jnp.max(dl_ref[...], axis=1, keepdims=True))
            dv_s[...] += lax.dot_general(p.astype(BF16), do_ref[...].astype(BF16), TN, preferred_element_type=F32)
            dk_s[...] += lax.dot_general(ds.astype(BF16), q_ref[...].astype(BF16), TN, preferred_element_type=F32) * scale
            dc_s[...] -= jnp.sum(ds, axis=0, keepdims=True)

        @pl.when(i == nq - 1)
        def _():
            dk_ref[...] = dk_s[...]
            dv_ref[...] = dv_s[...]
            dc_ref[...] = dc_s[...]

    kspec = pl.BlockSpec((tq, HEAD), lambda b, h, j, i: (b * nq + j, h))
    vspec = pl.BlockSpec((tq, HEAD), lambda b, h, j, i: (b * nq + j, v_blk0 + h))
    qspec = pl.BlockSpec((tq, HEAD), lambda b, h, j, i: (b * nq + jnp.maximum(i, j), h))
    cq = pl.BlockSpec((None, None, 1, tq), lambda b, h, j, i: (b, h, 0, jnp.maximum(i, j)))
    ck = pl.BlockSpec((None, None, 1, tq), lambda b, h, j, i: (b, h, 0, j))
    return pl.pallas_call(
        body, name=name, grid=(bsz, nh, nq, nq), in_specs=[qspec, kspec, vspec, qspec, qspec, qspec, cq, ck],
        out_specs=[kspec, kspec, ck],
        out_shape=[jax.ShapeDtypeStruct((t, nh * HEAD), F32)] * 2 + [jax.ShapeDtypeStruct(c_rows.shape, F32)],
        scratch_shapes=[pltpu.VMEM((tq, HEAD), F32), pltpu.VMEM((tq, HEAD), F32), pltpu.VMEM((1, tq), F32)],
        compiler_params=_cparams(("parallel", "parallel", "parallel", "arbitrary")),
    )(qn, kn, proj, do, lse, delta, c_rows, c_rows)


def _gdn_chunk(qp, kp, vp, b_row, a_row, alog, dtb, state):
    hv = vp.shape[0]
    c = qp.shape[1]
    qc, kc, vc = _silu(qp), _silu(kp), _silu(vp)
    qh = qc * lax.rsqrt(jnp.sum(qc * qc, -1, keepdims=True) + RMS_EPS) * (HEAD ** -0.5)
    kh = kc * lax.rsqrt(jnp.sum(kc * kc, -1, keepdims=True) + RMS_EPS)
    q = jnp.stack([qh[h // 2] for h in range(hv)])
    k = jnp.stack([kh[h // 2] for h in range(hv)])
    beta_row = jax.nn.sigmoid(b_row)
    g_row = -jnp.exp(alog) * _softplus(a_row + dtb)
    ri = lax.broadcasted_iota(jnp.int32, (c, c), 0)
    ci = lax.broadcasted_iota(jnp.int32, (c, c), 1)
    eye = (ri == ci).astype(F32)
    tri = ri >= ci
    beta_col = jnp.sum(eye * beta_row, axis=2, keepdims=True)
    g_col = jnp.sum(eye * g_row, axis=2, keepdims=True)
    gc_col = jnp.sum(tri.astype(F32) * g_row, axis=2, keepdims=True)
    gc_row = jnp.sum(g_col * (ri <= ci).astype(F32), axis=1, keepdims=True)
    decay = jnp.where(tri, jnp.exp(jnp.where(tri, gc_col - gc_row, 0.0)), 0.0)
    kb = k * beta_col
    a_mat = jnp.where(ri > ci, _bdot(kb, k, BNT) * decay, 0.0)
    egc = jnp.exp(gc_col)
    inv = eye - a_mat
    pw = _bdot3(a_mat, a_mat, BNN)
    n_sq = max(1, (c - 1).bit_length() - 1)
    for it in range(n_sq):
        inv = inv + _bdot3(inv, pw, BNN)
        if it < n_sq - 1:
            pw = _bdot3(pw, pw, BNN)
    u = _bdot(inv, vc * beta_col, BNN)
    w = _bdot(inv, kb * egc, BNN)
    attn = _bdot(q, k, BNT) * decay
    g_last = jnp.sum(g_row, axis=2, keepdims=True)
    v_new = u - _bdot(w, state, BNN)
    o = _bdot(q * egc, state, BNN) + _bdot(attn, v_new, BNN)
    new_state = state * jnp.exp(g_last) + _bdot(k * jnp.exp(g_last - gc_col), v_new, BTN)
    return o, new_state


def _gdn_group(nvh):
    return 4 if nvh % 4 == 0 else 2


def _gdn_specs(nch, nkh, hb, rev):
    def n_of(n):
        return nch - 1 - n if rev else n

    hk = hb // 2
    per_k = pl.BlockSpec((GDN_CHUNK, hk * HEAD), lambda g, b, n: (b * nch + n_of(n), g))
    q = per_k
    k = pl.BlockSpec((GDN_CHUNK, hk * HEAD), lambda g, b, n: (b * nch + n_of(n), nkh // hk + g))
    v = pl.BlockSpec((GDN_CHUNK, hb * HEAD), lambda g, b, n: (b * nch + n_of(n), 2 * nkh // hb + g))
    per_v = pl.BlockSpec((GDN_CHUNK, hb * HEAD), lambda g, b, n: (b * nch + n_of(n), g))
    row = pl.BlockSpec((hb, None, None, 1, GDN_CHUNK), lambda g, b, n: (g, b, n_of(n), 0, 0))
    sc = pl.BlockSpec((hb, 1, 1), lambda g, b, n: (g, 0, 0))
    st = pl.BlockSpec((hb, None, None, HEAD, HEAD), lambda g, b, n: (g, b, n_of(n), 0, 0))
    return q, k, v, per_k, per_v, row, sc, st


def _stack_heads(ref, n):
    return jnp.stack([ref[:, h * HEAD:(h + 1) * HEAD] for h in range(n)])


def _gdn_fwd(conv, b_rows, a_rows, alog, dtb, bsz, seq, nvh, name):
    t = conv.shape[0]
    nch = seq // GDN_CHUNK
    nkh = nvh // 2
    hb = _gdn_group(nvh)
    q, k, v, _, per_v, row, sc, st = _gdn_specs(nch, nkh, hb, False)

    def body(q_ref, k_ref, v_ref, b_ref, a_ref, al_ref, dt_ref, o_ref, st_ref, state):
        @pl.when(pl.program_id(2) == 0)
        def _():
            state[...] = jnp.zeros_like(state)

        st_ref[...] = state[...]
        o, new_state = _gdn_chunk(_stack_heads(q_ref, hb // 2), _stack_heads(k_ref, hb // 2), _stack_heads(v_ref, hb),
                                  b_ref[...], a_ref[...], al_ref[...], dt_ref[...], state[...])
        for h in range(hb):
            o_ref[:, h * HEAD:(h + 1) * HEAD] = o[h]
        state[...] = new_state

    return pl.pallas_call(
        body, name=name, grid=(nvh // hb, bsz, nch), in_specs=[q, k, v, row, row, sc, sc], out_specs=[per_v, st],
        out_shape=[jax.ShapeDtypeStruct((t, nvh * HEAD), F32), jax.ShapeDtypeStruct((nvh, bsz, nch, HEAD, HEAD), F32)],
        scratch_shapes=[pltpu.VMEM((hb, HEAD, HEAD), F32)],
        compiler_params=_cparams(("parallel", "parallel", "arbitrary")),
    )(conv, conv, conv, b_rows, a_rows, alog, dtb)


def _gdn_bwd(conv, b_rows, a_rows, alog, dtb, states, do, bsz, seq, nvh, name):
    t = conv.shape[0]
    nch = seq // GDN_CHUNK
    nkh = nvh // 2
    hb = _gdn_group(nvh)
    q, k, v, per_k, per_v, row, sc, st = _gdn_specs(nch, nkh, hb, True)

    def body(q_ref, k_ref, v_ref, b_ref, a_ref, al_ref, dt_ref, st_ref, do_ref,
             dq_ref, dk_ref, dv_ref, db_ref, da_ref, dal_ref, ddt_ref, dstate):
        b, n = pl.program_id(1), pl.program_id(2)

        @pl.when(n == 0)
        def _():
            dstate[...] = jnp.zeros_like(dstate)

        _, vjp = jax.vjp(_gdn_chunk, _stack_heads(q_ref, hb // 2), _stack_heads(k_ref, hb // 2), _stack_heads(v_ref, hb),
                         b_ref[...], a_ref[...], al_ref[...], dt_ref[...], st_ref[...])
        dq, dk, dv, db, da, dal, ddt, dst = vjp((_stack_heads(do_ref, hb), dstate[...]))
        for h in range(hb // 2):
            dq_ref[:, h * HEAD:(h + 1) * HEAD] = dq[h]
            dk_ref[:, h * HEAD:(h + 1) * HEAD] = dk[h]
        for h in range(hb):
            dv_ref[:, h * HEAD:(h + 1) * HEAD] = dv[h]
        db_ref[...] = db
        da_ref[...] = da
        dstate[...] = dst
        start = jnp.logical_and(b == 0, n == 0)

        @pl.when(start)
        def _():
            dal_ref[...] = dal
            ddt_ref[...] = ddt

        @pl.when(jnp.logical_not(start))
        def _():
            dal_ref[...] += dal
            ddt_ref[...] += ddt

    f = lambda *s: jax.ShapeDtypeStruct(s, F32)
    return pl.pallas_call(
        body, name=name, grid=(nvh // hb, bsz, nch), in_specs=[q, k, v, row, row, sc, sc, st, per_v],
        out_specs=[per_k, per_k, per_v, row, row, sc, sc],
        out_shape=[f(t, nkh * HEAD), f(t, nkh * HEAD), f(t, nvh * HEAD), f(*b_rows.shape), f(*a_rows.shape), f(nvh, 1, 1), f(nvh, 1, 1)],
        scratch_shapes=[pltpu.VMEM((hb, HEAD, HEAD), F32)],
        compiler_params=_cparams(("arbitrary", "arbitrary", "arbitrary")),
    )(conv, conv, conv, b_rows, a_rows, alog, dtb, states, do)


ANY = pl.BlockSpec(memory_space=pl.ANY)
FLIPS = (2, 1, 3)


def _place():
    x, y, c = lax.axis_index("x"), lax.axis_index("y"), lax.axis_index("c")
    chips = [(1 - x, y), (x, 1 - y), (1 - x, 1 - y)]
    return x, y, c, chips


def _place_scalars():
    return jnp.stack([2 * lax.axis_index("x") + lax.axis_index("y"), lax.axis_index("c")]).astype(jnp.int32)


def _rows(start, size, mult=16):
    return pl.ds(pl.multiple_of(start, mult), size)


def _cast_window(w_stack, layer, cls, place, name):
    _, r, w = w_stack.shape
    if cls == "U":
        tr = _pick(r, 256, 16)
        grid = (r // tr, 1)
        in_spec = pl.BlockSpec((None, tr, w), lambda i, j, s: (layer, i, 0))
        out_spec = pl.BlockSpec((None, tr, w), lambda i, j, s: (s[0], i, 0))
        out_shape = (N_CHIPS, r, w)
    else:
        tr, tc = _pick(r, 512, 16), _pick(w, 1024, LANES)
        nbr, nbc = r // tr, w // tc
        grid = (nbr, nbc)
        in_spec = pl.BlockSpec((None, tr, tc), lambda i, j, s: (layer, i, j))
        if cls == "C":
            out_spec = pl.BlockSpec((tr, tc), lambda i, j, s: (i, s[0] * nbc + j))
            out_shape = (r, N_CHIPS * w)
        else:
            out_spec = pl.BlockSpec((tr, tc), lambda i, j, s: (s[0] * nbr + i, j))
            out_shape = (N_CHIPS * r, w)

    def body(s_ref, x_ref, o_ref):
        o_ref[...] = x_ref[...].astype(o_ref.dtype)

    gs = pltpu.PrefetchScalarGridSpec(num_scalar_prefetch=1, grid=grid, in_specs=[in_spec], out_specs=out_spec)
    return pl.pallas_call(body, name=name, grid_spec=gs, out_shape=jax.ShapeDtypeStruct(out_shape, BF16),
                          compiler_params=_cparams(("parallel", "parallel")))(place, w_stack)


def _halved(buf, cls):
    return {"C": buf.shape[0], "U": buf.shape[1], "R": buf.shape[0] // N_CHIPS}[cls]


def _part(buf, cls, chip, start, size):
    if cls == "C":
        w = buf.shape[1] // N_CHIPS
        return buf.at[_rows(start, size), pl.ds(chip * w, w)]
    if cls == "U":
        return buf.at[chip, _rows(start, size), :]
    r = buf.shape[0] // N_CHIPS
    return buf.at[_rows(chip * r + start, size), :]


def _gather_layer(bufs, classes, name):
    n = len(bufs)

    def body(*refs):
        outs = refs[n:2 * n]
        send_sems, recv_sems = refs[2 * n], refs[2 * n + 1]
        x, y, c, chips = _place()
        me = 2 * x + y

        def run(s_me):
            def copy(t, k, chip, start, to):
                size = _halved(outs[t], classes[t]) // 2
                part = _part(outs[t], classes[t], chip, start, size)
                return pltpu.make_async_remote_copy(src_ref=part, dst_ref=part, send_sem=send_sems.at[6 * t + k],
                                                    recv_sem=recv_sems.at[6 * t + k], device_id=to, device_id_type=MESH)
            started = []
            for t in range(n):
                half = _halved(outs[t], classes[t]) // 2
                for k, (cx, cy) in enumerate(chips):
                    cp = copy(t, k, s_me, c * half, (cx, cy, c))
                    cp.start()
                    started.append(cp)
            for t in range(n):
                half = _halved(outs[t], classes[t]) // 2
                for k in range(3):
                    copy(t, k, s_me ^ FLIPS[k], c * half, (x, y, c)).wait_recv()
                    fwd = copy(t, 3 + k, s_me ^ FLIPS[k], c * half, (x, y, 1 - c))
                    fwd.start()
                    started.append(fwd)
            for t in range(n):
                half = _halved(outs[t], classes[t]) // 2
                for k in range(3):
                    copy(t, 3 + k, s_me ^ FLIPS[k], (1 - c) * half, (x, y, c)).wait_recv()
            for cp in started:
                cp.wait_send()

        for s_me in range(N_CHIPS):
            pl.when(me == s_me)(functools.partial(run, s_me))

    return pl.pallas_call(
        body, name=name, in_specs=[ANY] * n, out_specs=[ANY] * n,
        out_shape=[jax.ShapeDtypeStruct(b.shape, b.dtype) for b in bufs],
        input_output_aliases={t: t for t in range(n)},
        scratch_shapes=[pltpu.SemaphoreType.DMA((6 * n,)), pltpu.SemaphoreType.DMA((6 * n,))],
    )(*bufs)


def _assemble(slots, width, name):
    _, r, w = slots.shape
    tr = _pick(r, 256, 16)

    def body(s_ref, o_ref):
        parts = [s_ref[s] for s in range(N_CHIPS)]
        if width > N_CHIPS * w:
            parts.append(jnp.zeros((tr, width - N_CHIPS * w), slots.dtype))
        o_ref[...] = jnp.concatenate(parts, axis=1)

    return pl.pallas_call(
        body, name=name, grid=(r // tr,), in_specs=[pl.BlockSpec((N_CHIPS, tr, w), lambda i: (0, i, 0))],
        out_specs=pl.BlockSpec((tr, width), lambda i: (i, 0)), out_shape=jax.ShapeDtypeStruct((r, width), slots.dtype),
        compiler_params=_cparams(("parallel",)),
    )(slots)


def _rs_pair(dws, classes, name):
    n = len(dws)

    def shape_of(d, cls):
        return (N_CHIPS, d.shape[1] // 2, d.shape[2]) if cls == "R" else (d.shape[0] // 2, d.shape[1])

    def body(*refs):
        ins, outs = refs[:n], refs[n:2 * n]
        send_sems, recv_sems = refs[2 * n], refs[2 * n + 1]
        x, y, c, _ = _place()
        cps = []
        for t in range(n):
            if classes[t] == "R":
                h = ins[t].shape[1] // 2
                src = ins[t].at[:, _rows((1 - c) * h, h), :]
            else:
                h = ins[t].shape[0] // 2
                src = ins[t].at[_rows((1 - c) * h, h), :]
            cp = pltpu.make_async_remote_copy(src_ref=src, dst_ref=outs[t], send_sem=send_sems.at[t], recv_sem=recv_sems.at[t],
                                              device_id=(x, y, 1 - c), device_id_type=MESH)
            cp.start()
            cps.append(cp)
        for cp in cps:
            cp.wait()

    return pl.pallas_call(
        body, name=name, in_specs=[ANY] * n, out_specs=[ANY] * n,
        out_shape=[jax.ShapeDtypeStruct(shape_of(d, cls), d.dtype) for d, cls in zip(dws, classes)],
        scratch_shapes=[pltpu.SemaphoreType.DMA((n,)), pltpu.SemaphoreType.DMA((n,))],
    )(*dws)


def _rs_add(dw, got, cls, w, place, name):
    if cls == "R":
        _, r, _ = dw.shape
        h = r // 2
        th, tc = _pick(h, 256, 16), _pick(w, 1024, LANES)
        nbh = h // th
        grid = (N_CHIPS, nbh, w // tc)
        in_specs = [pl.BlockSpec((None, th, tc), lambda s, i, j, p: (s, p[1] * nbh + i, j)),
                    pl.BlockSpec((None, th, tc), lambda s, i, j, p: (s, i, j))]
        out_spec = pl.BlockSpec((None, th, tc), lambda s, i, j, p: (s, i, j))
        sem = ("parallel", "parallel", "parallel")

        def body(p_ref, a_ref, b_ref, o_ref):
            o_ref[...] = (a_ref[...] + b_ref[...]).astype(o_ref.dtype)
    elif cls == "C":
        r = dw.shape[0]
        h = r // 2
        th, tc = _pick(h, 256, 16), _pick(w, 1024, LANES)
        nbh, nbc = h // th, w // tc
        grid = (N_CHIPS, nbh, nbc)
        in_specs = [pl.BlockSpec((th, tc), lambda s, i, j, p: (p[1] * nbh + i, s * nbc + j)),
                    pl.BlockSpec((th, tc), lambda s, i, j, p: (i, s * nbc + j))]
        out_spec = pl.BlockSpec((None, th, tc), lambda s, i, j, p: (s, i, j))
        sem = ("parallel", "parallel", "parallel")

        def body(p_ref, a_ref, b_ref, o_ref):
            o_ref[...] = (a_ref[...] + b_ref[...]).astype(o_ref.dtype)
    else:
        r, wp = dw.shape
        h = r // 2
        th = _pick(h, 64, 16)
        nbh = h // th
        grid = (nbh,)
        in_specs = [pl.BlockSpec((th, wp), lambda i, p: (p[1] * nbh + i, 0)), pl.BlockSpec((th, wp), lambda i, p: (i, 0))]
        out_spec = pl.BlockSpec((N_CHIPS, th, w), lambda i, p: (0, i, 0))
        sem = ("parallel",)

        def body(p_ref, a_ref, b_ref, o_ref):
            tot = a_ref[...] + b_ref[...]
            for s in range(N_CHIPS):
                o_ref[s] = tot[:, s * w:(s + 1) * w].astype(o_ref.dtype)

    gs = pltpu.PrefetchScalarGridSpec(num_scalar_prefetch=1, grid=grid, in_specs=in_specs, out_specs=out_spec)
    return pl.pallas_call(body, name=name, grid_spec=gs, out_shape=jax.ShapeDtypeStruct((N_CHIPS, h, w), BF16),
                          compiler_params=_cparams(sem))(place, dw, got)


def _rs_chip(zs, name):
    n = len(zs)

    def body(*refs):
        ins, outs = refs[:n], refs[n:2 * n]
        send_sems, recv_sems = refs[2 * n], refs[2 * n + 1]
        x, y, c, chips = _place()
        cps = []
        for t in range(n):
            for k, (cx, cy) in enumerate(chips):
                cp = pltpu.make_async_remote_copy(src_ref=ins[t].at[2 * cx + cy], dst_ref=outs[t].at[k], send_sem=send_sems.at[3 * t + k],
                                                  recv_sem=recv_sems.at[3 * t + k], device_id=(cx, cy, c), device_id_type=MESH)
                cp.start()
                cps.append(cp)
        for cp in cps:
            cp.wait()

    return pl.pallas_call(
        body, name=name, in_specs=[ANY] * n, out_specs=[ANY] * n,
        out_shape=[jax.ShapeDtypeStruct((3,) + z.shape[1:], z.dtype) for z in zs],
        scratch_shapes=[pltpu.SemaphoreType.DMA((3 * n,)), pltpu.SemaphoreType.DMA((3 * n,))],
    )(*zs)


def _rs_sum(z, parts, place, name):
    _, h, w = z.shape
    th = _pick(h, 256, 16)
    tc = _pick(w, 1024, LANES) if w % LANES == 0 else w
    nbh = h // th

    def body(p_ref, z_ref, k_ref, o_ref):
        acc = z_ref[...].astype(F32)
        for k in range(3):
            acc = acc + k_ref[k].astype(F32)
        o_ref[...] = acc

    gs = pltpu.PrefetchScalarGridSpec(
        num_scalar_prefetch=1, grid=(nbh, w // tc),
        in_specs=[pl.BlockSpec((None, th, tc), lambda i, j, p: (p[0], i, j)), pl.BlockSpec((3, th, tc), lambda i, j, p: (0, i, j))],
        out_specs=pl.BlockSpec((th, tc), lambda i, j, p: (p[1] * nbh + i, j)))
    return pl.pallas_call(body, name=name, grid_spec=gs, out_shape=jax.ShapeDtypeStruct((2 * h, w), F32),
                          compiler_params=_cparams(("parallel", "parallel")))(place, z, parts)


def _rs_join(bufs, name):
    n = len(bufs)

    def body(*refs):
        outs = refs[n:2 * n]
        send_sems, recv_sems = refs[2 * n], refs[2 * n + 1]
        x, y, c, _ = _place()
        cps = []
        for t in range(n):
            h = outs[t].shape[0] // 2
            mine = outs[t].at[_rows(c * h, h), :]
            cp = pltpu.make_async_remote_copy(src_ref=mine, dst_ref=mine, send_sem=send_sems.at[t], recv_sem=recv_sems.at[t],
                                              device_id=(x, y, 1 - c), device_id_type=MESH)
            cp.start()
            cps.append(cp)
        for t in range(n):
            h = outs[t].shape[0] // 2
            other = outs[t].at[_rows((1 - c) * h, h), :]
            pltpu.make_async_remote_copy(src_ref=other, dst_ref=other, send_sem=send_sems.at[t], recv_sem=recv_sems.at[t],
                                         device_id=(x, y, c), device_id_type=MESH).wait_recv()
        for cp in cps:
            cp.wait_send()

    return pl.pallas_call(
        body, name=name, in_specs=[ANY] * n, out_specs=[ANY] * n,
        out_shape=[jax.ShapeDtypeStruct(b.shape, b.dtype) for b in bufs],
        input_output_aliases={t: t for t in range(n)},
        scratch_shapes=[pltpu.SemaphoreType.DMA((n,)), pltpu.SemaphoreType.DMA((n,))],
    )(*bufs)


def _reduce_scatter_layer(dws, classes, widths, place, tag):
    gots = _rs_pair(dws, classes, f"rs_pair_{tag}")
    zs = [_rs_add(d, g, cls, w, place, f"rs_add_{tag}_{t}") for t, (d, g, cls, w) in enumerate(zip(dws, gots, classes, widths))]
    parts = _rs_chip(zs, f"rs_chip_{tag}")
    halves = [_rs_sum(z, p, place, f"rs_sum_{tag}_{t}") for t, (z, p) in enumerate(zip(zs, parts))]
    return _rs_join(halves, f"rs_join_{tag}")


def _gather_shards(mine, name):
    r, w = mine.shape
    rh = r // 2

    def body(mine_ref, out_ref, send_sems, recv_sems, local_sem):
        x, y, c, chips = _place()
        me = 2 * x + y
        half = _rows(c * rh, rh)
        other = _rows((1 - c) * rh, rh)

        def copy(k, src, chip, rows, to):
            return pltpu.make_async_remote_copy(src_ref=src, dst_ref=out_ref.at[chip, rows], send_sem=send_sems.at[k],
                                                recv_sem=recv_sems.at[k], device_id=to, device_id_type=MESH)

        local = pltpu.make_async_copy(mine_ref, out_ref.at[me], local_sem)
        local.start()
        sends = [copy(k, mine_ref.at[half], me, half, (cx, cy, c)) for k, (cx, cy) in enumerate(chips)]
        for s in sends:
            s.start()
        passed = []
        for k, (cx, cy) in enumerate(chips):
            chip = 2 * cx + cy
            copy(k, mine_ref.at[half], chip, half, (x, y, c)).wait_recv()
            fwd = copy(3 + k, out_ref.at[chip, half], chip, half, (x, y, 1 - c))
            fwd.start()
            passed.append(fwd)
        for k, (cx, cy) in enumerate(chips):
            copy(3 + k, mine_ref.at[half], 2 * cx + cy, other, (x, y, c)).wait_recv()
        for s in sends + passed:
            s.wait_send()
        local.wait()

    return pl.pallas_call(
        body, name=name, in_specs=[ANY], out_specs=ANY, out_shape=jax.ShapeDtypeStruct((N_CHIPS, r, w), mine.dtype),
        scratch_shapes=[pltpu.SemaphoreType.DMA((6,)), pltpu.SemaphoreType.DMA((6,)), pltpu.SemaphoreType.DMA],
    )(mine)


def _sum_slots(slots, name):
    n, r, w = slots.shape
    tr = _pick(r, 256, SUBLANES)

    def body(s_ref, o_ref):
        acc = s_ref[0]
        for k in range(1, n):
            acc = acc + s_ref[k]
        o_ref[...] = acc

    return pl.pallas_call(
        body, name=name, grid=(r // tr,), in_specs=[pl.BlockSpec((n, tr, w), lambda i: (0, i, 0))],
        out_specs=pl.BlockSpec((tr, w), lambda i: (i, 0)), out_shape=jax.ShapeDtypeStruct((r, w), F32),
        compiler_params=_cparams(("parallel",)),
    )(slots)


def _all_to_all_sum(flat, name):
    r, w = flat.shape

    def body(f_ref, out_ref, send_sems, recv_sems, local_sem):
        x, y, c, _ = _place()
        me = 4 * x + 2 * y + c
        local = pltpu.make_async_copy(f_ref, out_ref.at[me], local_sem)
        local.start()
        sends = []
        for k in range(1, 8):
            peer = (x ^ (k >> 2), y ^ ((k >> 1) & 1), c ^ (k & 1))
            s = pltpu.make_async_remote_copy(src_ref=f_ref, dst_ref=out_ref.at[me], send_sem=send_sems.at[k - 1],
                                             recv_sem=recv_sems.at[k - 1], device_id=peer, device_id_type=MESH)
            s.start()
            sends.append(s)
        for k in range(1, 8):
            peer_slot = 4 * (x ^ (k >> 2)) + 2 * (y ^ ((k >> 1) & 1)) + (c ^ (k & 1))
            pltpu.make_async_remote_copy(src_ref=f_ref, dst_ref=out_ref.at[peer_slot], send_sem=send_sems.at[k - 1],
                                         recv_sem=recv_sems.at[k - 1], device_id=(x, y, c), device_id_type=MESH).wait_recv()
        for s in sends:
            s.wait_send()
        local.wait()

    slots = pl.pallas_call(
        body, name=name, in_specs=[ANY], out_specs=ANY, out_shape=jax.ShapeDtypeStruct((8, r, w), flat.dtype),
        scratch_shapes=[pltpu.SemaphoreType.DMA((7,)), pltpu.SemaphoreType.DMA((7,)), pltpu.SemaphoreType.DMA],
    )(flat)
    return _sum_slots(slots, name + "_sum")


def _pack(pieces, lead):
    flat = []
    n_lead = len(lead)
    for p in pieces:
        f = p.reshape(*lead, -1)
        pad = (-f.shape[-1]) % PACK_COLS
        if pad:
            f = jnp.pad(f, [(0, 0)] * n_lead + [(0, pad)])
        flat.append(f)
    f = jnp.concatenate(flat, axis=-1) if len(flat) > 1 else flat[0]
    pad = (-f.shape[-1]) % (32 * PACK_COLS)
    if pad:
        f = jnp.pad(f, [(0, 0)] * n_lead + [(0, pad)])
    return f.reshape(*lead, -1, PACK_COLS)


def _unpack(buf, shapes, lead):
    f = buf.reshape(*lead, -1)
    out, off = [], 0
    for shp in shapes:
        n = 1
        for s in shp:
            n *= s
        out.append(f[..., off:off + n].reshape(*lead, *shp))
        off += n + ((-n) % PACK_COLS)
    return out


def _join(g, axis):
    g = jnp.moveaxis(g, 0, axis)
    return g.reshape(*g.shape[:axis], g.shape[axis] * g.shape[axis + 1], *g.shape[axis + 2:])


def _split(full, axis):
    shp = full.shape
    g = full.reshape(*shp[:axis], N_CHIPS, shp[axis] // N_CHIPS, *shp[axis + 1:])
    return jnp.moveaxis(g, axis, 0)


def _norm_bwd(xs, dres, dhs, gain, name):
    def fn(x, dr, *rest):
        dh = rest[0]
        for d in rest[1:-1]:
            dh = dh + d
        _, vjp = jax.vjp(_rms, x, rest[-1])
        dx, dg = vjp(dh.astype(F32))
        return dr + dx, dg
    return _rowmap(fn, [xs, dres] + list(dhs), [gain], name, n_acc=1)


def _ffn_fwd(x1, h2, wg, wu, conv_w, conv_b, wd, seq, tag):
    gpre = _matmul(h2, wg, "nn", F32, f"ffn_gate_{tag}")
    up = _matmul(h2, wu, "nn", F32, f"ffn_up_{tag}")
    gate = _conv_fwd(gpre, gpre.shape[1], conv_w, conv_b, seq, f"ffn_conv_{tag}")
    act, = _rowmap(lambda g, u: (_silu(g) * u).astype(BF16), [gate, up], [], f"ffn_act_{tag}", tc=_pick(gate.shape[1], 1024, LANES))
    f = _matmul(act, wd, "nn", F32, f"ffn_down_{tag}")
    return f, (gpre, up, gate, act)


def _ffn_bwd(dx2, h2, saved, wg, wu, conv_w, wd, seq, tag):
    gpre, up, gate, act = saved
    da = _matmul(dx2, wd, "nt", F32, f"ffn_dact_{tag}")
    d_wd = _matmul(act, dx2, "tn", F32, f"ffn_dwd_{tag}")

    def act_bwd(g, u, d):
        _, vjp = jax.vjp(lambda g_, u_: _silu(g_) * u_, g, u)
        return vjp(d)
    dgate, dup = _rowmap(act_bwd, [gate, up, da], [], f"ffn_dactfn_{tag}", tc=_pick(gate.shape[1], 1024, LANES))
    dgpre, d_cw, d_cb = _conv_bwd(dgate, gpre, gpre.shape[1], conv_w, seq, True, f"ffn_dconv_{tag}")
    dh_a = _matmul(dgpre, wg, "nt", F32, f"ffn_dh_gate_{tag}")
    dh_b = _matmul(dup, wu, "nt", F32, f"ffn_dh_up_{tag}")
    d_wg = _matmul(h2, dgpre, "tn", F32, f"ffn_dwg_{tag}")
    d_wu = _matmul(h2, dup, "tn", F32, f"ffn_dwu_{tag}")
    return (dh_a, dh_b), dict(ffn_w_gate=d_wg, ffn_w_up=d_wu, ffn_conv_w=d_cw, ffn_conv_b=d_cb.reshape(-1), ffn_w_down=d_wd)


def _gmlp_fwd(h, w, tag):
    a = w["a_w_out"].shape[0]
    p = _matmul(h, w["a_w_in"], "nn", F32, f"a_in_{tag}")
    b_in, vnorm = w["a_b_in"].reshape(1, -1), w["a_v_norm"].reshape(1, -1)

    def fn(p_, b_, g_):
        hh = _gelu_tanh(p_ + b_)
        return hh[:, :a], _rms(hh[:, a:], g_)
    u, vn = _rowmap(fn, [p], [b_in, vnorm], f"a_gelu_{tag}")
    b_col = w["a_b_s"][:, :, None]
    y = _sgu_fwd(vn, u, w["a_w_s"], b_col, f"a_sgu_{tag}")
    m = _matmul(y, w["a_w_out"], "nn", F32, f"a_out_{tag}")
    return m, (p, u, vn, y, fn, b_in, vnorm, b_col)


def _gmlp_bwd(dm, h, saved, w, tag):
    p, u, vn, y, fn, b_in, vnorm, b_col = saved
    dy = _matmul(dm, w["a_w_out"], "nt", F32, f"a_dy_{tag}")
    d_wout = _matmul(y, dm, "tn", F32, f"a_dwout_{tag}")
    dvn, du, d_ws, d_bcol = _sgu_bwd(vn, u, dy, w["a_w_s"], b_col, f"a_dsgu_{tag}")

    def bwd(p_, du_, dvn_, b_, g_):
        _, vjp = jax.vjp(fn, p_, b_, g_)
        return vjp((du_, dvn_))
    dp, d_bin, d_vnorm = _rowmap(bwd, [p, du, dvn], [b_in, vnorm], f"a_dgelu_{tag}", n_acc=2)
    dh = _matmul(dp, w["a_w_in"], "nt", F32, f"a_dh_{tag}")
    d_win = _matmul(h, dp, "tn", F32, f"a_dwin_{tag}")
    return (dh,), dict(a_w_in=d_win, a_b_in=d_bin.reshape(-1), a_v_norm=d_vnorm.reshape(-1), a_w_s=d_ws, a_b_s=d_bcol[:, :, 0], a_w_out=d_wout)


def _fox_fwd_mixer(h, w, bsz, seq, tag):
    d = h.shape[1]
    nh = d // HEAD
    win = w["b_w_in"]
    wp = win.shape[1]
    proj = _matmul(h, win, "nn", F32, f"b_in_{tag}")
    gq, gk = w["b_q_norm"].reshape(1, HEAD), w["b_k_norm"].reshape(1, HEAD)
    bf = jnp.pad(w["b_b_f"].reshape(1, nh), ((0, 0), (0, LANES - nh)))

    def prep(pq, pk, pfl, gq_, gk_, bf_):
        qn = jnp.concatenate([_rms(x, gq_) for x in _heads(pq)], axis=1)
        kn = jnp.concatenate([_rms(x, gk_) for x in _heads(pk)], axis=1)
        return qn, kn, -_softplus(-(pfl + bf_))
    wins = [_win(proj, d, 0), _win(proj, d, 1), _win(proj, LANES, 4 * d // LANES)]

    def prep_fwd(pq, pk, pfl, gq_, gk_, bf_):
        qn, kn, lf = prep(pq, pk, pfl, gq_, gk_, bf_)
        return qn.astype(BF16), kn.astype(BF16), lf
    qn, kn, lf = _rowmap(prep_fwd, wins, [gq, gk, bf], f"b_prep_{tag}")
    cs = _cumsum_rows(lf, seq, False, f"b_cumsum_{tag}")
    c_rows = cs[:, :nh].reshape(bsz, seq, nh).transpose(0, 2, 1)[:, :, None, :]
    o, lse = _fox_fwd(qn, kn, proj, 2 * nh, c_rows, bsz, seq, nh, f"b_attn_{tag}")
    og = _win(proj, d, 3)
    y, = _rowmap(lambda o_, g_: (o_ * jax.nn.sigmoid(g_)).astype(BF16), [o, og], [], f"b_gate_{tag}")
    m = _matmul(y, w["b_w_out"], "nn", F32, f"b_out_{tag}")
    return m, (proj, qn, kn, c_rows, o, lse, y, prep, wins, (gq, gk, bf), wp)


def _fox_bwd_mixer(dm, h, saved, w, bsz, seq, tag):
    proj, qn, kn, c_rows, o, lse, y, prep, wins, (gq, gk, bf), wp = saved
    d = h.shape[1]
    nh = d // HEAD
    dy = _matmul(dm, w["b_w_out"], "nt", F32, f"b_dy_{tag}")
    d_wout = _matmul(y, dm, "tn", F32, f"b_dwout_{tag}")
    og = _win(proj, d, 3)

    def gate_bwd(o_, g_, dy_):
        _, vjp = jax.vjp(lambda a, b: a * jax.nn.sigmoid(b), o_, g_)
        return vjp(dy_)
    do, dog = _rowmap(gate_bwd, [o, og, dy], [], f"b_dgate_{tag}")
    dqn, delta = _fox_bwd_q(qn, kn, proj, 2 * nh, do, lse, c_rows, bsz, seq, nh, f"b_dattn_q_{tag}")
    dkn, dv, dc_rows = _fox_bwd_kv(qn, kn, proj, 2 * nh, do, lse, delta, c_rows, bsz, seq, nh, f"b_dattn_kv_{tag}")
    dc = dc_rows[:, :, 0, :].transpose(0, 2, 1).reshape(bsz * seq, nh)
    dc = jnp.pad(dc, ((0, 0), (0, LANES - nh)))
    dlf = _cumsum_rows(dc, seq, True, f"b_dcumsum_{tag}")
    extra = wp - (4 * d + LANES)

    def prep_bwd(pq, pk, pfl, dqn_, dkn_, dv_, dog_, dlf_, gq_, gk_, bf_):
        _, vjp = jax.vjp(prep, pq, pk, pfl, gq_, gk_, bf_)
        dpq, dpk, dpfl, dgq, dgk, dbf = vjp((dqn_, dkn_, dlf_))
        parts = [dpq, dpk, dv_, dog_, dpfl]
        if extra:
            parts.append(jnp.zeros((pq.shape[0], extra), F32))
        return jnp.concatenate(parts, axis=1), dgq, dgk, dbf
    dproj, d_gq, d_gk, d_bf = _rowmap(prep_bwd, wins + [dqn, dkn, dv, dog, dlf], [gq, gk, bf], f"b_dprep_{tag}", n_acc=3)
    dh = _matmul(dproj, w["b_w_in"], "nt", F32, f"b_dh_{tag}")
    d_win = _matmul(h, dproj, "tn", F32, f"b_dwin_{tag}")
    return (dh,), dict(b_w_in=d_win, b_b_f=d_bf[0, :nh], b_q_norm=d_gq.reshape(-1), b_k_norm=d_gk.reshape(-1), b_w_out=d_wout)


def _gdn_fwd_mixer(h, w, bsz, seq, tag):
    d = h.shape[1]
    nkh = d // HEAD
    nvh = 2 * nkh
    dqkv = (2 * nkh + nvh) * HEAD
    dz = nvh * HEAD
    nch = seq // GDN_CHUNK
    proj = _matmul(h, w["c_w_in"], "nn", F32, f"c_in_{tag}")
    conv = _conv_fwd(proj, dqkv, w["c_conv_w"], None, seq, f"c_conv_{tag}")

    def rows_of(cols):
        return cols.reshape(bsz, nch, GDN_CHUNK, nvh).transpose(3, 0, 1, 2)[:, :, :, None, :]
    b_rows = rows_of(proj[:, dqkv + dz:dqkv + dz + nvh])
    a_rows = rows_of(proj[:, dqkv + dz + nvh:dqkv + dz + 2 * nvh])
    alog, dtb = w["c_a_log"].reshape(nvh, 1, 1), w["c_dt_bias"].reshape(nvh, 1, 1)
    o, states = _gdn_fwd(conv, b_rows, a_rows, alog, dtb, bsz, seq, nvh, f"c_core_{tag}")
    gn = w["c_out_norm"].reshape(1, HEAD)
    zwin = _win(proj, dz, dqkv // dz)

    def outfn(o_, z_, g_):
        return jnp.concatenate([_rms(a, g_) * _silu(b) for a, b in zip(_heads(o_), _heads(z_))], axis=1)
    y, = _rowmap(lambda o_, z_, g_: outfn(o_, z_, g_).astype(BF16), [o, zwin], [gn], f"c_outnorm_{tag}")
    m = _matmul(y, w["c_w_out"], "nn", F32, f"c_out_{tag}")
    return m, (proj, conv, b_rows, a_rows, alog, dtb, o, states, y, gn, zwin, outfn)


def _gdn_bwd_mixer(dm, h, saved, w, bsz, seq, tag):
    proj, conv, b_rows, a_rows, alog, dtb, o, states, y, gn, zwin, outfn = saved
    d = h.shape[1]
    nkh = d // HEAD
    nvh = 2 * nkh
    dk_, dv_ = nkh * HEAD, nvh * HEAD
    dqkv = 2 * dk_ + dv_
    dz = dv_
    wp = proj.shape[1]
    dy = _matmul(dm, w["c_w_out"], "nt", F32, f"c_dy_{tag}")
    d_wout = _matmul(y, dm, "tn", F32, f"c_dwout_{tag}")

    def out_bwd(o_, z_, dy_, g_):
        _, vjp = jax.vjp(outfn, o_, z_, g_)
        return vjp(dy_)
    do, dzz, d_gn = _rowmap(out_bwd, [o, zwin, dy], [gn], f"c_doutnorm_{tag}", n_acc=1)
    dq, dk, dv, db_rows, da_rows, d_alog, d_dtb = _gdn_bwd(conv, b_rows, a_rows, alog, dtb, states, do, bsz, seq, nvh, f"c_dcore_{tag}")
    cw = w["c_conv_w"]
    dq_pre, d_cwq = _conv_bwd(dq, proj, dk_, cw[:, :dk_], seq, False, f"c_dconv_q_{tag}", xcol=0)
    dk_pre, d_cwk = _conv_bwd(dk, proj, dk_, cw[:, dk_:2 * dk_], seq, False, f"c_dconv_k_{tag}", xcol=dk_)
    dv_pre, d_cwv = _conv_bwd(dv, proj, dv_, cw[:, 2 * dk_:], seq, False, f"c_dconv_v_{tag}", xcol=2 * dk_)
    d_cw = jnp.concatenate([d_cwq, d_cwk, d_cwv], axis=1)

    def cols_of(rows):
        return rows[:, :, :, 0, :].transpose(1, 2, 3, 0).reshape(bsz * seq, nvh)
    dba = jnp.concatenate([cols_of(db_rows), cols_of(da_rows)], axis=1)
    dba = jnp.pad(dba, ((0, 0), (0, wp - dqkv - dz - 2 * nvh)))
    dproj, = _rowmap(lambda *parts: jnp.concatenate(parts, axis=1), [dq_pre, dk_pre, dv_pre, dzz, dba], [], f"c_dproj_{tag}")
    dh = _matmul(dproj, w["c_w_in"], "nt", F32, f"c_dh_{tag}")
    d_win = _matmul(h, dproj, "tn", F32, f"c_dwin_{tag}")
    return (dh,), dict(c_w_in=d_win, c_conv_w=d_cw, c_a_log=d_alog.reshape(-1), c_dt_bias=d_dtb.reshape(-1),
                       c_out_norm=d_gn.reshape(-1), c_w_out=d_wout)


_MIXER_FWD = (lambda h, w, bsz, seq, tag: _gmlp_fwd(h, w, tag), _fox_fwd_mixer, _gdn_fwd_mixer)
_MIXER_BWD = (lambda dm, h, s, w, bsz, seq, tag: _gmlp_bwd(dm, h, s, w, tag), _fox_bwd_mixer, _gdn_bwd_mixer)


def _local_step(x, target, layers, bsz, seq):
    t, d = x.shape
    depth = len(layers)
    saved = []
    m_prev = None
    xin = x
    for i, w in enumerate(layers):
        tag = f"l{i}"
        g_mix, g_ffn = w["norm_mix"].reshape(1, d), w["norm_ffn"].reshape(1, d)
        if i == 0:
            h, = _rowmap(lambda x_, g_: _rms(x_, g_).astype(BF16), [xin], [g_mix], f"norm_mix_{tag}")
            xl = xin
        else:
            xl, h = _rowmap(lambda x_, m_, g_: (x_ + m_, _rms(x_ + m_, g_).astype(BF16)), [xin, m_prev], [g_mix], f"norm_mix_{tag}")
        m, msaved = _MIXER_FWD[i % 3](h, w, bsz, seq, tag)
        x1, h2 = _rowmap(lambda x_, m_, g_: (x_ + m_, _rms(x_ + m_, g_).astype(BF16)), [xl, m], [g_ffn], f"norm_ffn_{tag}")
        f, fsaved = _ffn_fwd(x1, h2, w["ffn_w_gate"], w["ffn_w_up"], w["ffn_conv_w"], w["ffn_conv_b"].reshape(1, -1), w["ffn_w_down"], seq, tag)
        saved.append((xl, h, msaved, x1, h2, fsaved))
        xin, m_prev = x1, f

    def loss_fn(x_, f_, tg_):
        e = x_ + f_ - tg_
        return e * (1.0 / d), jnp.full((1, LANES), (0.5 / d) * jnp.sum(e * e), F32)
    dx, loss_acc = _rowmap(loss_fn, [xin, m_prev, target], [], "loss", n_acc=1)
    loss = loss_acc[0, 0]

    grads = [None] * depth
    for i in reversed(range(depth)):
        w = layers[i]
        tag = f"l{i}"
        xl, h, msaved, x1, h2, fsaved = saved[i]
        g_mix, g_ffn = w["norm_mix"].reshape(1, d), w["norm_ffn"].reshape(1, d)
        dhs, gw = _ffn_bwd(dx, h2, fsaved, w["ffn_w_gate"], w["ffn_w_up"], w["ffn_conv_w"], w["ffn_w_down"], seq, tag)
        dx1, d_gffn = _norm_bwd(x1, dx, dhs, g_ffn, f"dnorm_ffn_{tag}")
        dhs, gm = _MIXER_BWD[i % 3](dx1, h, msaved, w, bsz, seq, tag)
        dx, d_gmix = _norm_bwd(xl, dx1, dhs, g_mix, f"dnorm_mix_{tag}")
        gw.update(gm)
        gw["norm_mix"], gw["norm_ffn"] = d_gmix.reshape(-1), d_gffn.reshape(-1)
        grads[i] = gw
    return loss, dx, grads


def _adamw_math(w_, g_, m_, v_):
    m_new = ADAM_B1 * m_ + (1.0 - ADAM_B1) * g_
    v_new = ADAM_B2 * v_ + (1.0 - ADAM_B2) * (g_ * g_)
    m_hat = m_new / (1.0 - ADAM_B1 ** ADAM_STEP)
    v_hat = v_new / (1.0 - ADAM_B2 ** ADAM_STEP)
    delta = -ADAM_LR * (m_hat / (jnp.sqrt(v_hat) + ADAM_EPS) + ADAM_WD * w_)
    return delta, m_new, v_new


def _adamw(w, g, m, v, name):
    shape = w.shape
    if w.ndim == 1:
        w, g, m, v = (a.reshape(1, -1) for a in (w, g, m, v))
    return [o.reshape(shape) for o in _elementwise(_adamw_math, [w, g, m, v], 3, name)]


def _adamw_layers(w, gs, m, v, name):
    nl, r, c = w.shape
    tr = _pick(r, max(SUBLANES, (1 << 19) // c // SUBLANES * SUBLANES), SUBLANES)

    def body(*refs):
        w_ref, m_ref, v_ref = refs[:3]
        g_refs = refs[3:3 + nl]
        go_ref, d_ref, mo_ref, vo_ref = refs[3 + nl:]
        layer = pl.program_id(0)
        for k in range(nl):
            @pl.when(layer == k)
            def _(k=k):
                g = g_refs[k][...]
                delta, m_new, v_new = _adamw_math(w_ref[...], g, m_ref[...], v_ref[...])
                go_ref[...] = g
                d_ref[...] = delta
                mo_ref[...] = m_new
                vo_ref[...] = v_new

    st = pl.BlockSpec((None, tr, c), lambda l, i: (l, i, 0))
    g_specs = [pl.BlockSpec((tr, c), functools.partial(lambda l, i, k: (jnp.where(l == k, i, 0), 0), k=k)) for k in range(nl)]
    return pl.pallas_call(
        body, name=name, grid=(nl, r // tr), in_specs=[st, st, st] + g_specs, out_specs=[st] * 4,
        out_shape=[jax.ShapeDtypeStruct(w.shape, F32)] * 4, compiler_params=_cparams(("parallel", "parallel")),
    )(w, m, v, *gs)


WEIGHTS = ['norm_mix', 'norm_ffn', 'ffn_w_gate', 'ffn_w_up', 'ffn_conv_w', 'ffn_conv_b', 'ffn_w_down', 'a_w_in', 'a_b_in', 'a_v_norm',
           'a_w_s', 'a_b_s', 'a_w_out', 'b_w_in', 'b_b_f', 'b_q_norm', 'b_k_norm', 'b_w_out', 'c_w_in', 'c_conv_w', 'c_a_log',
           'c_dt_bias', 'c_out_norm', 'c_w_out']
BIG = {'ffn_w_gate': 1, 'ffn_w_up': 1, 'ffn_w_down': 0, 'a_w_in': 1, 'a_w_out': 0, 'b_w_in': 1, 'b_w_out': 0, 'c_w_in': 1, 'c_w_out': 0}
SMALL_SHARDED = {'ffn_conv_w': 1, 'a_b_in': 0, 'a_v_norm': 0, 'c_conv_w': 1}
MIXER_NAMES = (('a_w_in', 'a_b_in', 'a_v_norm', 'a_w_s', 'a_b_s', 'a_w_out'), ('b_w_in', 'b_b_f', 'b_q_norm', 'b_k_norm', 'b_w_out'),
               ('c_w_in', 'c_conv_w', 'c_a_log', 'c_dt_bias', 'c_out_norm', 'c_w_out'))
FFN_NAMES = ('norm_mix', 'norm_ffn', 'ffn_w_gate', 'ffn_w_up', 'ffn_conv_w', 'ffn_conv_b', 'ffn_w_down')


def _layer_entries(depth):
    out = []
    for i in range(depth):
        kind, j = i % 3, i // 3
        out.append([(n, i) for n in FFN_NAMES] + [(n, j) for n in MIXER_NAMES[kind]])
    return out


def _layout(name, shard_shape):
    if BIG[name] == 0:
        return "R"
    return "C" if shard_shape[-1] % LANES == 0 else "U"


def _train_step(x, target, params, moments_m, moments_v):
    bsz, seq, d = x.shape
    depth = params['norm_mix'].shape[0]
    entries = _layer_entries(depth)
    place = _place_scalars()

    small_list = [(n, j) for n in SMALL_SHARDED for j in range(params[n].shape[0])]
    small_buf = _gather_shards(_pack([params[n][j] for n, j in small_list], ()), "gather_small")
    small_full = {}
    for (n, j), g in zip(small_list, _unpack(small_buf, [params[n][j].shape for n, j in small_list], (N_CHIPS,))):
        small_full[(n, j)] = _join(g, SMALL_SHARDED[n])

    layers, big_of = [], []
    for i, ent in enumerate(entries):
        big = [(n, j, _layout(n, params[n].shape[1:])) for n, j in ent if n in BIG]
        bufs = [_cast_window(params[n], j, cls, place, f"cast_{n}_l{i}") for n, j, cls in big]
        bufs = _gather_layer(bufs, [cls for _, _, cls in big], f"gather_l{i}")
        w = {}
        for (n, j, cls), buf in zip(big, bufs):
            w[n] = _assemble(buf, _pad_cols(N_CHIPS * buf.shape[2]), f"assemble_{n}_l{i}") if cls == "U" else buf
        for n, j in ent:
            if n in SMALL_SHARDED:
                w[n] = small_full[(n, j)]
            elif n not in BIG:
                w[n] = params[n][j]
        layers.append(w)
        big_of.append(big)

    loss_local, dx, grads = _local_step(x.reshape(bsz * seq, d), target.reshape(bsz * seq, d), layers, bsz, seq)
    loss = lax.psum(loss_local, ("x", "y", "c"))

    total = {}
    for i, big in enumerate(big_of):
        dws, classes, widths = [], [], []
        for n, j, cls in big:
            g = grads[i][n]
            shard = params[n].shape[1:]
            dws.append(g.reshape(N_CHIPS, shard[0], shard[1]) if cls == "R" else g)
            classes.append(cls)
            widths.append(shard[1])
        for (n, j, _), red in zip(big, _reduce_scatter_layer(dws, classes, widths, place, f"l{i}")):
            total[(n, j)] = red
    layer_of = {(n, j): i for i, ent in enumerate(entries) for n, j in ent}
    packed = _pack([_split(grads[layer_of[(n, j)]][n], SMALL_SHARDED[n]) for n, j in small_list], (N_CHIPS,))
    red = _reduce_scatter_layer([packed], ["R"], [PACK_COLS], place, "small")[0]
    for (n, j), g in zip(small_list, _unpack(red, [params[n][j].shape for n, j in small_list], ())):
        total[(n, j)] = g
    repl = [(n, j) for n in WEIGHTS if n not in BIG and n not in SMALL_SHARDED for j in range(params[n].shape[0])]
    flat = jnp.concatenate([grads[layer_of[k]][k[0]].reshape(-1) for k in repl])
    n_flat = flat.shape[0]
    flat = jnp.pad(flat, (0, (-n_flat) % (SUBLANES * LANES))).reshape(-1, LANES)
    flat = _all_to_all_sum(flat, "allreduce_small").reshape(-1)
    off = 0
    for k in repl:
        shp = params[k[0]][k[1]].shape
        n = 1
        for s in shp:
            n *= s
        total[k] = flat[off:off + n].reshape(shp)
        off += n

    grad_w, delta_w, new_m, new_v = {}, {}, {}, {}
    for n in WEIGHTS:
        nl = params[n].shape[0]
        if n in BIG:
            grad_w[n], delta_w[n], new_m[n], new_v[n] = _adamw_layers(params[n], [total[(n, j)] for j in range(nl)], moments_m[n], moments_v[n], f"adamw_{n}")
        else:
            g = jnp.stack([total[(n, j)] for j in range(nl)])
            grad_w[n] = g
            delta_w[n], new_m[n], new_v[n] = _adamw(params[n], g, moments_m[n], moments_v[n], f"adamw_{n}")
    return (loss, dx.reshape(bsz, seq, d), *[grad_w[n] for n in WEIGHTS], *[delta_w[n] for n in WEIGHTS],
            *[new_m[n] for n in WEIGHTS], *[new_v[n] for n in WEIGHTS])


def kernel(x, norm_mix, norm_ffn, ffn_w_gate, ffn_w_up, ffn_conv_w, ffn_conv_b, ffn_w_down, a_w_in, a_b_in, a_v_norm, a_w_s, a_b_s, a_w_out, b_w_in, b_b_f, b_q_norm, b_k_norm, b_w_out, c_w_in, c_conv_w, c_a_log, c_dt_bias, c_out_norm, c_w_out, loss_target, m_norm_mix, m_norm_ffn, m_ffn_w_gate, m_ffn_w_up, m_ffn_conv_w, m_ffn_conv_b, m_ffn_w_down, m_a_w_in, m_a_b_in, m_a_v_norm, m_a_w_s, m_a_b_s, m_a_w_out, m_b_w_in, m_b_b_f, m_b_q_norm, m_b_k_norm, m_b_w_out, m_c_w_in, m_c_conv_w, m_c_a_log, m_c_dt_bias, m_c_out_norm, m_c_w_out, v_norm_mix, v_norm_ffn, v_ffn_w_gate, v_ffn_w_up, v_ffn_conv_w, v_ffn_conv_b, v_ffn_w_down, v_a_w_in, v_a_b_in, v_a_v_norm, v_a_w_s, v_a_b_s, v_a_w_out, v_b_w_in, v_b_b_f, v_b_q_norm, v_b_k_norm, v_b_w_out, v_c_w_in, v_c_conv_w, v_c_a_log, v_c_dt_bias, v_c_out_norm, v_c_w_out):
    given = dict(locals())
    params = {n: given[n] for n in WEIGHTS}
    moments_m = {n: given["m_" + n] for n in WEIGHTS}
    moments_v = {n: given["v_" + n] for n in WEIGHTS}
    return _train_step(x, loss_target, params, moments_m, moments_v)
```

```python
import functools

import jax
import jax.numpy as jnp
from jax import lax
from jax.experimental import pallas as pl
from jax.experimental.pallas import tpu as pltpu

F32 = jnp.float32
BF16 = jnp.bfloat16
HI = lax.Precision.HIGHEST
MESH = pl.DeviceIdType.MESH

RMS_EPS = 1e-6
ADAM_LR, ADAM_B1, ADAM_B2, ADAM_EPS, ADAM_WD, ADAM_STEP = 0.001, 0.9, 0.999, 1e-08, 0.01, 10
A_CHUNK, HEAD, GDN_CHUNK = 128, 128, 64
LANES, SUBLANES = 128, 8
PACK_COLS = 1024
N_CHIPS = 4
VMEM_LIMIT = 56 * 1024 * 1024
ROWMAP_BUDGET = 20 * 1024 * 1024

NN = (((1,), (0,)), ((), ()))
NT = (((1,), (1,)), ((), ()))
TN = (((0,), (0,)), ((), ()))
BNN = (((2,), (1,)), ((0,), (0,)))
BNT = (((2,), (2,)), ((0,), (0,)))
BTN = (((1,), (1,)), ((0,), (0,)))


def _pick(n, cap, mult):
    if n <= cap:
        return n
    best = None
    for d in range(mult, cap + 1, mult):
        if n % d == 0:
            best = d
    if best is None:
        raise ValueError(f"no tile for {n} (cap {cap}, multiple of {mult})")
    return best


def _pad_cols(n):
    j = -(-n // LANES)
    while not (j <= 8 or any(j % d == 0 for d in (4, 5, 6, 7, 8))):
        j += 1
    return j * LANES


def _cparams(sem):
    return pltpu.CompilerParams(dimension_semantics=sem, vmem_limit_bytes=VMEM_LIMIT)


def _matmul(a, b, kind, out_dtype, name, job=None):
    if kind == "nn":
        (m, k), (k2, n) = a.shape, b.shape
    elif kind == "nt":
        (m, k), (n, k2) = a.shape, b.shape
    else:
        (k, m), (k2, n) = a.shape, b.shape
    assert k == k2, (name, a.shape, b.shape)
    tm, tn, tk = _pick(m, 1024, LANES), _pick(n, 1024, LANES), _pick(k, 2048, LANES)
    ni, nj, nk = m // tm, n // tn, k // tk
    dims = {"nn": NN, "nt": NT, "tn": TN}[kind]
    a_spec = pl.BlockSpec((tk, tm), lambda i, j, kk: (kk, i)) if kind == "tn" else pl.BlockSpec((tm, tk), lambda i, j, kk: (i, kk))
    b_spec = pl.BlockSpec((tn, tk), lambda i, j, kk: (j, kk)) if kind == "nt" else pl.BlockSpec((tk, tn), lambda i, j, kk: (kk, j))
    n_jin, n_jout = (len(job["ins"]), len(job["out_shapes"])) if job else (0, 0)

    def body(a_ref, b_ref, *rest):
        jins, o_ref, jouts = rest[:n_jin], rest[n_jin], rest[n_jin + 1:n_jin + 1 + n_jout]
        scratch = rest[n_jin + 1 + n_jout:]
        i, j, kk = pl.program_id(0), pl.program_id(1), pl.program_id(2)
        if job:
            first = jnp.logical_and(jnp.logical_and(i == 0, j == 0), kk == 0)
            job["start"](jins, jouts, scratch[-2], scratch[-1], first)
        prod = lax.dot_general(a_ref[...].astype(BF16), b_ref[...].astype(BF16), dims, preferred_element_type=F32)
        if nk == 1:
            o_ref[...] = prod.astype(o_ref.dtype)
        else:
            acc_ref = scratch[0]

            @pl.when(kk == 0)
            def _():
                acc_ref[...] = prod

            @pl.when(kk > 0)
            def _():
                acc_ref[...] += prod

            @pl.when(kk == nk - 1)
            def _():
                o_ref[...] = acc_ref[...].astype(o_ref.dtype)
        if job:
            last = jnp.logical_and(jnp.logical_and(i == ni - 1, j == nj - 1), kk == nk - 1)
            job["finish"](jins, jouts, scratch[-2], scratch[-1], last)

    scratch_shapes = [pltpu.VMEM((tm, tn), F32)] if nk > 1 else []
    out_specs = pl.BlockSpec((tm, tn), lambda i, j, kk: (i, j))
    out_shape = jax.ShapeDtypeStruct((m, n), out_dtype)
    if not job:
        return pl.pallas_call(
            body, name=name, grid=(ni, nj, nk), in_specs=[a_spec, b_spec], out_specs=out_specs, out_shape=out_shape,
            scratch_shapes=scratch_shapes, compiler_params=_cparams(("parallel", "parallel", "arbitrary")),
        )(a, b)
    scratch_shapes += [pltpu.SemaphoreType.DMA((job["n_sems"],)), pltpu.SemaphoreType.DMA((job["n_sems"],))]
    res = pl.pallas_call(
        body, name=name, grid=(ni, nj, nk), in_specs=[a_spec, b_spec] + [ANY] * n_jin,
        out_specs=[out_specs] + [ANY] * n_jout, out_shape=[out_shape] + list(job["out_shapes"]),
        input_output_aliases={2 + t: 1 + t for t in range(n_jin)} if job["alias"] else {},
        scratch_shapes=scratch_shapes, compiler_params=_cparams(("arbitrary", "arbitrary", "arbitrary")),
    )(a, b, *job["ins"])
    return res[0], list(res[1:])


def _win(arr, width=None, blk=0):
    return (arr, arr.shape[1] if width is None else width, blk)


def _rowmap(fn, rows, params, name, n_acc=0, tc=None, col_params=()):
    rows = [r if isinstance(r, tuple) else _win(r) for r in rows]
    t = rows[0][0].shape[0]
    widths = [tc if tc is not None else w for (_, w, _) in rows]

    def blocks_for(tr):
        rb = [jax.ShapeDtypeStruct((tr, w), a.dtype) for (a, _, _), w in zip(rows, widths)]
        pb = [jax.ShapeDtypeStruct((p.shape[0], tc) if (i in col_params) else p.shape, p.dtype) for i, p in enumerate(params)]
        return rb, pb

    rb, pb = blocks_for(SUBLANES * 2)
    outs = jax.eval_shape(fn, *rb, *pb)
    outs = list(outs) if isinstance(outs, (tuple, list)) else [outs]
    n_row = len(outs) - n_acc
    row_bytes = sum(w * a.dtype.itemsize for (a, _, _), w in zip(rows, widths)) + sum(o.shape[1] * o.dtype.itemsize for o in outs[:n_row])
    tr = 16
    while tr * 2 <= 512 and t % (tr * 2) == 0 and (tr * 2) * row_bytes * 5 <= ROWMAP_BUDGET:
        tr *= 2
    rb, pb = blocks_for(tr)
    outs = jax.eval_shape(fn, *rb, *pb)
    outs = list(outs) if isinstance(outs, (tuple, list)) else [outs]
    n_in = len(rows) + len(params)

    if tc is None:
        grid = (t // tr,)
        row_axis = 0
        in_specs = [pl.BlockSpec((tr, w), functools.partial(lambda i, b: (i, b), b=blk)) for (_, w, blk) in rows]
        in_specs += [pl.BlockSpec(p.shape, functools.partial(lambda i, nd: (0,) * nd, nd=p.ndim)) for p in params]
        out_specs = [pl.BlockSpec((tr, o.shape[1]), lambda i: (i, 0)) for o in outs[:n_row]]
        out_specs += [pl.BlockSpec(o.shape, functools.partial(lambda i, nd: (0,) * nd, nd=len(o.shape))) for o in outs[n_row:]]
        out_shape = [jax.ShapeDtypeStruct((t, o.shape[1]), o.dtype) for o in outs[:n_row]]
        out_shape += [jax.ShapeDtypeStruct(o.shape, o.dtype) for o in outs[n_row:]]
        sem = ("arbitrary",) if n_acc else ("parallel",)
    else:
        wtot = rows[0][1]
        grid = (wtot // tc, t // tr)
        row_axis = 1
        in_specs = [pl.BlockSpec((tr, tc), functools.partial(lambda j, i, b: (i, j + b), b=blk)) for (_, _, blk) in rows]
        for i, p in enumerate(params):
            if i in col_params:
                in_specs.append(pl.BlockSpec((p.shape[0], tc), lambda j, i: (0, j)))
            else:
                in_specs.append(pl.BlockSpec(p.shape, functools.partial(lambda j, i, nd: (0,) * nd, nd=p.ndim)))
        out_specs = [pl.BlockSpec((tr, tc), lambda j, i: (i, j)) for _ in outs[:n_row]]
        out_specs += [pl.BlockSpec((o.shape[0], tc), lambda j, i: (0, j)) for o in outs[n_row:]]
        out_shape = [jax.ShapeDtypeStruct((t, wtot), o.dtype) for o in outs[:n_row]]
        out_shape += [jax.ShapeDtypeStruct((o.shape[0], wtot), o.dtype) for o in outs[n_row:]]
        sem = ("parallel", "arbitrary") if n_acc else ("parallel", "parallel")

    def body(*refs):
        ins, ors = refs[:n_in], refs[n_in:]
        res = fn(*[r[...] for r in ins])
        res = list(res) if isinstance(res, (tuple, list)) else [res]
        for o, r in zip(ors[:n_row], res[:n_row]):
            o[...] = r.astype(o.dtype)
        if n_acc:
            i = pl.program_id(row_axis)
            for o, r in zip(ors[n_row:], res[n_row:]):
                @pl.when(i == 0)
                def _(o=o, r=r):
                    o[...] = r.astype(o.dtype)

                @pl.when(i > 0)
                def _(o=o, r=r):
                    o[...] += r.astype(o.dtype)

    res = pl.pallas_call(
        body, name=name, grid=grid, in_specs=in_specs, out_specs=out_specs, out_shape=out_shape,
        compiler_params=_cparams(sem),
    )(*[a for (a, _, _) in rows], *params)
    return res


def _elementwise(fn, arrays, n_out, name):
    shape = arrays[0].shape
    cols = shape[-1]
    rws = 1
    for s in shape[:-1]:
        rws *= s
    arrs = [a.reshape(rws, cols) for a in arrays]
    per_row = cols * 4 * (len(arrays) + n_out) * 3
    tr = rws
    if rws * per_row > ROWMAP_BUDGET:
        tr = _pick(rws, max(SUBLANES, ROWMAP_BUDGET // per_row), SUBLANES)

    def body(*refs):
        res = fn(*[r[...] for r in refs[:len(arrs)]])
        for o, r in zip(refs[len(arrs):], res):
            o[...] = r

    spec = pl.BlockSpec((tr, cols), lambda i: (i, 0))
    outs = pl.pallas_call(
        body, name=name, grid=(rws // tr,), in_specs=[spec] * len(arrs), out_specs=[spec] * n_out,
        out_shape=[jax.ShapeDtypeStruct((rws, cols), F32)] * n_out, compiler_params=_cparams(("parallel",)),
    )(*arrs)
    return [o.reshape(shape) for o in outs]


def _rms(x, g):
    return x * lax.rsqrt(jnp.mean(x * x, axis=-1, keepdims=True) + RMS_EPS) * g


def _silu(x):
    return x * jax.nn.sigmoid(x)


def _softplus(x):
    return jnp.maximum(x, 0.0) + jnp.log(1.0 + jnp.exp(-jnp.abs(x)))


def _gelu_tanh(x):
    return 0.5 * x * (1.0 + jnp.tanh(0.7978845608028654 * (x + 0.044715 * (x * x * x))))


def _heads(x):
    return [x[:, h * HEAD:(h + 1) * HEAD] for h in range(x.shape[1] // HEAD)]


def _dot(a, b, dims=NN):
    return lax.dot_general(a, b, dims, precision=HI, preferred_element_type=F32)


def _bdot(a, b, dims):
    return lax.dot_general(a.astype(BF16), b.astype(BF16), dims, preferred_element_type=F32)


def _bdot3(a, b, dims):
    return lax.dot_general(a, b, dims, precision=lax.Precision.HIGH, preferred_element_type=F32)


def _eye(n):
    return (lax.broadcasted_iota(jnp.int32, (n, n), 0) == lax.broadcasted_iota(jnp.int32, (n, n), 1)).astype(F32)


def _tri(n):
    return lax.broadcasted_iota(jnp.int32, (n, n), 0) >= lax.broadcasted_iota(jnp.int32, (n, n), 1)


def _row_to_col(row):
    return jnp.sum(_eye(row.shape[1]) * row, axis=1, keepdims=True)


def _conv_tiles(w, seq):
    return _pick(seq, 512, SUBLANES), _pick(w, 512, LANES)


def _conv_fwd(x, width, w, bias, seq, name):
    t = x.shape[0]
    kk = w.shape[0]
    tr, tc = _conv_tiles(width, seq)
    hb = tr // SUBLANES

    def body(*refs):
        if bias is None:
            x_ref, h_ref, w_ref, o_ref = refs
        else:
            x_ref, h_ref, w_ref, b_ref, o_ref = refs
        i = pl.program_id(1)
        first = (i * tr) % seq == 0
        halo = jnp.where(first, 0.0, h_ref[...])
        xe = jnp.concatenate([halo, x_ref[...]], axis=0)
        wv = w_ref[...]
        acc = xe[SUBLANES:, :] * wv[kk - 1:kk, :]
        for s in range(1, kk):
            acc = acc + pltpu.roll(xe, s, 0)[SUBLANES:, :] * wv[kk - 1 - s:kk - s, :]
        if bias is not None:
            acc = acc + b_ref[...]
        o_ref[...] = acc

    in_specs = [pl.BlockSpec((tr, tc), lambda j, i: (i, j)),
                pl.BlockSpec((SUBLANES, tc), lambda j, i: (jnp.maximum(i * hb - 1, 0), j)),
                pl.BlockSpec((kk, tc), lambda j, i: (0, j))]
    ops = [x, x, w]
    if bias is not None:
        in_specs.append(pl.BlockSpec((1, tc), lambda j, i: (0, j)))
        ops.append(bias)
    return pl.pallas_call(
        body, name=name, grid=(width // tc, t // tr), in_specs=in_specs,
        out_specs=pl.BlockSpec((tr, tc), lambda j, i: (i, j)), out_shape=jax.ShapeDtypeStruct((t, width), F32),
        compiler_params=_cparams(("parallel", "parallel")),
    )(*ops)


def _conv_bwd(dy, x, width, w, seq, with_bias, name, xcol=0, dx_dtype=F32):
    t = x.shape[0]
    kk = w.shape[0]
    tr, tc = _conv_tiles(width, seq)
    hb = tr // SUBLANES
    n_halo_blocks = t // SUBLANES
    assert xcol % tc == 0
    xb = xcol // tc

    def body(dy_ref, dyn_ref, x_ref, xh_ref, w_ref, dx_ref, dw_ref, *rest):
        i = pl.program_id(1)
        first = (i * tr) % seq == 0
        last = ((i + 1) * tr) % seq == 0
        dyc = dy_ref[...]
        dye = jnp.concatenate([dyc, jnp.where(last, 0.0, dyn_ref[...])], axis=0)
        xe = jnp.concatenate([jnp.where(first, 0.0, xh_ref[...]), x_ref[...]], axis=0)
        wv = w_ref[...]
        dx = dyc * wv[kk - 1:kk, :]
        dws = [None] * kk
        dws[kk - 1] = jnp.sum(dyc * xe[SUBLANES:, :], axis=0, keepdims=True)
        for s in range(1, kk):
            dx = dx + pltpu.roll(dye, tr + SUBLANES - s, 0)[:tr, :] * wv[kk - 1 - s:kk - s, :]
            dws[kk - 1 - s] = jnp.sum(dyc * pltpu.roll(xe, s, 0)[SUBLANES:, :], axis=0, keepdims=True)
        dx_ref[...] = dx.astype(dx_ref.dtype)

        @pl.when(i == 0)
        def _():
            for j in range(kk):
                dw_ref[j:j + 1, :] = dws[j]
            if with_bias:
                rest[0][...] = jnp.sum(dyc, axis=0, keepdims=True)

        @pl.when(i > 0)
        def _():
            for j in range(kk):
                dw_ref[j:j + 1, :] += dws[j]
            if with_bias:
                rest[0][...] += jnp.sum(dyc, axis=0, keepdims=True)

    cur = pl.BlockSpec((tr, tc), lambda j, i: (i, j))
    in_specs = [cur, pl.BlockSpec((SUBLANES, tc), lambda j, i: (jnp.minimum((i + 1) * hb, n_halo_blocks - 1), j)),
                pl.BlockSpec((tr, tc), lambda j, i: (i, j + xb)),
                pl.BlockSpec((SUBLANES, tc), lambda j, i: (jnp.maximum(i * hb - 1, 0), j + xb)),
                pl.BlockSpec((kk, tc), lambda j, i: (0, j))]
    out_specs = [cur, pl.BlockSpec((kk, tc), lambda j, i: (0, j))]
    out_shape = [jax.ShapeDtypeStruct((t, width), dx_dtype), jax.ShapeDtypeStruct((kk, width), F32)]
    if with_bias:
        out_specs.append(pl.BlockSpec((1, tc), lambda j, i: (0, j)))
        out_shape.append(jax.ShapeDtypeStruct((1, width), F32))
    return pl.pallas_call(
        body, name=name, grid=(width // tc, t // tr), in_specs=in_specs, out_specs=out_specs, out_shape=out_shape,
        compiler_params=_cparams(("parallel", "arbitrary")),
    )(dy, dy, x, x, w)


def _cumsum_rows(x, seq, reverse, name):
    t, w = x.shape
    tb = _pick(seq, 256, SUBLANES)
    nb = seq // tb

    def pos(b, i):
        return (b * nb + (nb - 1 - i if reverse else i), 0)

    def body(x_ref, o_ref, carry):
        i = pl.program_id(1)

        @pl.when(i == 0)
        def _():
            carry[...] = jnp.zeros_like(carry)

        blk = x_ref[...]
        r = lax.broadcasted_iota(jnp.int32, (tb, tb), 0)
        c = lax.broadcasted_iota(jnp.int32, (tb, tb), 1)
        m = ((r <= c) if reverse else (r >= c)).astype(F32)
        o_ref[...] = _dot(m, blk) + carry[...]
        carry[...] += jnp.sum(blk, axis=0, keepdims=True)

    return pl.pallas_call(
        body, name=name, grid=(t // seq, nb), in_specs=[pl.BlockSpec((tb, w), pos)], out_specs=pl.BlockSpec((tb, w), pos),
        out_shape=jax.ShapeDtypeStruct((t, w), F32), scratch_shapes=[pltpu.VMEM((1, w), F32)],
        compiler_params=_cparams(("parallel", "arbitrary")),
    )(x)


def _sgu_fwd(vn, u, w_s, b_col, name):
    t, a = vn.shape
    g = a // HEAD

    def body(v_ref, u_ref, w_ref, b_ref, y_ref):
        tri = _tri(A_CHUNK)
        for gi in range(g):
            sl = slice(gi * HEAD, (gi + 1) * HEAD)
            wc = jnp.where(tri, w_ref[gi], 0.0)
            sv = _dot(wc, v_ref[:, sl]) + b_ref[gi]
            y_ref[:, sl] = (u_ref[:, sl] * sv).astype(y_ref.dtype)

    blk = pl.BlockSpec((A_CHUNK, a), lambda i: (i, 0))
    return pl.pallas_call(
        body, name=name, grid=(t // A_CHUNK,),
        in_specs=[blk, blk, pl.BlockSpec(w_s.shape, lambda i: (0, 0, 0)), pl.BlockSpec(b_col.shape, lambda i: (0, 0, 0))],
        out_specs=blk, out_shape=jax.ShapeDtypeStruct((t, a), BF16), compiler_params=_cparams(("parallel",)),
    )(vn, u, w_s, b_col)


def _sgu_bwd(vn, u, dy, w_s, b_col, name):
    t, a = vn.shape
    g = a // HEAD

    def body(v_ref, u_ref, dy_ref, w_ref, b_ref, dv_ref, du_ref, dw_ref, db_ref):
        i = pl.program_id(0)
        tri = _tri(A_CHUNK)
        for gi in range(g):
            sl = slice(gi * HEAD, (gi + 1) * HEAD)
            wc = jnp.where(tri, w_ref[gi], 0.0)
            v = v_ref[:, sl]
            sv = _dot(wc, v) + b_ref[gi]
            dyb = dy_ref[:, sl]
            du_ref[:, sl] = dyb * sv
            dsv = dyb * u_ref[:, sl]
            dv_ref[:, sl] = _dot(wc, dsv, TN)
            dw = jnp.where(tri, _dot(dsv, v, NT), 0.0)
            db = jnp.sum(dsv, axis=1, keepdims=True)

            @pl.when(i == 0)
            def _(gi=gi, dw=dw, db=db):
                dw_ref[gi] = dw
                db_ref[gi] = db

            @pl.when(i > 0)
            def _(gi=gi, dw=dw, db=db):
                dw_ref[gi] += dw
                db_ref[gi] += db

    blk = pl.BlockSpec((A_CHUNK, a), lambda i: (i, 0))
    wsp = pl.BlockSpec(w_s.shape, lambda i: (0, 0, 0))
    bsp = pl.BlockSpec(b_col.shape, lambda i: (0, 0, 0))
    return pl.pallas_call(
        body, name=name, grid=(t // A_CHUNK,), in_specs=[blk, blk, blk, wsp, bsp], out_specs=[blk, blk, wsp, bsp],
        out_shape=[jax.ShapeDtypeStruct((t, a), F32), jax.ShapeDtypeStruct((t, a), F32),
                   jax.ShapeDtypeStruct(w_s.shape, F32), jax.ShapeDtypeStruct(b_col.shape, F32)],
        compiler_params=_cparams(("arbitrary",)),
    )(vn, u, dy, w_s, b_col)


def _fox_scores(q, k, cq_row, ck_row, diag, scale):
    s = lax.dot_general(q.astype(BF16), k.astype(BF16), NT, preferred_element_type=F32) * scale
    s = s + _row_to_col(cq_row) - ck_row
    mask = jnp.logical_or(jnp.logical_not(diag), _tri(q.shape[0]))
    return s, mask


def _fox_fwd(qn, kn, proj, v_blk0, c_rows, bsz, seq, nh, name):
    t = qn.shape[0]
    tq = _pick(seq, 512, LANES)
    nq = seq // tq
    scale = HEAD ** -0.5

    def body(q_ref, k_ref, v_ref, cq_ref, ck_ref, o_ref, lse_ref, m_s, l_s, acc_s):
        i, j = pl.program_id(2), pl.program_id(3)

        @pl.when(j == 0)
        def _():
            m_s[...] = jnp.full_like(m_s, -jnp.inf)
            l_s[...] = jnp.zeros_like(l_s)
            acc_s[...] = jnp.zeros_like(acc_s)

        @pl.when(j <= i)
        def _():
            s, mask = _fox_scores(q_ref[...], k_ref[...], cq_ref[...], ck_ref[...], j == i, scale)
            s = jnp.where(mask, s, -jnp.inf)
            m_new = jnp.maximum(m_s[...], jnp.max(s, axis=1, keepdims=True))
            p = jnp.exp(s - m_new)
            alpha = jnp.exp(m_s[...] - m_new)
            l_s[...] = alpha * l_s[...] + jnp.sum(p, axis=1, keepdims=True)
            acc_s[...] = alpha * acc_s[...] + lax.dot_general(p.astype(BF16), v_ref[...].astype(BF16), NN, preferred_element_type=F32)
            m_s[...] = m_new

        @pl.when(j == nq - 1)
        def _():
            o_ref[...] = acc_s[...] / l_s[...]
            lse_ref[...] = jnp.broadcast_to(m_s[...] + jnp.log(l_s[...]), lse_ref.shape)

    qspec = pl.BlockSpec((tq, HEAD), lambda b, h, i, j: (b * nq + i, h))
    kspec = pl.BlockSpec((tq, HEAD), lambda b, h, i, j: (b * nq + jnp.minimum(i, j), h))
    vspec = pl.BlockSpec((tq, HEAD), lambda b, h, i, j: (b * nq + jnp.minimum(i, j), v_blk0 + h))
    cq = pl.BlockSpec((None, None, 1, tq), lambda b, h, i, j: (b, h, 0, i))
    ck = pl.BlockSpec((None, None, 1, tq), lambda b, h, i, j: (b, h, 0, jnp.minimum(i, j)))
    return pl.pallas_call(
        body, name=name, grid=(bsz, nh, nq, nq), in_specs=[qspec, kspec, vspec, cq, ck], out_specs=[qspec, qspec],
        out_shape=[jax.ShapeDtypeStruct((t, nh * HEAD), F32)] * 2,
        scratch_shapes=[pltpu.VMEM((tq, 1), F32), pltpu.VMEM((tq, 1), F32), pltpu.VMEM((tq, HEAD), F32)],
        compiler_params=_cparams(("parallel", "parallel", "parallel", "arbitrary")),
    )(qn, kn, proj, c_rows, c_rows)


def _fox_p_dp(q, k, v, do, lse, cq_row, ck_row, diag, scale):
    s, mask = _fox_scores(q, k, cq_row, ck_row, diag, scale)
    p = jnp.where(mask, jnp.exp(s - jnp.max(lse, axis=1, keepdims=True)), 0.0)
    dp = lax.dot_general(do.astype(BF16), v.astype(BF16), NT, preferred_element_type=F32)
    return p, dp


def _fox_bwd_q(qn, kn, proj, v_blk0, do, lse, c_rows, bsz, seq, nh, name):
    t = qn.shape[0]
    tq = _pick(seq, 512, LANES)
    nq = seq // tq
    scale = HEAD ** -0.5

    def key_block(jj):
        return jnp.where(jj >= nq, jj - nq, jj)

    def body(q_ref, k_ref, v_ref, do_ref, lse_ref, cq_ref, ck_ref, dq_ref, dl_ref, dq_s, dl_s):
        i, jj = pl.program_id(2), pl.program_id(3)
        j = key_block(jj)

        @pl.when(jj == 0)
        def _():
            dq_s[...] = jnp.zeros_like(dq_s)
            dl_s[...] = jnp.zeros_like(dl_s)

        @pl.when(j <= i)
        def _():
            p, dp = _fox_p_dp(q_ref[...], k_ref[...], v_ref[...], do_ref[...], lse_ref[...], cq_ref[...], ck_ref[...], j == i, scale)

            @pl.when(jj < nq)
            def _():
                dl_s[...] += jnp.sum(p * dp, axis=1, keepdims=True)

            @pl.when(jj >= nq)
            def _():
                ds = p * (dp - dl_s[...])
                dq_s[...] += lax.dot_general(ds.astype(BF16), k_ref[...].astype(BF16), NN, preferred_element_type=F32) * scale

        @pl.when(jj == 2 * nq - 1)
        def _():
            dq_ref[...] = dq_s[...]
            dl_ref[...] = jnp.broadcast_to(dl_s[...], dl_ref.shape)

    qspec = pl.BlockSpec((tq, HEAD), lambda b, h, i, jj: (b * nq + i, h))
    kspec = pl.BlockSpec((tq, HEAD), lambda b, h, i, jj: (b * nq + jnp.minimum(i, key_block(jj)), h))
    vspec = pl.BlockSpec((tq, HEAD), lambda b, h, i, jj: (b * nq + jnp.minimum(i, key_block(jj)), v_blk0 + h))
    cq = pl.BlockSpec((None, None, 1, tq), lambda b, h, i, jj: (b, h, 0, i))
    ck = pl.BlockSpec((None, None, 1, tq), lambda b, h, i, jj: (b, h, 0, jnp.minimum(i, key_block(jj))))
    return pl.pallas_call(
        body, name=name, grid=(bsz, nh, nq, 2 * nq), in_specs=[qspec, kspec, vspec, qspec, qspec, cq, ck],
        out_specs=[qspec, qspec], out_shape=[jax.ShapeDtypeStruct((t, nh * HEAD), F32)] * 2,
        scratch_shapes=[pltpu.VMEM((tq, HEAD), F32), pltpu.VMEM((tq, 1), F32)],
        compiler_params=_cparams(("parallel", "parallel", "parallel", "arbitrary")),
    )(qn, kn, proj, do, lse, c_rows, c_rows)


def _fox_bwd_kv(qn, kn, proj, v_blk0, do, lse, delta, c_rows, bsz, seq, nh, name):
    t = qn.shape[0]
    tq = _pick(seq, 512, LANES)
    nq = seq // tq
    scale = HEAD ** -0.5

    def body(q_ref, k_ref, v_ref, do_ref, lse_ref, dl_ref, cq_ref, ck_ref, dk_ref, dv_ref, dc_ref, dk_s, dv_s, dc_s):
        j, i = pl.program_id(2), pl.program_id(3)

        @pl.when(i == 0)
        def _():
            dk_s[...] = jnp.zeros_like(dk_s)
            dv_s[...] = jnp.zeros_like(dv_s)
            dc_s[...] = jnp.zeros_like(dc_s)

        @pl.when(i >= j)
        def _():
            p, dp = _fox_p_dp(q_ref[...], k_ref[...], v_ref[...], do_ref[...], lse_ref[...], cq_ref[...], ck_ref[...], j == i, scale)
            ds = p * (dp - jnp.max(dl_ref[...], axis=1, keepdims=True))
            dv_s[...] += lax.dot_general(p.astype(BF16), do_ref[...].astype(BF16), TN, preferred_element_type=F32)
            dk_s[...] += lax.dot_general(ds.astype(BF16), q_ref[...].astype(BF16), TN, preferred_element_type=F32) * scale
            dc_s[...] -= jnp.sum(ds, axis=0, keepdims=True)

        @pl.when(i == nq - 1)
        def _():
            dk_ref[...] = dk_s[...]
            dv_ref[...] = dv_s[...]
            dc_ref[...] = dc_s[...]

    kspec = pl.BlockSpec((tq, HEAD), lambda b, h, j, i: (b * nq + j, h))
    vspec = pl.BlockSpec((tq, HEAD), lambda b, h, j, i: (b * nq + j, v_blk0 + h))
    qspec = pl.BlockSpec((tq, HEAD), lambda b, h, j, i: (b * nq + jnp.maximum(i, j), h))
    cq = pl.BlockSpec((None, None, 1, tq), lambda b, h, j, i: (b, h, 0, jnp.maximum(i, j)))
    ck = pl.BlockSpec((None, None, 1, tq), lambda b, h, j, i: (b, h, 0, j))
    return pl.pallas_call(
        body, name=name, grid=(bsz, nh, nq, nq), in_specs=[qspec, kspec, vspec, qspec, qspec, qspec, cq, ck],
        out_specs=[kspec, kspec, ck],
        out_shape=[jax.ShapeDtypeStruct((t, nh * HEAD), F32)] * 2 + [jax.ShapeDtypeStruct(c_rows.shape, F32)],
        scratch_shapes=[pltpu.VMEM((tq, HEAD), F32), pltpu.VMEM((tq, HEAD), F32), pltpu.VMEM((1, tq), F32)],
        compiler_params=_cparams(("parallel", "parallel", "parallel", "arbitrary")),
    )(qn, kn, proj, do, lse, delta, c_rows, c_rows)


def _gdn_chunk(qp, kp, vp, b_row, a_row, alog, dtb, state):
    hv = vp.shape[0]
    c = qp.shape[1]
    qc, kc, vc = _silu(qp), _silu(kp), _silu(vp)
    qh = qc * lax.rsqrt(jnp.sum(qc * qc, -1, keepdims=True) + RMS_EPS) * (HEAD ** -0.5)
    kh = kc * lax.rsqrt(jnp.sum(kc * kc, -1, keepdims=True) + RMS_EPS)
    q = jnp.stack([qh[h // 2] for h in range(hv)])
    k = jnp.stack([kh[h // 2] for h in range(hv)])
    beta_row = jax.nn.sigmoid(b_row)
    g_row = -jnp.exp(alog) * _softplus(a_row + dtb)
    ri = lax.broadcasted_iota(jnp.int32, (c, c), 0)
    ci = lax.broadcasted_iota(jnp.int32, (c, c), 1)
    eye = (ri == ci).astype(F32)
    tri = ri >= ci
    beta_col = jnp.sum(eye * beta_row, axis=2, keepdims=True)
    g_col = jnp.sum(eye * g_row, axis=2, keepdims=True)
    gc_col = jnp.sum(tri.astype(F32) * g_row, axis=2, keepdims=True)
    gc_row = jnp.sum(g_col * (ri <= ci).astype(F32), axis=1, keepdims=True)
    decay = jnp.where(tri, jnp.exp(jnp.where(tri, gc_col - gc_row, 0.0)), 0.0)
    kb = k * beta_col
    a_mat = jnp.where(ri > ci, _bdot(kb, k, BNT) * decay, 0.0)
    egc = jnp.exp(gc_col)
    inv = eye - a_mat
    pw = _bdot3(a_mat, a_mat, BNN)
    n_sq = max(1, (c - 1).bit_length() - 1)
    for it in range(n_sq):
        inv = inv + _bdot3(inv, pw, BNN)
        if it < n_sq - 1:
            pw = _bdot3(pw, pw, BNN)
    u = _bdot(inv, vc * beta_col, BNN)
    w = _bdot(inv, kb * egc, BNN)
    attn = _bdot(q, k, BNT) * decay
    g_last = jnp.sum(g_row, axis=2, keepdims=True)
    v_new = u - _bdot(w, state, BNN)
    o = _bdot(q * egc, state, BNN) + _bdot(attn, v_new, BNN)
    new_state = state * jnp.exp(g_last) + _bdot(k * jnp.exp(g_last - gc_col), v_new, BTN)
    return o, new_state


def _gdn_group(nvh):
    return 4 if nvh % 4 == 0 else 2


def _gdn_specs(nch, nkh, hb, rev):
    def n_of(n):
        return nch - 1 - n if rev else n

    hk = hb // 2
    per_k = pl.BlockSpec((GDN_CHUNK, hk * HEAD), lambda g, b, n: (b * nch + n_of(n), g))
    q = per_k
    k = pl.BlockSpec((GDN_CHUNK, hk * HEAD), lambda g, b, n: (b * nch + n_of(n), nkh // hk + g))
    v = pl.BlockSpec((GDN_CHUNK, hb * HEAD), lambda g, b, n: (b * nch + n_of(n), 2 * nkh // hb + g))
    per_v = pl.BlockSpec((GDN_CHUNK, hb * HEAD), lambda g, b, n: (b * nch + n_of(n), g))
    row = pl.BlockSpec((hb, None, None, 1, GDN_CHUNK), lambda g, b, n: (g, b, n_of(n), 0, 0))
    sc = pl.BlockSpec((hb, 1, 1), lambda g, b, n: (g, 0, 0))
    st = pl.BlockSpec((hb, None, None, HEAD, HEAD), lambda g, b, n: (g, b, n_of(n), 0, 0))
    return q, k, v, per_k, per_v, row, sc, st


def _stack_heads(ref, n):
    return jnp.stack([ref[:, h * HEAD:(h + 1) * HEAD] for h in range(n)])


def _gdn_fwd(conv, b_rows, a_rows, alog, dtb, bsz, seq, nvh, name):
    t = conv.shape[0]
    nch = seq // GDN_CHUNK
    nkh = nvh // 2
    hb = _gdn_group(nvh)
    q, k, v, _, per_v, row, sc, st = _gdn_specs(nch, nkh, hb, False)

    def body(q_ref, k_ref, v_ref, b_ref, a_ref, al_ref, dt_ref, o_ref, st_ref, state):
        @pl.when(pl.program_id(2) == 0)
        def _():
            state[...] = jnp.zeros_like(state)

        st_ref[...] = state[...]
        o, new_state = _gdn_chunk(_stack_heads(q_ref, hb // 2), _stack_heads(k_ref, hb // 2), _stack_heads(v_ref, hb),
                                  b_ref[...], a_ref[...], al_ref[...], dt_ref[...], state[...])
        for h in range(hb):
            o_ref[:, h * HEAD:(h + 1) * HEAD] = o[h]
        state[...] = new_state

    return pl.pallas_call(
        body, name=name, grid=(nvh // hb, bsz, nch), in_specs=[q, k, v, row, row, sc, sc], out_specs=[per_v, st],
        out_shape=[jax.ShapeDtypeStruct((t, nvh * HEAD), F32), jax.ShapeDtypeStruct((nvh, bsz, nch, HEAD, HEAD), F32)],
        scratch_shapes=[pltpu.VMEM((hb, HEAD, HEAD), F32)],
        compiler_params=_cparams(("parallel", "parallel", "arbitrary")),
    )(conv, conv, conv, b_rows, a_rows, alog, dtb)


def _gdn_bwd(conv, b_rows, a_rows, alog, dtb, states, do, bsz, seq, nvh, name):
    t = conv.shape[0]
    nch = seq // GDN_CHUNK
    nkh = nvh // 2
    hb = _gdn_group(nvh)
    q, k, v, per_k, per_v, row, sc, st = _gdn_specs(nch, nkh, hb, True)

    def body(q_ref, k_ref, v_ref, b_ref, a_ref, al_ref, dt_ref, st_ref, do_ref,
             dq_ref, dk_ref, dv_ref, db_ref, da_ref, dal_ref, ddt_ref, dstate):
        b, n = pl.program_id(1), pl.program_id(2)

        @pl.when(n == 0)
        def _():
            dstate[...] = jnp.zeros_like(dstate)

        _, vjp = jax.vjp(_gdn_chunk, _stack_heads(q_ref, hb // 2), _stack_heads(k_ref, hb // 2), _stack_heads(v_ref, hb),
                         b_ref[...], a_ref[...], al_ref[...], dt_ref[...], st_ref[...])
        dq, dk, dv, db, da, dal, ddt, dst = vjp((_stack_heads(do_ref, hb), dstate[...]))
        for h in range(hb // 2):
            dq_ref[:, h * HEAD:(h + 1) * HEAD] = dq[h]
            dk_ref[:, h * HEAD:(h + 1) * HEAD] = dk[h]
        for h in range(hb):
            dv_ref[:, h * HEAD:(h + 1) * HEAD] = dv[h]
        db_ref[...] = db
        da_ref[...] = da
        dstate[...] = dst
        start = jnp.logical_and(b == 0, n == 0)

        @pl.when(start)
        def _():
            dal_ref[...] = dal
            ddt_ref[...] = ddt

        @pl.when(jnp.logical_not(start))
        def _():
            dal_ref[...] += dal
            ddt_ref[...] += ddt

    f = lambda *s: jax.ShapeDtypeStruct(s, F32)
    return pl.pallas_call(
        body, name=name, grid=(nvh // hb, bsz, nch), in_specs=[q, k, v, row, row, sc, sc, st, per_v],
        out_specs=[per_k, per_k, per_v, row, row, sc, sc],
        out_shape=[f(t, nkh * HEAD), f(t, nkh * HEAD), f(t, nvh * HEAD), f(*b_rows.shape), f(*a_rows.shape), f(nvh, 1, 1), f(nvh, 1, 1)],
        scratch_shapes=[pltpu.VMEM((hb, HEAD, HEAD), F32)],
        compiler_params=_cparams(("arbitrary", "arbitrary", "arbitrary")),
    )(conv, conv, conv, b_rows, a_rows, alog, dtb, states, do)


ANY = pl.BlockSpec(memory_space=pl.ANY)
FLIPS = (2, 1, 3)


def _place():
    x, y, c = lax.axis_index("x"), lax.axis_index("y"), lax.axis_index("c")
    chips = [(1 - x, y), (x, 1 - y), (1 - x, 1 - y)]
    return x, y, c, chips


def _chip_index():
    return 2 * lax.axis_index("x") + lax.axis_index("y")


def _core_index():
    return lax.axis_index("c")


def _rows(start, size, mult=16):
    return pl.ds(pl.multiple_of(start, mult), size)


def _cast_window(w_stack, layer, cls, name):
    _, r, w = w_stack.shape
    if cls == "U":
        tr = _pick(r, 256, 16)
        grid = (r // tr, 1)
        in_spec = pl.BlockSpec((None, tr, w), lambda i, j: (layer, i, 0))
        out_spec = pl.BlockSpec((None, tr, w), lambda i, j: (_chip_index(), i, 0))
        out_shape = (N_CHIPS, r, w)
    else:
        tr, tc = _pick(r, 512, 16), _pick(w, 1024, LANES)
        nbr, nbc = r // tr, w // tc
        grid = (nbr, nbc)
        in_spec = pl.BlockSpec((None, tr, tc), lambda i, j: (layer, i, j))
        if cls == "C":
            out_spec = pl.BlockSpec((tr, tc), lambda i, j: (i, _chip_index() * nbc + j))
            out_shape = (r, N_CHIPS * w)
        else:
            out_spec = pl.BlockSpec((tr, tc), lambda i, j: (_chip_index() * nbr + i, j))
            out_shape = (N_CHIPS * r, w)

    def body(x_ref, o_ref):
        o_ref[...] = x_ref[...].astype(o_ref.dtype)

    return pl.pallas_call(body, name=name, grid=grid, in_specs=[in_spec], out_specs=out_spec,
                          out_shape=jax.ShapeDtypeStruct(out_shape, BF16), compiler_params=_cparams(("parallel", "parallel")))(w_stack)


def _halved(buf, cls):
    return {"C": buf.shape[0], "U": buf.shape[1], "R": buf.shape[0] // N_CHIPS}[cls]


def _part(buf, cls, chip, start, size):
    if cls == "C":
        w = buf.shape[1] // N_CHIPS
        return buf.at[_rows(start, size), pl.ds(chip * w, w)]
    if cls == "U":
        return buf.at[chip, _rows(start, size), :]
    r = buf.shape[0] // N_CHIPS
    return buf.at[_rows(chip * r + start, size), :]


def _gather_layer(bufs, classes, name):
    n = len(bufs)
    job = _gather_job(bufs, classes)

    def body(*refs):
        outs = refs[n:2 * n]
        job["start"](refs[:n], outs, refs[2 * n], refs[2 * n + 1], True)
        job["finish"](refs[:n], outs, refs[2 * n], refs[2 * n + 1], True)

    return pl.pallas_call(
        body, name=name, in_specs=[ANY] * n, out_specs=[ANY] * n, out_shape=job["out_shapes"],
        input_output_aliases={t: t for t in range(n)},
        scratch_shapes=[pltpu.SemaphoreType.DMA((job["n_sems"],)), pltpu.SemaphoreType.DMA((job["n_sems"],))],
    )(*bufs)


def _on_chip(when, fn):
    x, y, _, _ = _place()
    me = 2 * x + y
    for s in range(N_CHIPS):
        cond = (me == s) if when is True else jnp.logical_and(when, me == s)
        pl.when(cond)(functools.partial(fn, s))


def _gather_job(bufs, classes):
    n = len(bufs)

    def copy(outs, send_sems, recv_sems, t, k, chip, start, to):
        size = _halved(outs[t], classes[t]) // 2
        part = _part(outs[t], classes[t], chip, start, size)
        return pltpu.make_async_remote_copy(src_ref=part, dst_ref=part, send_sem=send_sems.at[6 * t + k],
                                            recv_sem=recv_sems.at[6 * t + k], device_id=to, device_id_type=MESH)

    def halves(outs):
        return [_halved(outs[t], classes[t]) // 2 for t in range(n)]

    def start(ins, outs, send_sems, recv_sems, when):
        x, y, c, chips = _place()

        def run(s_me):
            for t, half in enumerate(halves(outs)):
                for k, (cx, cy) in enumerate(chips):
                    copy(outs, send_sems, recv_sems, t, k, s_me, c * half, (cx, cy, c)).start()
        _on_chip(when, run)

    def finish(ins, outs, send_sems, recv_sems, when):
        x, y, c, chips = _place()

        def run(s_me):
            passed = []
            for t, half in enumerate(halves(outs)):
                for k in range(3):
                    copy(outs, send_sems, recv_sems, t, k, s_me ^ FLIPS[k], c * half, (x, y, c)).wait_recv()
                    fwd = copy(outs, send_sems, recv_sems, t, 3 + k, s_me ^ FLIPS[k], c * half, (x, y, 1 - c))
                    fwd.start()
                    passed.append(fwd)
            for t, half in enumerate(halves(outs)):
                for k in range(3):
                    copy(outs, send_sems, recv_sems, t, 3 + k, s_me ^ FLIPS[k], (1 - c) * half, (x, y, c)).wait_recv()
            for t, half in enumerate(halves(outs)):
                for k, (cx, cy) in enumerate(chips):
                    copy(outs, send_sems, recv_sems, t, k, s_me, c * half, (cx, cy, c)).wait_send()
            for fwd in passed:
                fwd.wait_send()
        _on_chip(when, run)

    return dict(ins=list(bufs), alias=True, n_sems=6 * n, start=start, finish=finish,
                out_shapes=[jax.ShapeDtypeStruct(b.shape, b.dtype) for b in bufs])


def _assemble(slots, width, name):
    _, r, w = slots.shape
    tr = _pick(r, 256, 16)

    def body(s_ref, o_ref):
        parts = [s_ref[s] for s in range(N_CHIPS)]
        if width > N_CHIPS * w:
            parts.append(jnp.zeros((tr, width - N_CHIPS * w), slots.dtype))
        o_ref[...] = jnp.concatenate(parts, axis=1)

    return pl.pallas_call(
        body, name=name, grid=(r // tr,), in_specs=[pl.BlockSpec((N_CHIPS, tr, w), lambda i: (0, i, 0))],
        out_specs=pl.BlockSpec((tr, width), lambda i: (i, 0)), out_shape=jax.ShapeDtypeStruct((r, width), slots.dtype),
        compiler_params=_cparams(("parallel",)),
    )(slots)


def _rs_pair(dws, classes, name):
    n = len(dws)

    def shape_of(d, cls):
        return (N_CHIPS, d.shape[1] // 2, d.shape[2]) if cls == "R" else (d.shape[0] // 2, d.shape[1])

    def body(*refs):
        ins, outs = refs[:n], refs[n:2 * n]
        send_sems, recv_sems = refs[2 * n], refs[2 * n + 1]
        x, y, c, _ = _place()
        cps = []
        for t in range(n):
            if classes[t] == "R":
                h = ins[t].shape[1] // 2
                src = ins[t].at[:, _rows((1 - c) * h, h), :]
            else:
                h = ins[t].shape[0] // 2
                src = ins[t].at[_rows((1 - c) * h, h), :]
            cp = pltpu.make_async_remote_copy(src_ref=src, dst_ref=outs[t], send_sem=send_sems.at[t], recv_sem=recv_sems.at[t],
                                              device_id=(x, y, 1 - c), device_id_type=MESH)
            cp.start()
            cps.append(cp)
        for cp in cps:
            cp.wait()

    return pl.pallas_call(
        body, name=name, in_specs=[ANY] * n, out_specs=[ANY] * n,
        out_shape=[jax.ShapeDtypeStruct(shape_of(d, cls), d.dtype) for d, cls in zip(dws, classes)],
        scratch_shapes=[pltpu.SemaphoreType.DMA((n,)), pltpu.SemaphoreType.DMA((n,))],
    )(*dws)


def _rs_add(dw, got, cls, w, name):
    if cls == "R":
        _, r, _ = dw.shape
        h = r // 2
        th, tc = _pick(h, 256, 16), _pick(w, 1024, LANES)
        nbh = h // th
        grid = (N_CHIPS, nbh, w // tc)
        in_specs = [pl.BlockSpec((None, th, tc), lambda s, i, j: (s, _core_index() * nbh + i, j)),
                    pl.BlockSpec((None, th, tc), lambda s, i, j: (s, i, j))]
        out_spec = pl.BlockSpec((None, th, tc), lambda s, i, j: (s, i, j))
        sem = ("parallel", "parallel", "parallel")

        def body(a_ref, b_ref, o_ref):
            o_ref[...] = (a_ref[...] + b_ref[...]).astype(o_ref.dtype)
    elif cls == "C":
        r = dw.shape[0]
        h = r // 2
        th, tc = _pick(h, 256, 16), _pick(w, 1024, LANES)
        nbh, nbc = h // th, w // tc
        grid = (N_CHIPS, nbh, nbc)
        in_specs = [pl.BlockSpec((th, tc), lambda s, i, j: (_core_index() * nbh + i, s * nbc + j)),
                    pl.BlockSpec((th, tc), lambda s, i, j: (i, s * nbc + j))]
        out_spec = pl.BlockSpec((None, th, tc), lambda s, i, j: (s, i, j))
        sem = ("parallel", "parallel", "parallel")

        def body(a_ref, b_ref, o_ref):
            o_ref[...] = (a_ref[...] + b_ref[...]).astype(o_ref.dtype)
    else:
        r, wp = dw.shape
        h = r // 2
        th = _pick(h, 64, 16)
        nbh = h // th
        grid = (nbh,)
        in_specs = [pl.BlockSpec((th, wp), lambda i: (_core_index() * nbh + i, 0)), pl.BlockSpec((th, wp), lambda i: (i, 0))]
        out_spec = pl.BlockSpec((N_CHIPS, th, w), lambda i: (0, i, 0))
        sem = ("parallel",)

        def body(a_ref, b_ref, o_ref):
            tot = a_ref[...] + b_ref[...]
            for s in range(N_CHIPS):
                o_ref[s] = tot[:, s * w:(s + 1) * w].astype(o_ref.dtype)

    return pl.pallas_call(body, name=name, grid=grid, in_specs=in_specs, out_specs=out_spec,
                          out_shape=jax.ShapeDtypeStruct((N_CHIPS, h, w), BF16), compiler_params=_cparams(sem))(dw, got)


def _rs_chip(zs, name):
    n = len(zs)
    job = _chip_job(zs)

    def body(*refs):
        job["start"](refs[:n], refs[n:2 * n], refs[2 * n], refs[2 * n + 1], True)
        job["finish"](refs[:n], refs[n:2 * n], refs[2 * n], refs[2 * n + 1], True)

    return pl.pallas_call(
        body, name=name, in_specs=[ANY] * n, out_specs=[ANY] * n, out_shape=job["out_shapes"],
        scratch_shapes=[pltpu.SemaphoreType.DMA((job["n_sems"],)), pltpu.SemaphoreType.DMA((job["n_sems"],))],
    )(*zs)


def _chip_job(zs):
    n = len(zs)

    def copies(ins, outs, send_sems, recv_sems):
        x, y, c, chips = _place()
        return [pltpu.make_async_remote_copy(src_ref=ins[t].at[2 * cx + cy], dst_ref=outs[t].at[k], send_sem=send_sems.at[3 * t + k],
                                             recv_sem=recv_sems.at[3 * t + k], device_id=(cx, cy, c), device_id_type=MESH)
                for t in range(n) for k, (cx, cy) in enumerate(chips)]

    def start(ins, outs, send_sems, recv_sems, when):
        def run():
            for cp in copies(ins, outs, send_sems, recv_sems):
                cp.start()
        run() if when is True else pl.when(when)(run)

    def finish(ins, outs, send_sems, recv_sems, when):
        def run():
            for cp in copies(ins, outs, send_sems, recv_sems):
                cp.wait()
        run() if when is True else pl.when(when)(run)

    return dict(ins=list(zs), alias=False, n_sems=3 * n, start=start, finish=finish,
                out_shapes=[jax.ShapeDtypeStruct((3,) + z.shape[1:], z.dtype) for z in zs])


def _rs_sum(z, parts, name):
    _, h, w = z.shape
    th = _pick(h, 256, 16)
    tc = _pick(w, 1024, LANES) if w % LANES == 0 else w
    nbh = h // th

    def body(z_ref, k_ref, o_ref):
        acc = z_ref[...].astype(F32)
        for k in range(3):
            acc = acc + k_ref[k].astype(F32)
        o_ref[...] = acc

    return pl.pallas_call(
        body, name=name, grid=(nbh, w // tc),
        in_specs=[pl.BlockSpec((None, th, tc), lambda i, j: (_chip_index(), i, j)), pl.BlockSpec((3, th, tc), lambda i, j: (0, i, j))],
        out_specs=pl.BlockSpec((th, tc), lambda i, j: (_core_index() * nbh + i, j)),
        out_shape=jax.ShapeDtypeStruct((2 * h, w), F32), compiler_params=_cparams(("parallel", "parallel")))(z, parts)


def _rs_join(bufs, name):
    n = len(bufs)

    def body(*refs):
        outs = refs[n:2 * n]
        send_sems, recv_sems = refs[2 * n], refs[2 * n + 1]
        x, y, c, _ = _place()
        cps = []
        for t in range(n):
            h = outs[t].shape[0] // 2
            mine = outs[t].at[_rows(c * h, h), :]
            cp = pltpu.make_async_remote_copy(src_ref=mine, dst_ref=mine, send_sem=send_sems.at[t], recv_sem=recv_sems.at[t],
                                              device_id=(x, y, 1 - c), device_id_type=MESH)
            cp.start()
            cps.append(cp)
        for t in range(n):
            h = outs[t].shape[0] // 2
            other = outs[t].at[_rows((1 - c) * h, h), :]
            pltpu.make_async_remote_copy(src_ref=other, dst_ref=other, send_sem=send_sems.at[t], recv_sem=recv_sems.at[t],
                                         device_id=(x, y, c), device_id_type=MESH).wait_recv()
        for cp in cps:
            cp.wait_send()

    return pl.pallas_call(
        body, name=name, in_specs=[ANY] * n, out_specs=[ANY] * n,
        out_shape=[jax.ShapeDtypeStruct(b.shape, b.dtype) for b in bufs],
        input_output_aliases={t: t for t in range(n)},
        scratch_shapes=[pltpu.SemaphoreType.DMA((n,)), pltpu.SemaphoreType.DMA((n,))],
    )(*bufs)


def _rs_chip_sums(dws, classes, widths, tag):
    gots = _rs_pair(dws, classes, f"rs_pair_{tag}")
    return [_rs_add(d, g, cls, w, f"rs_add_{tag}_{t}") for t, (d, g, cls, w) in enumerate(zip(dws, gots, classes, widths))]


def _rs_finish(zs, parts, tag):
    halves = [_rs_sum(z, p, f"rs_sum_{tag}_{t}") for t, (z, p) in enumerate(zip(zs, parts))]
    return _rs_join(halves, f"rs_join_{tag}")


def _reduce_scatter_layer(dws, classes, widths, tag):
    zs = _rs_chip_sums(dws, classes, widths, tag)
    return _rs_finish(zs, _rs_chip(zs, f"rs_chip_{tag}"), tag)


def _gather_shards(mine, name):
    r, w = mine.shape
    rh = r // 2

    def body(mine_ref, out_ref, send_sems, recv_sems, local_sem):
        x, y, c, chips = _place()
        me = 2 * x + y
        half = _rows(c * rh, rh)
        other = _rows((1 - c) * rh, rh)

        def copy(k, src, chip, rows, to):
            return pltpu.make_async_remote_copy(src_ref=src, dst_ref=out_ref.at[chip, rows], send_sem=send_sems.at[k],
                                                recv_sem=recv_sems.at[k], device_id=to, device_id_type=MESH)

        local = pltpu.make_async_copy(mine_ref, out_ref.at[me], local_sem)
        local.start()
        sends = [copy(k, mine_ref.at[half], me, half, (cx, cy, c)) for k, (cx, cy) in enumerate(chips)]
        for s in sends:
            s.start()
        passed = []
        for k, (cx, cy) in enumerate(chips):
            chip = 2 * cx + cy
            copy(k, mine_ref.at[half], chip, half, (x, y, c)).wait_recv()
            fwd = copy(3 + k, out_ref.at[chip, half], chip, half, (x, y, 1 - c))
            fwd.start()
            passed.append(fwd)
        for k, (cx, cy) in enumerate(chips):
            copy(3 + k, mine_ref.at[half], 2 * cx + cy, other, (x, y, c)).wait_recv()
        for s in sends + passed:
            s.wait_send()
        local.wait()

    return pl.pallas_call(
        body, name=name, in_specs=[ANY], out_specs=ANY, out_shape=jax.ShapeDtypeStruct((N_CHIPS, r, w), mine.dtype),
        scratch_shapes=[pltpu.SemaphoreType.DMA((6,)), pltpu.SemaphoreType.DMA((6,)), pltpu.SemaphoreType.DMA],
    )(mine)


def _sum_slots(slots, name):
    n, r, w = slots.shape
    tr = _pick(r, 256, SUBLANES)

    def body(s_ref, o_ref):
        acc = s_ref[0]
        for k in range(1, n):
            acc = acc + s_ref[k]
        o_ref[...] = acc

    return pl.pallas_call(
        body, name=name, grid=(r // tr,), in_specs=[pl.BlockSpec((n, tr, w), lambda i: (0, i, 0))],
        out_specs=pl.BlockSpec((tr, w), lambda i: (i, 0)), out_shape=jax.ShapeDtypeStruct((r, w), F32),
        compiler_params=_cparams(("parallel",)),
    )(slots)


def _all_to_all_sum(flat, name):
    r, w = flat.shape

    def body(f_ref, out_ref, send_sems, recv_sems, local_sem):
        x, y, c, _ = _place()
        me = 4 * x + 2 * y + c
        local = pltpu.make_async_copy(f_ref, out_ref.at[me], local_sem)
        local.start()
        sends = []
        for k in range(1, 8):
            peer = (x ^ (k >> 2), y ^ ((k >> 1) & 1), c ^ (k & 1))
            s = pltpu.make_async_remote_copy(src_ref=f_ref, dst_ref=out_ref.at[me], send_sem=send_sems.at[k - 1],
                                             recv_sem=recv_sems.at[k - 1], device_id=peer, device_id_type=MESH)
            s.start()
            sends.append(s)
        for k in range(1, 8):
            peer_slot = 4 * (x ^ (k >> 2)) + 2 * (y ^ ((k >> 1) & 1)) + (c ^ (k & 1))
            pltpu.make_async_remote_copy(src_ref=f_ref, dst_ref=out_ref.at[peer_slot], send_sem=send_sems.at[k - 1],
                                         recv_sem=recv_sems.at[k - 1], device_id=(x, y, c), device_id_type=MESH).wait_recv()
        for s in sends:
            s.wait_send()
        local.wait()

    slots = pl.pallas_call(
        body, name=name, in_specs=[ANY], out_specs=ANY, out_shape=jax.ShapeDtypeStruct((8, r, w), flat.dtype),
        scratch_shapes=[pltpu.SemaphoreType.DMA((7,)), pltpu.SemaphoreType.DMA((7,)), pltpu.SemaphoreType.DMA],
    )(flat)
    return _sum_slots(slots, name + "_sum")


def _pack(pieces, lead):
    flat = []
    n_lead = len(lead)
    for p in pieces:
        f = p.reshape(*lead, -1)
        pad = (-f.shape[-1]) % PACK_COLS
        if pad:
            f = jnp.pad(f, [(0, 0)] * n_lead + [(0, pad)])
        flat.append(f)
    f = jnp.concatenate(flat, axis=-1) if len(flat) > 1 else flat[0]
    pad = (-f.shape[-1]) % (32 * PACK_COLS)
    if pad:
        f = jnp.pad(f, [(0, 0)] * n_lead + [(0, pad)])
    return f.reshape(*lead, -1, PACK_COLS)


def _unpack(buf, shapes, lead):
    f = buf.reshape(*lead, -1)
    out, off = [], 0
    for shp in shapes:
        n = 1
        for s in shp:
            n *= s
        out.append(f[..., off:off + n].reshape(*lead, *shp))
        off += n + ((-n) % PACK_COLS)
    return out


def _join(g, axis):
    g = jnp.moveaxis(g, 0, axis)
    return g.reshape(*g.shape[:axis], g.shape[axis] * g.shape[axis + 1], *g.shape[axis + 2:])


def _split(full, axis):
    shp = full.shape
    g = full.reshape(*shp[:axis], N_CHIPS, shp[axis] // N_CHIPS, *shp[axis + 1:])
    return jnp.moveaxis(g, axis, 0)


def _norm_bwd(xs, dres, dhs, gain, name):
    def fn(x, dr, *rest):
        dh = rest[0]
        for d in rest[1:-1]:
            dh = dh + d
        _, vjp = jax.vjp(_rms, x, rest[-1])
        dx, dg = vjp(dh.astype(F32))
        return dr + dx, dg
    return _rowmap(fn, [xs, dres] + list(dhs), [gain], name, n_acc=1)


def _mm(a, b, kind, dtype, name, job):
    if job is None:
        return _matmul(a, b, kind, dtype, name), None
    return _matmul(a, b, kind, dtype, name, job=job)


def _ffn_fwd(x1, h2, wg, wu, conv_w, conv_b, wd, seq, tag, jobs=(None, None, None)):
    gpre, j0 = _mm(h2, wg, "nn", F32, f"ffn_gate_{tag}", jobs[0])
    up, j1 = _mm(h2, wu, "nn", BF16, f"ffn_up_{tag}", jobs[1])
    gate = _conv_fwd(gpre, gpre.shape[1], conv_w, conv_b, seq, f"ffn_conv_{tag}")
    act, = _rowmap(lambda g, u: (_silu(g) * u).astype(BF16), [gate, up], [], f"ffn_act_{tag}", tc=_pick(gate.shape[1], 1024, LANES))
    f, j2 = _mm(act, wd, "nn", F32, f"ffn_down_{tag}", jobs[2])
    return f, (gpre, up, gate, act), [j0, j1, j2]


def _ffn_bwd(dx2, h2, saved, wg, wu, conv_w, wd, seq, tag, jobs=(None, None, None)):
    gpre, up, gate, act = saved
    da, j0 = _mm(dx2, wd, "nt", BF16, f"ffn_dact_{tag}", jobs[0])
    d_wd, j1 = _mm(act, dx2, "tn", F32, f"ffn_dwd_{tag}", jobs[1])

    def act_bwd(g, u, d):
        _, vjp = jax.vjp(lambda g_, u_: _silu(g_) * u_, g, u)
        return vjp(d.astype(F32))
    dgate, dup = _rowmap(act_bwd, [gate, up, da], [], f"ffn_dactfn_{tag}", tc=_pick(gate.shape[1], 1024, LANES))
    dgpre, d_cw, d_cb = _conv_bwd(dgate, gpre, gpre.shape[1], conv_w, seq, True, f"ffn_dconv_{tag}", dx_dtype=BF16)
    dh_a = _matmul(dgpre, wg, "nt", F32, f"ffn_dh_gate_{tag}")
    dh_b = _matmul(dup, wu, "nt", F32, f"ffn_dh_up_{tag}")
    d_wg, j2 = _mm(h2, dgpre, "tn", F32, f"ffn_dwg_{tag}", jobs[2])
    d_wu = _matmul(h2, dup, "tn", F32, f"ffn_dwu_{tag}")
    grads = dict(ffn_w_gate=d_wg, ffn_w_up=d_wu, ffn_conv_w=d_cw, ffn_conv_b=d_cb.reshape(-1), ffn_w_down=d_wd)
    return (dh_a, dh_b), grads, [j0, j1, j2]


def _gmlp_fwd(h, w, tag):
    a = w["a_w_out"].shape[0]
    p = _matmul(h, w["a_w_in"], "nn", F32, f"a_in_{tag}")
    b_in, vnorm = w["a_b_in"].reshape(1, -1), w["a_v_norm"].reshape(1, -1)

    def fn(p_, b_, g_):
        hh = _gelu_tanh(p_ + b_)
        return hh[:, :a], _rms(hh[:, a:], g_)
    u, vn = _rowmap(fn, [p], [b_in, vnorm], f"a_gelu_{tag}")
    b_col = w["a_b_s"][:, :, None]
    y = _sgu_fwd(vn, u, w["a_w_s"], b_col, f"a_sgu_{tag}")
    m = _matmul(y, w["a_w_out"], "nn", F32, f"a_out_{tag}")
    return m, (p, u, vn, y, fn, b_in, vnorm, b_col)


def _gmlp_bwd(dm, h, saved, w, tag):
    p, u, vn, y, fn, b_in, vnorm, b_col = saved
    dy = _matmul(dm, w["a_w_out"], "nt", F32, f"a_dy_{tag}")
    d_wout = _matmul(y, dm, "tn", F32, f"a_dwout_{tag}")
    dvn, du, d_ws, d_bcol = _sgu_bwd(vn, u, dy, w["a_w_s"], b_col, f"a_dsgu_{tag}")

    def bwd(p_, du_, dvn_, b_, g_):
        _, vjp = jax.vjp(fn, p_, b_, g_)
        return vjp((du_, dvn_))
    dp, d_bin, d_vnorm = _rowmap(bwd, [p, du, dvn], [b_in, vnorm], f"a_dgelu_{tag}", n_acc=2)
    dh = _matmul(dp, w["a_w_in"], "nt", F32, f"a_dh_{tag}")
    d_win = _matmul(h, dp, "tn", F32, f"a_dwin_{tag}")
    return (dh,), dict(a_w_in=d_win, a_b_in=d_bin.reshape(-1), a_v_norm=d_vnorm.reshape(-1), a_w_s=d_ws, a_b_s=d_bcol[:, :, 0], a_w_out=d_wout)


def _fox_fwd_mixer(h, w, bsz, seq, tag):
    d = h.shape[1]
    nh = d // HEAD
    win = w["b_w_in"]
    wp = win.shape[1]
    proj = _matmul(h, win, "nn", F32, f"b_in_{tag}")
    gq, gk = w["b_q_norm"].reshape(1, HEAD), w["b_k_norm"].reshape(1, HEAD)
    bf = jnp.pad(w["b_b_f"].reshape(1, nh), ((0, 0), (0, LANES - nh)))

    def prep(pq, pk, pfl, gq_, gk_, bf_):
        qn = jnp.concatenate([_rms(x, gq_) for x in _heads(pq)], axis=1)
        kn = jnp.concatenate([_rms(x, gk_) for x in _heads(pk)], axis=1)
        return qn, kn, -_softplus(-(pfl + bf_))
    wins = [_win(proj, d, 0), _win(proj, d, 1), _win(proj, LANES, 4 * d // LANES)]

    def prep_fwd(pq, pk, pfl, gq_, gk_, bf_):
        qn, kn, lf = prep(pq, pk, pfl, gq_, gk_, bf_)
        return qn.astype(BF16), kn.astype(BF16), lf
    qn, kn, lf = _rowmap(prep_fwd, wins, [gq, gk, bf], f"b_prep_{tag}")
    cs = _cumsum_rows(lf, seq, False, f"b_cumsum_{tag}")
    c_rows = cs[:, :nh].reshape(bsz, seq, nh).transpose(0, 2, 1)[:, :, None, :]
    o, lse = _fox_fwd(qn, kn, proj, 2 * nh, c_rows, bsz, seq, nh, f"b_attn_{tag}")
    og = _win(proj, d, 3)
    y, = _rowmap(lambda o_, g_: (o_ * jax.nn.sigmoid(g_)).astype(BF16), [o, og], [], f"b_gate_{tag}")
    m = _matmul(y, w["b_w_out"], "nn", F32, f"b_out_{tag}")
    return m, (proj, qn, kn, c_rows, o, lse, y, prep, wins, (gq, gk, bf), wp)


def _fox_bwd_mixer(dm, h, saved, w, bsz, seq, tag):
    proj, qn, kn, c_rows, o, lse, y, prep, wins, (gq, gk, bf), wp = saved
    d = h.shape[1]
    nh = d // HEAD
    dy = _matmul(dm, w["b_w_out"], "nt", F32, f"b_dy_{tag}")
    d_wout = _matmul(y, dm, "tn", F32, f"b_dwout_{tag}")
    og = _win(proj, d, 3)

    def gate_bwd(o_, g_, dy_):
        _, vjp = jax.vjp(lambda a, b: a * jax.nn.sigmoid(b), o_, g_)
        return vjp(dy_)
    do, dog = _rowmap(gate_bwd, [o, og, dy], [], f"b_dgate_{tag}")
    dqn, delta = _fox_bwd_q(qn, kn, proj, 2 * nh, do, lse, c_rows, bsz, seq, nh, f"b_dattn_q_{tag}")
    dkn, dv, dc_rows = _fox_bwd_kv(qn, kn, proj, 2 * nh, do, lse, delta, c_rows, bsz, seq, nh, f"b_dattn_kv_{tag}")
    dc = dc_rows[:, :, 0, :].transpose(0, 2, 1).reshape(bsz * seq, nh)
    dc = jnp.pad(dc, ((0, 0), (0, LANES - nh)))
    dlf = _cumsum_rows(dc, seq, True, f"b_dcumsum_{tag}")
    extra = wp - (4 * d + LANES)

    def prep_bwd(pq, pk, pfl, dqn_, dkn_, dv_, dog_, dlf_, gq_, gk_, bf_):
        _, vjp = jax.vjp(prep, pq, pk, pfl, gq_, gk_, bf_)
        dpq, dpk, dpfl, dgq, dgk, dbf = vjp((dqn_, dkn_, dlf_))
        parts = [dpq, dpk, dv_, dog_, dpfl]
        if extra:
            parts.append(jnp.zeros((pq.shape[0], extra), F32))
        return jnp.concatenate(parts, axis=1), dgq, dgk, dbf
    dproj, d_gq, d_gk, d_bf = _rowmap(prep_bwd, wins + [dqn, dkn, dv, dog, dlf], [gq, gk, bf], f"b_dprep_{tag}", n_acc=3)
    dh = _matmul(dproj, w["b_w_in"], "nt", F32, f"b_dh_{tag}")
    d_win = _matmul(h, dproj, "tn", F32, f"b_dwin_{tag}")
    return (dh,), dict(b_w_in=d_win, b_b_f=d_bf[0, :nh], b_q_norm=d_gq.reshape(-1), b_k_norm=d_gk.reshape(-1), b_w_out=d_wout)


def _gdn_fwd_mixer(h, w, bsz, seq, tag):
    d = h.shape[1]
    nkh = d // HEAD
    nvh = 2 * nkh
    dqkv = (2 * nkh + nvh) * HEAD
    dz = nvh * HEAD
    nch = seq // GDN_CHUNK
    proj = _matmul(h, w["c_w_in"], "nn", F32, f"c_in_{tag}")
    conv = _conv_fwd(proj, dqkv, w["c_conv_w"], None, seq, f"c_conv_{tag}")

    def rows_of(cols):
        return cols.reshape(bsz, nch, GDN_CHUNK, nvh).transpose(3, 0, 1, 2)[:, :, :, None, :]
    b_rows = rows_of(proj[:, dqkv + dz:dqkv + dz + nvh])
    a_rows = rows_of(proj[:, dqkv + dz + nvh:dqkv + dz + 2 * nvh])
    alog, dtb = w["c_a_log"].reshape(nvh, 1, 1), w["c_dt_bias"].reshape(nvh, 1, 1)
    o, states = _gdn_fwd(conv, b_rows, a_rows, alog, dtb, bsz, seq, nvh, f"c_core_{tag}")
    gn = w["c_out_norm"].reshape(1, HEAD)
    zwin = _win(proj, dz, dqkv // dz)

    def outfn(o_, z_, g_):
        return jnp.concatenate([_rms(a, g_) * _silu(b) for a, b in zip(_heads(o_), _heads(z_))], axis=1)
    y, = _rowmap(lambda o_, z_, g_: outfn(o_, z_, g_).astype(BF16), [o, zwin], [gn], f"c_outnorm_{tag}")
    m = _matmul(y, w["c_w_out"], "nn", F32, f"c_out_{tag}")
    return m, (proj, conv, b_rows, a_rows, alog, dtb, o, states, y, gn, zwin, outfn)


def _gdn_bwd_mixer(dm, h, saved, w, bsz, seq, tag):
    proj, conv, b_rows, a_rows, alog, dtb, o, states, y, gn, zwin, outfn = saved
    d = h.shape[1]
    nkh = d // HEAD
    nvh = 2 * nkh
    dk_, dv_ = nkh * HEAD, nvh * HEAD
    dqkv = 2 * dk_ + dv_
    dz = dv_
    wp = proj.shape[1]
    dy = _matmul(dm, w["c_w_out"], "nt", F32, f"c_dy_{tag}")
    d_wout = _matmul(y, dm, "tn", F32, f"c_dwout_{tag}")

    def out_bwd(o_, z_, dy_, g_):
        _, vjp = jax.vjp(outfn, o_, z_, g_)
        return vjp(dy_)
    do, dzz, d_gn = _rowmap(out_bwd, [o, zwin, dy], [gn], f"c_doutnorm_{tag}", n_acc=1)
    dq, dk, dv, db_rows, da_rows, d_alog, d_dtb = _gdn_bwd(conv, b_rows, a_rows, alog, dtb, states, do, bsz, seq, nvh, f"c_dcore_{tag}")
    cw = w["c_conv_w"]
    dq_pre, d_cwq = _conv_bwd(dq, proj, dk_, cw[:, :dk_], seq, False, f"c_dconv_q_{tag}", xcol=0)
    dk_pre, d_cwk = _conv_bwd(dk, proj, dk_, cw[:, dk_:2 * dk_], seq, False, f"c_dconv_k_{tag}", xcol=dk_)
    dv_pre, d_cwv = _conv_bwd(dv, proj, dv_, cw[:, 2 * dk_:], seq, False, f"c_dconv_v_{tag}", xcol=2 * dk_)
    d_cw = jnp.concatenate([d_cwq, d_cwk, d_cwv], axis=1)

    def cols_of(rows):
        return rows[:, :, :, 0, :].transpose(1, 2, 3, 0).reshape(bsz * seq, nvh)
    dba = jnp.concatenate([cols_of(db_rows), cols_of(da_rows)], axis=1)
    dba = jnp.pad(dba, ((0, 0), (0, wp - dqkv - dz - 2 * nvh)))
    dproj, = _rowmap(lambda *parts: jnp.concatenate(parts, axis=1), [dq_pre, dk_pre, dv_pre, dzz, dba], [], f"c_dproj_{tag}")
    dh = _matmul(dproj, w["c_w_in"], "nt", F32, f"c_dh_{tag}")
    d_win = _matmul(h, dproj, "tn", F32, f"c_dwin_{tag}")
    return (dh,), dict(c_w_in=d_win, c_conv_w=d_cw, c_a_log=d_alog.reshape(-1), c_dt_bias=d_dtb.reshape(-1),
                       c_out_norm=d_gn.reshape(-1), c_w_out=d_wout)


_MIXER_FWD = (lambda h, w, bsz, seq, tag: _gmlp_fwd(h, w, tag), _fox_fwd_mixer, _gdn_fwd_mixer)
_MIXER_BWD = (lambda dm, h, s, w, bsz, seq, tag: _gmlp_bwd(dm, h, s, w, tag), _fox_bwd_mixer, _gdn_bwd_mixer)


class _NoPlan:
    def __init__(self, layers):
        self.layers = layers

    def weights(self, i):
        return self.layers[i]

    def fwd_jobs(self, i):
        return (None, None, None)

    def fwd_done(self, i, outs):
        pass

    def bwd_jobs(self, i):
        return (None, None, None)

    def bwd_done(self, i, outs):
        pass

    def grads_ready(self, i, grads):
        pass


def _local_step(x, target, plan, depth, bsz, seq):
    t, d = x.shape
    saved = []
    m_prev = None
    xin = x
    for i in range(depth):
        w = plan.weights(i)
        tag = f"l{i}"
        g_mix, g_ffn = w["norm_mix"].reshape(1, d), w["norm_ffn"].reshape(1, d)
        if i == 0:
            h, = _rowmap(lambda x_, g_: _rms(x_, g_).astype(BF16), [xin], [g_mix], f"norm_mix_{tag}")
            xl = xin
        else:
            xl, h = _rowmap(lambda x_, m_, g_: (x_ + m_, _rms(x_ + m_, g_).astype(BF16)), [xin, m_prev], [g_mix], f"norm_mix_{tag}")
        m, msaved = _MIXER_FWD[i % 3](h, w, bsz, seq, tag)
        x1, h2 = _rowmap(lambda x_, m_, g_: (x_ + m_, _rms(x_ + m_, g_).astype(BF16)), [xl, m], [g_ffn], f"norm_ffn_{tag}")
        f, fsaved, jouts = _ffn_fwd(x1, h2, w["ffn_w_gate"], w["ffn_w_up"], w["ffn_conv_w"], w["ffn_conv_b"].reshape(1, -1), w["ffn_w_down"],
                                    seq, tag, plan.fwd_jobs(i))
        plan.fwd_done(i, jouts)
        saved.append((xl, h, msaved, x1, h2, fsaved))
        xin, m_prev = x1, f

    def loss_fn(x_, f_, tg_):
        e = x_ + f_ - tg_
        return e * (1.0 / d), jnp.full((1, LANES), (0.5 / d) * jnp.sum(e * e), F32)
    dx, loss_acc = _rowmap(loss_fn, [xin, m_prev, target], [], "loss", n_acc=1)
    loss = loss_acc[0, 0]

    grads = [None] * depth
    for i in reversed(range(depth)):
        w = plan.weights(i)
        tag = f"l{i}"
        xl, h, msaved, x1, h2, fsaved = saved[i]
        g_mix, g_ffn = w["norm_mix"].reshape(1, d), w["norm_ffn"].reshape(1, d)
        dhs, gw, jouts = _ffn_bwd(dx, h2, fsaved, w["ffn_w_gate"], w["ffn_w_up"], w["ffn_conv_w"], w["ffn_w_down"], seq, tag, plan.bwd_jobs(i))
        plan.bwd_done(i, jouts)
        dx1, d_gffn = _norm_bwd(x1, dx, dhs, g_ffn, f"dnorm_ffn_{tag}")
        dhs, gm = _MIXER_BWD[i % 3](dx1, h, msaved, w, bsz, seq, tag)
        dx, d_gmix = _norm_bwd(xl, dx1, dhs, g_mix, f"dnorm_mix_{tag}")
        gw.update(gm)
        gw["norm_mix"], gw["norm_ffn"] = d_gmix.reshape(-1), d_gffn.reshape(-1)
        grads[i] = gw
        plan.grads_ready(i, gw)
    return loss, dx, grads


def _adamw_math(w_, g_, m_, v_):
    m_new = ADAM_B1 * m_ + (1.0 - ADAM_B1) * g_
    v_new = ADAM_B2 * v_ + (1.0 - ADAM_B2) * (g_ * g_)
    m_hat = m_new / (1.0 - ADAM_B1 ** ADAM_STEP)
    v_hat = v_new / (1.0 - ADAM_B2 ** ADAM_STEP)
    delta = -ADAM_LR * (m_hat / (jnp.sqrt(v_hat) + ADAM_EPS) + ADAM_WD * w_)
    return delta, m_new, v_new


def _adamw(w, g, m, v, name):
    shape = w.shape
    if w.ndim == 1:
        w, g, m, v = (a.reshape(1, -1) for a in (w, g, m, v))
    return [o.reshape(shape) for o in _elementwise(_adamw_math, [w, g, m, v], 3, name)]


def _adamw_layers(w, gs, m, v, name):
    nl, r, c = w.shape
    tr = _pick(r, max(SUBLANES, (1 << 19) // c // SUBLANES * SUBLANES), SUBLANES)

    def body(*refs):
        w_ref, m_ref, v_ref = refs[:3]
        g_refs = refs[3:3 + nl]
        go_ref, d_ref, mo_ref, vo_ref = refs[3 + nl:]
        layer = pl.program_id(0)
        for k in range(nl):
            @pl.when(layer == k)
            def _(k=k):
                g = g_refs[k][...]
                delta, m_new, v_new = _adamw_math(w_ref[...], g, m_ref[...], v_ref[...])
                go_ref[...] = g
                d_ref[...] = delta
                mo_ref[...] = m_new
                vo_ref[...] = v_new

    st = pl.BlockSpec((None, tr, c), lambda l, i: (l, i, 0))
    g_specs = [pl.BlockSpec((tr, c), functools.partial(lambda l, i, k: (jnp.where(l == k, i, 0), 0), k=k)) for k in range(nl)]
    return pl.pallas_call(
        body, name=name, grid=(nl, r // tr), in_specs=[st, st, st] + g_specs, out_specs=[st] * 4,
        out_shape=[jax.ShapeDtypeStruct(w.shape, F32)] * 4, compiler_params=_cparams(("parallel", "parallel")),
    )(w, m, v, *gs)


WEIGHTS = ['norm_mix', 'norm_ffn', 'ffn_w_gate', 'ffn_w_up', 'ffn_conv_w', 'ffn_conv_b', 'ffn_w_down', 'a_w_in', 'a_b_in', 'a_v_norm',
           'a_w_s', 'a_b_s', 'a_w_out', 'b_w_in', 'b_b_f', 'b_q_norm', 'b_k_norm', 'b_w_out', 'c_w_in', 'c_conv_w', 'c_a_log',
           'c_dt_bias', 'c_out_norm', 'c_w_out']
BIG = {'ffn_w_gate': 1, 'ffn_w_up': 1, 'ffn_w_down': 0, 'a_w_in': 1, 'a_w_out': 0, 'b_w_in': 1, 'b_w_out': 0, 'c_w_in': 1, 'c_w_out': 0}
SMALL_SHARDED = {'ffn_conv_w': 1, 'a_b_in': 0, 'a_v_norm': 0, 'c_conv_w': 1}
MIXER_NAMES = (('a_w_in', 'a_b_in', 'a_v_norm', 'a_w_s', 'a_b_s', 'a_w_out'), ('b_w_in', 'b_b_f', 'b_q_norm', 'b_k_norm', 'b_w_out'),
               ('c_w_in', 'c_conv_w', 'c_a_log', 'c_dt_bias', 'c_out_norm', 'c_w_out'))
FFN_NAMES = ('norm_mix', 'norm_ffn', 'ffn_w_gate', 'ffn_w_up', 'ffn_conv_w', 'ffn_conv_b', 'ffn_w_down')


def _layer_entries(depth):
    out = []
    for i in range(depth):
        kind, j = i % 3, i // 3
        out.append([(n, i) for n in FFN_NAMES] + [(n, j) for n in MIXER_NAMES[kind]])
    return out


def _layout(name, shard_shape):
    if BIG[name] == 0:
        return "R"
    return "C" if shard_shape[-1] % LANES == 0 else "U"


JOB_GROUPS = ((3, 4), (0, 1), (2,))


class _Plan:
    def __init__(self, params, entries, small_full):
        self.params, self.entries, self.small_full = params, entries, small_full
        self.depth = len(entries)
        self.big = [[(n, j, _layout(n, params[n].shape[1:])) for n, j in ent if n in BIG] for ent in entries]
        self.layers = [None] * self.depth
        self.zs = [None] * self.depth
        self.total = {}
        bufs = self._cast(0)
        self._install(0, _gather_layer(bufs, [cls for _, _, cls in self.big[0]], "gather_l0"))

    def _cast(self, i):
        return [_cast_window(self.params[n], j, cls, f"cast_{n}_l{i}") for n, j, cls in self.big[i]]

    def _install(self, i, bufs):
        w = {}
        for (n, j, cls), buf in zip(self.big[i], bufs):
            w[n] = _assemble(buf, _pad_cols(N_CHIPS * buf.shape[2]), f"assemble_{n}_l{i}") if cls == "U" else buf
        for n, j in self.entries[i]:
            if n in SMALL_SHARDED:
                w[n] = self.small_full[(n, j)]
            elif n not in BIG:
                w[n] = self.params[n][j]
        self.layers[i] = w

    def weights(self, i):
        return self.layers[i]

    def fwd_jobs(self, i):
        if i + 1 >= self.depth:
            return (None, None, None)
        bufs = self._cast(i + 1)
        classes = [cls for _, _, cls in self.big[i + 1]]
        return [_gather_job([bufs[t] for t in grp], [classes[t] for t in grp]) for grp in JOB_GROUPS]

    def fwd_done(self, i, outs):
        if i + 1 < self.depth:
            bufs = [None] * len(self.big[i + 1])
            for grp, got in zip(JOB_GROUPS, outs):
                for t, buf in zip(grp, got):
                    bufs[t] = buf
            self._install(i + 1, bufs)

    def grads_ready(self, i, grads):
        dws, classes, widths = [], [], []
        for n, j, cls in self.big[i]:
            shard = self.params[n].shape[1:]
            dws.append(grads[n].reshape(N_CHIPS, shard[0], shard[1]) if cls == "R" else grads[n])
            classes.append(cls)
            widths.append(shard[1])
        self.zs[i] = _rs_chip_sums(dws, classes, widths, f"l{i}")
        if i == 0:
            self._finish(0, _rs_chip(self.zs[0], "rs_chip_l0"))

    def bwd_jobs(self, i):
        if i + 1 >= self.depth:
            return (None, None, None)
        return [_chip_job([self.zs[i + 1][t] for t in grp]) for grp in JOB_GROUPS]

    def bwd_done(self, i, outs):
        if i + 1 < self.depth:
            parts = [None] * len(self.big[i + 1])
            for grp, got in zip(JOB_GROUPS, outs):
                for t, p in zip(grp, got):
                    parts[t] = p
            self._finish(i + 1, parts)

    def _finish(self, i, parts):
        for (n, j, _), red in zip(self.big[i], _rs_finish(self.zs[i], parts, f"l{i}")):
            self.total[(n, j)] = red


def _train_step(x, target, params, moments_m, moments_v):
    bsz, seq, d = x.shape
    depth = params['norm_mix'].shape[0]
    entries = _layer_entries(depth)

    small_list = [(n, j) for n in SMALL_SHARDED for j in range(params[n].shape[0])]
    small_buf = _gather_shards(_pack([params[n][j] for n, j in small_list], ()), "gather_small")
    small_full = {}
    for (n, j), g in zip(small_list, _unpack(small_buf, [params[n][j].shape for n, j in small_list], (N_CHIPS,))):
        small_full[(n, j)] = _join(g, SMALL_SHARDED[n])

    plan = _Plan(params, entries, small_full)
    loss_local, dx, grads = _local_step(x.reshape(bsz * seq, d), target.reshape(bsz * seq, d), plan, depth, bsz, seq)
    loss = lax.psum(loss_local, ("x", "y", "c"))

    total = plan.total
    layer_of = {(n, j): i for i, ent in enumerate(entries) for n, j in ent}
    packed = _pack([_split(grads[layer_of[(n, j)]][n], SMALL_SHARDED[n]) for n, j in small_list], (N_CHIPS,))
    red = _reduce_scatter_layer([packed], ["R"], [PACK_COLS], "small")[0]
    for (n, j), g in zip(small_list, _unpack(red, [params[n][j].shape for n, j in small_list], ())):
        total[(n, j)] = g
    repl = [(n, j) for n in WEIGHTS if n not in BIG and n not in SMALL_SHARDED for j in range(params[n].shape[0])]
    flat = jnp.concatenate([grads[layer_of[k]][k[0]].reshape(-1) for k in repl])
    n_flat = flat.shape[0]
    flat = jnp.pad(flat, (0, (-n_flat) % (SUBLANES * LANES))).reshape(-1, LANES)
    flat = _all_to_all_sum(flat, "allreduce_small").reshape(-1)
    off = 0
    for k in repl:
        shp = params[k[0]][k[1]].shape
        n = 1
        for s in shp:
            n *= s
        total[k] = flat[off:off + n].reshape(shp)
        off += n

    grad_w, delta_w, new_m, new_v = {}, {}, {}, {}
    for n in WEIGHTS:
        nl = params[n].shape[0]
        if n in BIG:
            grad_w[n], delta_w[n], new_m[n], new_v[n] = _adamw_layers(params[n], [total[(n, j)] for j in range(nl)], moments_m[n], moments_v[n], f"adamw_{n}")
        else:
            g = jnp.stack([total[(n, j)] for j in range(nl)])
            grad_w[n] = g
            delta_w[n], new_m[n], new_v[n] = _adamw(params[n], g, moments_m[n], moments_v[n], f"adamw_{n}")
    return (loss, dx.reshape(bsz, seq, d), *[grad_w[n] for n in WEIGHTS], *[delta_w[n] for n in WEIGHTS],
            *[new_m[n] for n in WEIGHTS], *[new_v[n] for n in WEIGHTS])


def kernel(x, norm_mix, norm_ffn, ffn_w_gate, ffn_w_up, ffn_conv_w, ffn_conv_b, ffn_w_down, a_w_in, a_b_in, a_v_norm, a_w_s, a_b_s, a_w_out, b_w_in, b_b_f, b_q_norm, b_k_norm, b_w_out, c_w_in, c_conv_w, c_a_log, c_dt_bias, c_out_norm, c_w_out, loss_target, m_norm_mix, m_norm_ffn, m_ffn_w_gate, m_ffn_w_up, m_ffn_conv_w, m_ffn_conv_b, m_ffn_w_down, m_a_w_in, m_a_b_in, m_a_v_norm, m_a_w_s, m_a_b_s, m_a_w_out, m_b_w_in, m_b_b_f, m_b_q_norm, m_b_k_norm, m_b_w_out, m_c_w_in, m_c_conv_w, m_c_a_log, m_c_dt_bias, m_c_out_norm, m_c_w_out, v_norm_mix, v_norm_ffn, v_ffn_w_gate, v_ffn_w_up, v_ffn_conv_w, v_ffn_conv_b, v_ffn_w_down, v_a_w_in, v_a_b_in, v_a_v_norm, v_a_w_s, v_a_b_s, v_a_w_out, v_b_w_in, v_b_b_f, v_b_q_norm, v_b_k_norm, v_b_w_out, v_c_w_in, v_c_conv_w, v_c_a_log, v_c_dt_bias, v_c_out_norm, v_c_w_out):
    given = dict(locals())
    params = {n: given[n] for n in WEIGHTS}
    moments_m = {n: given["m_" + n] for n in WEIGHTS}
    moments_v = {n: given["v_" + n] for n in WEIGHTS}
    return _train_step(x, loss_target, params, moments_m, moments_v)
```

```python
import functools

import jax
import jax.numpy as jnp
from jax import lax
from jax.experimental import pallas as pl
from jax.experimental.pallas import tpu as pltpu

F32 = jnp.float32
BF16 = jnp.bfloat16
HI = lax.Precision.HIGHEST
MESH = pl.DeviceIdType.MESH

RMS_EPS = 1e-6
ADAM_LR, ADAM_B1, ADAM_B2, ADAM_EPS, ADAM_WD, ADAM_STEP = 0.001, 0.9, 0.999, 1e-08, 0.01, 10
A_CHUNK, HEAD, GDN_CHUNK = 128, 128, 64
LANES, SUBLANES = 128, 8
PACK_COLS = 1024
N_CHIPS = 4
VMEM_LIMIT = 56 * 1024 * 1024
ROWMAP_BUDGET = 20 * 1024 * 1024

NN = (((1,), (0,)), ((), ()))
NT = (((1,), (1,)), ((), ()))
TN = (((0,), (0,)), ((), ()))
BNN = (((2,), (1,)), ((0,), (0,)))
BNT = (((2,), (2,)), ((0,), (0,)))
BTN = (((1,), (1,)), ((0,), (0,)))


def _pick(n, cap, mult):
    if n <= cap:
        return n
    best = None
    for d in range(mult, cap + 1, mult):
        if n % d == 0:
            best = d
    if best is None:
        raise ValueError(f"no tile for {n} (cap {cap}, multiple of {mult})")
    return best


def _pad_cols(n):
    j = -(-n // LANES)
    while not (j <= 8 or any(j % d == 0 for d in (4, 5, 6, 7, 8))):
        j += 1
    return j * LANES


def _cparams(sem):
    return pltpu.CompilerParams(dimension_semantics=sem, vmem_limit_bytes=VMEM_LIMIT)


def _matmul(a, b, kind, out_dtype, name, job=None):
    if kind == "nn":
        (m, k), (k2, n) = a.shape, b.shape
    elif kind == "nt":
        (m, k), (n, k2) = a.shape, b.shape
    else:
        (k, m), (k2, n) = a.shape, b.shape
    assert k == k2, (name, a.shape, b.shape)
    tm, tn, tk = _pick(m, 1024, LANES), _pick(n, 1024, LANES), _pick(k, 2048, LANES)
    ni, nj, nk = m // tm, n // tn, k // tk
    dims = {"nn": NN, "nt": NT, "tn": TN}[kind]
    a_spec = pl.BlockSpec((tk, tm), lambda i, j, kk: (kk, i)) if kind == "tn" else pl.BlockSpec((tm, tk), lambda i, j, kk: (i, kk))
    b_spec = pl.BlockSpec((tn, tk), lambda i, j, kk: (j, kk)) if kind == "nt" else pl.BlockSpec((tk, tn), lambda i, j, kk: (kk, j))
    n_jin, n_jout = (len(job["ins"]), len(job["out_shapes"])) if job else (0, 0)

    def body(a_ref, b_ref, *rest):
        jins, o_ref, jouts = rest[:n_jin], rest[n_jin], rest[n_jin + 1:n_jin + 1 + n_jout]
        scratch = rest[n_jin + 1 + n_jout:]
        i, j, kk = pl.program_id(0), pl.program_id(1), pl.program_id(2)
        if job:
            first = jnp.logical_and(jnp.logical_and(i == 0, j == 0), kk == 0)
            job["start"](jins, jouts, scratch[-2], scratch[-1], first)
        prod = lax.dot_general(a_ref[...].astype(BF16), b_ref[...].astype(BF16), dims, preferred_element_type=F32)
        if nk == 1:
            o_ref[...] = prod.astype(o_ref.dtype)
        else:
            acc_ref = scratch[0]

            @pl.when(kk == 0)
            def _():
                acc_ref[...] = prod

            @pl.when(kk > 0)
            def _():
                acc_ref[...] += prod

            @pl.when(kk == nk - 1)
            def _():
                o_ref[...] = acc_ref[...].astype(o_ref.dtype)
        if job:
            last = jnp.logical_and(jnp.logical_and(i == ni - 1, j == nj - 1), kk == nk - 1)
            job["finish"](jins, jouts, scratch[-2], scratch[-1], last)

    scratch_shapes = [pltpu.VMEM((tm, tn), F32)] if nk > 1 else []
    out_specs = pl.BlockSpec((tm, tn), lambda i, j, kk: (i, j))
    out_shape = jax.ShapeDtypeStruct((m, n), out_dtype)
    if not job:
        return pl.pallas_call(
            body, name=name, grid=(ni, nj, nk), in_specs=[a_spec, b_spec], out_specs=out_specs, out_shape=out_shape,
            scratch_shapes=scratch_shapes, compiler_params=_cparams(("parallel", "parallel", "arbitrary")),
        )(a, b)
    scratch_shapes += [pltpu.SemaphoreType.DMA((job["n_sems"],)), pltpu.SemaphoreType.DMA((job["n_sems"],))]
    res = pl.pallas_call(
        body, name=name, grid=(ni, nj, nk), in_specs=[a_spec, b_spec] + [ANY] * n_jin,
        out_specs=[out_specs] + [ANY] * n_jout, out_shape=[out_shape] + list(job["out_shapes"]),
        input_output_aliases={2 + t: 1 + t for t in range(n_jin)} if job["alias"] else {},
        scratch_shapes=scratch_shapes, compiler_params=_cparams(("arbitrary", "arbitrary", "arbitrary")),
    )(a, b, *job["ins"])
    return res[0], list(res[1:])


def _win(arr, width=None, blk=0):
    return (arr, arr.shape[1] if width is None else width, blk)


def _rowmap(fn, rows, params, name, n_acc=0, tc=None, col_params=()):
    rows = [r if isinstance(r, tuple) else _win(r) for r in rows]
    t = rows[0][0].shape[0]
    widths = [tc if tc is not None else w for (_, w, _) in rows]

    def blocks_for(tr):
        rb = [jax.ShapeDtypeStruct((tr, w), a.dtype) for (a, _, _), w in zip(rows, widths)]
        pb = [jax.ShapeDtypeStruct((p.shape[0], tc) if (i in col_params) else p.shape, p.dtype) for i, p in enumerate(params)]
        return rb, pb

    rb, pb = blocks_for(SUBLANES * 2)
    outs = jax.eval_shape(fn, *rb, *pb)
    outs = list(outs) if isinstance(outs, (tuple, list)) else [outs]
    n_row = len(outs) - n_acc
    row_bytes = sum(w * a.dtype.itemsize for (a, _, _), w in zip(rows, widths)) + sum(o.shape[1] * o.dtype.itemsize for o in outs[:n_row])
    tr = 16
    while tr * 2 <= 512 and t % (tr * 2) == 0 and (tr * 2) * row_bytes * 5 <= ROWMAP_BUDGET:
        tr *= 2
    rb, pb = blocks_for(tr)
    outs = jax.eval_shape(fn, *rb, *pb)
    outs = list(outs) if isinstance(outs, (tuple, list)) else [outs]
    n_in = len(rows) + len(params)

    if tc is None:
        grid = (t // tr,)
        row_axis = 0
        in_specs = [pl.BlockSpec((tr, w), functools.partial(lambda i, b: (i, b), b=blk)) for (_, w, blk) in rows]
        in_specs += [pl.BlockSpec(p.shape, functools.partial(lambda i, nd: (0,) * nd, nd=p.ndim)) for p in params]
        out_specs = [pl.BlockSpec((tr, o.shape[1]), lambda i: (i, 0)) for o in outs[:n_row]]
        out_specs += [pl.BlockSpec(o.shape, functools.partial(lambda i, nd: (0,) * nd, nd=len(o.shape))) for o in outs[n_row:]]
        out_shape = [jax.ShapeDtypeStruct((t, o.shape[1]), o.dtype) for o in outs[:n_row]]
        out_shape += [jax.ShapeDtypeStruct(o.shape, o.dtype) for o in outs[n_row:]]
        sem = ("arbitrary",) if n_acc else ("parallel",)
    else:
        wtot = rows[0][1]
        grid = (wtot // tc, t // tr)
        row_axis = 1
        in_specs = [pl.BlockSpec((tr, tc), functools.partial(lambda j, i, b: (i, j + b), b=blk)) for (_, _, blk) in rows]
        for i, p in enumerate(params):
            if i in col_params:
                in_specs.append(pl.BlockSpec((p.shape[0], tc), lambda j, i: (0, j)))
            else:
                in_specs.append(pl.BlockSpec(p.shape, functools.partial(lambda j, i, nd: (0,) * nd, nd=p.ndim)))
        out_specs = [pl.BlockSpec((tr, tc), lambda j, i: (i, j)) for _ in outs[:n_row]]
        out_specs += [pl.BlockSpec((o.shape[0], tc), lambda j, i: (0, j)) for o in outs[n_row:]]
        out_shape = [jax.ShapeDtypeStruct((t, wtot), o.dtype) for o in outs[:n_row]]
        out_shape += [jax.ShapeDtypeStruct((o.shape[0], wtot), o.dtype) for o in outs[n_row:]]
        sem = ("parallel", "arbitrary") if n_acc else ("parallel", "parallel")

    def body(*refs):
        ins, ors = refs[:n_in], refs[n_in:]
        res = fn(*[r[...] for r in ins])
        res = list(res) if isinstance(res, (tuple, list)) else [res]
        for o, r in zip(ors[:n_row], res[:n_row]):
            o[...] = r.astype(o.dtype)
        if n_acc:
            i = pl.program_id(row_axis)
            for o, r in zip(ors[n_row:], res[n_row:]):
                @pl.when(i == 0)
                def _(o=o, r=r):
                    o[...] = r.astype(o.dtype)

                @pl.when(i > 0)
                def _(o=o, r=r):
                    o[...] += r.astype(o.dtype)

    res = pl.pallas_call(
        body, name=name, grid=grid, in_specs=in_specs, out_specs=out_specs, out_shape=out_shape,
        compiler_params=_cparams(sem),
    )(*[a for (a, _, _) in rows], *params)
    return res


def _elementwise(fn, arrays, n_out, name):
    shape = arrays[0].shape
    cols = shape[-1]
    rws = 1
    for s in shape[:-1]:
        rws *= s
    arrs = [a.reshape(rws, cols) for a in arrays]
    per_row = cols * 4 * (len(arrays) + n_out) * 3
    tr = rws
    if rws * per_row > ROWMAP_BUDGET:
        tr = _pick(rws, max(SUBLANES, ROWMAP_BUDGET // per_row), SUBLANES)

    def body(*refs):
        res = fn(*[r[...] for r in refs[:len(arrs)]])
        for o, r in zip(refs[len(arrs):], res):
            o[...] = r

    spec = pl.BlockSpec((tr, cols), lambda i: (i, 0))
    outs = pl.pallas_call(
        body, name=name, grid=(rws // tr,), in_specs=[spec] * len(arrs), out_specs=[spec] * n_out,
        out_shape=[jax.ShapeDtypeStruct((rws, cols), F32)] * n_out, compiler_params=_cparams(("parallel",)),
    )(*arrs)
    return [o.reshape(shape) for o in outs]


def _rms(x, g):
    return x * lax.rsqrt(jnp.mean(x * x, axis=-1, keepdims=True) + RMS_EPS) * g


def _silu(x):
    return x * jax.nn.sigmoid(x)


def _softplus(x):
    return jnp.maximum(x, 0.0) + jnp.log(1.0 + jnp.exp(-jnp.abs(x)))


def _gelu_tanh(x):
    return 0.5 * x * (1.0 + jnp.tanh(0.7978845608028654 * (x + 0.044715 * (x * x * x))))


def _heads(x):
    return [x[:, h * HEAD:(h + 1) * HEAD] for h in range(x.shape[1] // HEAD)]


def _dot(a, b, dims=NN):
    return lax.dot_general(a, b, dims, precision=HI, preferred_element_type=F32)


def _bdot(a, b, dims):
    return lax.dot_general(a.astype(BF16), b.astype(BF16), dims, preferred_element_type=F32)


def _bdot3(a, b, dims):
    return lax.dot_general(a, b, dims, precision=lax.Precision.HIGH, preferred_element_type=F32)


def _eye(n):
    return (lax.broadcasted_iota(jnp.int32, (n, n), 0) == lax.broadcasted_iota(jnp.int32, (n, n), 1)).astype(F32)


def _tri(n):
    return lax.broadcasted_iota(jnp.int32, (n, n), 0) >= lax.broadcasted_iota(jnp.int32, (n, n), 1)


def _row_to_col(row):
    return jnp.sum(_eye(row.shape[1]) * row, axis=1, keepdims=True)


def _conv_tiles(w, seq):
    return _pick(seq, 512, SUBLANES), _pick(w, 512, LANES)


def _conv_fwd(x, width, w, bias, seq, name):
    t = x.shape[0]
    kk = w.shape[0]
    tr, tc = _conv_tiles(width, seq)
    hb = tr // SUBLANES

    def body(*refs):
        if bias is None:
            x_ref, h_ref, w_ref, o_ref = refs
        else:
            x_ref, h_ref, w_ref, b_ref, o_ref = refs
        i = pl.program_id(1)
        first = (i * tr) % seq == 0
        halo = jnp.where(first, 0.0, h_ref[...])
        xe = jnp.concatenate([halo, x_ref[...]], axis=0)
        wv = w_ref[...]
        acc = xe[SUBLANES:, :] * wv[kk - 1:kk, :]
        for s in range(1, kk):
            acc = acc + pltpu.roll(xe, s, 0)[SUBLANES:, :] * wv[kk - 1 - s:kk - s, :]
        if bias is not None:
            acc = acc + b_ref[...]
        o_ref[...] = acc

    in_specs = [pl.BlockSpec((tr, tc), lambda j, i: (i, j)),
                pl.BlockSpec((SUBLANES, tc), lambda j, i: (jnp.maximum(i * hb - 1, 0), j)),
                pl.BlockSpec((kk, tc), lambda j, i: (0, j))]
    ops = [x, x, w]
    if bias is not None:
        in_specs.append(pl.BlockSpec((1, tc), lambda j, i: (0, j)))
        ops.append(bias)
    return pl.pallas_call(
        body, name=name, grid=(width // tc, t // tr), in_specs=in_specs,
        out_specs=pl.BlockSpec((tr, tc), lambda j, i: (i, j)), out_shape=jax.ShapeDtypeStruct((t, width), F32),
        compiler_params=_cparams(("parallel", "parallel")),
    )(*ops)


def _conv_bwd(dy, x, width, w, seq, with_bias, name, xcol=0, dx_dtype=F32):
    t = x.shape[0]
    kk = w.shape[0]
    tr, tc = _conv_tiles(width, seq)
    hb = tr // SUBLANES
    n_halo_blocks = t // SUBLANES
    assert xcol % tc == 0
    xb = xcol // tc

    def body(dy_ref, dyn_ref, x_ref, xh_ref, w_ref, dx_ref, dw_ref, *rest):
        i = pl.program_id(1)
        first = (i * tr) % seq == 0
        last = ((i + 1) * tr) % seq == 0
        dyc = dy_ref[...]
        dye = jnp.concatenate([dyc, jnp.where(last, 0.0, dyn_ref[...])], axis=0)
        xe = jnp.concatenate([jnp.where(first, 0.0, xh_ref[...]), x_ref[...]], axis=0)
        wv = w_ref[...]
        dx = dyc * wv[kk - 1:kk, :]
        dws = [None] * kk
        dws[kk - 1] = jnp.sum(dyc * xe[SUBLANES:, :], axis=0, keepdims=True)
        for s in range(1, kk):
            dx = dx + pltpu.roll(dye, tr + SUBLANES - s, 0)[:tr, :] * wv[kk - 1 - s:kk - s, :]
            dws[kk - 1 - s] = jnp.sum(dyc * pltpu.roll(xe, s, 0)[SUBLANES:, :], axis=0, keepdims=True)
        dx_ref[...] = dx.astype(dx_ref.dtype)

        @pl.when(i == 0)
        def _():
            for j in range(kk):
                dw_ref[j:j + 1, :] = dws[j]
            if with_bias:
                rest[0][...] = jnp.sum(dyc, axis=0, keepdims=True)

        @pl.when(i > 0)
        def _():
            for j in range(kk):
                dw_ref[j:j + 1, :] += dws[j]
            if with_bias:
                rest[0][...] += jnp.sum(dyc, axis=0, keepdims=True)

    cur = pl.BlockSpec((tr, tc), lambda j, i: (i, j))
    in_specs = [cur, pl.BlockSpec((SUBLANES, tc), lambda j, i: (jnp.minimum((i + 1) * hb, n_halo_blocks - 1), j)),
                pl.BlockSpec((tr, tc), lambda j, i: (i, j + xb)),
                pl.BlockSpec((SUBLANES, tc), lambda j, i: (jnp.maximum(i * hb - 1, 0), j + xb)),
                pl.BlockSpec((kk, tc), lambda j, i: (0, j))]
    out_specs = [cur, pl.BlockSpec((kk, tc), lambda j, i: (0, j))]
    out_shape = [jax.ShapeDtypeStruct((t, width), dx_dtype), jax.ShapeDtypeStruct((kk, width), F32)]
    if with_bias:
        out_specs.append(pl.BlockSpec((1, tc), lambda j, i: (0, j)))
        out_shape.append(jax.ShapeDtypeStruct((1, width), F32))
    return pl.pallas_call(
        body, name=name, grid=(width // tc, t // tr), in_specs=in_specs, out_specs=out_specs, out_shape=out_shape,
        compiler_params=_cparams(("parallel", "arbitrary")),
    )(dy, dy, x, x, w)


def _cumsum_rows(x, seq, reverse, name):
    t, w = x.shape
    tb = _pick(seq, 256, SUBLANES)
    nb = seq // tb

    def pos(b, i):
        return (b * nb + (nb - 1 - i if reverse else i), 0)

    def body(x_ref, o_ref, carry):
        i = pl.program_id(1)

        @pl.when(i == 0)
        def _():
            carry[...] = jnp.zeros_like(carry)

        blk = x_ref[...]
        r = lax.broadcasted_iota(jnp.int32, (tb, tb), 0)
        c = lax.broadcasted_iota(jnp.int32, (tb, tb), 1)
        m = ((r <= c) if reverse else (r >= c)).astype(F32)
        o_ref[...] = _dot(m, blk) + carry[...]
        carry[...] += jnp.sum(blk, axis=0, keepdims=True)

    return pl.pallas_call(
        body, name=name, grid=(t // seq, nb), in_specs=[pl.BlockSpec((tb, w), pos)], out_specs=pl.BlockSpec((tb, w), pos),
        out_shape=jax.ShapeDtypeStruct((t, w), F32), scratch_shapes=[pltpu.VMEM((1, w), F32)],
        compiler_params=_cparams(("parallel", "arbitrary")),
    )(x)


def _sgu_fwd(vn, u, w_s, b_col, name):
    t, a = vn.shape
    g = a // HEAD

    def body(v_ref, u_ref, w_ref, b_ref, y_ref):
        tri = _tri(A_CHUNK)
        for gi in range(g):
            sl = slice(gi * HEAD, (gi + 1) * HEAD)
            wc = jnp.where(tri, w_ref[gi], 0.0)
            sv = _dot(wc, v_ref[:, sl]) + b_ref[gi]
            y_ref[:, sl] = (u_ref[:, sl] * sv).astype(y_ref.dtype)

    blk = pl.BlockSpec((A_CHUNK, a), lambda i: (i, 0))
    return pl.pallas_call(
        body, name=name, grid=(t // A_CHUNK,),
        in_specs=[blk, blk, pl.BlockSpec(w_s.shape, lambda i: (0, 0, 0)), pl.BlockSpec(b_col.shape, lambda i: (0, 0, 0))],
        out_specs=blk, out_shape=jax.ShapeDtypeStruct((t, a), BF16), compiler_params=_cparams(("parallel",)),
    )(vn, u, w_s, b_col)


def _sgu_bwd(vn, u, dy, w_s, b_col, name):
    t, a = vn.shape
    g = a // HEAD

    def body(v_ref, u_ref, dy_ref, w_ref, b_ref, dv_ref, du_ref, dw_ref, db_ref):
        i = pl.program_id(0)
        tri = _tri(A_CHUNK)
        for gi in range(g):
            sl = slice(gi * HEAD, (gi + 1) * HEAD)
            wc = jnp.where(tri, w_ref[gi], 0.0)
            v = v_ref[:, sl]
            sv = _dot(wc, v) + b_ref[gi]
            dyb = dy_ref[:, sl]
            du_ref[:, sl] = dyb * sv
            dsv = dyb * u_ref[:, sl]
            dv_ref[:, sl] = _dot(wc, dsv, TN)
            dw = jnp.where(tri, _dot(dsv, v, NT), 0.0)
            db = jnp.sum(dsv, axis=1, keepdims=True)

            @pl.when(i == 0)
            def _(gi=gi, dw=dw, db=db):
                dw_ref[gi] = dw
                db_ref[gi] = db

            @pl.when(i > 0)
            def _(gi=gi, dw=dw, db=db):
                dw_ref[gi] += dw
                db_ref[gi] += db

    blk = pl.BlockSpec((A_CHUNK, a), lambda i: (i, 0))
    wsp = pl.BlockSpec(w_s.shape, lambda i: (0, 0, 0))
    bsp = pl.BlockSpec(b_col.shape, lambda i: (0, 0, 0))
    return pl.pallas_call(
        body, name=name, grid=(t // A_CHUNK,), in_specs=[blk, blk, blk, wsp, bsp], out_specs=[blk, blk, wsp, bsp],
        out_shape=[jax.ShapeDtypeStruct((t, a), F32), jax.ShapeDtypeStruct((t, a), F32),
                   jax.ShapeDtypeStruct(w_s.shape, F32), jax.ShapeDtypeStruct(b_col.shape, F32)],
        compiler_params=_cparams(("arbitrary",)),
    )(vn, u, dy, w_s, b_col)


def _fox_scores(q, k, cq_row, ck_row, diag, scale):
    s = lax.dot_general(q.astype(BF16), k.astype(BF16), NT, preferred_element_type=F32) * scale
    s = s + _row_to_col(cq_row) - ck_row
    mask = jnp.logical_or(jnp.logical_not(diag), _tri(q.shape[0]))
    return s, mask


def _fox_fwd(qn, kn, proj, v_blk0, c_rows, bsz, seq, nh, name):
    t = qn.shape[0]
    tq = _pick(seq, 512, LANES)
    nq = seq // tq
    scale = HEAD ** -0.5

    def body(q_ref, k_ref, v_ref, cq_ref, ck_ref, o_ref, lse_ref, m_s, l_s, acc_s):
        i, j = pl.program_id(2), pl.program_id(3)

        @pl.when(j == 0)
        def _():
            m_s[...] = jnp.full_like(m_s, -jnp.inf)
            l_s[...] = jnp.zeros_like(l_s)
            acc_s[...] = jnp.zeros_like(acc_s)

        @pl.when(j <= i)
        def _():
            s, mask = _fox_scores(q_ref[...], k_ref[...], cq_ref[...], ck_ref[...], j == i, scale)
            s = jnp.where(mask, s, -jnp.inf)
            m_new = jnp.maximum(m_s[...], jnp.max(s, axis=1, keepdims=True))
            p = jnp.exp(s - m_new)
            alpha = jnp.exp(m_s[...] - m_new)
            l_s[...] = alpha * l_s[...] + jnp.sum(p, axis=1, keepdims=True)
            acc_s[...] = alpha * acc_s[...] + lax.dot_general(p.astype(BF16), v_ref[...].astype(BF16), NN, preferred_element_type=F32)
            m_s[...] = m_new

        @pl.when(j == nq - 1)
        def _():
            o_ref[...] = acc_s[...] / l_s[...]
            lse_ref[...] = jnp.broadcast_to(m_s[...] + jnp.log(l_s[...]), lse_ref.shape)

    qspec = pl.BlockSpec((tq, HEAD), lambda b, h, i, j: (b * nq + i, h))
    kspec = pl.BlockSpec((tq, HEAD), lambda b, h, i, j: (b * nq + jnp.minimum(i, j), h))
    vspec = pl.BlockSpec((tq, HEAD), lambda b, h, i, j: (b * nq + jnp.minimum(i, j), v_blk0 + h))
    cq = pl.BlockSpec((None, None, 1, tq), lambda b, h, i, j: (b, h, 0, i))
    ck = pl.BlockSpec((None, None, 1, tq), lambda b, h, i, j: (b, h, 0, jnp.minimum(i, j)))
    return pl.pallas_call(
        body, name=name, grid=(bsz, nh, nq, nq), in_specs=[qspec, kspec, vspec, cq, ck], out_specs=[qspec, qspec],
        out_shape=[jax.ShapeDtypeStruct((t, nh * HEAD), F32)] * 2,
        scratch_shapes=[pltpu.VMEM((tq, 1), F32), pltpu.VMEM((tq, 1), F32), pltpu.VMEM((tq, HEAD), F32)],
        compiler_params=_cparams(("parallel", "parallel", "parallel", "arbitrary")),
    )(qn, kn, proj, c_rows, c_rows)


def _fox_p_dp(q, k, v, do, lse, cq_row, ck_row, diag, scale):
    s, mask = _fox_scores(q, k, cq_row, ck_row, diag, scale)
    p = jnp.where(mask, jnp.exp(s - jnp.max(lse, axis=1, keepdims=True)), 0.0)
    dp = lax.dot_general(do.astype(BF16), v.astype(BF16), NT, preferred_element_type=F32)
    return p, dp


def _fox_bwd_q(qn, kn, proj, v_blk0, do, lse, c_rows, bsz, seq, nh, name):
    t = qn.shape[0]
    tq = _pick(seq, 512, LANES)
    nq = seq // tq
    scale = HEAD ** -0.5

    def key_block(jj):
        return jnp.where(jj >= nq, jj - nq, jj)

    def body(q_ref, k_ref, v_ref, do_ref, lse_ref, cq_ref, ck_ref, dq_ref, dl_ref, dq_s, dl_s):
        i, jj = pl.program_id(2), pl.program_id(3)
        j = key_block(jj)

        @pl.when(jj == 0)
        def _():
            dq_s[...] = jnp.zeros_like(dq_s)
            dl_s[...] = jnp.zeros_like(dl_s)

        @pl.when(j <= i)
        def _():
            p, dp = _fox_p_dp(q_ref[...], k_ref[...], v_ref[...], do_ref[...], lse_ref[...], cq_ref[...], ck_ref[...], j == i, scale)

            @pl.when(jj < nq)
            def _():
                dl_s[...] += jnp.sum(p * dp, axis=1, keepdims=True)

            @pl.when(jj >= nq)
            def _():
                ds = p * (dp - dl_s[...])
                dq_s[...] += lax.dot_general(ds.astype(BF16), k_ref[...].astype(BF16), NN, preferred_element_type=F32) * scale

        @pl.when(jj == 2 * nq - 1)
        def _():
            dq_ref[...] = dq_s[...]
            dl_ref[...] = jnp.broadcast_to(dl_s[...], dl_ref.shape)

    qspec = pl.BlockSpec((tq, HEAD), lambda b, h, i, jj: (b * nq + i, h))
    kspec = pl.BlockSpec((tq, HEAD), lambda b, h, i, jj: (b * nq + jnp.minimum(i, key_block(jj)), h))
    vspec = pl.BlockSpec((tq, HEAD), lambda b, h, i, jj: (b * nq + jnp.minimum(i, key_block(jj)), v_blk0 + h))
    cq = pl.BlockSpec((None, None, 1, tq), lambda b, h, i, jj: (b, h, 0, i))
    ck = pl.BlockSpec((None, None, 1, tq), lambda b, h, i, jj: (b, h, 0, jnp.minimum(i, key_block(jj))))
    return pl.pallas_call(
        body, name=name, grid=(bsz, nh, nq, 2 * nq), in_specs=[qspec, kspec, vspec, qspec, qspec, cq, ck],
        out_specs=[qspec, qspec], out_shape=[jax.ShapeDtypeStruct((t, nh * HEAD), F32)] * 2,
        scratch_shapes=[pltpu.VMEM((tq, HEAD), F32), pltpu.VMEM((tq, 1), F32)],
        compiler_params=_cparams(("parallel", "parallel", "parallel", "arbitrary")),
    )(qn, kn, proj, do, lse, c_rows, c_rows)


def _fox_bwd_kv(qn, kn, proj, v_blk0, do, lse, delta, c_rows, bsz, seq, nh, name):
    t = qn.shape[0]
    tq = _pick(seq, 512, LANES)
    nq = seq // tq
    scale = HEAD ** -0.5

    def body(q_ref, k_ref, v_ref, do_ref, lse_ref, dl_ref, cq_ref, ck_ref, dk_ref, dv_ref, dc_ref, dk_s, dv_s, dc_s):
        j, i = pl.program_id(2), pl.program_id(3)

        @pl.when(i == 0)
        def _():
            dk_s[...] = jnp.zeros_like(dk_s)
            dv_s[...] = jnp.zeros_like(dv_s)
            dc_s[...] = jnp.zeros_like(dc_s)

        @pl.when(i >= j)
        def _():
            p, dp = _fox_p_dp(q_ref[...], k_ref[...], v_ref[...], do_ref[...], lse_ref[...], cq_ref[...], ck_ref[...], j == i, scale)
            ds = p * (dp - jnp.max(dl_ref[...], axis=1, keepdims=True))
            dv_s[...] += lax.dot_general(p.astype(BF16), do_ref[...].astype(BF16), TN, preferred_element_type=F32)
            dk_s[...] += lax.dot_general(ds.astype(BF16), q_ref[...].astype(BF16), TN, preferred_element_type=F32) * scale
            dc_s[...] -= jnp.sum(ds, axis=0, keepdims=True)

        @pl.when(i == nq - 1)
        def _():
            dk_ref[...] = dk_s[...]
            dv_ref[...] = dv_s[...]
            dc_ref[...] = dc_s[...]

    kspec = pl.BlockSpec((tq, HEAD), lambda b, h, j, i: (b * nq + j, h))
    vspec = pl.BlockSpec((tq, HEAD), lambda b, h, j, i: (b * nq + j, v_blk0 + h))
    qspec = pl.BlockSpec((tq, HEAD), lambda b, h, j, i: (b * nq + jnp.maximum(i, j), h))
    cq = pl.BlockSpec((None, None, 1, tq), lambda b, h, j, i: (b, h, 0, jnp.maximum(i, j)))
    ck = pl.BlockSpec((None, None, 1, tq), lambda b, h, j, i: (b, h, 0, j))
    return pl.pallas_call(
        body, name=name, grid=(bsz, nh, nq, nq), in_specs=[qspec, kspec, vspec, qspec, qspec, qspec, cq, ck],
        out_specs=[kspec, kspec, ck],
        out_shape=[jax.ShapeDtypeStruct((t, nh * HEAD), F32)] * 2 + [jax.ShapeDtypeStruct(c_rows.shape, F32)],
        scratch_shapes=[pltpu.VMEM((tq, HEAD), F32), pltpu.VMEM((tq, HEAD), F32), pltpu.VMEM((1, tq), F32)],
        compiler_params=_cparams(("parallel", "parallel", "parallel", "arbitrary")),
    )(qn, kn, proj, do, lse, delta, c_rows, c_rows)


@jax.custom_vjp
def _unit_lower_inverse(a_mat):
    c = a_mat.shape[-1]
    inv = _eye(c) - a_mat
    pw = _bdot3(a_mat, a_mat, BNN)
    n_sq = max(1, (c - 1).bit_length() - 1)
    for it in range(n_sq):
        inv = inv + _bdot3(inv, pw, BNN)
        if it < n_sq - 1:
            pw = _bdot3(pw, pw, BNN)
    return inv


def _unit_lower_inverse_fwd(a_mat):
    inv = _unit_lower_inverse(a_mat)
    return inv, inv


def _unit_lower_inverse_bwd(inv, d_inv):
    return (-_bdot3(_bdot3(inv, d_inv, BTN), inv, BNT),)


_unit_lower_inverse.defvjp(_unit_lower_inverse_fwd, _unit_lower_inverse_bwd)


def _gdn_chunk(qp, kp, vp, b_row, a_row, alog, dtb, state):
    hv = vp.shape[0]
    c = qp.shape[1]
    qc, kc, vc = _silu(qp), _silu(kp), _silu(vp)
    qh = qc * lax.rsqrt(jnp.sum(qc * qc, -1, keepdims=True) + RMS_EPS) * (HEAD ** -0.5)
    kh = kc * lax.rsqrt(jnp.sum(kc * kc, -1, keepdims=True) + RMS_EPS)
    q = jnp.stack([qh[h // 2] for h in range(hv)])
    k = jnp.stack([kh[h // 2] for h in range(hv)])
    beta_row = jax.nn.sigmoid(b_row)
    g_row = -jnp.exp(alog) * _softplus(a_row + dtb)
    ri = lax.broadcasted_iota(jnp.int32, (c, c), 0)
    ci = lax.broadcasted_iota(jnp.int32, (c, c), 1)
    eye = (ri == ci).astype(F32)
    tri = ri >= ci
    beta_col = jnp.sum(eye * beta_row, axis=2, keepdims=True)
    g_col = jnp.sum(eye * g_row, axis=2, keepdims=True)
    gc_col = jnp.sum(tri.astype(F32) * g_row, axis=2, keepdims=True)
    gc_row = jnp.sum(g_col * (ri <= ci).astype(F32), axis=1, keepdims=True)
    decay = jnp.where(tri, jnp.exp(jnp.where(tri, gc_col - gc_row, 0.0)), 0.0)
    kb = k * beta_col
    a_mat = jnp.where(ri > ci, _bdot(kb, k, BNT) * decay, 0.0)
    egc = jnp.exp(gc_col)
    inv = _unit_lower_inverse(a_mat)
    u = _bdot(inv, vc * beta_col, BNN)
    w = _bdot(inv, kb * egc, BNN)
    attn = _bdot(q, k, BNT) * decay
    g_last = jnp.sum(g_row, axis=2, keepdims=True)
    v_new = u - _bdot(w, state, BNN)
    o = _bdot(q * egc, state, BNN) + _bdot(attn, v_new, BNN)
    new_state = state * jnp.exp(g_last) + _bdot(k * jnp.exp(g_last - gc_col), v_new, BTN)
    return o, new_state


def _gdn_group(nvh):
    return 4 if nvh % 4 == 0 else 2


def _gdn_specs(nch, nkh, hb, rev):
    def n_of(n):
        return nch - 1 - n if rev else n

    hk = hb // 2
    per_k = pl.BlockSpec((GDN_CHUNK, hk * HEAD), lambda g, b, n: (b * nch + n_of(n), g))
    q = per_k
    k = pl.BlockSpec((GDN_CHUNK, hk * HEAD), lambda g, b, n: (b * nch + n_of(n), nkh // hk + g))
    v = pl.BlockSpec((GDN_CHUNK, hb * HEAD), lambda g, b, n: (b * nch + n_of(n), 2 * nkh // hb + g))
    per_v = pl.BlockSpec((GDN_CHUNK, hb * HEAD), lambda g, b, n: (b * nch + n_of(n), g))
    row = pl.BlockSpec((hb, None, None, 1, GDN_CHUNK), lambda g, b, n: (g, b, n_of(n), 0, 0))
    sc = pl.BlockSpec((hb, 1, 1), lambda g, b, n: (g, 0, 0))
    st = pl.BlockSpec((hb, None, None, HEAD, HEAD), lambda g, b, n: (g, b, n_of(n), 0, 0))
    return q, k, v, per_k, per_v, row, sc, st


def _stack_heads(ref, n):
    return jnp.stack([ref[:, h * HEAD:(h + 1) * HEAD] for h in range(n)])


def _gdn_fwd(conv, b_rows, a_rows, alog, dtb, bsz, seq, nvh, name):
    t = conv.shape[0]
    nch = seq // GDN_CHUNK
    nkh = nvh // 2
    hb = _gdn_group(nvh)
    q, k, v, _, per_v, row, sc, st = _gdn_specs(nch, nkh, hb, False)

    def body(q_ref, k_ref, v_ref, b_ref, a_ref, al_ref, dt_ref, o_ref, st_ref, state):
        @pl.when(pl.program_id(2) == 0)
        def _():
            state[...] = jnp.zeros_like(state)

        st_ref[...] = state[...]
        o, new_state = _gdn_chunk(_stack_heads(q_ref, hb // 2), _stack_heads(k_ref, hb // 2), _stack_heads(v_ref, hb),
                                  b_ref[...], a_ref[...], al_ref[...], dt_ref[...], state[...])
        for h in range(hb):
            o_ref[:, h * HEAD:(h + 1) * HEAD] = o[h]
        state[...] = new_state

    return pl.pallas_call(
        body, name=name, grid=(nvh // hb, bsz, nch), in_specs=[q, k, v, row, row, sc, sc], out_specs=[per_v, st],
        out_shape=[jax.ShapeDtypeStruct((t, nvh * HEAD), F32), jax.ShapeDtypeStruct((nvh, bsz, nch, HEAD, HEAD), F32)],
        scratch_shapes=[pltpu.VMEM((hb, HEAD, HEAD), F32)],
        compiler_params=_cparams(("parallel", "parallel", "arbitrary")),
    )(conv, conv, conv, b_rows, a_rows, alog, dtb)


def _gdn_bwd(conv, b_rows, a_rows, alog, dtb, states, do, bsz, seq, nvh, name):
    t = conv.shape[0]
    nch = seq // GDN_CHUNK
    nkh = nvh // 2
    hb = _gdn_group(nvh)
    q, k, v, per_k, per_v, row, sc, st = _gdn_specs(nch, nkh, hb, True)

    def body(q_ref, k_ref, v_ref, b_ref, a_ref, al_ref, dt_ref, st_ref, do_ref,
             dq_ref, dk_ref, dv_ref, db_ref, da_ref, dal_ref, ddt_ref, dstate):
        b, n = pl.program_id(1), pl.program_id(2)

        @pl.when(n == 0)
        def _():
            dstate[...] = jnp.zeros_like(dstate)

        _, vjp = jax.vjp(_gdn_chunk, _stack_heads(q_ref, hb // 2), _stack_heads(k_ref, hb // 2), _stack_heads(v_ref, hb),
                         b_ref[...], a_ref[...], al_ref[...], dt_ref[...], st_ref[...])
        dq, dk, dv, db, da, dal, ddt, dst = vjp((_stack_heads(do_ref, hb), dstate[...]))
        for h in range(hb // 2):
            dq_ref[:, h * HEAD:(h + 1) * HEAD] = dq[h]
            dk_ref[:, h * HEAD:(h + 1) * HEAD] = dk[h]
        for h in range(hb):
            dv_ref[:, h * HEAD:(h + 1) * HEAD] = dv[h]
        db_ref[...] = db
        da_ref[...] = da
        dstate[...] = dst
        start = jnp.logical_and(b == 0, n == 0)

        @pl.when(start)
        def _():
            dal_ref[...] = dal
            ddt_ref[...] = ddt

        @pl.when(jnp.logical_not(start))
        def _():
            dal_ref[...] += dal
            ddt_ref[...] += ddt

    f = lambda *s: jax.ShapeDtypeStruct(s, F32)
    return pl.pallas_call(
        body, name=name, grid=(nvh // hb, bsz, nch), in_specs=[q, k, v, row, row, sc, sc, st, per_v],
        out_specs=[per_k, per_k, per_v, row, row, sc, sc],
        out_shape=[f(t, nkh * HEAD), f(t, nkh * HEAD), f(t, nvh * HEAD), f(*b_rows.shape), f(*a_rows.shape), f(nvh, 1, 1), f(nvh, 1, 1)],
        scratch_shapes=[pltpu.VMEM((hb, HEAD, HEAD), F32)],
        compiler_params=_cparams(("arbitrary", "arbitrary", "arbitrary")),
    )(conv, conv, conv, b_rows, a_rows, alog, dtb, states, do)


ANY = pl.BlockSpec(memory_space=pl.ANY)
FLIPS = (2, 1, 3)


def _place():
    x, y, c = lax.axis_index("x"), lax.axis_index("y"), lax.axis_index("c")
    chips = [(1 - x, y), (x, 1 - y), (1 - x, 1 - y)]
    return x, y, c, chips


def _chip_index():
    return 2 * lax.axis_index("x") + lax.axis_index("y")


def _core_index():
    return lax.axis_index("c")


def _wide(w):
    return w if (w <= 4096 or w % LANES) else _pick(w, 2048, LANES)


def _rows(start, size, mult=16):
    return pl.ds(pl.multiple_of(start, mult), size)


def _cast_window(w_stack, layer, cls, name):
    _, r, w = w_stack.shape
    if cls == "U":
        tr = _pick(r, 256, 16)
        grid = (r // tr, 1)
        in_spec = pl.BlockSpec((None, tr, w), lambda i, j: (layer, i, 0))
        out_spec = pl.BlockSpec((None, tr, w), lambda i, j: (_chip_index(), i, 0))
        out_shape = (N_CHIPS, r, w)
    else:
        tr, tc = _pick(r, 512, 16), _wide(w)
        nbr, nbc = r // tr, w // tc
        grid = (nbr, nbc)
        in_spec = pl.BlockSpec((None, tr, tc), lambda i, j: (layer, i, j))
        if cls == "C":
            out_spec = pl.BlockSpec((tr, tc), lambda i, j: (i, _chip_index() * nbc + j))
            out_shape = (r, N_CHIPS * w)
        else:
            out_spec = pl.BlockSpec((tr, tc), lambda i, j: (_chip_index() * nbr + i, j))
            out_shape = (N_CHIPS * r, w)

    def body(x_ref, o_ref):
        o_ref[...] = x_ref[...].astype(o_ref.dtype)

    return pl.pallas_call(body, name=name, grid=grid, in_specs=[in_spec], out_specs=out_spec,
                          out_shape=jax.ShapeDtypeStruct(out_shape, BF16), compiler_params=_cparams(("parallel", "parallel")))(w_stack)


def _halved(buf, cls):
    return {"C": buf.shape[0], "U": buf.shape[1], "R": buf.shape[0] // N_CHIPS}[cls]


def _part(buf, cls, chip, start, size):
    if cls == "C":
        w = buf.shape[1] // N_CHIPS
        return buf.at[_rows(start, size), pl.ds(chip * w, w)]
    if cls == "U":
        return buf.at[chip, _rows(start, size), :]
    r = buf.shape[0] // N_CHIPS
    return buf.at[_rows(chip * r + start, size), :]


def _gather_layer(bufs, classes, name):
    n = len(bufs)
    job = _gather_job(bufs, classes)

    def body(*refs):
        outs = refs[n:2 * n]
        job["start"](refs[:n], outs, refs[2 * n], refs[2 * n + 1], True)
        job["finish"](refs[:n], outs, refs[2 * n], refs[2 * n + 1], True)

    return pl.pallas_call(
        body, name=name, in_specs=[ANY] * n, out_specs=[ANY] * n, out_shape=job["out_shapes"],
        input_output_aliases={t: t for t in range(n)},
        scratch_shapes=[pltpu.SemaphoreType.DMA((job["n_sems"],)), pltpu.SemaphoreType.DMA((job["n_sems"],))],
    )(*bufs)


def _on_chip(when, fn):
    x, y, _, _ = _place()
    me = 2 * x + y
    for s in range(N_CHIPS):
        cond = (me == s) if when is True else jnp.logical_and(when, me == s)
        pl.when(cond)(functools.partial(fn, s))


def _gather_job(bufs, classes):
    n = len(bufs)

    def copy(outs, send_sems, recv_sems, t, k, chip, start, to):
        size = _halved(outs[t], classes[t]) // 2
        part = _part(outs[t], classes[t], chip, start, size)
        return pltpu.make_async_remote_copy(src_ref=part, dst_ref=part, send_sem=send_sems.at[6 * t + k],
                                            recv_sem=recv_sems.at[6 * t + k], device_id=to, device_id_type=MESH)

    def halves(outs):
        return [_halved(outs[t], classes[t]) // 2 for t in range(n)]

    def start(ins, outs, send_sems, recv_sems, when):
        x, y, c, chips = _place()

        def run(s_me):
            for t, half in enumerate(halves(outs)):
                for k, (cx, cy) in enumerate(chips):
                    copy(outs, send_sems, recv_sems, t, k, s_me, c * half, (cx, cy, c)).start()
        _on_chip(when, run)

    def finish(ins, outs, send_sems, recv_sems, when):
        x, y, c, chips = _place()

        def run(s_me):
            passed = []
            for t, half in enumerate(halves(outs)):
                for k in range(3):
                    copy(outs, send_sems, recv_sems, t, k, s_me ^ FLIPS[k], c * half, (x, y, c)).wait_recv()
                    fwd = copy(outs, send_sems, recv_sems, t, 3 + k, s_me ^ FLIPS[k], c * half, (x, y, 1 - c))
                    fwd.start()
                    passed.append(fwd)
            for t, half in enumerate(halves(outs)):
                for k in range(3):
                    copy(outs, send_sems, recv_sems, t, 3 + k, s_me ^ FLIPS[k], (1 - c) * half, (x, y, c)).wait_recv()
            for t, half in enumerate(halves(outs)):
                for k, (cx, cy) in enumerate(chips):
                    copy(outs, send_sems, recv_sems, t, k, s_me, c * half, (cx, cy, c)).wait_send()
            for fwd in passed:
                fwd.wait_send()
        _on_chip(when, run)

    return dict(ins=list(bufs), alias=True, n_sems=6 * n, start=start, finish=finish,
                out_shapes=[jax.ShapeDtypeStruct(b.shape, b.dtype) for b in bufs])


def _assemble(slots, width, name):
    _, r, w = slots.shape
    tr = _pick(r, 256, 16)

    def body(s_ref, o_ref):
        parts = [s_ref[s] for s in range(N_CHIPS)]
        if width > N_CHIPS * w:
            parts.append(jnp.zeros((tr, width - N_CHIPS * w), slots.dtype))
        o_ref[...] = jnp.concatenate(parts, axis=1)

    return pl.pallas_call(
        body, name=name, grid=(r // tr,), in_specs=[pl.BlockSpec((N_CHIPS, tr, w), lambda i: (0, i, 0))],
        out_specs=pl.BlockSpec((tr, width), lambda i: (i, 0)), out_shape=jax.ShapeDtypeStruct((r, width), slots.dtype),
        compiler_params=_cparams(("parallel",)),
    )(slots)


def _rs_pair(dws, classes, name):
    n = len(dws)

    def shape_of(d, cls):
        return (N_CHIPS, d.shape[1] // 2, d.shape[2]) if cls == "R" else (d.shape[0] // 2, d.shape[1])

    def body(*refs):
        ins, outs = refs[:n], refs[n:2 * n]
        send_sems, recv_sems = refs[2 * n], refs[2 * n + 1]
        x, y, c, _ = _place()
        cps = []
        for t in range(n):
            if classes[t] == "R":
                h = ins[t].shape[1] // 2
                src = ins[t].at[:, _rows((1 - c) * h, h), :]
            else:
                h = ins[t].shape[0] // 2
                src = ins[t].at[_rows((1 - c) * h, h), :]
            cp = pltpu.make_async_remote_copy(src_ref=src, dst_ref=outs[t], send_sem=send_sems.at[t], recv_sem=recv_sems.at[t],
                                              device_id=(x, y, 1 - c), device_id_type=MESH)
            cp.start()
            cps.append(cp)
        for cp in cps:
            cp.wait()

    return pl.pallas_call(
        body, name=name, in_specs=[ANY] * n, out_specs=[ANY] * n,
        out_shape=[jax.ShapeDtypeStruct(shape_of(d, cls), d.dtype) for d, cls in zip(dws, classes)],
        scratch_shapes=[pltpu.SemaphoreType.DMA((n,)), pltpu.SemaphoreType.DMA((n,))],
    )(*dws)


def _rs_add(dw, got, cls, w, name):
    if cls == "R":
        _, r, _ = dw.shape
        h = r // 2
        th, tc = _pick(h, 256, 16), _wide(w)
        nbh = h // th
        grid = (N_CHIPS, nbh, w // tc)
        in_specs = [pl.BlockSpec((None, th, tc), lambda s, i, j: (s, _core_index() * nbh + i, j)),
                    pl.BlockSpec((None, th, tc), lambda s, i, j: (s, i, j))]
        out_spec = pl.BlockSpec((None, th, tc), lambda s, i, j: (s, i, j))
        sem = ("parallel", "parallel", "parallel")

        def body(a_ref, b_ref, o_ref):
            o_ref[...] = (a_ref[...] + b_ref[...]).astype(o_ref.dtype)
    elif cls == "C":
        r = dw.shape[0]
        h = r // 2
        th, tc = _pick(h, 256, 16), _wide(w)
        nbh, nbc = h // th, w // tc
        grid = (N_CHIPS, nbh, nbc)
        in_specs = [pl.BlockSpec((th, tc), lambda s, i, j: (_core_index() * nbh + i, s * nbc + j)),
                    pl.BlockSpec((th, tc), lambda s, i, j: (i, s * nbc + j))]
        out_spec = pl.BlockSpec((None, th, tc), lambda s, i, j: (s, i, j))
        sem = ("parallel", "parallel", "parallel")

        def body(a_ref, b_ref, o_ref):
            o_ref[...] = (a_ref[...] + b_ref[...]).astype(o_ref.dtype)
    else:
        r, wp = dw.shape
        h = r // 2
        th = _pick(h, 64, 16)
        nbh = h // th
        grid = (nbh,)
        in_specs = [pl.BlockSpec((th, wp), lambda i: (_core_index() * nbh + i, 0)), pl.BlockSpec((th, wp), lambda i: (i, 0))]
        out_spec = pl.BlockSpec((N_CHIPS, th, w), lambda i: (0, i, 0))
        sem = ("parallel",)

        def body(a_ref, b_ref, o_ref):
            tot = a_ref[...] + b_ref[...]
            for s in range(N_CHIPS):
                o_ref[s] = tot[:, s * w:(s + 1) * w].astype(o_ref.dtype)

    return pl.pallas_call(body, name=name, grid=grid, in_specs=in_specs, out_specs=out_spec,
                          out_shape=jax.ShapeDtypeStruct((N_CHIPS, h, w), BF16), compiler_params=_cparams(sem))(dw, got)


def _rs_chip(zs, name):
    n = len(zs)
    job = _chip_job(zs)

    def body(*refs):
        job["start"](refs[:n], refs[n:2 * n], refs[2 * n], refs[2 * n + 1], True)
        job["finish"](refs[:n], refs[n:2 * n], refs[2 * n], refs[2 * n + 1], True)

    return pl.pallas_call(
        body, name=name, in_specs=[ANY] * n, out_specs=[ANY] * n, out_shape=job["out_shapes"],
        scratch_shapes=[pltpu.SemaphoreType.DMA((job["n_sems"],)), pltpu.SemaphoreType.DMA((job["n_sems"],))],
    )(*zs)


def _chip_job(zs):
    n = len(zs)

    def copies(ins, outs, send_sems, recv_sems):
        x, y, c, chips = _place()
        return [pltpu.make_async_remote_copy(src_ref=ins[t].at[2 * cx + cy], dst_ref=outs[t].at[k], send_sem=send_sems.at[3 * t + k],
                                             recv_sem=recv_sems.at[3 * t + k], device_id=(cx, cy, c), device_id_type=MESH)
                for t in range(n) for k, (cx, cy) in enumerate(chips)]

    def start(ins, outs, send_sems, recv_sems, when):
        def run():
            for cp in copies(ins, outs, send_sems, recv_sems):
                cp.start()
        run() if when is True else pl.when(when)(run)

    def finish(ins, outs, send_sems, recv_sems, when):
        def run():
            for cp in copies(ins, outs, send_sems, recv_sems):
                cp.wait()
        run() if when is True else pl.when(when)(run)

    return dict(ins=list(zs), alias=False, n_sems=3 * n, start=start, finish=finish,
                out_shapes=[jax.ShapeDtypeStruct((3,) + z.shape[1:], z.dtype) for z in zs])


def _rs_sum(z, parts, name):
    _, h, w = z.shape
    th = _pick(h, 256, 16)
    tc = _wide(w)
    nbh = h // th

    def body(z_ref, k_ref, o_ref):
        acc = z_ref[...].astype(F32)
        for k in range(3):
            acc = acc + k_ref[k].astype(F32)
        o_ref[...] = acc

    return pl.pallas_call(
        body, name=name, grid=(nbh, w // tc),
        in_specs=[pl.BlockSpec((None, th, tc), lambda i, j: (_chip_index(), i, j)), pl.BlockSpec((3, th, tc), lambda i, j: (0, i, j))],
        out_specs=pl.BlockSpec((th, tc), lambda i, j: (_core_index() * nbh + i, j)),
        out_shape=jax.ShapeDtypeStruct((2 * h, w), F32), compiler_params=_cparams(("parallel", "parallel")))(z, parts)


def _rs_join(bufs, name):
    n = len(bufs)

    def body(*refs):
        outs = refs[n:2 * n]
        send_sems, recv_sems = refs[2 * n], refs[2 * n + 1]
        x, y, c, _ = _place()
        cps = []
        for t in range(n):
            h = outs[t].shape[0] // 2
            mine = outs[t].at[_rows(c * h, h), :]
            cp = pltpu.make_async_remote_copy(src_ref=mine, dst_ref=mine, send_sem=send_sems.at[t], recv_sem=recv_sems.at[t],
                                              device_id=(x, y, 1 - c), device_id_type=MESH)
            cp.start()
            cps.append(cp)
        for t in range(n):
            h = outs[t].shape[0] // 2
            other = outs[t].at[_rows((1 - c) * h, h), :]
            pltpu.make_async_remote_copy(src_ref=other, dst_ref=other, send_sem=send_sems.at[t], recv_sem=recv_sems.at[t],
                                         device_id=(x, y, c), device_id_type=MESH).wait_recv()
        for cp in cps:
            cp.wait_send()

    return pl.pallas_call(
        body, name=name, in_specs=[ANY] * n, out_specs=[ANY] * n,
        out_shape=[jax.ShapeDtypeStruct(b.shape, b.dtype) for b in bufs],
        input_output_aliases={t: t for t in range(n)},
        scratch_shapes=[pltpu.SemaphoreType.DMA((n,)), pltpu.SemaphoreType.DMA((n,))],
    )(*bufs)


def _rs_chip_sums(dws, classes, widths, tag):
    gots = _rs_pair(dws, classes, f"rs_pair_{tag}")
    return [_rs_add(d, g, cls, w, f"rs_add_{tag}_{t}") for t, (d, g, cls, w) in enumerate(zip(dws, gots, classes, widths))]


def _rs_finish(zs, parts, tag):
    halves = [_rs_sum(z, p, f"rs_sum_{tag}_{t}") for t, (z, p) in enumerate(zip(zs, parts))]
    return _rs_join(halves, f"rs_join_{tag}")


def _reduce_scatter_layer(dws, classes, widths, tag):
    zs = _rs_chip_sums(dws, classes, widths, tag)
    return _rs_finish(zs, _rs_chip(zs, f"rs_chip_{tag}"), tag)


def _gather_shards(mine, name):
    r, w = mine.shape
    rh = r // 2

    def body(mine_ref, out_ref, send_sems, recv_sems, local_sem):
        x, y, c, chips = _place()
        me = 2 * x + y
        half = _rows(c * rh, rh)
        other = _rows((1 - c) * rh, rh)

        def copy(k, src, chip, rows, to):
            return pltpu.make_async_remote_copy(src_ref=src, dst_ref=out_ref.at[chip, rows], send_sem=send_sems.at[k],
                                                recv_sem=recv_sems.at[k], device_id=to, device_id_type=MESH)

        local = pltpu.make_async_copy(mine_ref, out_ref.at[me], local_sem)
        local.start()
        sends = [copy(k, mine_ref.at[half], me, half, (cx, cy, c)) for k, (cx, cy) in enumerate(chips)]
        for s in sends:
            s.start()
        passed = []
        for k, (cx, cy) in enumerate(chips):
            chip = 2 * cx + cy
            copy(k, mine_ref.at[half], chip, half, (x, y, c)).wait_recv()
            fwd = copy(3 + k, out_ref.at[chip, half], chip, half, (x, y, 1 - c))
            fwd.start()
            passed.append(fwd)
        for k, (cx, cy) in enumerate(chips):
            copy(3 + k, mine_ref.at[half], 2 * cx + cy, other, (x, y, c)).wait_recv()
        for s in sends + passed:
            s.wait_send()
        local.wait()

    return pl.pallas_call(
        body, name=name, in_specs=[ANY], out_specs=ANY, out_shape=jax.ShapeDtypeStruct((N_CHIPS, r, w), mine.dtype),
        scratch_shapes=[pltpu.SemaphoreType.DMA((6,)), pltpu.SemaphoreType.DMA((6,)), pltpu.SemaphoreType.DMA],
    )(mine)


def _sum_slots(slots, name):
    n, r, w = slots.shape
    tr = _pick(r, 256, SUBLANES)

    def body(s_ref, o_ref):
        acc = s_ref[0]
        for k in range(1, n):
            acc = acc + s_ref[k]
        o_ref[...] = acc

    return pl.pallas_call(
        body, name=name, grid=(r // tr,), in_specs=[pl.BlockSpec((n, tr, w), lambda i: (0, i, 0))],
        out_specs=pl.BlockSpec((tr, w), lambda i: (i, 0)), out_shape=jax.ShapeDtypeStruct((r, w), F32),
        compiler_params=_cparams(("parallel",)),
    )(slots)


def _all_to_all_sum(flat, name):
    r, w = flat.shape

    def body(f_ref, out_ref, send_sems, recv_sems, local_sem):
        x, y, c, _ = _place()
        me = 4 * x + 2 * y + c
        local = pltpu.make_async_copy(f_ref, out_ref.at[me], local_sem)
        local.start()
        sends = []
        for k in range(1, 8):
            peer = (x ^ (k >> 2), y ^ ((k >> 1) & 1), c ^ (k & 1))
            s = pltpu.make_async_remote_copy(src_ref=f_ref, dst_ref=out_ref.at[me], send_sem=send_sems.at[k - 1],
                                             recv_sem=recv_sems.at[k - 1], device_id=peer, device_id_type=MESH)
            s.start()
            sends.append(s)
        for k in range(1, 8):
            peer_slot = 4 * (x ^ (k >> 2)) + 2 * (y ^ ((k >> 1) & 1)) + (c ^ (k & 1))
            pltpu.make_async_remote_copy(src_ref=f_ref, dst_ref=out_ref.at[peer_slot], send_sem=send_sems.at[k - 1],
                                         recv_sem=recv_sems.at[k - 1], device_id=(x, y, c), device_id_type=MESH).wait_recv()
        for s in sends:
            s.wait_send()
        local.wait()

    slots = pl.pallas_call(
        body, name=name, in_specs=[ANY], out_specs=ANY, out_shape=jax.ShapeDtypeStruct((8, r, w), flat.dtype),
        scratch_shapes=[pltpu.SemaphoreType.DMA((7,)), pltpu.SemaphoreType.DMA((7,)), pltpu.SemaphoreType.DMA],
    )(flat)
    return _sum_slots(slots, name + "_sum")


def _pack(pieces, lead):
    flat = []
    n_lead = len(lead)
    for p in pieces:
        f = p.reshape(*lead, -1)
        pad = (-f.shape[-1]) % PACK_COLS
        if pad:
            f = jnp.pad(f, [(0, 0)] * n_lead + [(0, pad)])
        flat.append(f)
    f = jnp.concatenate(flat, axis=-1) if len(flat) > 1 else flat[0]
    pad = (-f.shape[-1]) % (32 * PACK_COLS)
    if pad:
        f = jnp.pad(f, [(0, 0)] * n_lead + [(0, pad)])
    return f.reshape(*lead, -1, PACK_COLS)


def _unpack(buf, shapes, lead):
    f = buf.reshape(*lead, -1)
    out, off = [], 0
    for shp in shapes:
        n = 1
        for s in shp:
            n *= s
        out.append(f[..., off:off + n].reshape(*lead, *shp))
        off += n + ((-n) % PACK_COLS)
    return out


def _join(g, axis):
    g = jnp.moveaxis(g, 0, axis)
    return g.reshape(*g.shape[:axis], g.shape[axis] * g.shape[axis + 1], *g.shape[axis + 2:])


def _split(full, axis):
    shp = full.shape
    g = full.reshape(*shp[:axis], N_CHIPS, shp[axis] // N_CHIPS, *shp[axis + 1:])
    return jnp.moveaxis(g, axis, 0)


def _norm_bwd(xs, dres, dhs, gain, name):
    def fn(x, dr, *rest):
        dh = rest[0]
        for d in rest[1:-1]:
            dh = dh + d
        _, vjp = jax.vjp(_rms, x, rest[-1])
        dx, dg = vjp(dh.astype(F32))
        return dr + dx, dg
    return _rowmap(fn, [xs, dres] + list(dhs), [gain], name, n_acc=1)


def _mm(a, b, kind, dtype, name, job):
    if job is None:
        return _matmul(a, b, kind, dtype, name), None
    return _matmul(a, b, kind, dtype, name, job=job)


def _ffn_fwd(x1, h2, wg, wu, conv_w, conv_b, wd, seq, tag, jobs=(None, None, None)):
    gpre, j0 = _mm(h2, wg, "nn", F32, f"ffn_gate_{tag}", jobs[0])
    up, j1 = _mm(h2, wu, "nn", BF16, f"ffn_up_{tag}", jobs[1])
    gate = _conv_fwd(gpre, gpre.shape[1], conv_w, conv_b, seq, f"ffn_conv_{tag}")
    act, = _rowmap(lambda g, u: (_silu(g) * u).astype(BF16), [gate, up], [], f"ffn_act_{tag}", tc=_pick(gate.shape[1], 1024, LANES))
    f, j2 = _mm(act, wd, "nn", F32, f"ffn_down_{tag}", jobs[2])
    return f, (gpre, up, gate, act), [j0, j1, j2]


def _ffn_bwd(dx2, h2, saved, wg, wu, conv_w, wd, seq, tag, jobs=(None,) * 5):
    gpre, up, gate, act = saved
    da, j0 = _mm(dx2, wd, "nt", BF16, f"ffn_dact_{tag}", jobs[0])
    d_wd, j1 = _mm(act, dx2, "tn", F32, f"ffn_dwd_{tag}", jobs[1])

    def act_bwd(g, u, d):
        _, vjp = jax.vjp(lambda g_, u_: _silu(g_) * u_, g, u)
        return vjp(d.astype(F32))
    dgate, dup = _rowmap(act_bwd, [gate, up, da], [], f"ffn_dactfn_{tag}", tc=_pick(gate.shape[1], 1024, LANES))
    dgpre, d_cw, d_cb = _conv_bwd(dgate, gpre, gpre.shape[1], conv_w, seq, True, f"ffn_dconv_{tag}", dx_dtype=BF16)
    dh_a, j2 = _mm(dgpre, wg, "nt", F32, f"ffn_dh_gate_{tag}", jobs[2])
    dh_b, j3 = _mm(dup, wu, "nt", F32, f"ffn_dh_up_{tag}", jobs[3])
    d_wg, j4 = _mm(h2, dgpre, "tn", F32, f"ffn_dwg_{tag}", jobs[4])
    d_wu = _matmul(h2, dup, "tn", F32, f"ffn_dwu_{tag}")
    grads = dict(ffn_w_gate=d_wg, ffn_w_up=d_wu, ffn_conv_w=d_cw, ffn_conv_b=d_cb.reshape(-1), ffn_w_down=d_wd)
    return (dh_a, dh_b), grads, [j0, j1, j2, j3, j4]


def _gmlp_fwd(h, w, tag, jobs=(None, None)):
    a = w["a_w_out"].shape[0]
    p, j0 = _mm(h, w["a_w_in"], "nn", F32, f"a_in_{tag}", jobs[0])
    b_in, vnorm = w["a_b_in"].reshape(1, -1), w["a_v_norm"].reshape(1, -1)

    def fn(p_, b_, g_):
        hh = _gelu_tanh(p_ + b_)
        return hh[:, :a], _rms(hh[:, a:], g_)
    u, vn = _rowmap(fn, [p], [b_in, vnorm], f"a_gelu_{tag}")
    b_col = w["a_b_s"][:, :, None]
    y = _sgu_fwd(vn, u, w["a_w_s"], b_col, f"a_sgu_{tag}")
    m, j1 = _mm(y, w["a_w_out"], "nn", F32, f"a_out_{tag}", jobs[1])
    return m, (p, u, vn, y, fn, b_in, vnorm, b_col), [j0, j1]


def _gmlp_bwd(dm, h, saved, w, tag):
    p, u, vn, y, fn, b_in, vnorm, b_col = saved
    dy = _matmul(dm, w["a_w_out"], "nt", F32, f"a_dy_{tag}")
    d_wout = _matmul(y, dm, "tn", F32, f"a_dwout_{tag}")
    dvn, du, d_ws, d_bcol = _sgu_bwd(vn, u, dy, w["a_w_s"], b_col, f"a_dsgu_{tag}")

    def bwd(p_, du_, dvn_, b_, g_):
        _, vjp = jax.vjp(fn, p_, b_, g_)
        return vjp((du_, dvn_))
    dp, d_bin, d_vnorm = _rowmap(bwd, [p, du, dvn], [b_in, vnorm], f"a_dgelu_{tag}", n_acc=2)
    dh = _matmul(dp, w["a_w_in"], "nt", F32, f"a_dh_{tag}")
    d_win = _matmul(h, dp, "tn", F32, f"a_dwin_{tag}")
    return (dh,), dict(a_w_in=d_win, a_b_in=d_bin.reshape(-1), a_v_norm=d_vnorm.reshape(-1), a_w_s=d_ws, a_b_s=d_bcol[:, :, 0], a_w_out=d_wout)


def _fox_fwd_mixer(h, w, bsz, seq, tag, jobs=(None, None)):
    d = h.shape[1]
    nh = d // HEAD
    win = w["b_w_in"]
    wp = win.shape[1]
    proj, j0 = _mm(h, win, "nn", F32, f"b_in_{tag}", jobs[0])
    gq, gk = w["b_q_norm"].reshape(1, HEAD), w["b_k_norm"].reshape(1, HEAD)
    bf = jnp.pad(w["b_b_f"].reshape(1, nh), ((0, 0), (0, LANES - nh)))

    def prep(pq, pk, pfl, gq_, gk_, bf_):
        qn = jnp.concatenate([_rms(x, gq_) for x in _heads(pq)], axis=1)
        kn = jnp.concatenate([_rms(x, gk_) for x in _heads(pk)], axis=1)
        return qn, kn, -_softplus(-(pfl + bf_))
    wins = [_win(proj, d, 0), _win(proj, d, 1), _win(proj, LANES, 4 * d // LANES)]

    def prep_fwd(pq, pk, pfl, gq_, gk_, bf_):
        qn, kn, lf = prep(pq, pk, pfl, gq_, gk_, bf_)
        return qn.astype(BF16), kn.astype(BF16), lf
    qn, kn, lf = _rowmap(prep_fwd, wins, [gq, gk, bf], f"b_prep_{tag}")
    cs = _cumsum_rows(lf, seq, False, f"b_cumsum_{tag}")
    c_rows = cs[:, :nh].reshape(bsz, seq, nh).transpose(0, 2, 1)[:, :, None, :]
    o, lse = _fox_fwd(qn, kn, proj, 2 * nh, c_rows, bsz, seq, nh, f"b_attn_{tag}")
    og = _win(proj, d, 3)
    y, = _rowmap(lambda o_, g_: (o_ * jax.nn.sigmoid(g_)).astype(BF16), [o, og], [], f"b_gate_{tag}")
    m, j1 = _mm(y, w["b_w_out"], "nn", F32, f"b_out_{tag}", jobs[1])
    return m, (proj, qn, kn, c_rows, o, lse, y, prep, wins, (gq, gk, bf), wp), [j0, j1]


def _fox_bwd_mixer(dm, h, saved, w, bsz, seq, tag):
    proj, qn, kn, c_rows, o, lse, y, prep, wins, (gq, gk, bf), wp = saved
    d = h.shape[1]
    nh = d // HEAD
    dy = _matmul(dm, w["b_w_out"], "nt", F32, f"b_dy_{tag}")
    d_wout = _matmul(y, dm, "tn", F32, f"b_dwout_{tag}")
    og = _win(proj, d, 3)

    def gate_bwd(o_, g_, dy_):
        _, vjp = jax.vjp(lambda a, b: a * jax.nn.sigmoid(b), o_, g_)
        return vjp(dy_)
    do, dog = _rowmap(gate_bwd, [o, og, dy], [], f"b_dgate_{tag}")
    dqn, delta = _fox_bwd_q(qn, kn, proj, 2 * nh, do, lse, c_rows, bsz, seq, nh, f"b_dattn_q_{tag}")
    dkn, dv, dc_rows = _fox_bwd_kv(qn, kn, proj, 2 * nh, do, lse, delta, c_rows, bsz, seq, nh, f"b_dattn_kv_{tag}")
    dc = dc_rows[:, :, 0, :].transpose(0, 2, 1).reshape(bsz * seq, nh)
    dc = jnp.pad(dc, ((0, 0), (0, LANES - nh)))
    dlf = _cumsum_rows(dc, seq, True, f"b_dcumsum_{tag}")
    extra = wp - (4 * d + LANES)

    def prep_bwd(pq, pk, pfl, dqn_, dkn_, dv_, dog_, dlf_, gq_, gk_, bf_):
        _, vjp = jax.vjp(prep, pq, pk, pfl, gq_, gk_, bf_)
        dpq, dpk, dpfl, dgq, dgk, dbf = vjp((dqn_, dkn_, dlf_))
        parts = [dpq, dpk, dv_, dog_, dpfl]
        if extra:
            parts.append(jnp.zeros((pq.shape[0], extra), F32))
        return jnp.concatenate(parts, axis=1), dgq, dgk, dbf
    dproj, d_gq, d_gk, d_bf = _rowmap(prep_bwd, wins + [dqn, dkn, dv, dog, dlf], [gq, gk, bf], f"b_dprep_{tag}", n_acc=3)
    dh = _matmul(dproj, w["b_w_in"], "nt", F32, f"b_dh_{tag}")
    d_win = _matmul(h, dproj, "tn", F32, f"b_dwin_{tag}")
    return (dh,), dict(b_w_in=d_win, b_b_f=d_bf[0, :nh], b_q_norm=d_gq.reshape(-1), b_k_norm=d_gk.reshape(-1), b_w_out=d_wout)


def _gdn_fwd_mixer(h, w, bsz, seq, tag, jobs=(None, None)):
    d = h.shape[1]
    nkh = d // HEAD
    nvh = 2 * nkh
    dqkv = (2 * nkh + nvh) * HEAD
    dz = nvh * HEAD
    nch = seq // GDN_CHUNK
    proj, j0 = _mm(h, w["c_w_in"], "nn", F32, f"c_in_{tag}", jobs[0])
    conv = _conv_fwd(proj, dqkv, w["c_conv_w"], None, seq, f"c_conv_{tag}")

    def rows_of(cols):
        return cols.reshape(bsz, nch, GDN_CHUNK, nvh).transpose(3, 0, 1, 2)[:, :, :, None, :]
    b_rows = rows_of(proj[:, dqkv + dz:dqkv + dz + nvh])
    a_rows = rows_of(proj[:, dqkv + dz + nvh:dqkv + dz + 2 * nvh])
    alog, dtb = w["c_a_log"].reshape(nvh, 1, 1), w["c_dt_bias"].reshape(nvh, 1, 1)
    o, states = _gdn_fwd(conv, b_rows, a_rows, alog, dtb, bsz, seq, nvh, f"c_core_{tag}")
    gn = w["c_out_norm"].reshape(1, HEAD)
    zwin = _win(proj, dz, dqkv // dz)

    def outfn(o_, z_, g_):
        return jnp.concatenate([_rms(a, g_) * _silu(b) for a, b in zip(_heads(o_), _heads(z_))], axis=1)
    y, = _rowmap(lambda o_, z_, g_: outfn(o_, z_, g_).astype(BF16), [o, zwin], [gn], f"c_outnorm_{tag}")
    m, j1 = _mm(y, w["c_w_out"], "nn", F32, f"c_out_{tag}", jobs[1])
    return m, (proj, conv, b_rows, a_rows, alog, dtb, o, states, y, gn, zwin, outfn), [j0, j1]


def _gdn_bwd_mixer(dm, h, saved, w, bsz, seq, tag):
    proj, conv, b_rows, a_rows, alog, dtb, o, states, y, gn, zwin, outfn = saved
    d = h.shape[1]
    nkh = d // HEAD
    nvh = 2 * nkh
    dk_, dv_ = nkh * HEAD, nvh * HEAD
    dqkv = 2 * dk_ + dv_
    dz = dv_
    wp = proj.shape[1]
    dy = _matmul(dm, w["c_w_out"], "nt", F32, f"c_dy_{tag}")
    d_wout = _matmul(y, dm, "tn", F32, f"c_dwout_{tag}")

    def out_bwd(o_, z_, dy_, g_):
        _, vjp = jax.vjp(outfn, o_, z_, g_)
        return vjp(dy_)
    do, dzz, d_gn = _rowmap(out_bwd, [o, zwin, dy], [gn], f"c_doutnorm_{tag}", n_acc=1)
    dq, dk, dv, db_rows, da_rows, d_alog, d_dtb = _gdn_bwd(conv, b_rows, a_rows, alog, dtb, states, do, bsz, seq, nvh, f"c_dcore_{tag}")
    cw = w["c_conv_w"]
    dq_pre, d_cwq = _conv_bwd(dq, proj, dk_, cw[:, :dk_], seq, False, f"c_dconv_q_{tag}", xcol=0)
    dk_pre, d_cwk = _conv_bwd(dk, proj, dk_, cw[:, dk_:2 * dk_], seq, False, f"c_dconv_k_{tag}", xcol=dk_)
    dv_pre, d_cwv = _conv_bwd(dv, proj, dv_, cw[:, 2 * dk_:], seq, False, f"c_dconv_v_{tag}", xcol=2 * dk_)
    d_cw = jnp.concatenate([d_cwq, d_cwk, d_cwv], axis=1)

    def cols_of(rows):
        return rows[:, :, :, 0, :].transpose(1, 2, 3, 0).reshape(bsz * seq, nvh)
    dba = jnp.concatenate([cols_of(db_rows), cols_of(da_rows)], axis=1)
    dba = jnp.pad(dba, ((0, 0), (0, wp - dqkv - dz - 2 * nvh)))
    dproj, = _rowmap(lambda *parts: jnp.concatenate(parts, axis=1), [dq_pre, dk_pre, dv_pre, dzz, dba], [], f"c_dproj_{tag}")
    dh = _matmul(dproj, w["c_w_in"], "nt", F32, f"c_dh_{tag}")
    d_win = _matmul(h, dproj, "tn", F32, f"c_dwin_{tag}")
    return (dh,), dict(c_w_in=d_win, c_conv_w=d_cw, c_a_log=d_alog.reshape(-1), c_dt_bias=d_dtb.reshape(-1),
                       c_out_norm=d_gn.reshape(-1), c_w_out=d_wout)


_MIXER_FWD = (lambda h, w, bsz, seq, tag, jobs: _gmlp_fwd(h, w, tag, jobs), _fox_fwd_mixer, _gdn_fwd_mixer)
_MIXER_BWD = (lambda dm, h, s, w, bsz, seq, tag: _gmlp_bwd(dm, h, s, w, tag), _fox_bwd_mixer, _gdn_bwd_mixer)


class _NoPlan:
    def __init__(self, layers):
        self.layers = layers

    def weights(self, i):
        return self.layers[i]

    def fwd_jobs(self, i):
        return (None,) * 5

    def fwd_done(self, i, outs):
        pass

    def bwd_jobs(self, i):
        return (None,) * 5

    def bwd_done(self, i, outs):
        pass

    def grads_ready(self, i, grads):
        pass


def _local_step(x, target, plan, depth, bsz, seq):
    t, d = x.shape
    saved = []
    m_prev = None
    xin = x
    for i in range(depth):
        w = plan.weights(i)
        tag = f"l{i}"
        g_mix, g_ffn = w["norm_mix"].reshape(1, d), w["norm_ffn"].reshape(1, d)
        if i == 0:
            h, = _rowmap(lambda x_, g_: _rms(x_, g_).astype(BF16), [xin], [g_mix], f"norm_mix_{tag}")
            xl = xin
        else:
            xl, h = _rowmap(lambda x_, m_, g_: (x_ + m_, _rms(x_ + m_, g_).astype(BF16)), [xin, m_prev], [g_mix], f"norm_mix_{tag}")
        fjobs = plan.fwd_jobs(i)
        m, msaved, mouts = _MIXER_FWD[i % 3](h, w, bsz, seq, tag, fjobs[:2])
        x1, h2 = _rowmap(lambda x_, m_, g_: (x_ + m_, _rms(x_ + m_, g_).astype(BF16)), [xl, m], [g_ffn], f"norm_ffn_{tag}")
        f, fsaved, jouts = _ffn_fwd(x1, h2, w["ffn_w_gate"], w["ffn_w_up"], w["ffn_conv_w"], w["ffn_conv_b"].reshape(1, -1), w["ffn_w_down"],
                                    seq, tag, fjobs[2:])
        plan.fwd_done(i, mouts + jouts)
        saved.append((xl, h, msaved, x1, h2, fsaved))
        xin, m_prev = x1, f

    def loss_fn(x_, f_, tg_):
        e = x_ + f_ - tg_
        return e * (1.0 / d), jnp.full((1, LANES), (0.5 / d) * jnp.sum(e * e), F32)
    dx, loss_acc = _rowmap(loss_fn, [xin, m_prev, target], [], "loss", n_acc=1)
    loss = loss_acc[0, 0]

    grads = [None] * depth
    for i in reversed(range(depth)):
        w = plan.weights(i)
        tag = f"l{i}"
        xl, h, msaved, x1, h2, fsaved = saved[i]
        g_mix, g_ffn = w["norm_mix"].reshape(1, d), w["norm_ffn"].reshape(1, d)
        dhs, gw, jouts = _ffn_bwd(dx, h2, fsaved, w["ffn_w_gate"], w["ffn_w_up"], w["ffn_conv_w"], w["ffn_w_down"], seq, tag, plan.bwd_jobs(i))
        plan.bwd_done(i, jouts)
        dx1, d_gffn = _norm_bwd(x1, dx, dhs, g_ffn, f"dnorm_ffn_{tag}")
        dhs, gm = _MIXER_BWD[i % 3](dx1, h, msaved, w, bsz, seq, tag)
        dx, d_gmix = _norm_bwd(xl, dx1, dhs, g_mix, f"dnorm_mix_{tag}")
        gw.update(gm)
        gw["norm_mix"], gw["norm_ffn"] = d_gmix.reshape(-1), d_gffn.reshape(-1)
        grads[i] = gw
        plan.grads_ready(i, gw)
    return loss, dx, grads


def _adamw_math(w_, g_, m_, v_):
    m_new = ADAM_B1 * m_ + (1.0 - ADAM_B1) * g_
    v_new = ADAM_B2 * v_ + (1.0 - ADAM_B2) * (g_ * g_)
    m_hat = m_new / (1.0 - ADAM_B1 ** ADAM_STEP)
    v_hat = v_new / (1.0 - ADAM_B2 ** ADAM_STEP)
    delta = -ADAM_LR * (m_hat / (jnp.sqrt(v_hat) + ADAM_EPS) + ADAM_WD * w_)
    return delta, m_new, v_new


def _adamw(w, g, m, v, name):
    shape = w.shape
    if w.ndim == 1:
        w, g, m, v = (a.reshape(1, -1) for a in (w, g, m, v))
    return [o.reshape(shape) for o in _elementwise(_adamw_math, [w, g, m, v], 3, name)]


def _adamw_layers(w, gs, m, v, name):
    nl, r, c = w.shape
    tr = _pick(r, max(SUBLANES, (1 << 19) // c // SUBLANES * SUBLANES), SUBLANES)

    def body(*refs):
        w_ref, m_ref, v_ref = refs[:3]
        g_refs = refs[3:3 + nl]
        go_ref, d_ref, mo_ref, vo_ref = refs[3 + nl:]
        layer = pl.program_id(0)
        for k in range(nl):
            @pl.when(layer == k)
            def _(k=k):
                g = g_refs[k][...]
                delta, m_new, v_new = _adamw_math(w_ref[...], g, m_ref[...], v_ref[...])
                go_ref[...] = g
                d_ref[...] = delta
                mo_ref[...] = m_new
                vo_ref[...] = v_new

    st = pl.BlockSpec((None, tr, c), lambda l, i: (l, i, 0))
    g_specs = [pl.BlockSpec((tr, c), functools.partial(lambda l, i, k: (jnp.where(l == k, i, 0), 0), k=k)) for k in range(nl)]
    return pl.pallas_call(
        body, name=name, grid=(nl, r // tr), in_specs=[st, st, st] + g_specs, out_specs=[st] * 4,
        out_shape=[jax.ShapeDtypeStruct(w.shape, F32)] * 4, compiler_params=_cparams(("parallel", "parallel")),
    )(w, m, v, *gs)


WEIGHTS = ['norm_mix', 'norm_ffn', 'ffn_w_gate', 'ffn_w_up', 'ffn_conv_w', 'ffn_conv_b', 'ffn_w_down', 'a_w_in', 'a_b_in', 'a_v_norm',
           'a_w_s', 'a_b_s', 'a_w_out', 'b_w_in', 'b_b_f', 'b_q_norm', 'b_k_norm', 'b_w_out', 'c_w_in', 'c_conv_w', 'c_a_log',
           'c_dt_bias', 'c_out_norm', 'c_w_out']
BIG = {'ffn_w_gate': 1, 'ffn_w_up': 1, 'ffn_w_down': 0, 'a_w_in': 1, 'a_w_out': 0, 'b_w_in': 1, 'b_w_out': 0, 'c_w_in': 1, 'c_w_out': 0}
SMALL_SHARDED = {'ffn_conv_w': 1, 'a_b_in': 0, 'a_v_norm': 0, 'c_conv_w': 1}
MIXER_NAMES = (('a_w_in', 'a_b_in', 'a_v_norm', 'a_w_s', 'a_b_s', 'a_w_out'), ('b_w_in', 'b_b_f', 'b_q_norm', 'b_k_norm', 'b_w_out'),
               ('c_w_in', 'c_conv_w', 'c_a_log', 'c_dt_bias', 'c_out_norm', 'c_w_out'))
FFN_NAMES = ('norm_mix', 'norm_ffn', 'ffn_w_gate', 'ffn_w_up', 'ffn_conv_w', 'ffn_conv_b', 'ffn_w_down')


def _layer_entries(depth):
    out = []
    for i in range(depth):
        kind, j = i % 3, i // 3
        out.append([(n, i) for n in FFN_NAMES] + [(n, j) for n in MIXER_NAMES[kind]])
    return out


def _layout(name, shard_shape):
    if BIG[name] == 0:
        return "R"
    return "C" if shard_shape[-1] % LANES == 0 else "U"


JOB_GROUPS = ((3,), (4,), (0,), (1,), (2,))


class _Plan:
    def __init__(self, params, entries, small_full):
        self.params, self.entries, self.small_full = params, entries, small_full
        self.depth = len(entries)
        self.big = [[(n, j, _layout(n, params[n].shape[1:])) for n, j in ent if n in BIG] for ent in entries]
        self.layers = [None] * self.depth
        self.zs = [None] * self.depth
        self.total = {}
        bufs = self._cast(0)
        self._install(0, _gather_layer(bufs, [cls for _, _, cls in self.big[0]], "gather_l0"))

    def _cast(self, i):
        return [_cast_window(self.params[n], j, cls, f"cast_{n}_l{i}") for n, j, cls in self.big[i]]

    def _install(self, i, bufs):
        w = {}
        for (n, j, cls), buf in zip(self.big[i], bufs):
            w[n] = _assemble(buf, _pad_cols(N_CHIPS * buf.shape[2]), f"assemble_{n}_l{i}") if cls == "U" else buf
        for n, j in self.entries[i]:
            if n in SMALL_SHARDED:
                w[n] = self.small_full[(n, j)]
            elif n not in BIG:
                w[n] = self.params[n][j]
        self.layers[i] = w

    def weights(self, i):
        return self.layers[i]

    def fwd_jobs(self, i):
        if i + 1 >= self.depth:
            return (None,) * len(JOB_GROUPS)
        bufs = self._cast(i + 1)
        classes = [cls for _, _, cls in self.big[i + 1]]
        return [_gather_job([bufs[t] for t in grp], [classes[t] for t in grp]) for grp in JOB_GROUPS]

    def fwd_done(self, i, outs):
        if i + 1 < self.depth:
            bufs = [None] * len(self.big[i + 1])
            for grp, got in zip(JOB_GROUPS, outs):
                for t, buf in zip(grp, got):
                    bufs[t] = buf
            self._install(i + 1, bufs)

    def grads_ready(self, i, grads):
        dws, classes, widths = [], [], []
        for n, j, cls in self.big[i]:
            shard = self.params[n].shape[1:]
            dws.append(grads[n].reshape(N_CHIPS, shard[0], shard[1]) if cls == "R" else grads[n])
            classes.append(cls)
            widths.append(shard[1])
        self.zs[i] = _rs_chip_sums(dws, classes, widths, f"l{i}")
        if i == 0:
            self._finish(0, _rs_chip(self.zs[0], "rs_chip_l0"))

    def bwd_jobs(self, i):
        if i + 1 >= self.depth:
            return (None,) * len(JOB_GROUPS)
        return [_chip_job([self.zs[i + 1][t] for t in grp]) for grp in JOB_GROUPS]

    def bwd_done(self, i, outs):
        if i + 1 < self.depth:
            parts = [None] * len(self.big[i + 1])
            for grp, got in zip(JOB_GROUPS, outs):
                for t, p in zip(grp, got):
                    parts[t] = p
            self._finish(i + 1, parts)

    def _finish(self, i, parts):
        for (n, j, _), red in zip(self.big[i], _rs_finish(self.zs[i], parts, f"l{i}")):
            self.total[(n, j)] = red


def _train_step(x, target, params, moments_m, moments_v):
    bsz, seq, d = x.shape
    depth = params['norm_mix'].shape[0]
    entries = _layer_entries(depth)

    small_list = [(n, j) for n in SMALL_SHARDED for j in range(params[n].shape[0])]
    small_buf = _gather_shards(_pack([params[n][j] for n, j in small_list], ()), "gather_small")
    small_full = {}
    for (n, j), g in zip(small_list, _unpack(small_buf, [params[n][j].shape for n, j in small_list], (N_CHIPS,))):
        small_full[(n, j)] = _join(g, SMALL_SHARDED[n])

    plan = _Plan(params, entries, small_full)
    loss_local, dx, grads = _local_step(x.reshape(bsz * seq, d), target.reshape(bsz * seq, d), plan, depth, bsz, seq)
    loss = lax.psum(loss_local, ("x", "y", "c"))

    total = plan.total
    layer_of = {(n, j): i for i, ent in enumerate(entries) for n, j in ent}
    packed = _pack([_split(grads[layer_of[(n, j)]][n], SMALL_SHARDED[n]) for n, j in small_list], (N_CHIPS,))
    red = _reduce_scatter_layer([packed], ["R"], [PACK_COLS], "small")[0]
    for (n, j), g in zip(small_list, _unpack(red, [params[n][j].shape for n, j in small_list], ())):
        total[(n, j)] = g
    repl = [(n, j) for n in WEIGHTS if n not in BIG and n not in SMALL_SHARDED for j in range(params[n].shape[0])]
    flat = jnp.concatenate([grads[layer_of[k]][k[0]].reshape(-1) for k in repl])
    n_flat = flat.shape[0]
    flat = jnp.pad(flat, (0, (-n_flat) % (SUBLANES * LANES))).reshape(-1, LANES)
    flat = _all_to_all_sum(flat, "allreduce_small").reshape(-1)
    off = 0
    for k in repl:
        shp = params[k[0]][k[1]].shape
        n = 1
        for s in shp:
            n *= s
        total[k] = flat[off:off + n].reshape(shp)
        off += n

    grad_w, delta_w, new_m, new_v = {}, {}, {}, {}
    for n in WEIGHTS:
        nl = params[n].shape[0]
        if n in BIG:
            grad_w[n], delta_w[n], new_m[n], new_v[n] = _adamw_layers(params[n], [total[(n, j)] for j in range(nl)], moments_m[n], moments_v[n], f"adamw_{n}")
        else:
            g = jnp.stack([total[(n, j)] for j in range(nl)])
            grad_w[n] = g
            delta_w[n], new_m[n], new_v[n] = _adamw(params[n], g, moments_m[n], moments_v[n], f"adamw_{n}")
    return (loss, dx.reshape(bsz, seq, d), *[grad_w[n] for n in WEIGHTS], *[delta_w[n] for n in WEIGHTS],
            *[new_m[n] for n in WEIGHTS], *[new_v[n] for n in WEIGHTS])


def kernel(x, norm_mix, norm_ffn, ffn_w_gate, ffn_w_up, ffn_conv_w, ffn_conv_b, ffn_w_down, a_w_in, a_b_in, a_v_norm, a_w_s, a_b_s, a_w_out, b_w_in, b_b_f, b_q_norm, b_k_norm, b_w_out, c_w_in, c_conv_w, c_a_log, c_dt_bias, c_out_norm, c_w_out, loss_target, m_norm_mix, m_norm_ffn, m_ffn_w_gate, m_ffn_w_up, m_ffn_conv_w, m_ffn_conv_b, m_ffn_w_down, m_a_w_in, m_a_b_in, m_a_v_norm, m_a_w_s, m_a_b_s, m_a_w_out, m_b_w_in, m_b_b_f, m_b_q_norm, m_b_k_norm, m_b_w_out, m_c_w_in, m_c_conv_w, m_c_a_log, m_c_dt_bias, m_c_out_norm, m_c_w_out, v_norm_mix, v_norm_ffn, v_ffn_w_gate, v_ffn_w_up, v_ffn_conv_w, v_ffn_conv_b, v_ffn_w_down, v_a_w_in, v_a_b_in, v_a_v_norm, v_a_w_s, v_a_b_s, v_a_w_out, v_b_w_in, v_b_b_f, v_b_q_norm, v_b_k_norm, v_b_w_out, v_c_w_in, v_c_conv_w, v_c_a_log, v_c_dt_bias, v_c_out_norm, v_c_w_out):
    given = dict(locals())
    params = {n: given[n] for n in WEIGHTS}
    moments_m = {n: given["m_" + n] for n in WEIGHTS}
    moments_v = {n: given["v_" + n] for n in WEIGHTS}
    return _train_step(x, loss_target, params, moments_m, moments_v)
```

```python
import functools

import jax
import jax.numpy as jnp
from jax import lax
from jax.experimental import pallas as pl
from jax.experimental.pallas import tpu as pltpu

F32 = jnp.float32
BF16 = jnp.bfloat16
HI = lax.Precision.HIGHEST
MESH = pl.DeviceIdType.MESH

RMS_EPS = 1e-6
ADAM_LR, ADAM_B1, ADAM_B2, ADAM_EPS, ADAM_WD, ADAM_STEP = 0.001, 0.9, 0.999, 1e-08, 0.01, 10
A_CHUNK, HEAD, GDN_CHUNK = 128, 128, 64
LANES, SUBLANES = 128, 8
PACK_COLS = 1024
N_CHIPS = 4
VMEM_LIMIT = 56 * 1024 * 1024
ROWMAP_BUDGET = 20 * 1024 * 1024

NN = (((1,), (0,)), ((), ()))
NT = (((1,), (1,)), ((), ()))
TN = (((0,), (0,)), ((), ()))
BNN = (((2,), (1,)), ((0,), (0,)))
BNT = (((2,), (2,)), ((0,), (0,)))
BTN = (((1,), (1,)), ((0,), (0,)))


def _pick(n, cap, mult):
    if n <= cap:
        return n
    best = None
    for d in range(mult, cap + 1, mult):
        if n % d == 0:
            best = d
    if best is None:
        raise ValueError(f"no tile for {n} (cap {cap}, multiple of {mult})")
    return best


def _pad_cols(n):
    j = -(-n // LANES)
    while not (j <= 8 or any(j % d == 0 for d in (4, 5, 6, 7, 8))):
        j += 1
    return j * LANES


def _cparams(sem):
    return pltpu.CompilerParams(dimension_semantics=sem, vmem_limit_bytes=VMEM_LIMIT)


def _matmul(a, b, kind, out_dtype, name, job=None):
    if kind == "nn":
        (m, k), (k2, n) = a.shape, b.shape
    elif kind == "nt":
        (m, k), (n, k2) = a.shape, b.shape
    else:
        (k, m), (k2, n) = a.shape, b.shape
    assert k == k2, (name, a.shape, b.shape)
    tm, tn, tk = _pick(m, 1024, LANES), _pick(n, 1024, LANES), _pick(k, 2048, LANES)
    ni, nj, nk = m // tm, n // tn, k // tk
    dims = {"nn": NN, "nt": NT, "tn": TN}[kind]
    a_spec = pl.BlockSpec((tk, tm), lambda i, j, kk: (kk, i)) if kind == "tn" else pl.BlockSpec((tm, tk), lambda i, j, kk: (i, kk))
    b_spec = pl.BlockSpec((tn, tk), lambda i, j, kk: (j, kk)) if kind == "nt" else pl.BlockSpec((tk, tn), lambda i, j, kk: (kk, j))
    n_jin, n_jout = (len(job["ins"]), len(job["out_shapes"])) if job else (0, 0)

    def body(a_ref, b_ref, *rest):
        jins, o_ref, jouts = rest[:n_jin], rest[n_jin], rest[n_jin + 1:n_jin + 1 + n_jout]
        scratch = rest[n_jin + 1 + n_jout:]
        i, j, kk = pl.program_id(0), pl.program_id(1), pl.program_id(2)
        if job:
            first = jnp.logical_and(jnp.logical_and(i == 0, j == 0), kk == 0)
            job["start"](jins, jouts, scratch[-2], scratch[-1], first)
        prod = lax.dot_general(a_ref[...].astype(BF16), b_ref[...].astype(BF16), dims, preferred_element_type=F32)
        if nk == 1:
            o_ref[...] = prod.astype(o_ref.dtype)
        else:
            acc_ref = scratch[0]

            @pl.when(kk == 0)
            def _():
                acc_ref[...] = prod

            @pl.when(kk > 0)
            def _():
                acc_ref[...] += prod

            @pl.when(kk == nk - 1)
            def _():
                o_ref[...] = acc_ref[...].astype(o_ref.dtype)
        if job:
            last = jnp.logical_and(jnp.logical_and(i == ni - 1, j == nj - 1), kk == nk - 1)
            job["finish"](jins, jouts, scratch[-2], scratch[-1], last)

    scratch_shapes = [pltpu.VMEM((tm, tn), F32)] if nk > 1 else []
    out_specs = pl.BlockSpec((tm, tn), lambda i, j, kk: (i, j))
    out_shape = jax.ShapeDtypeStruct((m, n), out_dtype)
    if not job:
        return pl.pallas_call(
            body, name=name, grid=(ni, nj, nk), in_specs=[a_spec, b_spec], out_specs=out_specs, out_shape=out_shape,
            scratch_shapes=scratch_shapes, compiler_params=_cparams(("parallel", "parallel", "arbitrary")),
        )(a, b)
    scratch_shapes += [pltpu.SemaphoreType.DMA((job["n_sems"],)), pltpu.SemaphoreType.DMA((job["n_sems"],))]
    res = pl.pallas_call(
        body, name=name, grid=(ni, nj, nk), in_specs=[a_spec, b_spec] + [ANY] * n_jin,
        out_specs=[out_specs] + [ANY] * n_jout, out_shape=[out_shape] + list(job["out_shapes"]),
        input_output_aliases={2 + t: 1 + t for t in range(n_jin)} if job["alias"] else {},
        scratch_shapes=scratch_shapes, compiler_params=_cparams(("arbitrary", "arbitrary", "arbitrary")),
    )(a, b, *job["ins"])
    return res[0], list(res[1:])


def _win(arr, width=None, blk=0):
    return (arr, arr.shape[1] if width is None else width, blk)


def _rowmap(fn, rows, params, name, n_acc=0, tc=None, col_params=()):
    rows = [r if isinstance(r, tuple) else _win(r) for r in rows]
    t = rows[0][0].shape[0]
    widths = [tc if tc is not None else w for (_, w, _) in rows]

    def blocks_for(tr):
        rb = [jax.ShapeDtypeStruct((tr, w), a.dtype) for (a, _, _), w in zip(rows, widths)]
        pb = [jax.ShapeDtypeStruct((p.shape[0], tc) if (i in col_params) else p.shape, p.dtype) for i, p in enumerate(params)]
        return rb, pb

    rb, pb = blocks_for(SUBLANES * 2)
    outs = jax.eval_shape(fn, *rb, *pb)
    outs = list(outs) if isinstance(outs, (tuple, list)) else [outs]
    n_row = len(outs) - n_acc
    row_bytes = sum(w * a.dtype.itemsize for (a, _, _), w in zip(rows, widths)) + sum(o.shape[1] * o.dtype.itemsize for o in outs[:n_row])
    tr = 16
    while tr * 2 <= 512 and t % (tr * 2) == 0 and (tr * 2) * row_bytes * 5 <= ROWMAP_BUDGET:
        tr *= 2
    rb, pb = blocks_for(tr)
    outs = jax.eval_shape(fn, *rb, *pb)
    outs = list(outs) if isinstance(outs, (tuple, list)) else [outs]
    n_in = len(rows) + len(params)

    if tc is None:
        grid = (t // tr,)
        row_axis = 0
        in_specs = [pl.BlockSpec((tr, w), functools.partial(lambda i, b: (i, b), b=blk)) for (_, w, blk) in rows]
        in_specs += [pl.BlockSpec(p.shape, functools.partial(lambda i, nd: (0,) * nd, nd=p.ndim)) for p in params]
        out_specs = [pl.BlockSpec((tr, o.shape[1]), lambda i: (i, 0)) for o in outs[:n_row]]
        out_specs += [pl.BlockSpec(o.shape, functools.partial(lambda i, nd: (0,) * nd, nd=len(o.shape))) for o in outs[n_row:]]
        out_shape = [jax.ShapeDtypeStruct((t, o.shape[1]), o.dtype) for o in outs[:n_row]]
        out_shape += [jax.ShapeDtypeStruct(o.shape, o.dtype) for o in outs[n_row:]]
        sem = ("arbitrary",) if n_acc else ("parallel",)
    else:
        wtot = rows[0][1]
        grid = (wtot // tc, t // tr)
        row_axis = 1
        in_specs = [pl.BlockSpec((tr, tc), functools.partial(lambda j, i, b: (i, j + b), b=blk)) for (_, _, blk) in rows]
        for i, p in enumerate(params):
            if i in col_params:
                in_specs.append(pl.BlockSpec((p.shape[0], tc), lambda j, i: (0, j)))
            else:
                in_specs.append(pl.BlockSpec(p.shape, functools.partial(lambda j, i, nd: (0,) * nd, nd=p.ndim)))
        out_specs = [pl.BlockSpec((tr, tc), lambda j, i: (i, j)) for _ in outs[:n_row]]
        out_specs += [pl.BlockSpec((o.shape[0], tc), lambda j, i: (0, j)) for o in outs[n_row:]]
        out_shape = [jax.ShapeDtypeStruct((t, wtot), o.dtype) for o in outs[:n_row]]
        out_shape += [jax.ShapeDtypeStruct((o.shape[0], wtot), o.dtype) for o in outs[n_row:]]
        sem = ("parallel", "arbitrary") if n_acc else ("parallel", "parallel")

    def body(*refs):
        ins, ors = refs[:n_in], refs[n_in:]
        res = fn(*[r[...] for r in ins])
        res = list(res) if isinstance(res, (tuple, list)) else [res]
        for o, r in zip(ors[:n_row], res[:n_row]):
            o[...] = r.astype(o.dtype)
        if n_acc:
            i = pl.program_id(row_axis)
            for o, r in zip(ors[n_row:], res[n_row:]):
                @pl.when(i == 0)
                def _(o=o, r=r):
                    o[...] = r.astype(o.dtype)

                @pl.when(i > 0)
                def _(o=o, r=r):
                    o[...] += r.astype(o.dtype)

    res = pl.pallas_call(
        body, name=name, grid=grid, in_specs=in_specs, out_specs=out_specs, out_shape=out_shape,
        compiler_params=_cparams(sem),
    )(*[a for (a, _, _) in rows], *params)
    return res


def _elementwise(fn, arrays, n_out, name):
    shape = arrays[0].shape
    cols = shape[-1]
    rws = 1
    for s in shape[:-1]:
        rws *= s
    arrs = [a.reshape(rws, cols) for a in arrays]
    per_row = cols * 4 * (len(arrays) + n_out) * 3
    tr = rws
    if rws * per_row > ROWMAP_BUDGET:
        tr = _pick(rws, max(SUBLANES, ROWMAP_BUDGET // per_row), SUBLANES)

    def body(*refs):
        res = fn(*[r[...] for r in refs[:len(arrs)]])
        for o, r in zip(refs[len(arrs):], res):
            o[...] = r

    spec = pl.BlockSpec((tr, cols), lambda i: (i, 0))
    outs = pl.pallas_call(
        body, name=name, grid=(rws // tr,), in_specs=[spec] * len(arrs), out_specs=[spec] * n_out,
        out_shape=[jax.ShapeDtypeStruct((rws, cols), F32)] * n_out, compiler_params=_cparams(("parallel",)),
    )(*arrs)
    return [o.reshape(shape) for o in outs]


def _rms(x, g):
    return x * lax.rsqrt(jnp.mean(x * x, axis=-1, keepdims=True) + RMS_EPS) * g


def _silu(x):
    return x * jax.nn.sigmoid(x)


def _softplus(x):
    return jnp.maximum(x, 0.0) + jnp.log(1.0 + jnp.exp(-jnp.abs(x)))


def _gelu_tanh(x):
    return 0.5 * x * (1.0 + jnp.tanh(0.7978845608028654 * (x + 0.044715 * (x * x * x))))


def _heads(x):
    return [x[:, h * HEAD:(h + 1) * HEAD] for h in range(x.shape[1] // HEAD)]


def _dot(a, b, dims=NN):
    return lax.dot_general(a, b, dims, precision=HI, preferred_element_type=F32)


def _bdot(a, b, dims):
    return lax.dot_general(a.astype(BF16), b.astype(BF16), dims, preferred_element_type=F32)


def _bdot3(a, b, dims):
    return lax.dot_general(a, b, dims, precision=lax.Precision.HIGH, preferred_element_type=F32)


def _eye(n):
    return (lax.broadcasted_iota(jnp.int32, (n, n), 0) == lax.broadcasted_iota(jnp.int32, (n, n), 1)).astype(F32)


def _tri(n):
    return lax.broadcasted_iota(jnp.int32, (n, n), 0) >= lax.broadcasted_iota(jnp.int32, (n, n), 1)


def _row_to_col(row):
    return jnp.sum(_eye(row.shape[1]) * row, axis=1, keepdims=True)


def _conv_tiles(w, seq):
    return _pick(seq, 512, SUBLANES), _pick(w, 512, LANES)


def _conv_fwd(x, width, w, bias, seq, name):
    t = x.shape[0]
    kk = w.shape[0]
    tr, tc = _conv_tiles(width, seq)
    hb = tr // SUBLANES

    def body(*refs):
        if bias is None:
            x_ref, h_ref, w_ref, o_ref = refs
        else:
            x_ref, h_ref, w_ref, b_ref, o_ref = refs
        i = pl.program_id(1)
        first = (i * tr) % seq == 0
        halo = jnp.where(first, 0.0, h_ref[...])
        xe = jnp.concatenate([halo, x_ref[...]], axis=0)
        wv = w_ref[...]
        acc = xe[SUBLANES:, :] * wv[kk - 1:kk, :]
        for s in range(1, kk):
            acc = acc + pltpu.roll(xe, s, 0)[SUBLANES:, :] * wv[kk - 1 - s:kk - s, :]
        if bias is not None:
            acc = acc + b_ref[...]
        o_ref[...] = acc

    in_specs = [pl.BlockSpec((tr, tc), lambda j, i: (i, j)),
                pl.BlockSpec((SUBLANES, tc), lambda j, i: (jnp.maximum(i * hb - 1, 0), j)),
                pl.BlockSpec((kk, tc), lambda j, i: (0, j))]
    ops = [x, x, w]
    if bias is not None:
        in_specs.append(pl.BlockSpec((1, tc), lambda j, i: (0, j)))
        ops.append(bias)
    return pl.pallas_call(
        body, name=name, grid=(width // tc, t // tr), in_specs=in_specs,
        out_specs=pl.BlockSpec((tr, tc), lambda j, i: (i, j)), out_shape=jax.ShapeDtypeStruct((t, width), F32),
        compiler_params=_cparams(("parallel", "parallel")),
    )(*ops)


def _conv_bwd(dy, x, width, w, seq, with_bias, name, xcol=0, dx_dtype=F32):
    t = x.shape[0]
    kk = w.shape[0]
    tr, tc = _conv_tiles(width, seq)
    hb = tr // SUBLANES
    n_halo_blocks = t // SUBLANES
    assert xcol % tc == 0
    xb = xcol // tc

    def body(dy_ref, dyn_ref, x_ref, xh_ref, w_ref, dx_ref, dw_ref, *rest):
        i = pl.program_id(1)
        first = (i * tr) % seq == 0
        last = ((i + 1) * tr) % seq == 0
        dyc = dy_ref[...]
        dye = jnp.concatenate([dyc, jnp.where(last, 0.0, dyn_ref[...])], axis=0)
        xe = jnp.concatenate([jnp.where(first, 0.0, xh_ref[...]), x_ref[...]], axis=0)
        wv = w_ref[...]
        dx = dyc * wv[kk - 1:kk, :]
        dws = [None] * kk
        dws[kk - 1] = jnp.sum(dyc * xe[SUBLANES:, :], axis=0, keepdims=True)
        for s in range(1, kk):
            dx = dx + pltpu.roll(dye, tr + SUBLANES - s, 0)[:tr, :] * wv[kk - 1 - s:kk - s, :]
            dws[kk - 1 - s] = jnp.sum(dyc * pltpu.roll(xe, s, 0)[SUBLANES:, :], axis=0, keepdims=True)
        dx_ref[...] = dx.astype(dx_ref.dtype)

        @pl.when(i == 0)
        def _():
            for j in range(kk):
                dw_ref[j:j + 1, :] = dws[j]
            if with_bias:
                rest[0][...] = jnp.sum(dyc, axis=0, keepdims=True)

        @pl.when(i > 0)
        def _():
            for j in range(kk):
                dw_ref[j:j + 1, :] += dws[j]
            if with_bias:
                rest[0][...] += jnp.sum(dyc, axis=0, keepdims=True)

    cur = pl.BlockSpec((tr, tc), lambda j, i: (i, j))
    in_specs = [cur, pl.BlockSpec((SUBLANES, tc), lambda j, i: (jnp.minimum((i + 1) * hb, n_halo_blocks - 1), j)),
                pl.BlockSpec((tr, tc), lambda j, i: (i, j + xb)),
                pl.BlockSpec((SUBLANES, tc), lambda j, i: (jnp.maximum(i * hb - 1, 0), j + xb)),
                pl.BlockSpec((kk, tc), lambda j, i: (0, j))]
    out_specs = [cur, pl.BlockSpec((kk, tc), lambda j, i: (0, j))]
    out_shape = [jax.ShapeDtypeStruct((t, width), dx_dtype), jax.ShapeDtypeStruct((kk, width), F32)]
    if with_bias:
        out_specs.append(pl.BlockSpec((1, tc), lambda j, i: (0, j)))
        out_shape.append(jax.ShapeDtypeStruct((1, width), F32))
    return pl.pallas_call(
        body, name=name, grid=(width // tc, t // tr), in_specs=in_specs, out_specs=out_specs, out_shape=out_shape,
        compiler_params=_cparams(("parallel", "arbitrary")),
    )(dy, dy, x, x, w)


def _cumsum_rows(x, seq, reverse, name):
    t, w = x.shape
    tb = _pick(seq, 256, SUBLANES)
    nb = seq // tb

    def pos(b, i):
        return (b * nb + (nb - 1 - i if reverse else i), 0)

    def body(x_ref, o_ref, carry):
        i = pl.program_id(1)

        @pl.when(i == 0)
        def _():
            carry[...] = jnp.zeros_like(carry)

        blk = x_ref[...]
        r = lax.broadcasted_iota(jnp.int32, (tb, tb), 0)
        c = lax.broadcasted_iota(jnp.int32, (tb, tb), 1)
        m = ((r <= c) if reverse else (r >= c)).astype(F32)
        o_ref[...] = _dot(m, blk) + carry[...]
        carry[...] += jnp.sum(blk, axis=0, keepdims=True)

    return pl.pallas_call(
        body, name=name, grid=(t // seq, nb), in_specs=[pl.BlockSpec((tb, w), pos)], out_specs=pl.BlockSpec((tb, w), pos),
        out_shape=jax.ShapeDtypeStruct((t, w), F32), scratch_shapes=[pltpu.VMEM((1, w), F32)],
        compiler_params=_cparams(("parallel", "arbitrary")),
    )(x)


def _sgu_fwd(vn, u, w_s, b_col, name):
    t, a = vn.shape
    g = a // HEAD

    def body(v_ref, u_ref, w_ref, b_ref, y_ref):
        tri = _tri(A_CHUNK)
        for gi in range(g):
            sl = slice(gi * HEAD, (gi + 1) * HEAD)
            wc = jnp.where(tri, w_ref[gi], 0.0)
            sv = _bdot(wc, v_ref[:, sl], NN) + b_ref[gi]
            y_ref[:, sl] = (u_ref[:, sl] * sv).astype(y_ref.dtype)

    blk = pl.BlockSpec((A_CHUNK, a), lambda i: (i, 0))
    return pl.pallas_call(
        body, name=name, grid=(t // A_CHUNK,),
        in_specs=[blk, blk, pl.BlockSpec(w_s.shape, lambda i: (0, 0, 0)), pl.BlockSpec(b_col.shape, lambda i: (0, 0, 0))],
        out_specs=blk, out_shape=jax.ShapeDtypeStruct((t, a), BF16), compiler_params=_cparams(("parallel",)),
    )(vn, u, w_s, b_col)


def _sgu_bwd(vn, u, dy, w_s, b_col, name):
    t, a = vn.shape
    g = a // HEAD

    def body(v_ref, u_ref, dy_ref, w_ref, b_ref, dv_ref, du_ref, dw_ref, db_ref):
        i = pl.program_id(0)
        tri = _tri(A_CHUNK)
        for gi in range(g):
            sl = slice(gi * HEAD, (gi + 1) * HEAD)
            wc = jnp.where(tri, w_ref[gi], 0.0)
            v = v_ref[:, sl]
            sv = _bdot(wc, v, NN) + b_ref[gi]
            dyb = dy_ref[:, sl]
            du_ref[:, sl] = dyb * sv
            dsv = dyb * u_ref[:, sl]
            dv_ref[:, sl] = _bdot(wc, dsv, TN)
            dw = jnp.where(tri, _bdot(dsv, v, NT), 0.0)
            db = jnp.sum(dsv, axis=1, keepdims=True)

            @pl.when(i == 0)
            def _(gi=gi, dw=dw, db=db):
                dw_ref[gi] = dw
                db_ref[gi] = db

            @pl.when(i > 0)
            def _(gi=gi, dw=dw, db=db):
                dw_ref[gi] += dw
                db_ref[gi] += db

    blk = pl.BlockSpec((A_CHUNK, a), lambda i: (i, 0))
    wsp = pl.BlockSpec(w_s.shape, lambda i: (0, 0, 0))
    bsp = pl.BlockSpec(b_col.shape, lambda i: (0, 0, 0))
    return pl.pallas_call(
        body, name=name, grid=(t // A_CHUNK,), in_specs=[blk, blk, blk, wsp, bsp], out_specs=[blk, blk, wsp, bsp],
        out_shape=[jax.ShapeDtypeStruct((t, a), F32), jax.ShapeDtypeStruct((t, a), F32),
                   jax.ShapeDtypeStruct(w_s.shape, F32), jax.ShapeDtypeStruct(b_col.shape, F32)],
        compiler_params=_cparams(("arbitrary",)),
    )(vn, u, dy, w_s, b_col)


def _fox_scores(q, k, cq_row, ck_row, diag, scale):
    s = lax.dot_general(q.astype(BF16), k.astype(BF16), NT, preferred_element_type=F32) * scale
    s = s + _row_to_col(cq_row) - ck_row
    mask = jnp.logical_or(jnp.logical_not(diag), _tri(q.shape[0]))
    return s, mask


def _fox_fwd(qn, kn, proj, v_blk0, c_rows, bsz, seq, nh, name):
    t = qn.shape[0]
    tq = _pick(seq, 512, LANES)
    nq = seq // tq
    scale = HEAD ** -0.5

    def body(q_ref, k_ref, v_ref, cq_ref, ck_ref, o_ref, lse_ref, m_s, l_s, acc_s):
        i, j = pl.program_id(2), pl.program_id(3)

        @pl.when(j == 0)
        def _():
            m_s[...] = jnp.full_like(m_s, -jnp.inf)
            l_s[...] = jnp.zeros_like(l_s)
            acc_s[...] = jnp.zeros_like(acc_s)

        @pl.when(j <= i)
        def _():
            s, mask = _fox_scores(q_ref[...], k_ref[...], cq_ref[...], ck_ref[...], j == i, scale)
            s = jnp.where(mask, s, -jnp.inf)
            m_new = jnp.maximum(m_s[...], jnp.max(s, axis=1, keepdims=True))
            p = jnp.exp(s - m_new)
            alpha = jnp.exp(m_s[...] - m_new)
            l_s[...] = alpha * l_s[...] + jnp.sum(p, axis=1, keepdims=True)
            acc_s[...] = alpha * acc_s[...] + lax.dot_general(p.astype(BF16), v_ref[...].astype(BF16), NN, preferred_element_type=F32)
            m_s[...] = m_new

        @pl.when(j == nq - 1)
        def _():
            o_ref[...] = acc_s[...] / l_s[...]
            lse_ref[...] = jnp.broadcast_to(m_s[...] + jnp.log(l_s[...]), lse_ref.shape)

    qspec = pl.BlockSpec((tq, HEAD), lambda b, h, i, j: (b * nq + i, h))
    kspec = pl.BlockSpec((tq, HEAD), lambda b, h, i, j: (b * nq + jnp.minimum(i, j), h))
    vspec = pl.BlockSpec((tq, HEAD), lambda b, h, i, j: (b * nq + jnp.minimum(i, j), v_blk0 + h))
    cq = pl.BlockSpec((None, None, 1, tq), lambda b, h, i, j: (b, h, 0, i))
    ck = pl.BlockSpec((None, None, 1, tq), lambda b, h, i, j: (b, h, 0, jnp.minimum(i, j)))
    return pl.pallas_call(
        body, name=name, grid=(bsz, nh, nq, nq), in_specs=[qspec, kspec, vspec, cq, ck], out_specs=[qspec, qspec],
        out_shape=[jax.ShapeDtypeStruct((t, nh * HEAD), F32)] * 2,
        scratch_shapes=[pltpu.VMEM((tq, 1), F32), pltpu.VMEM((tq, 1), F32), pltpu.VMEM((tq, HEAD), F32)],
        compiler_params=_cparams(("parallel", "parallel", "parallel", "arbitrary")),
    )(qn, kn, proj, c_rows, c_rows)


def _fox_p_dp(q, k, v, do, lse, cq_row, ck_row, diag, scale):
    s, mask = _fox_scores(q, k, cq_row, ck_row, diag, scale)
    p = jnp.where(mask, jnp.exp(s - jnp.max(lse, axis=1, keepdims=True)), 0.0)
    dp = lax.dot_general(do.astype(BF16), v.astype(BF16), NT, preferred_element_type=F32)
    return p, dp


def _fox_bwd_q(qn, kn, proj, v_blk0, do, lse, c_rows, bsz, seq, nh, name):
    t = qn.shape[0]
    tq = _pick(seq, 512, LANES)
    nq = seq // tq
    scale = HEAD ** -0.5

    def key_block(jj):
        return jnp.where(jj >= nq, jj - nq, jj)

    def body(q_ref, k_ref, v_ref, do_ref, lse_ref, cq_ref, ck_ref, dq_ref, dl_ref, dq_s, dl_s):
        i, jj = pl.program_id(2), pl.program_id(3)
        j = key_block(jj)

        @pl.when(jj == 0)
        def _():
            dq_s[...] = jnp.zeros_like(dq_s)
            dl_s[...] = jnp.zeros_like(dl_s)

        @pl.when(j <= i)
        def _():
            p, dp = _fox_p_dp(q_ref[...], k_ref[...], v_ref[...], do_ref[...], lse_ref[...], cq_ref[...], ck_ref[...], j == i, scale)

            @pl.when(jj < nq)
            def _():
                dl_s[...] += jnp.sum(p * dp, axis=1, keepdims=True)

            @pl.when(jj >= nq)
            def _():
                ds = p * (dp - dl_s[...])
                dq_s[...] += lax.dot_general(ds.astype(BF16), k_ref[...].astype(BF16), NN, preferred_element_type=F32) * scale

        @pl.when(jj == 2 * nq - 1)
        def _():
            dq_ref[...] = dq_s[...]
            dl_ref[...] = jnp.broadcast_to(dl_s[...], dl_ref.shape)

    qspec = pl.BlockSpec((tq, HEAD), lambda b, h, i, jj: (b * nq + i, h))
    kspec = pl.BlockSpec((tq, HEAD), lambda b, h, i, jj: (b * nq + jnp.minimum(i, key_block(jj)), h))
    vspec = pl.BlockSpec((tq, HEAD), lambda b, h, i, jj: (b * nq + jnp.minimum(i, key_block(jj)), v_blk0 + h))
    cq = pl.BlockSpec((None, None, 1, tq), lambda b, h, i, jj: (b, h, 0, i))
    ck = pl.BlockSpec((None, None, 1, tq), lambda b, h, i, jj: (b, h, 0, jnp.minimum(i, key_block(jj))))
    return pl.pallas_call(
        body, name=name, grid=(bsz, nh, nq, 2 * nq), in_specs=[qspec, kspec, vspec, qspec, qspec, cq, ck],
        out_specs=[qspec, qspec], out_shape=[jax.ShapeDtypeStruct((t, nh * HEAD), F32)] * 2,
        scratch_shapes=[pltpu.VMEM((tq, HEAD), F32), pltpu.VMEM((tq, 1), F32)],
        compiler_params=_cparams(("parallel", "parallel", "parallel", "arbitrary")),
    )(qn, kn, proj, do, lse, c_rows, c_rows)


def _fox_bwd_kv(qn, kn, proj, v_blk0, do, lse, delta, c_rows, bsz, seq, nh, name):
    t = qn.shape[0]
    tq = _pick(seq, 512, LANES)
    nq = seq // tq
    scale = HEAD ** -0.5

    def body(q_ref, k_ref, v_ref, do_ref, lse_ref, dl_ref, cq_ref, ck_ref, dk_ref, dv_ref, dc_ref, dk_s, dv_s, dc_s):
        j, i = pl.program_id(2), pl.program_id(3)

        @pl.when(i == 0)
        def _():
            dk_s[...] = jnp.zeros_like(dk_s)
            dv_s[...] = jnp.zeros_like(dv_s)
            dc_s[...] = jnp.zeros_like(dc_s)

        @pl.when(i >= j)
        def _():
            p, dp = _fox_p_dp(q_ref[...], k_ref[...], v_ref[...], do_ref[...], lse_ref[...], cq_ref[...], ck_ref[...], j == i, scale)
            ds = p * (dp - jnp.max(dl_ref[...], axis=1, keepdims=True))
            dv_s[...] += lax.dot_general(p.astype(BF16), do_ref[...].astype(BF16), TN, preferred_element_type=F32)
            dk_s[...] += lax.dot_general(ds.astype(BF16), q_ref[...].astype(BF16), TN, preferred_element_type=F32) * scale
            dc_s[...] -= jnp.sum(ds, axis=0, keepdims=True)

        @pl.when(i == nq - 1)
        def _():
            dk_ref[...] = dk_s[...]
            dv_ref[...] = dv_s[...]
            dc_ref[...] = dc_s[...]

    kspec = pl.BlockSpec((tq, HEAD), lambda b, h, j, i: (b * nq + j, h))
    vspec = pl.BlockSpec((tq, HEAD), lambda b, h, j, i: (b * nq + j, v_blk0 + h))
    qspec = pl.BlockSpec((tq, HEAD), lambda b, h, j, i: (b * nq + jnp.maximum(i, j), h))
    cq = pl.BlockSpec((None, None, 1, tq), lambda b, h, j, i: (b, h, 0, jnp.maximum(i, j)))
    ck = pl.BlockSpec((None, None, 1, tq), lambda b, h, j, i: (b, h, 0, j))
    return pl.pallas_call(
        body, name=name, grid=(bsz, nh, nq, nq), in_specs=[qspec, kspec, vspec, qspec, qspec, qspec, cq, ck],
        out_specs=[kspec, kspec, ck],
        out_shape=[jax.ShapeDtypeStruct((t, nh * HEAD), F32)] * 2 + [jax.ShapeDtypeStruct(c_rows.shape, F32)],
        scratch_shapes=[pltpu.VMEM((tq, HEAD), F32), pltpu.VMEM((tq, HEAD), F32), pltpu.VMEM((1, tq), F32)],
        compiler_params=_cparams(("parallel", "parallel", "parallel", "arbitrary")),
    )(qn, kn, proj, do, lse, delta, c_rows, c_rows)


@jax.custom_vjp
def _unit_lower_inverse(a_mat):
    c = a_mat.shape[-1]
    inv = _eye(c) - a_mat
    pw = _bdot3(a_mat, a_mat, BNN)
    n_sq = max(1, (c - 1).bit_length() - 1)
    for it in range(n_sq):
        inv = inv + _bdot3(inv, pw, BNN)
        if it < n_sq - 1:
            pw = _bdot3(pw, pw, BNN)
    return inv


def _unit_lower_inverse_fwd(a_mat):
    inv = _unit_lower_inverse(a_mat)
    return inv, inv


def _unit_lower_inverse_bwd(inv, d_inv):
    return (-_bdot3(_bdot3(inv, d_inv, BTN), inv, BNT),)


_unit_lower_inverse.defvjp(_unit_lower_inverse_fwd, _unit_lower_inverse_bwd)


def _gdn_chunk(qp, kp, vp, b_row, a_row, alog, dtb, state):
    hv = vp.shape[0]
    c = qp.shape[1]
    qc, kc, vc = _silu(qp), _silu(kp), _silu(vp)
    qh = qc * lax.rsqrt(jnp.sum(qc * qc, -1, keepdims=True) + RMS_EPS) * (HEAD ** -0.5)
    kh = kc * lax.rsqrt(jnp.sum(kc * kc, -1, keepdims=True) + RMS_EPS)
    q = jnp.stack([qh[h // 2] for h in range(hv)])
    k = jnp.stack([kh[h // 2] for h in range(hv)])
    beta_row = jax.nn.sigmoid(b_row)
    g_row = -jnp.exp(alog) * _softplus(a_row + dtb)
    ri = lax.broadcasted_iota(jnp.int32, (c, c), 0)
    ci = lax.broadcasted_iota(jnp.int32, (c, c), 1)
    eye = (ri == ci).astype(F32)
    tri = ri >= ci
    beta_col = jnp.sum(eye * beta_row, axis=2, keepdims=True)
    g_col = jnp.sum(eye * g_row, axis=2, keepdims=True)
    gc_col = jnp.sum(tri.astype(F32) * g_row, axis=2, keepdims=True)
    gc_row = jnp.sum(g_col * (ri <= ci).astype(F32), axis=1, keepdims=True)
    decay = jnp.where(tri, jnp.exp(jnp.where(tri, gc_col - gc_row, 0.0)), 0.0)
    kb = k * beta_col
    a_mat = jnp.where(ri > ci, _bdot(kb, k, BNT) * decay, 0.0)
    egc = jnp.exp(gc_col)
    inv = _unit_lower_inverse(a_mat)
    u = _bdot(inv, vc * beta_col, BNN)
    w = _bdot(inv, kb * egc, BNN)
    attn = _bdot(q, k, BNT) * decay
    g_last = jnp.sum(g_row, axis=2, keepdims=True)
    v_new = u - _bdot(w, state, BNN)
    o = _bdot(q * egc, state, BNN) + _bdot(attn, v_new, BNN)
    new_state = state * jnp.exp(g_last) + _bdot(k * jnp.exp(g_last - gc_col), v_new, BTN)
    return o, new_state


def _gdn_group(nvh):
    return 4 if nvh % 4 == 0 else 2


def _gdn_specs(nch, nkh, hb, rev):
    def n_of(n):
        return nch - 1 - n if rev else n

    hk = hb // 2
    per_k = pl.BlockSpec((GDN_CHUNK, hk * HEAD), lambda g, b, n: (b * nch + n_of(n), g))
    q = per_k
    k = pl.BlockSpec((GDN_CHUNK, hk * HEAD), lambda g, b, n: (b * nch + n_of(n), nkh // hk + g))
    v = pl.BlockSpec((GDN_CHUNK, hb * HEAD), lambda g, b, n: (b * nch + n_of(n), 2 * nkh // hb + g))
    per_v = pl.BlockSpec((GDN_CHUNK, hb * HEAD), lambda g, b, n: (b * nch + n_of(n), g))
    row = pl.BlockSpec((hb, None, None, 1, GDN_CHUNK), lambda g, b, n: (g, b, n_of(n), 0, 0))
    sc = pl.BlockSpec((hb, 1, 1), lambda g, b, n: (g, 0, 0))
    st = pl.BlockSpec((hb, None, None, HEAD, HEAD), lambda g, b, n: (g, b, n_of(n), 0, 0))
    return q, k, v, per_k, per_v, row, sc, st


def _stack_heads(ref, n):
    return jnp.stack([ref[:, h * HEAD:(h + 1) * HEAD] for h in range(n)])


def _gdn_fwd(conv, b_rows, a_rows, alog, dtb, bsz, seq, nvh, name):
    t = conv.shape[0]
    nch = seq // GDN_CHUNK
    nkh = nvh // 2
    hb = _gdn_group(nvh)
    q, k, v, _, per_v, row, sc, st = _gdn_specs(nch, nkh, hb, False)

    def body(q_ref, k_ref, v_ref, b_ref, a_ref, al_ref, dt_ref, o_ref, st_ref, state):
        @pl.when(pl.program_id(2) == 0)
        def _():
            state[...] = jnp.zeros_like(state)

        st_ref[...] = state[...]
        o, new_state = _gdn_chunk(_stack_heads(q_ref, hb // 2), _stack_heads(k_ref, hb // 2), _stack_heads(v_ref, hb),
                                  b_ref[...], a_ref[...], al_ref[...], dt_ref[...], state[...])
        for h in range(hb):
            o_ref[:, h * HEAD:(h + 1) * HEAD] = o[h]
        state[...] = new_state

    return pl.pallas_call(
        body, name=name, grid=(nvh // hb, bsz, nch), in_specs=[q, k, v, row, row, sc, sc], out_specs=[per_v, st],
        out_shape=[jax.ShapeDtypeStruct((t, nvh * HEAD), F32), jax.ShapeDtypeStruct((nvh, bsz, nch, HEAD, HEAD), F32)],
        scratch_shapes=[pltpu.VMEM((hb, HEAD, HEAD), F32)],
        compiler_params=_cparams(("parallel", "parallel", "arbitrary")),
    )(conv, conv, conv, b_rows, a_rows, alog, dtb)


def _gdn_bwd(conv, b_rows, a_rows, alog, dtb, states, do, bsz, seq, nvh, name):
    t = conv.shape[0]
    nch = seq // GDN_CHUNK
    nkh = nvh // 2
    hb = _gdn_group(nvh)
    q, k, v, per_k, per_v, row, sc, st = _gdn_specs(nch, nkh, hb, True)

    def body(q_ref, k_ref, v_ref, b_ref, a_ref, al_ref, dt_ref, st_ref, do_ref,
             dq_ref, dk_ref, dv_ref, db_ref, da_ref, dal_ref, ddt_ref, dstate):
        b, n = pl.program_id(1), pl.program_id(2)

        @pl.when(n == 0)
        def _():
            dstate[...] = jnp.zeros_like(dstate)

        _, vjp = jax.vjp(_gdn_chunk, _stack_heads(q_ref, hb // 2), _stack_heads(k_ref, hb // 2), _stack_heads(v_ref, hb),
                         b_ref[...], a_ref[...], al_ref[...], dt_ref[...], st_ref[...])
        dq, dk, dv, db, da, dal, ddt, dst = vjp((_stack_heads(do_ref, hb), dstate[...]))
        for h in range(hb // 2):
            dq_ref[:, h * HEAD:(h + 1) * HEAD] = dq[h]
            dk_ref[:, h * HEAD:(h + 1) * HEAD] = dk[h]
        for h in range(hb):
            dv_ref[:, h * HEAD:(h + 1) * HEAD] = dv[h]
        db_ref[...] = db
        da_ref[...] = da
        dstate[...] = dst
        start = jnp.logical_and(b == 0, n == 0)

        @pl.when(start)
        def _():
            dal_ref[...] = dal
            ddt_ref[...] = ddt

        @pl.when(jnp.logical_not(start))
        def _():
            dal_ref[...] += dal
            ddt_ref[...] += ddt

    f = lambda *s: jax.ShapeDtypeStruct(s, F32)
    return pl.pallas_call(
        body, name=name, grid=(nvh // hb, bsz, nch), in_specs=[q, k, v, row, row, sc, sc, st, per_v],
        out_specs=[per_k, per_k, per_v, row, row, sc, sc],
        out_shape=[f(t, nkh * HEAD), f(t, nkh * HEAD), f(t, nvh * HEAD), f(*b_rows.shape), f(*a_rows.shape), f(nvh, 1, 1), f(nvh, 1, 1)],
        scratch_shapes=[pltpu.VMEM((hb, HEAD, HEAD), F32)],
        compiler_params=_cparams(("arbitrary", "arbitrary", "arbitrary")),
    )(conv, conv, conv, b_rows, a_rows, alog, dtb, states, do)


ANY = pl.BlockSpec(memory_space=pl.ANY)
FLIPS = (2, 1, 3)


def _place():
    x, y, c = lax.axis_index("x"), lax.axis_index("y"), lax.axis_index("c")
    chips = [(1 - x, y), (x, 1 - y), (1 - x, 1 - y)]
    return x, y, c, chips


def _chip_index():
    return 2 * lax.axis_index("x") + lax.axis_index("y")


def _core_index():
    return lax.axis_index("c")


def _wide(w):
    return w if (w <= 4096 or w % LANES) else _pick(w, 2048, LANES)


def _rows(start, size, mult=16):
    return pl.ds(pl.multiple_of(start, mult), size)


def _cast_window(w_stack, layer, cls, name):
    _, r, w = w_stack.shape
    if cls == "U":
        tr = _pick(r, 256, 16)
        grid = (r // tr, 1)
        in_spec = pl.BlockSpec((None, tr, w), lambda i, j: (layer, i, 0))
        out_spec = pl.BlockSpec((None, tr, w), lambda i, j: (_chip_index(), i, 0))
        out_shape = (N_CHIPS, r, w)
    else:
        tr, tc = _pick(r, 512, 16), _wide(w)
        nbr, nbc = r // tr, w // tc
        grid = (nbr, nbc)
        in_spec = pl.BlockSpec((None, tr, tc), lambda i, j: (layer, i, j))
        if cls == "C":
            out_spec = pl.BlockSpec((tr, tc), lambda i, j: (i, _chip_index() * nbc + j))
            out_shape = (r, N_CHIPS * w)
        else:
            out_spec = pl.BlockSpec((tr, tc), lambda i, j: (_chip_index() * nbr + i, j))
            out_shape = (N_CHIPS * r, w)

    def body(x_ref, o_ref):
        o_ref[...] = x_ref[...].astype(o_ref.dtype)

    return pl.pallas_call(body, name=name, grid=grid, in_specs=[in_spec], out_specs=out_spec,
                          out_shape=jax.ShapeDtypeStruct(out_shape, BF16), compiler_params=_cparams(("parallel", "parallel")))(w_stack)


def _halved(buf, cls):
    return {"C": buf.shape[0], "U": buf.shape[1], "R": buf.shape[0] // N_CHIPS}[cls]


def _part(buf, cls, chip, start, size):
    if cls == "C":
        w = buf.shape[1] // N_CHIPS
        return buf.at[_rows(start, size), pl.ds(chip * w, w)]
    if cls == "U":
        return buf.at[chip, _rows(start, size), :]
    r = buf.shape[0] // N_CHIPS
    return buf.at[_rows(chip * r + start, size), :]


def _gather_layer(bufs, classes, name):
    n = len(bufs)
    job = _gather_job(bufs, classes)

    def body(*refs):
        outs = refs[n:2 * n]
        job["start"](refs[:n], outs, refs[2 * n], refs[2 * n + 1], True)
        job["finish"](refs[:n], outs, refs[2 * n], refs[2 * n + 1], True)

    return pl.pallas_call(
        body, name=name, in_specs=[ANY] * n, out_specs=[ANY] * n, out_shape=job["out_shapes"],
        input_output_aliases={t: t for t in range(n)},
        scratch_shapes=[pltpu.SemaphoreType.DMA((job["n_sems"],)), pltpu.SemaphoreType.DMA((job["n_sems"],))],
    )(*bufs)


def _on_chip(when, fn):
    x, y, _, _ = _place()
    me = 2 * x + y
    for s in range(N_CHIPS):
        cond = (me == s) if when is True else jnp.logical_and(when, me == s)
        pl.when(cond)(functools.partial(fn, s))


def _gather_job(bufs, classes):
    n = len(bufs)

    def copy(outs, send_sems, recv_sems, t, k, chip, start, to):
        size = _halved(outs[t], classes[t]) // 2
        part = _part(outs[t], classes[t], chip, start, size)
        return pltpu.make_async_remote_copy(src_ref=part, dst_ref=part, send_sem=send_sems.at[6 * t + k],
                                            recv_sem=recv_sems.at[6 * t + k], device_id=to, device_id_type=MESH)

    def halves(outs):
        return [_halved(outs[t], classes[t]) // 2 for t in range(n)]

    def start(ins, outs, send_sems, recv_sems, when):
        x, y, c, chips = _place()

        def run(s_me):
            for t, half in enumerate(halves(outs)):
                for k, (cx, cy) in enumerate(chips):
                    copy(outs, send_sems, recv_sems, t, k, s_me, c * half, (cx, cy, c)).start()
        _on_chip(when, run)

    def finish(ins, outs, send_sems, recv_sems, when):
        x, y, c, chips = _place()

        def run(s_me):
            passed = []
            for t, half in enumerate(halves(outs)):
                for k in range(3):
                    copy(outs, send_sems, recv_sems, t, k, s_me ^ FLIPS[k], c * half, (x, y, c)).wait_recv()
                    fwd = copy(outs, send_sems, recv_sems, t, 3 + k, s_me ^ FLIPS[k], c * half, (x, y, 1 - c))
                    fwd.start()
                    passed.append(fwd)
            for t, half in enumerate(halves(outs)):
                for k in range(3):
                    copy(outs, send_sems, recv_sems, t, 3 + k, s_me ^ FLIPS[k], (1 - c) * half, (x, y, c)).wait_recv()
            for t, half in enumerate(halves(outs)):
                for k, (cx, cy) in enumerate(chips):
                    copy(outs, send_sems, recv_sems, t, k, s_me, c * half, (cx, cy, c)).wait_send()
            for fwd in passed:
                fwd.wait_send()
        _on_chip(when, run)

    return dict(ins=list(bufs), alias=True, n_sems=6 * n, start=start, finish=finish,
                out_shapes=[jax.ShapeDtypeStruct(b.shape, b.dtype) for b in bufs])


def _assemble(slots, width, name):
    _, r, w = slots.shape
    tr = _pick(r, 256, 16)

    def body(s_ref, o_ref):
        parts = [s_ref[s] for s in range(N_CHIPS)]
        if width > N_CHIPS * w:
            parts.append(jnp.zeros((tr, width - N_CHIPS * w), slots.dtype))
        o_ref[...] = jnp.concatenate(parts, axis=1)

    return pl.pallas_call(
        body, name=name, grid=(r // tr,), in_specs=[pl.BlockSpec((N_CHIPS, tr, w), lambda i: (0, i, 0))],
        out_specs=pl.BlockSpec((tr, width), lambda i: (i, 0)), out_shape=jax.ShapeDtypeStruct((r, width), slots.dtype),
        compiler_params=_cparams(("parallel",)),
    )(slots)


def _rs_pair(dws, classes, name):
    n = len(dws)

    def shape_of(d, cls):
        return (N_CHIPS, d.shape[1] // 2, d.shape[2]) if cls == "R" else (d.shape[0] // 2, d.shape[1])

    def body(*refs):
        ins, outs = refs[:n], refs[n:2 * n]
        send_sems, recv_sems = refs[2 * n], refs[2 * n + 1]
        x, y, c, _ = _place()
        cps = []
        for t in range(n):
            if classes[t] == "R":
                h = ins[t].shape[1] // 2
                src = ins[t].at[:, _rows((1 - c) * h, h), :]
            else:
                h = ins[t].shape[0] // 2
                src = ins[t].at[_rows((1 - c) * h, h), :]
            cp = pltpu.make_async_remote_copy(src_ref=src, dst_ref=outs[t], send_sem=send_sems.at[t], recv_sem=recv_sems.at[t],
                                              device_id=(x, y, 1 - c), device_id_type=MESH)
            cp.start()
            cps.append(cp)
        for cp in cps:
            cp.wait()

    return pl.pallas_call(
        body, name=name, in_specs=[ANY] * n, out_specs=[ANY] * n,
        out_shape=[jax.ShapeDtypeStruct(shape_of(d, cls), d.dtype) for d, cls in zip(dws, classes)],
        scratch_shapes=[pltpu.SemaphoreType.DMA((n,)), pltpu.SemaphoreType.DMA((n,))],
    )(*dws)


def _rs_add(dw, got, cls, w, name):
    if cls == "R":
        _, r, _ = dw.shape
        h = r // 2
        th, tc = _pick(h, 256, 16), _wide(w)
        nbh = h // th
        grid = (N_CHIPS, nbh, w // tc)
        in_specs = [pl.BlockSpec((None, th, tc), lambda s, i, j: (s, _core_index() * nbh + i, j)),
                    pl.BlockSpec((None, th, tc), lambda s, i, j: (s, i, j))]
        out_spec = pl.BlockSpec((None, th, tc), lambda s, i, j: (s, i, j))
        sem = ("parallel", "parallel", "parallel")

        def body(a_ref, b_ref, o_ref):
            o_ref[...] = (a_ref[...].astype(F32) + b_ref[...].astype(F32)).astype(o_ref.dtype)
    elif cls == "C":
        r = dw.shape[0]
        h = r // 2
        th, tc = _pick(h, 256, 16), _wide(w)
        nbh, nbc = h // th, w // tc
        grid = (N_CHIPS, nbh, nbc)
        in_specs = [pl.BlockSpec((th, tc), lambda s, i, j: (_core_index() * nbh + i, s * nbc + j)),
                    pl.BlockSpec((th, tc), lambda s, i, j: (i, s * nbc + j))]
        out_spec = pl.BlockSpec((None, th, tc), lambda s, i, j: (s, i, j))
        sem = ("parallel", "parallel", "parallel")

        def body(a_ref, b_ref, o_ref):
            o_ref[...] = (a_ref[...].astype(F32) + b_ref[...].astype(F32)).astype(o_ref.dtype)
    else:
        r, wp = dw.shape
        h = r // 2
        th = _pick(h, 64, 16)
        nbh = h // th
        grid = (nbh,)
        in_specs = [pl.BlockSpec((th, wp), lambda i: (_core_index() * nbh + i, 0)), pl.BlockSpec((th, wp), lambda i: (i, 0))]
        out_spec = pl.BlockSpec((N_CHIPS, th, w), lambda i: (0, i, 0))
        sem = ("parallel",)

        def body(a_ref, b_ref, o_ref):
            tot = a_ref[...].astype(F32) + b_ref[...].astype(F32)
            for s in range(N_CHIPS):
                o_ref[s] = tot[:, s * w:(s + 1) * w].astype(o_ref.dtype)

    return pl.pallas_call(body, name=name, grid=grid, in_specs=in_specs, out_specs=out_spec,
                          out_shape=jax.ShapeDtypeStruct((N_CHIPS, h, w), BF16), compiler_params=_cparams(sem))(dw, got)


def _rs_chip(zs, name):
    n = len(zs)
    job = _chip_job(zs)

    def body(*refs):
        job["start"](refs[:n], refs[n:2 * n], refs[2 * n], refs[2 * n + 1], True)
        job["finish"](refs[:n], refs[n:2 * n], refs[2 * n], refs[2 * n + 1], True)

    return pl.pallas_call(
        body, name=name, in_specs=[ANY] * n, out_specs=[ANY] * n, out_shape=job["out_shapes"],
        scratch_shapes=[pltpu.SemaphoreType.DMA((job["n_sems"],)), pltpu.SemaphoreType.DMA((job["n_sems"],))],
    )(*zs)


def _chip_job(zs):
    n = len(zs)

    def copies(ins, outs, send_sems, recv_sems):
        x, y, c, chips = _place()
        return [pltpu.make_async_remote_copy(src_ref=ins[t].at[2 * cx + cy], dst_ref=outs[t].at[k], send_sem=send_sems.at[3 * t + k],
                                             recv_sem=recv_sems.at[3 * t + k], device_id=(cx, cy, c), device_id_type=MESH)
                for t in range(n) for k, (cx, cy) in enumerate(chips)]

    def start(ins, outs, send_sems, recv_sems, when):
        def run():
            for cp in copies(ins, outs, send_sems, recv_sems):
                cp.start()
        run() if when is True else pl.when(when)(run)

    def finish(ins, outs, send_sems, recv_sems, when):
        def run():
            for cp in copies(ins, outs, send_sems, recv_sems):
                cp.wait()
        run() if when is True else pl.when(when)(run)

    return dict(ins=list(zs), alias=False, n_sems=3 * n, start=start, finish=finish,
                out_shapes=[jax.ShapeDtypeStruct((3,) + z.shape[1:], z.dtype) for z in zs])


def _rs_sum(z, parts, name):
    _, h, w = z.shape
    th = _pick(h, 256, 16)
    tc = _wide(w)
    nbh = h // th

    def body(z_ref, k_ref, o_ref):
        acc = z_ref[...].astype(F32)
        for k in range(3):
            acc = acc + k_ref[k].astype(F32)
        o_ref[...] = acc

    return pl.pallas_call(
        body, name=name, grid=(nbh, w // tc),
        in_specs=[pl.BlockSpec((None, th, tc), lambda i, j: (_chip_index(), i, j)), pl.BlockSpec((3, th, tc), lambda i, j: (0, i, j))],
        out_specs=pl.BlockSpec((th, tc), lambda i, j: (_core_index() * nbh + i, j)),
        out_shape=jax.ShapeDtypeStruct((2 * h, w), F32), compiler_params=_cparams(("parallel", "parallel")))(z, parts)


def _rs_join(bufs, name):
    n = len(bufs)

    def body(*refs):
        outs = refs[n:2 * n]
        send_sems, recv_sems = refs[2 * n], refs[2 * n + 1]
        x, y, c, _ = _place()
        cps = []
        for t in range(n):
            h = outs[t].shape[0] // 2
            mine = outs[t].at[_rows(c * h, h), :]
            cp = pltpu.make_async_remote_copy(src_ref=mine, dst_ref=mine, send_sem=send_sems.at[t], recv_sem=recv_sems.at[t],
                                              device_id=(x, y, 1 - c), device_id_type=MESH)
            cp.start()
            cps.append(cp)
        for t in range(n):
            h = outs[t].shape[0] // 2
            other = outs[t].at[_rows((1 - c) * h, h), :]
            pltpu.make_async_remote_copy(src_ref=other, dst_ref=other, send_sem=send_sems.at[t], recv_sem=recv_sems.at[t],
                                         device_id=(x, y, c), device_id_type=MESH).wait_recv()
        for cp in cps:
            cp.wait_send()

    return pl.pallas_call(
        body, name=name, in_specs=[ANY] * n, out_specs=[ANY] * n,
        out_shape=[jax.ShapeDtypeStruct(b.shape, b.dtype) for b in bufs],
        input_output_aliases={t: t for t in range(n)},
        scratch_shapes=[pltpu.SemaphoreType.DMA((n,)), pltpu.SemaphoreType.DMA((n,))],
    )(*bufs)


def _rs_chip_sums(dws, classes, widths, tag):
    gots = _rs_pair(dws, classes, f"rs_pair_{tag}")
    return [_rs_add(d, g, cls, w, f"rs_add_{tag}_{t}") for t, (d, g, cls, w) in enumerate(zip(dws, gots, classes, widths))]


def _rs_finish(zs, parts, tag):
    halves = [_rs_sum(z, p, f"rs_sum_{tag}_{t}") for t, (z, p) in enumerate(zip(zs, parts))]
    return _rs_join(halves, f"rs_join_{tag}")


def _reduce_scatter_layer(dws, classes, widths, tag):
    zs = _rs_chip_sums(dws, classes, widths, tag)
    return _rs_finish(zs, _rs_chip(zs, f"rs_chip_{tag}"), tag)


def _gather_shards(mine, name):
    r, w = mine.shape
    rh = r // 2

    def body(mine_ref, out_ref, send_sems, recv_sems, local_sem):
        x, y, c, chips = _place()
        me = 2 * x + y
        half = _rows(c * rh, rh)
        other = _rows((1 - c) * rh, rh)

        def copy(k, src, chip, rows, to):
            return pltpu.make_async_remote_copy(src_ref=src, dst_ref=out_ref.at[chip, rows], send_sem=send_sems.at[k],
                                                recv_sem=recv_sems.at[k], device_id=to, device_id_type=MESH)

        local = pltpu.make_async_copy(mine_ref, out_ref.at[me], local_sem)
        local.start()
        sends = [copy(k, mine_ref.at[half], me, half, (cx, cy, c)) for k, (cx, cy) in enumerate(chips)]
        for s in sends:
            s.start()
        passed = []
        for k, (cx, cy) in enumerate(chips):
            chip = 2 * cx + cy
            copy(k, mine_ref.at[half], chip, half, (x, y, c)).wait_recv()
            fwd = copy(3 + k, out_ref.at[chip, half], chip, half, (x, y, 1 - c))
            fwd.start()
            passed.append(fwd)
        for k, (cx, cy) in enumerate(chips):
            copy(3 + k, mine_ref.at[half], 2 * cx + cy, other, (x, y, c)).wait_recv()
        for s in sends + passed:
            s.wait_send()
        local.wait()

    return pl.pallas_call(
        body, name=name, in_specs=[ANY], out_specs=ANY, out_shape=jax.ShapeDtypeStruct((N_CHIPS, r, w), mine.dtype),
        scratch_shapes=[pltpu.SemaphoreType.DMA((6,)), pltpu.SemaphoreType.DMA((6,)), pltpu.SemaphoreType.DMA],
    )(mine)


def _sum_slots(slots, name):
    n, r, w = slots.shape
    tr = _pick(r, 256, SUBLANES)

    def body(s_ref, o_ref):
        acc = s_ref[0]
        for k in range(1, n):
            acc = acc + s_ref[k]
        o_ref[...] = acc

    return pl.pallas_call(
        body, name=name, grid=(r // tr,), in_specs=[pl.BlockSpec((n, tr, w), lambda i: (0, i, 0))],
        out_specs=pl.BlockSpec((tr, w), lambda i: (i, 0)), out_shape=jax.ShapeDtypeStruct((r, w), F32),
        compiler_params=_cparams(("parallel",)),
    )(slots)


def _all_to_all_sum(flat, name):
    r, w = flat.shape

    def body(f_ref, out_ref, send_sems, recv_sems, local_sem):
        x, y, c, _ = _place()
        me = 4 * x + 2 * y + c
        local = pltpu.make_async_copy(f_ref, out_ref.at[me], local_sem)
        local.start()
        sends = []
        for k in range(1, 8):
            peer = (x ^ (k >> 2), y ^ ((k >> 1) & 1), c ^ (k & 1))
            s = pltpu.make_async_remote_copy(src_ref=f_ref, dst_ref=out_ref.at[me], send_sem=send_sems.at[k - 1],
                                             recv_sem=recv_sems.at[k - 1], device_id=peer, device_id_type=MESH)
            s.start()
            sends.append(s)
        for k in range(1, 8):
            peer_slot = 4 * (x ^ (k >> 2)) + 2 * (y ^ ((k >> 1) & 1)) + (c ^ (k & 1))
            pltpu.make_async_remote_copy(src_ref=f_ref, dst_ref=out_ref.at[peer_slot], send_sem=send_sems.at[k - 1],
                                         recv_sem=recv_sems.at[k - 1], device_id=(x, y, c), device_id_type=MESH).wait_recv()
        for s in sends:
            s.wait_send()
        local.wait()

    slots = pl.pallas_call(
        body, name=name, in_specs=[ANY], out_specs=ANY, out_shape=jax.ShapeDtypeStruct((8, r, w), flat.dtype),
        scratch_shapes=[pltpu.SemaphoreType.DMA((7,)), pltpu.SemaphoreType.DMA((7,)), pltpu.SemaphoreType.DMA],
    )(flat)
    return _sum_slots(slots, name + "_sum")


def _pack(pieces, lead):
    flat = []
    n_lead = len(lead)
    for p in pieces:
        f = p.reshape(*lead, -1)
        pad = (-f.shape[-1]) % PACK_COLS
        if pad:
            f = jnp.pad(f, [(0, 0)] * n_lead + [(0, pad)])
        flat.append(f)
    f = jnp.concatenate(flat, axis=-1) if len(flat) > 1 else flat[0]
    pad = (-f.shape[-1]) % (32 * PACK_COLS)
    if pad:
        f = jnp.pad(f, [(0, 0)] * n_lead + [(0, pad)])
    return f.reshape(*lead, -1, PACK_COLS)


def _unpack(buf, shapes, lead):
    f = buf.reshape(*lead, -1)
    out, off = [], 0
    for shp in shapes:
        n = 1
        for s in shp:
            n *= s
        out.append(f[..., off:off + n].reshape(*lead, *shp))
        off += n + ((-n) % PACK_COLS)
    return out


def _join(g, axis):
    g = jnp.moveaxis(g, 0, axis)
    return g.reshape(*g.shape[:axis], g.shape[axis] * g.shape[axis + 1], *g.shape[axis + 2:])


def _split(full, axis):
    shp = full.shape
    g = full.reshape(*shp[:axis], N_CHIPS, shp[axis] // N_CHIPS, *shp[axis + 1:])
    return jnp.moveaxis(g, axis, 0)


def _norm_bwd(xs, dres, dhs, gain, name):
    def fn(x, dr, *rest):
        dh = rest[0]
        for d in rest[1:-1]:
            dh = dh + d
        _, vjp = jax.vjp(_rms, x, rest[-1])
        dx, dg = vjp(dh.astype(F32))
        return dr + dx, dg
    return _rowmap(fn, [xs, dres] + list(dhs), [gain], name, n_acc=1)


def _mm(a, b, kind, dtype, name, job):
    if job is None:
        return _matmul(a, b, kind, dtype, name), None
    return _matmul(a, b, kind, dtype, name, job=job)


def _ffn_fwd(x1, h2, w, seq, tag, jobs=(None, None, None)):
    gpre, j0 = _mm(h2, w["ffn_w_gate"], "nn", F32, f"ffn_gate_{tag}", jobs[0])
    up, j1 = _mm(h2, w["ffn_w_up"], "nn", BF16, f"ffn_up_{tag}", jobs[1])
    gate = _conv_fwd(gpre, gpre.shape[1], w["ffn_conv_w"], w["ffn_conv_b"].reshape(1, -1), seq, f"ffn_conv_{tag}")
    act, = _rowmap(lambda g, u: (_silu(g) * u).astype(BF16), [gate, up], [], f"ffn_act_{tag}", tc=_pick(gate.shape[1], 1024, LANES))
    wd = w["ffn_w_down"] if "ffn_w_down" in w else j0[0]
    f, j2 = _mm(act, wd, "nn", F32, f"ffn_down_{tag}", jobs[2])
    return f, (gpre, up, gate, act), [j0, j1, j2]


def _ffn_bwd(dx2, h2, saved, wg, wu, conv_w, wd, seq, tag, jobs=(None,) * 5):
    gpre, up, gate, act = saved
    da, j0 = _mm(dx2, wd, "nt", BF16, f"ffn_dact_{tag}", jobs[0])
    d_wd, j1 = _mm(act, dx2, "tn", BF16,f"ffn_dwd_{tag}", jobs[1])

    def act_bwd(g, u, d):
        _, vjp = jax.vjp(lambda g_, u_: _silu(g_) * u_, g, u)
        return vjp(d.astype(F32))
    dgate, dup = _rowmap(act_bwd, [gate, up, da], [], f"ffn_dactfn_{tag}", tc=_pick(gate.shape[1], 1024, LANES))
    dgpre, d_cw, d_cb = _conv_bwd(dgate, gpre, gpre.shape[1], conv_w, seq, True, f"ffn_dconv_{tag}", dx_dtype=BF16)
    dh_a, j2 = _mm(dgpre, wg, "nt", F32, f"ffn_dh_gate_{tag}", jobs[2])
    dh_b, j3 = _mm(dup, wu, "nt", F32, f"ffn_dh_up_{tag}", jobs[3])
    d_wg, j4 = _mm(h2, dgpre, "tn", BF16,f"ffn_dwg_{tag}", jobs[4])
    d_wu = _matmul(h2, dup, "tn", BF16,f"ffn_dwu_{tag}")
    grads = dict(ffn_w_gate=d_wg, ffn_w_up=d_wu, ffn_conv_w=d_cw, ffn_conv_b=d_cb.reshape(-1), ffn_w_down=d_wd)
    return (dh_a, dh_b), grads, [j0, j1, j2, j3, j4]


def _gmlp_fwd(h, w, tag, jobs=(None, None)):
    a = w["a_w_out"].shape[0]
    p, j0 = _mm(h, w["a_w_in"], "nn", F32, f"a_in_{tag}", jobs[0])
    b_in, vnorm = w["a_b_in"].reshape(1, -1), w["a_v_norm"].reshape(1, -1)

    def fn(p_, b_, g_):
        hh = _gelu_tanh(p_ + b_)
        return hh[:, :a], _rms(hh[:, a:], g_)
    u, vn = _rowmap(fn, [p], [b_in, vnorm], f"a_gelu_{tag}")
    b_col = w["a_b_s"][:, :, None]
    y = _sgu_fwd(vn, u, w["a_w_s"], b_col, f"a_sgu_{tag}")
    m, j1 = _mm(y, w["a_w_out"], "nn", F32, f"a_out_{tag}", jobs[1])
    return m, (p, u, vn, y, fn, b_in, vnorm, b_col), [j0, j1]


def _gmlp_bwd(dm, h, saved, w, tag):
    p, u, vn, y, fn, b_in, vnorm, b_col = saved
    dy = _matmul(dm, w["a_w_out"], "nt", F32, f"a_dy_{tag}")
    d_wout = _matmul(y, dm, "tn", BF16,f"a_dwout_{tag}")
    dvn, du, d_ws, d_bcol = _sgu_bwd(vn, u, dy, w["a_w_s"], b_col, f"a_dsgu_{tag}")

    def bwd(p_, du_, dvn_, b_, g_):
        _, vjp = jax.vjp(fn, p_, b_, g_)
        return vjp((du_, dvn_))
    dp, d_bin, d_vnorm = _rowmap(bwd, [p, du, dvn], [b_in, vnorm], f"a_dgelu_{tag}", n_acc=2)
    dh = _matmul(dp, w["a_w_in"], "nt", F32, f"a_dh_{tag}")
    d_win = _matmul(h, dp, "tn", BF16,f"a_dwin_{tag}")
    return (dh,), dict(a_w_in=d_win, a_b_in=d_bin.reshape(-1), a_v_norm=d_vnorm.reshape(-1), a_w_s=d_ws, a_b_s=d_bcol[:, :, 0], a_w_out=d_wout)


def _fox_fwd_mixer(h, w, bsz, seq, tag, jobs=(None, None)):
    d = h.shape[1]
    nh = d // HEAD
    win = w["b_w_in"]
    wp = win.shape[1]
    proj, j0 = _mm(h, win, "nn", F32, f"b_in_{tag}", jobs[0])
    gq, gk = w["b_q_norm"].reshape(1, HEAD), w["b_k_norm"].reshape(1, HEAD)
    bf = jnp.pad(w["b_b_f"].reshape(1, nh), ((0, 0), (0, LANES - nh)))

    def prep(pq, pk, pfl, gq_, gk_, bf_):
        qn = jnp.concatenate([_rms(x, gq_) for x in _heads(pq)], axis=1)
        kn = jnp.concatenate([_rms(x, gk_) for x in _heads(pk)], axis=1)
        return qn, kn, -_softplus(-(pfl + bf_))
    wins = [_win(proj, d, 0), _win(proj, d, 1), _win(proj, LANES, 4 * d // LANES)]

    def prep_fwd(pq, pk, pfl, gq_, gk_, bf_):
        qn, kn, lf = prep(pq, pk, pfl, gq_, gk_, bf_)
        return qn.astype(BF16), kn.astype(BF16), lf
    qn, kn, lf = _rowmap(prep_fwd, wins, [gq, gk, bf], f"b_prep_{tag}")
    cs = _cumsum_rows(lf, seq, False, f"b_cumsum_{tag}")
    c_rows = cs[:, :nh].reshape(bsz, seq, nh).transpose(0, 2, 1)[:, :, None, :]
    o, lse = _fox_fwd(qn, kn, proj, 2 * nh, c_rows, bsz, seq, nh, f"b_attn_{tag}")
    og = _win(proj, d, 3)
    y, = _rowmap(lambda o_, g_: (o_ * jax.nn.sigmoid(g_)).astype(BF16), [o, og], [], f"b_gate_{tag}")
    m, j1 = _mm(y, w["b_w_out"], "nn", F32, f"b_out_{tag}", jobs[1])
    return m, (proj, qn, kn, c_rows, o, lse, y, prep, wins, (gq, gk, bf), wp), [j0, j1]


def _fox_bwd_mixer(dm, h, saved, w, bsz, seq, tag):
    proj, qn, kn, c_rows, o, lse, y, prep, wins, (gq, gk, bf), wp = saved
    d = h.shape[1]
    nh = d // HEAD
    dy = _matmul(dm, w["b_w_out"], "nt", F32, f"b_dy_{tag}")
    d_wout = _matmul(y, dm, "tn", BF16,f"b_dwout_{tag}")
    og = _win(proj, d, 3)

    def gate_bwd(o_, g_, dy_):
        _, vjp = jax.vjp(lambda a, b: a * jax.nn.sigmoid(b), o_, g_)
        return vjp(dy_)
    do, dog = _rowmap(gate_bwd, [o, og, dy], [], f"b_dgate_{tag}")
    dqn, delta = _fox_bwd_q(qn, kn, proj, 2 * nh, do, lse, c_rows, bsz, seq, nh, f"b_dattn_q_{tag}")
    dkn, dv, dc_rows = _fox_bwd_kv(qn, kn, proj, 2 * nh, do, lse, delta, c_rows, bsz, seq, nh, f"b_dattn_kv_{tag}")
    dc = dc_rows[:, :, 0, :].transpose(0, 2, 1).reshape(bsz * seq, nh)
    dc = jnp.pad(dc, ((0, 0), (0, LANES - nh)))
    dlf = _cumsum_rows(dc, seq, True, f"b_dcumsum_{tag}")
    extra = wp - (4 * d + LANES)

    def prep_bwd(pq, pk, pfl, dqn_, dkn_, dv_, dog_, dlf_, gq_, gk_, bf_):
        _, vjp = jax.vjp(prep, pq, pk, pfl, gq_, gk_, bf_)
        dpq, dpk, dpfl, dgq, dgk, dbf = vjp((dqn_, dkn_, dlf_))
        parts = [dpq, dpk, dv_, dog_, dpfl]
        if extra:
            parts.append(jnp.zeros((pq.shape[0], extra), F32))
        return jnp.concatenate(parts, axis=1), dgq, dgk, dbf
    dproj, d_gq, d_gk, d_bf = _rowmap(prep_bwd, wins + [dqn, dkn, dv, dog, dlf], [gq, gk, bf], f"b_dprep_{tag}", n_acc=3)
    dh = _matmul(dproj, w["b_w_in"], "nt", F32, f"b_dh_{tag}")
    d_win = _matmul(h, dproj, "tn", BF16,f"b_dwin_{tag}")
    return (dh,), dict(b_w_in=d_win, b_b_f=d_bf[0, :nh], b_q_norm=d_gq.reshape(-1), b_k_norm=d_gk.reshape(-1), b_w_out=d_wout)


def _gdn_fwd_mixer(h, w, bsz, seq, tag, jobs=(None, None)):
    d = h.shape[1]
    nkh = d // HEAD
    nvh = 2 * nkh
    dqkv = (2 * nkh + nvh) * HEAD
    dz = nvh * HEAD
    nch = seq // GDN_CHUNK
    proj, j0 = _mm(h, w["c_w_in"], "nn", F32, f"c_in_{tag}", jobs[0])
    conv = _conv_fwd(proj, dqkv, w["c_conv_w"], None, seq, f"c_conv_{tag}")

    def rows_of(cols):
        return cols.reshape(bsz, nch, GDN_CHUNK, nvh).transpose(3, 0, 1, 2)[:, :, :, None, :]
    b_rows = rows_of(proj[:, dqkv + dz:dqkv + dz + nvh])
    a_rows = rows_of(proj[:, dqkv + dz + nvh:dqkv + dz + 2 * nvh])
    alog, dtb = w["c_a_log"].reshape(nvh, 1, 1), w["c_dt_bias"].reshape(nvh, 1, 1)
    o, states = _gdn_fwd(conv, b_rows, a_rows, alog, dtb, bsz, seq, nvh, f"c_core_{tag}")
    gn = w["c_out_norm"].reshape(1, HEAD)
    zwin = _win(proj, dz, dqkv // dz)

    def outfn(o_, z_, g_):
        return jnp.concatenate([_rms(a, g_) * _silu(b) for a, b in zip(_heads(o_), _heads(z_))], axis=1)
    y, = _rowmap(lambda o_, z_, g_: outfn(o_, z_, g_).astype(BF16), [o, zwin], [gn], f"c_outnorm_{tag}")
    m, j1 = _mm(y, w["c_w_out"], "nn", F32, f"c_out_{tag}", jobs[1])
    return m, (proj, conv, b_rows, a_rows, alog, dtb, o, states, y, gn, zwin, outfn), [j0, j1]


def _gdn_bwd_mixer(dm, h, saved, w, bsz, seq, tag):
    proj, conv, b_rows, a_rows, alog, dtb, o, states, y, gn, zwin, outfn = saved
    d = h.shape[1]
    nkh = d // HEAD
    nvh = 2 * nkh
    dk_, dv_ = nkh * HEAD, nvh * HEAD
    dqkv = 2 * dk_ + dv_
    dz = dv_
    wp = proj.shape[1]
    dy = _matmul(dm, w["c_w_out"], "nt", F32, f"c_dy_{tag}")
    d_wout = _matmul(y, dm, "tn", BF16,f"c_dwout_{tag}")

    def out_bwd(o_, z_, dy_, g_):
        _, vjp = jax.vjp(outfn, o_, z_, g_)
        return vjp(dy_)
    do, dzz, d_gn = _rowmap(out_bwd, [o, zwin, dy], [gn], f"c_doutnorm_{tag}", n_acc=1)
    dq, dk, dv, db_rows, da_rows, d_alog, d_dtb = _gdn_bwd(conv, b_rows, a_rows, alog, dtb, states, do, bsz, seq, nvh, f"c_dcore_{tag}")
    cw = w["c_conv_w"]
    dq_pre, d_cwq = _conv_bwd(dq, proj, dk_, cw[:, :dk_], seq, False, f"c_dconv_q_{tag}", xcol=0)
    dk_pre, d_cwk = _conv_bwd(dk, proj, dk_, cw[:, dk_:2 * dk_], seq, False, f"c_dconv_k_{tag}", xcol=dk_)
    dv_pre, d_cwv = _conv_bwd(dv, proj, dv_, cw[:, 2 * dk_:], seq, False, f"c_dconv_v_{tag}", xcol=2 * dk_)
    d_cw = jnp.concatenate([d_cwq, d_cwk, d_cwv], axis=1)

    def cols_of(rows):
        return rows[:, :, :, 0, :].transpose(1, 2, 3, 0).reshape(bsz * seq, nvh)
    dba = jnp.concatenate([cols_of(db_rows), cols_of(da_rows)], axis=1)
    dba = jnp.pad(dba, ((0, 0), (0, wp - dqkv - dz - 2 * nvh)))
    dproj, = _rowmap(lambda *parts: jnp.concatenate(parts, axis=1), [dq_pre, dk_pre, dv_pre, dzz, dba], [], f"c_dproj_{tag}")
    dh = _matmul(dproj, w["c_w_in"], "nt", F32, f"c_dh_{tag}")
    d_win = _matmul(h, dproj, "tn", BF16,f"c_dwin_{tag}")
    return (dh,), dict(c_w_in=d_win, c_conv_w=d_cw, c_a_log=d_alog.reshape(-1), c_dt_bias=d_dtb.reshape(-1),
                       c_out_norm=d_gn.reshape(-1), c_w_out=d_wout)


_MIXER_FWD = (lambda h, w, bsz, seq, tag, jobs: _gmlp_fwd(h, w, tag, jobs), _fox_fwd_mixer, _gdn_fwd_mixer)
_MIXER_BWD = (lambda dm, h, s, w, bsz, seq, tag: _gmlp_bwd(dm, h, s, w, tag), _fox_bwd_mixer, _gdn_bwd_mixer)


class _NoPlan:
    def __init__(self, layers):
        self.layers = layers

    def weights(self, i):
        return self.layers[i]

    def fwd_jobs(self, i):
        return (None,) * 5

    def fwd_done(self, i, first, outs):
        pass

    def bwd_jobs(self, i):
        return (None,) * 5

    def bwd_done(self, i, outs):
        pass

    def grads_ready(self, i, grads):
        pass


def _local_step(x, target, plan, depth, bsz, seq):
    t, d = x.shape
    saved = []
    m_prev = None
    xin = x
    for i in range(depth):
        w = plan.weights(i)
        tag = f"l{i}"
        g_mix, g_ffn = w["norm_mix"].reshape(1, d), w["norm_ffn"].reshape(1, d)
        if i == 0:
            h, = _rowmap(lambda x_, g_: _rms(x_, g_).astype(BF16), [xin], [g_mix], f"norm_mix_{tag}")
            xl = xin
        else:
            xl, h = _rowmap(lambda x_, m_, g_: (x_ + m_, _rms(x_ + m_, g_).astype(BF16)), [xin, m_prev], [g_mix], f"norm_mix_{tag}")
        fjobs = plan.fwd_jobs(i)
        m, msaved, mouts = _MIXER_FWD[i % 3](h, w, bsz, seq, tag, fjobs[:2])
        plan.fwd_done(i, 0, mouts)
        x1, h2 = _rowmap(lambda x_, m_, g_: (x_ + m_, _rms(x_ + m_, g_).astype(BF16)), [xl, m], [g_ffn], f"norm_ffn_{tag}")
        f, fsaved, jouts = _ffn_fwd(x1, h2, w, seq, tag, fjobs[2:])
        plan.fwd_done(i, 2, jouts)
        saved.append((xl, h, msaved, x1, h2, fsaved))
        xin, m_prev = x1, f

    def loss_fn(x_, f_, tg_):
        e = x_ + f_ - tg_
        return e * (1.0 / d), jnp.full((1, LANES), (0.5 / d) * jnp.sum(e * e), F32)
    dx, loss_acc = _rowmap(loss_fn, [xin, m_prev, target], [], "loss", n_acc=1)
    loss = loss_acc[0, 0]

    grads = [None] * depth
    for i in reversed(range(depth)):
        w = plan.weights(i)
        tag = f"l{i}"
        xl, h, msaved, x1, h2, fsaved = saved[i]
        g_mix, g_ffn = w["norm_mix"].reshape(1, d), w["norm_ffn"].reshape(1, d)
        dhs, gw, jouts = _ffn_bwd(dx, h2, fsaved, w["ffn_w_gate"], w["ffn_w_up"], w["ffn_conv_w"], w["ffn_w_down"], seq, tag, plan.bwd_jobs(i))
        plan.bwd_done(i, jouts)
        dx1, d_gffn = _norm_bwd(x1, dx, dhs, g_ffn, f"dnorm_ffn_{tag}")
        dhs, gm = _MIXER_BWD[i % 3](dx1, h, msaved, w, bsz, seq, tag)
        dx, d_gmix = _norm_bwd(xl, dx1, dhs, g_mix, f"dnorm_mix_{tag}")
        gw.update(gm)
        gw["norm_mix"], gw["norm_ffn"] = d_gmix.reshape(-1), d_gffn.reshape(-1)
        grads[i] = gw
        plan.grads_ready(i, gw)
    return loss, dx, grads


def _adamw_math(w_, g_, m_, v_):
    m_new = ADAM_B1 * m_ + (1.0 - ADAM_B1) * g_
    v_new = ADAM_B2 * v_ + (1.0 - ADAM_B2) * (g_ * g_)
    m_hat = m_new / (1.0 - ADAM_B1 ** ADAM_STEP)
    v_hat = v_new / (1.0 - ADAM_B2 ** ADAM_STEP)
    delta = -ADAM_LR * (m_hat / (jnp.sqrt(v_hat) + ADAM_EPS) + ADAM_WD * w_)
    return delta, m_new, v_new


def _adamw(w, g, m, v, name):
    shape = w.shape
    if w.ndim == 1:
        w, g, m, v = (a.reshape(1, -1) for a in (w, g, m, v))
    return [o.reshape(shape) for o in _elementwise(_adamw_math, [w, g, m, v], 3, name)]


def _adamw_layers(w, gs, m, v, name):
    nl, r, c = w.shape
    tr = _pick(r, max(SUBLANES, (1 << 19) // c // SUBLANES * SUBLANES), SUBLANES)

    def body(*refs):
        w_ref, m_ref, v_ref = refs[:3]
        g_refs = refs[3:3 + nl]
        go_ref, d_ref, mo_ref, vo_ref = refs[3 + nl:]
        layer = pl.program_id(0)
        for k in range(nl):
            @pl.when(layer == k)
            def _(k=k):
                g = g_refs[k][...]
                delta, m_new, v_new = _adamw_math(w_ref[...], g, m_ref[...], v_ref[...])
                go_ref[...] = g
                d_ref[...] = delta
                mo_ref[...] = m_new
                vo_ref[...] = v_new

    st = pl.BlockSpec((None, tr, c), lambda l, i: (l, i, 0))
    g_specs = [pl.BlockSpec((tr, c), functools.partial(lambda l, i, k: (jnp.where(l == k, i, 0), 0), k=k)) for k in range(nl)]
    return pl.pallas_call(
        body, name=name, grid=(nl, r // tr), in_specs=[st, st, st] + g_specs, out_specs=[st] * 4,
        out_shape=[jax.ShapeDtypeStruct(w.shape, F32)] * 4, compiler_params=_cparams(("parallel", "parallel")),
    )(w, m, v, *gs)


WEIGHTS = ['norm_mix', 'norm_ffn', 'ffn_w_gate', 'ffn_w_up', 'ffn_conv_w', 'ffn_conv_b', 'ffn_w_down', 'a_w_in', 'a_b_in', 'a_v_norm',
           'a_w_s', 'a_b_s', 'a_w_out', 'b_w_in', 'b_b_f', 'b_q_norm', 'b_k_norm', 'b_w_out', 'c_w_in', 'c_conv_w', 'c_a_log',
           'c_dt_bias', 'c_out_norm', 'c_w_out']
BIG = {'ffn_w_gate': 1, 'ffn_w_up': 1, 'ffn_w_down': 0, 'a_w_in': 1, 'a_w_out': 0, 'b_w_in': 1, 'b_w_out': 0, 'c_w_in': 1, 'c_w_out': 0}
SMALL_SHARDED = {'ffn_conv_w': 1, 'a_b_in': 0, 'a_v_norm': 0, 'c_conv_w': 1}
MIXER_NAMES = (('a_w_in', 'a_b_in', 'a_v_norm', 'a_w_s', 'a_b_s', 'a_w_out'), ('b_w_in', 'b_b_f', 'b_q_norm', 'b_k_norm', 'b_w_out'),
               ('c_w_in', 'c_conv_w', 'c_a_log', 'c_dt_bias', 'c_out_norm', 'c_w_out'))
FFN_NAMES = ('norm_mix', 'norm_ffn', 'ffn_w_gate', 'ffn_w_up', 'ffn_conv_w', 'ffn_conv_b', 'ffn_w_down')


def _layer_entries(depth):
    out = []
    for i in range(depth):
        kind, j = i % 3, i // 3
        out.append([(n, i) for n in FFN_NAMES] + [(n, j) for n in MIXER_NAMES[kind]])
    return out


def _layout(name, shard_shape):
    if BIG[name] == 0:
        return "R"
    return "C" if shard_shape[-1] % LANES == 0 else "U"


JOB_GROUPS = ((3,), (4,), (0,), (1,), (2,))


class _Plan:
    def __init__(self, params, entries, small_full):
        self.params, self.entries, self.small_full = params, entries, small_full
        self.depth = len(entries)
        self.big = [[(n, j, _layout(n, params[n].shape[1:])) for n, j in ent if n in BIG] for ent in entries]
        self.layers = [None] * self.depth
        self.zs = [None] * self.depth
        self.total = {}
        for i, ent in enumerate(entries):
            self.layers[i] = {n: (small_full[(n, j)] if n in SMALL_SHARDED else params[n][j]) for n, j in ent if n not in BIG}
        mixer = [self._cast(0, t) for t in (3, 4)]
        mixer = _gather_layer(mixer, [self.big[0][t][2] for t in (3, 4)], "gather_mixer_l0")
        self._install(0, 3, mixer[0])
        self._install(0, 4, mixer[1])

    def _cast(self, i, t):
        n, j, cls = self.big[i][t]
        return _cast_window(self.params[n], j, cls, f"cast_{n}_l{i}")

    def _install(self, i, t, buf):
        n, j, cls = self.big[i][t]
        self.layers[i][n] = _assemble(buf, _pad_cols(N_CHIPS * buf.shape[2]), f"assemble_{n}_l{i}") if cls == "U" else buf

    def weights(self, i):
        return self.layers[i]

    def fwd_jobs(self, i):
        todo = [(i, 0), (i, 1), (i, 2)] + ([(i + 1, 3), (i + 1, 4)] if i + 1 < self.depth else [])
        self.fwd_todo = todo
        jobs = [_gather_job([self._cast(li, t)], [self.big[li][t][2]]) for li, t in todo]
        return jobs + [None] * (5 - len(jobs))

    def fwd_done(self, i, first, outs):
        for (li, t), got in zip(self.fwd_todo[first:], outs):
            if got is not None:
                self._install(li, t, got[0])

    def grads_ready(self, i, grads):
        dws, classes, widths = [], [], []
        for n, j, cls in self.big[i]:
            shard = self.params[n].shape[1:]
            dws.append(grads[n].reshape(N_CHIPS, shard[0], shard[1]) if cls == "R" else grads[n])
            classes.append(cls)
            widths.append(shard[1])
        self.zs[i] = _rs_chip_sums(dws, classes, widths, f"l{i}")
        if i == 0:
            self._finish(0, _rs_chip(self.zs[0], "rs_chip_l0"))

    def bwd_jobs(self, i):
        if i + 1 >= self.depth:
            return (None,) * len(JOB_GROUPS)
        return [_chip_job([self.zs[i + 1][t] for t in grp]) for grp in JOB_GROUPS]

    def bwd_done(self, i, outs):
        if i + 1 < self.depth:
            parts = [None] * len(self.big[i + 1])
            for grp, got in zip(JOB_GROUPS, outs):
                for t, p in zip(grp, got):
                    parts[t] = p
            self._finish(i + 1, parts)

    def _finish(self, i, parts):
        for (n, j, _), red in zip(self.big[i], _rs_finish(self.zs[i], parts, f"l{i}")):
            self.total[(n, j)] = red


def _train_step(x, target, params, moments_m, moments_v):
    bsz, seq, d = x.shape
    depth = params['norm_mix'].shape[0]
    entries = _layer_entries(depth)

    small_list = [(n, j) for n in SMALL_SHARDED for j in range(params[n].shape[0])]
    small_buf = _gather_shards(_pack([params[n][j] for n, j in small_list], ()), "gather_small")
    small_full = {}
    for (n, j), g in zip(small_list, _unpack(small_buf, [params[n][j].shape for n, j in small_list], (N_CHIPS,))):
        small_full[(n, j)] = _join(g, SMALL_SHARDED[n])

    plan = _Plan(params, entries, small_full)
    loss_local, dx, grads = _local_step(x.reshape(bsz * seq, d), target.reshape(bsz * seq, d), plan, depth, bsz, seq)
    loss = lax.psum(loss_local, ("x", "y", "c"))

    total = plan.total
    layer_of = {(n, j): i for i, ent in enumerate(entries) for n, j in ent}
    packed = _pack([_split(grads[layer_of[(n, j)]][n], SMALL_SHARDED[n]) for n, j in small_list], (N_CHIPS,))
    red = _reduce_scatter_layer([packed], ["R"], [PACK_COLS], "small")[0]
    for (n, j), g in zip(small_list, _unpack(red, [params[n][j].shape for n, j in small_list], ())):
        total[(n, j)] = g
    repl = [(n, j) for n in WEIGHTS if n not in BIG and n not in SMALL_SHARDED for j in range(params[n].shape[0])]
    flat = jnp.concatenate([grads[layer_of[k]][k[0]].reshape(-1) for k in repl])
    n_flat = flat.shape[0]
    flat = jnp.pad(flat, (0, (-n_flat) % (SUBLANES * LANES))).reshape(-1, LANES)
    flat = _all_to_all_sum(flat, "allreduce_small").reshape(-1)
    off = 0
    for k in repl:
        shp = params[k[0]][k[1]].shape
        n = 1
        for s in shp:
            n *= s
        total[k] = flat[off:off + n].reshape(shp)
        off += n

    grad_w, delta_w, new_m, new_v = {}, {}, {}, {}
    for n in WEIGHTS:
        nl = params[n].shape[0]
        if n in BIG:
            grad_w[n], delta_w[n], new_m[n], new_v[n] = _adamw_layers(params[n], [total[(n, j)] for j in range(nl)], moments_m[n], moments_v[n], f"adamw_{n}")
        else:
            g = jnp.stack([total[(n, j)] for j in range(nl)])
            grad_w[n] = g
            delta_w[n], new_m[n], new_v[n] = _adamw(params[n], g, moments_m[n], moments_v[n], f"adamw_{n}")
    return (loss, dx.reshape(bsz, seq, d), *[grad_w[n] for n in WEIGHTS], *[delta_w[n] for n in WEIGHTS],
            *[new_m[n] for n in WEIGHTS], *[new_v[n] for n in WEIGHTS])


def kernel(x, norm_mix, norm_ffn, ffn_w_gate, ffn_w_up, ffn_conv_w, ffn_conv_b, ffn_w_down, a_w_in, a_b_in, a_v_norm, a_w_s, a_b_s, a_w_out, b_w_in, b_b_f, b_q_norm, b_k_norm, b_w_out, c_w_in, c_conv_w, c_a_log, c_dt_bias, c_out_norm, c_w_out, loss_target, m_norm_mix, m_norm_ffn, m_ffn_w_gate, m_ffn_w_up, m_ffn_conv_w, m_ffn_conv_b, m_ffn_w_down, m_a_w_in, m_a_b_in, m_a_v_norm, m_a_w_s, m_a_b_s, m_a_w_out, m_b_w_in, m_b_b_f, m_b_q_norm, m_b_k_norm, m_b_w_out, m_c_w_in, m_c_conv_w, m_c_a_log, m_c_dt_bias, m_c_out_norm, m_c_w_out, v_norm_mix, v_norm_ffn, v_ffn_w_gate, v_ffn_w_up, v_ffn_conv_w, v_ffn_conv_b, v_ffn_w_down, v_a_w_in, v_a_b_in, v_a_v_norm, v_a_w_s, v_a_b_s, v_a_w_out, v_b_w_in, v_b_b_f, v_b_q_norm, v_b_k_norm, v_b_w_out, v_c_w_in, v_c_conv_w, v_c_a_log, v_c_dt_bias, v_c_out_norm, v_c_w_out):
    given = dict(locals())
    params = {n: given[n] for n in WEIGHTS}
    moments_m = {n: given["m_" + n] for n in WEIGHTS}
    moments_v = {n: given["v_" + n] for n in WEIGHTS}
    return _train_step(x, loss_target, params, moments_m, moments_v)
```

```python
import functools

import jax
import jax.numpy as jnp
from jax import lax
from jax.experimental import pallas as pl
from jax.experimental.pallas import tpu as pltpu

F32 = jnp.float32
BF16 = jnp.bfloat16
HI = lax.Precision.HIGHEST
MESH = pl.DeviceIdType.MESH

RMS_EPS = 1e-6
ADAM_LR, ADAM_B1, ADAM_B2, ADAM_EPS, ADAM_WD, ADAM_STEP = 0.001, 0.9, 0.999, 1e-08, 0.01, 10
A_CHUNK, HEAD, GDN_CHUNK = 128, 128, 64
LANES, SUBLANES = 128, 8
PACK_COLS = 1024
N_CHIPS = 4
VMEM_LIMIT = 56 * 1024 * 1024
ROWMAP_BUDGET = 20 * 1024 * 1024

NN = (((1,), (0,)), ((), ()))
NT = (((1,), (1,)), ((), ()))
TN = (((0,), (0,)), ((), ()))
BNN = (((2,), (1,)), ((0,), (0,)))
BNT = (((2,), (2,)), ((0,), (0,)))
BTN = (((1,), (1,)), ((0,), (0,)))


def _pick(n, cap, mult):
    if n <= cap:
        return n
    best = None
    for d in range(mult, cap + 1, mult):
        if n % d == 0:
            best = d
    if best is None:
        raise ValueError(f"no tile for {n} (cap {cap}, multiple of {mult})")
    return best


def _pad_cols(n):
    j = -(-n // LANES)
    while not (j <= 8 or any(j % d == 0 for d in (4, 5, 6, 7, 8))):
        j += 1
    return j * LANES


def _cparams(sem):
    return pltpu.CompilerParams(dimension_semantics=sem, vmem_limit_bytes=VMEM_LIMIT)


def _matmul(a, b, kind, out_dtype, name, job=None):
    if kind == "nn":
        (m, k), (k2, n) = a.shape, b.shape
    elif kind == "nt":
        (m, k), (n, k2) = a.shape, b.shape
    else:
        (k, m), (k2, n) = a.shape, b.shape
    assert k == k2, (name, a.shape, b.shape)
    tm, tn, tk = _pick(m, 1024, LANES), _pick(n, 1024, LANES), _pick(k, 2048, LANES)
    ni, nj, nk = m // tm, n // tn, k // tk
    dims = {"nn": NN, "nt": NT, "tn": TN}[kind]
    a_spec = pl.BlockSpec((tk, tm), lambda i, j, kk: (kk, i)) if kind == "tn" else pl.BlockSpec((tm, tk), lambda i, j, kk: (i, kk))
    b_spec = pl.BlockSpec((tn, tk), lambda i, j, kk: (j, kk)) if kind == "nt" else pl.BlockSpec((tk, tn), lambda i, j, kk: (kk, j))
    n_jin, n_jout = (len(job["ins"]), len(job["out_shapes"])) if job else (0, 0)

    def body(a_ref, b_ref, *rest):
        jins, o_ref, jouts = rest[:n_jin], rest[n_jin], rest[n_jin + 1:n_jin + 1 + n_jout]
        scratch = rest[n_jin + 1 + n_jout:]
        i, j, kk = pl.program_id(0), pl.program_id(1), pl.program_id(2)
        if job:
            first = jnp.logical_and(jnp.logical_and(i == 0, j == 0), kk == 0)
            job["start"](jins, jouts, scratch[-2], scratch[-1], first)
        prod = lax.dot_general(a_ref[...].astype(BF16), b_ref[...].astype(BF16), dims, preferred_element_type=F32)
        if nk == 1:
            o_ref[...] = prod.astype(o_ref.dtype)
        else:
            acc_ref = scratch[0]

            @pl.when(kk == 0)
            def _():
                acc_ref[...] = prod

            @pl.when(kk > 0)
            def _():
                acc_ref[...] += prod

            @pl.when(kk == nk - 1)
            def _():
                o_ref[...] = acc_ref[...].astype(o_ref.dtype)
        if job:
            last = jnp.logical_and(jnp.logical_and(i == ni - 1, j == nj - 1), kk == nk - 1)
            job["finish"](jins, jouts, scratch[-2], scratch[-1], last)

    scratch_shapes = [pltpu.VMEM((tm, tn), F32)] if nk > 1 else []
    out_specs = pl.BlockSpec((tm, tn), lambda i, j, kk: (i, j))
    out_shape = jax.ShapeDtypeStruct((m, n), out_dtype)
    if not job:
        return pl.pallas_call(
            body, name=name, grid=(ni, nj, nk), in_specs=[a_spec, b_spec], out_specs=out_specs, out_shape=out_shape,
            scratch_shapes=scratch_shapes, compiler_params=_cparams(("parallel", "parallel", "arbitrary")),
        )(a, b)
    scratch_shapes += [pltpu.SemaphoreType.DMA((job["n_sems"],)), pltpu.SemaphoreType.DMA((job["n_sems"],))]
    res = pl.pallas_call(
        body, name=name, grid=(ni, nj, nk), in_specs=[a_spec, b_spec] + [ANY] * n_jin,
        out_specs=[out_specs] + [ANY] * n_jout, out_shape=[out_shape] + list(job["out_shapes"]),
        input_output_aliases={2 + t: 1 + t for t in range(n_jin)} if job["alias"] else {},
        scratch_shapes=scratch_shapes, compiler_params=_cparams(("arbitrary", "arbitrary", "arbitrary")),
    )(a, b, *job["ins"])
    return res[0], list(res[1:])


def _win(arr, width=None, blk=0):
    return (arr, arr.shape[1] if width is None else width, blk)


def _rowmap(fn, rows, params, name, n_acc=0, tc=None, col_params=()):
    rows = [r if isinstance(r, tuple) else _win(r) for r in rows]
    t = rows[0][0].shape[0]
    widths = [tc if tc is not None else w for (_, w, _) in rows]

    def blocks_for(tr):
        rb = [jax.ShapeDtypeStruct((tr, w), a.dtype) for (a, _, _), w in zip(rows, widths)]
        pb = [jax.ShapeDtypeStruct((p.shape[0], tc) if (i in col_params) else p.shape, p.dtype) for i, p in enumerate(params)]
        return rb, pb

    rb, pb = blocks_for(SUBLANES * 2)
    outs = jax.eval_shape(fn, *rb, *pb)
    outs = list(outs) if isinstance(outs, (tuple, list)) else [outs]
    n_row = len(outs) - n_acc
    row_bytes = sum(w * a.dtype.itemsize for (a, _, _), w in zip(rows, widths)) + sum(o.shape[1] * o.dtype.itemsize for o in outs[:n_row])
    tr = 16
    while tr * 2 <= 512 and t % (tr * 2) == 0 and (tr * 2) * row_bytes * 5 <= ROWMAP_BUDGET:
        tr *= 2
    rb, pb = blocks_for(tr)
    outs = jax.eval_shape(fn, *rb, *pb)
    outs = list(outs) if isinstance(outs, (tuple, list)) else [outs]
    n_in = len(rows) + len(params)

    if tc is None:
        grid = (t // tr,)
        row_axis = 0
        in_specs = [pl.BlockSpec((tr, w), functools.partial(lambda i, b: (i, b), b=blk)) for (_, w, blk) in rows]
        in_specs += [pl.BlockSpec(p.shape, functools.partial(lambda i, nd: (0,) * nd, nd=p.ndim)) for p in params]
        out_specs = [pl.BlockSpec((tr, o.shape[1]), lambda i: (i, 0)) for o in outs[:n_row]]
        out_specs += [pl.BlockSpec(o.shape, functools.partial(lambda i, nd: (0,) * nd, nd=len(o.shape))) for o in outs[n_row:]]
        out_shape = [jax.ShapeDtypeStruct((t, o.shape[1]), o.dtype) for o in outs[:n_row]]
        out_shape += [jax.ShapeDtypeStruct(o.shape, o.dtype) for o in outs[n_row:]]
        sem = ("arbitrary",) if n_acc else ("parallel",)
    else:
        wtot = rows[0][1]
        grid = (wtot // tc, t // tr)
        row_axis = 1
        in_specs = [pl.BlockSpec((tr, tc), functools.partial(lambda j, i, b: (i, j + b), b=blk)) for (_, _, blk) in rows]
        for i, p in enumerate(params):
            if i in col_params:
                in_specs.append(pl.BlockSpec((p.shape[0], tc), lambda j, i: (0, j)))
            else:
                in_specs.append(pl.BlockSpec(p.shape, functools.partial(lambda j, i, nd: (0,) * nd, nd=p.ndim)))
        out_specs = [pl.BlockSpec((tr, tc), lambda j, i: (i, j)) for _ in outs[:n_row]]
        out_specs += [pl.BlockSpec((o.shape[0], tc), lambda j, i: (0, j)) for o in outs[n_row:]]
        out_shape = [jax.ShapeDtypeStruct((t, wtot), o.dtype) for o in outs[:n_row]]
        out_shape += [jax.ShapeDtypeStruct((o.shape[0], wtot), o.dtype) for o in outs[n_row:]]
        sem = ("parallel", "arbitrary") if n_acc else ("parallel", "parallel")

    def body(*refs):
        ins, ors = refs[:n_in], refs[n_in:]
        res = fn(*[r[...] for r in ins])
        res = list(res) if isinstance(res, (tuple, list)) else [res]
        for o, r in zip(ors[:n_row], res[:n_row]):
            o[...] = r.astype(o.dtype)
        if n_acc:
            i = pl.program_id(row_axis)
            for o, r in zip(ors[n_row:], res[n_row:]):
                @pl.when(i == 0)
                def _(o=o, r=r):
                    o[...] = r.astype(o.dtype)

                @pl.when(i > 0)
                def _(o=o, r=r):
                    o[...] += r.astype(o.dtype)

    res = pl.pallas_call(
        body, name=name, grid=grid, in_specs=in_specs, out_specs=out_specs, out_shape=out_shape,
        compiler_params=_cparams(sem),
    )(*[a for (a, _, _) in rows], *params)
    return res


def _elementwise(fn, arrays, n_out, name):
    shape = arrays[0].shape
    cols = shape[-1]
    rws = 1
    for s in shape[:-1]:
        rws *= s
    arrs = [a.reshape(rws, cols) for a in arrays]
    per_row = cols * 4 * (len(arrays) + n_out) * 3
    tr = rws
    if rws * per_row > ROWMAP_BUDGET:
        tr = _pick(rws, max(SUBLANES, ROWMAP_BUDGET // per_row), SUBLANES)

    def body(*refs):
        res = fn(*[r[...] for r in refs[:len(arrs)]])
        for o, r in zip(refs[len(arrs):], res):
            o[...] = r

    spec = pl.BlockSpec((tr, cols), lambda i: (i, 0))
    outs = pl.pallas_call(
        body, name=name, grid=(rws // tr,), in_specs=[spec] * len(arrs), out_specs=[spec] * n_out,
        out_shape=[jax.ShapeDtypeStruct((rws, cols), F32)] * n_out, compiler_params=_cparams(("parallel",)),
    )(*arrs)
    return [o.reshape(shape) for o in outs]


def _rms(x, g):
    return x * lax.rsqrt(jnp.mean(x * x, axis=-1, keepdims=True) + RMS_EPS) * g


def _silu(x):
    return x * jax.nn.sigmoid(x)


def _softplus(x):
    return jnp.maximum(x, 0.0) + jnp.log(1.0 + jnp.exp(-jnp.abs(x)))


def _gelu_tanh(x):
    return 0.5 * x * (1.0 + jnp.tanh(0.7978845608028654 * (x + 0.044715 * (x * x * x))))


def _heads(x):
    return [x[:, h * HEAD:(h + 1) * HEAD] for h in range(x.shape[1] // HEAD)]


def _dot(a, b, dims=NN):
    return lax.dot_general(a, b, dims, precision=HI, preferred_element_type=F32)


def _bdot(a, b, dims):
    return lax.dot_general(a.astype(BF16), b.astype(BF16), dims, preferred_element_type=F32)


def _eye(n):
    return (lax.broadcasted_iota(jnp.int32, (n, n), 0) == lax.broadcasted_iota(jnp.int32, (n, n), 1)).astype(F32)


def _tri(n):
    return lax.broadcasted_iota(jnp.int32, (n, n), 0) >= lax.broadcasted_iota(jnp.int32, (n, n), 1)


def _row_to_col(row):
    return jnp.sum(_eye(row.shape[1]) * row, axis=1, keepdims=True)


def _conv_tiles(w, seq):
    return _pick(seq, 512, SUBLANES), _pick(w, 512, LANES)


def _conv_fwd(x, width, w, bias, seq, name):
    t = x.shape[0]
    kk = w.shape[0]
    tr, tc = _conv_tiles(width, seq)
    hb = tr // SUBLANES

    def body(*refs):
        if bias is None:
            x_ref, h_ref, w_ref, o_ref = refs
        else:
            x_ref, h_ref, w_ref, b_ref, o_ref = refs
        i = pl.program_id(1)
        first = (i * tr) % seq == 0
        halo = jnp.where(first, 0.0, h_ref[...])
        xe = jnp.concatenate([halo, x_ref[...]], axis=0)
        wv = w_ref[...]
        acc = xe[SUBLANES:, :] * wv[kk - 1:kk, :]
        for s in range(1, kk):
            acc = acc + pltpu.roll(xe, s, 0)[SUBLANES:, :] * wv[kk - 1 - s:kk - s, :]
        if bias is not None:
            acc = acc + b_ref[...]
        o_ref[...] = acc

    in_specs = [pl.BlockSpec((tr, tc), lambda j, i: (i, j)),
                pl.BlockSpec((SUBLANES, tc), lambda j, i: (jnp.maximum(i * hb - 1, 0), j)),
                pl.BlockSpec((kk, tc), lambda j, i: (0, j))]
    ops = [x, x, w]
    if bias is not None:
        in_specs.append(pl.BlockSpec((1, tc), lambda j, i: (0, j)))
        ops.append(bias)
    return pl.pallas_call(
        body, name=name, grid=(width // tc, t // tr), in_specs=in_specs,
        out_specs=pl.BlockSpec((tr, tc), lambda j, i: (i, j)), out_shape=jax.ShapeDtypeStruct((t, width), F32),
        compiler_params=_cparams(("parallel", "parallel")),
    )(*ops)


def _conv_bwd(dy, x, width, w, seq, with_bias, name, xcol=0, dx_dtype=F32):
    t = x.shape[0]
    kk = w.shape[0]
    tr, tc = _conv_tiles(width, seq)
    hb = tr // SUBLANES
    n_halo_blocks = t // SUBLANES
    assert xcol % tc == 0
    xb = xcol // tc

    def body(dy_ref, dyn_ref, x_ref, xh_ref, w_ref, dx_ref, dw_ref, *rest):
        i = pl.program_id(1)
        first = (i * tr) % seq == 0
        last = ((i + 1) * tr) % seq == 0
        dyc = dy_ref[...]
        dye = jnp.concatenate([dyc, jnp.where(last, 0.0, dyn_ref[...])], axis=0)
        xe = jnp.concatenate([jnp.where(first, 0.0, xh_ref[...]), x_ref[...]], axis=0)
        wv = w_ref[...]
        dx = dyc * wv[kk - 1:kk, :]
        dws = [None] * kk
        dws[kk - 1] = jnp.sum(dyc * xe[SUBLANES:, :], axis=0, keepdims=True)
        for s in range(1, kk):
            dx = dx + pltpu.roll(dye, tr + SUBLANES - s, 0)[:tr, :] * wv[kk - 1 - s:kk - s, :]
            dws[kk - 1 - s] = jnp.sum(dyc * pltpu.roll(xe, s, 0)[SUBLANES:, :], axis=0, keepdims=True)
        dx_ref[...] = dx.astype(dx_ref.dtype)

        @pl.when(i == 0)
        def _():
            for j in range(kk):
                dw_ref[j:j + 1, :] = dws[j]
            if with_bias:
                rest[0][...] = jnp.sum(dyc, axis=0, keepdims=True)

        @pl.when(i > 0)
        def _():
            for j in range(kk):
                dw_ref[j:j + 1, :] += dws[j]
            if with_bias:
                rest[0][...] += jnp.sum(dyc, axis=0, keepdims=True)

    cur = pl.BlockSpec((tr, tc), lambda j, i: (i, j))
    in_specs = [cur, pl.BlockSpec((SUBLANES, tc), lambda j, i: (jnp.minimum((i + 1) * hb, n_halo_blocks - 1), j)),
                pl.BlockSpec((tr, tc), lambda j, i: (i, j + xb)),
                pl.BlockSpec((SUBLANES, tc), lambda j, i: (jnp.maximum(i * hb - 1, 0), j + xb)),
                pl.BlockSpec((kk, tc), lambda j, i: (0, j))]
    out_specs = [cur, pl.BlockSpec((kk, tc), lambda j, i: (0, j))]
    out_shape = [jax.ShapeDtypeStruct((t, width), dx_dtype), jax.ShapeDtypeStruct((kk, width), F32)]
    if with_bias:
        out_specs.append(pl.BlockSpec((1, tc), lambda j, i: (0, j)))
        out_shape.append(jax.ShapeDtypeStruct((1, width), F32))
    return pl.pallas_call(
        body, name=name, grid=(width // tc, t // tr), in_specs=in_specs, out_specs=out_specs, out_shape=out_shape,
        compiler_params=_cparams(("parallel", "arbitrary")),
    )(dy, dy, x, x, w)


def _cumsum_rows(x, seq, reverse, name):
    t, w = x.shape
    tb = _pick(seq, 256, SUBLANES)
    nb = seq // tb

    def pos(b, i):
        return (b * nb + (nb - 1 - i if reverse else i), 0)

    def body(x_ref, o_ref, carry):
        i = pl.program_id(1)

        @pl.when(i == 0)
        def _():
            carry[...] = jnp.zeros_like(carry)

        blk = x_ref[...]
        r = lax.broadcasted_iota(jnp.int32, (tb, tb), 0)
        c = lax.broadcasted_iota(jnp.int32, (tb, tb), 1)
        m = ((r <= c) if reverse else (r >= c)).astype(F32)
        o_ref[...] = _dot(m, blk) + carry[...]
        carry[...] += jnp.sum(blk, axis=0, keepdims=True)

    return pl.pallas_call(
        body, name=name, grid=(t // seq, nb), in_specs=[pl.BlockSpec((tb, w), pos)], out_specs=pl.BlockSpec((tb, w), pos),
        out_shape=jax.ShapeDtypeStruct((t, w), F32), scratch_shapes=[pltpu.VMEM((1, w), F32)],
        compiler_params=_cparams(("parallel", "arbitrary")),
    )(x)


def _sgu_fwd(vn, u, w_s, b_col, name):
    t, a = vn.shape
    g = a // HEAD

    def body(v_ref, u_ref, w_ref, b_ref, y_ref):
        tri = _tri(A_CHUNK)
        for gi in range(g):
            sl = slice(gi * HEAD, (gi + 1) * HEAD)
            wc = jnp.where(tri, w_ref[gi], 0.0)
            sv = _bdot(wc, v_ref[:, sl], NN) + b_ref[gi]
            y_ref[:, sl] = (u_ref[:, sl] * sv).astype(y_ref.dtype)

    blk = pl.BlockSpec((A_CHUNK, a), lambda i: (i, 0))
    return pl.pallas_call(
        body, name=name, grid=(t // A_CHUNK,),
        in_specs=[blk, blk, pl.BlockSpec(w_s.shape, lambda i: (0, 0, 0)), pl.BlockSpec(b_col.shape, lambda i: (0, 0, 0))],
        out_specs=blk, out_shape=jax.ShapeDtypeStruct((t, a), BF16), compiler_params=_cparams(("parallel",)),
    )(vn, u, w_s, b_col)


def _sgu_bwd(vn, u, dy, w_s, b_col, name):
    t, a = vn.shape
    g = a // HEAD

    def body(v_ref, u_ref, dy_ref, w_ref, b_ref, dv_ref, du_ref, dw_ref, db_ref):
        i = pl.program_id(0)
        tri = _tri(A_CHUNK)
        for gi in range(g):
            sl = slice(gi * HEAD, (gi + 1) * HEAD)
            wc = jnp.where(tri, w_ref[gi], 0.0)
            v = v_ref[:, sl]
            sv = _bdot(wc, v, NN) + b_ref[gi]
            dyb = dy_ref[:, sl]
            du_ref[:, sl] = dyb * sv
            dsv = dyb * u_ref[:, sl]
            dv_ref[:, sl] = _bdot(wc, dsv, TN)
            dw = jnp.where(tri, _bdot(dsv, v, NT), 0.0)
            db = jnp.sum(dsv, axis=1, keepdims=True)

            @pl.when(i == 0)
            def _(gi=gi, dw=dw, db=db):
                dw_ref[gi] = dw
                db_ref[gi] = db

            @pl.when(i > 0)
            def _(gi=gi, dw=dw, db=db):
                dw_ref[gi] += dw
                db_ref[gi] += db

    blk = pl.BlockSpec((A_CHUNK, a), lambda i: (i, 0))
    wsp = pl.BlockSpec(w_s.shape, lambda i: (0, 0, 0))
    bsp = pl.BlockSpec(b_col.shape, lambda i: (0, 0, 0))
    return pl.pallas_call(
        body, name=name, grid=(t // A_CHUNK,), in_specs=[blk, blk, blk, wsp, bsp], out_specs=[blk, blk, wsp, bsp],
        out_shape=[jax.ShapeDtypeStruct((t, a), F32), jax.ShapeDtypeStruct((t, a), F32),
                   jax.ShapeDtypeStruct(w_s.shape, F32), jax.ShapeDtypeStruct(b_col.shape, F32)],
        compiler_params=_cparams(("arbitrary",)),
    )(vn, u, dy, w_s, b_col)


def _fox_scores(q, k, cq_row, ck_row, diag, scale):
    s = lax.dot_general(q.astype(BF16), k.astype(BF16), NT, preferred_element_type=F32) * scale
    s = s + _row_to_col(cq_row) - ck_row
    mask = jnp.logical_or(jnp.logical_not(diag), _tri(q.shape[0]))
    return s, mask


def _fox_fwd(qn, kn, proj, v_blk0, c_rows, bsz, seq, nh, name):
    t = qn.shape[0]
    tq = _pick(seq, 512, LANES)
    nq = seq // tq
    scale = HEAD ** -0.5

    def body(q_ref, k_ref, v_ref, cq_ref, ck_ref, o_ref, lse_ref, m_s, l_s, acc_s):
        i, j = pl.program_id(2), pl.program_id(3)

        @pl.when(j == 0)
        def _():
            m_s[...] = jnp.full_like(m_s, -jnp.inf)
            l_s[...] = jnp.zeros_like(l_s)
            acc_s[...] = jnp.zeros_like(acc_s)

        @pl.when(j <= i)
        def _():
            s, mask = _fox_scores(q_ref[...], k_ref[...], cq_ref[...], ck_ref[...], j == i, scale)
            s = jnp.where(mask, s, -jnp.inf)
            m_new = jnp.maximum(m_s[...], jnp.max(s, axis=1, keepdims=True))
            p = jnp.exp(s - m_new)
            alpha = jnp.exp(m_s[...] - m_new)
            l_s[...] = alpha * l_s[...] + jnp.sum(p, axis=1, keepdims=True)
            acc_s[...] = alpha * acc_s[...] + lax.dot_general(p.astype(BF16), v_ref[...].astype(BF16), NN, preferred_element_type=F32)
            m_s[...] = m_new

        @pl.when(j == nq - 1)
        def _():
            o_ref[...] = acc_s[...] / l_s[...]
            lse_ref[...] = jnp.broadcast_to(m_s[...] + jnp.log(l_s[...]), lse_ref.shape)

    qspec = pl.BlockSpec((tq, HEAD), lambda b, h, i, j: (b * nq + i, h))
    kspec = pl.BlockSpec((tq, HEAD), lambda b, h, i, j: (b * nq + jnp.minimum(i, j), h))
    vspec = pl.BlockSpec((tq, HEAD), lambda b, h, i, j: (b * nq + jnp.minimum(i, j), v_blk0 + h))
    cq = pl.BlockSpec((None, None, 1, tq), lambda b, h, i, j: (b, h, 0, i))
    ck = pl.BlockSpec((None, None, 1, tq), lambda b, h, i, j: (b, h, 0, jnp.minimum(i, j)))
    return pl.pallas_call(
        body, name=name, grid=(bsz, nh, nq, nq), in_specs=[qspec, kspec, vspec, cq, ck], out_specs=[qspec, qspec],
        out_shape=[jax.ShapeDtypeStruct((t, nh * HEAD), F32)] * 2,
        scratch_shapes=[pltpu.VMEM((tq, 1), F32), pltpu.VMEM((tq, 1), F32), pltpu.VMEM((tq, HEAD), F32)],
        compiler_params=_cparams(("parallel", "parallel", "parallel", "arbitrary")),
    )(qn, kn, proj, c_rows, c_rows)


def _fox_p_dp(q, k, v, do, lse, cq_row, ck_row, diag, scale):
    s, mask = _fox_scores(q, k, cq_row, ck_row, diag, scale)
    p = jnp.where(mask, jnp.exp(s - jnp.max(lse, axis=1, keepdims=True)), 0.0)
    dp = lax.dot_general(do.astype(BF16), v.astype(BF16), NT, preferred_element_type=F32)
    return p, dp


def _fox_bwd_q(qn, kn, proj, v_blk0, do, lse, c_rows, bsz, seq, nh, name):
    t = qn.shape[0]
    tq = _pick(seq, 512, LANES)
    nq = seq // tq
    scale = HEAD ** -0.5

    def key_block(jj):
        return jnp.where(jj >= nq, jj - nq, jj)

    def body(q_ref, k_ref, v_ref, do_ref, lse_ref, cq_ref, ck_ref, dq_ref, dl_ref, dq_s, dl_s):
        i, jj = pl.program_id(2), pl.program_id(3)
        j = key_block(jj)

        @pl.when(jj == 0)
        def _():
            dq_s[...] = jnp.zeros_like(dq_s)
            dl_s[...] = jnp.zeros_like(dl_s)

        @pl.when(j <= i)
        def _():
            p, dp = _fox_p_dp(q_ref[...], k_ref[...], v_ref[...], do_ref[...], lse_ref[...], cq_ref[...], ck_ref[...], j == i, scale)

            @pl.when(jj < nq)
            def _():
                dl_s[...] += jnp.sum(p * dp, axis=1, keepdims=True)

            @pl.when(jj >= nq)
            def _():
                ds = p * (dp - dl_s[...])
                dq_s[...] += lax.dot_general(ds.astype(BF16), k_ref[...].astype(BF16), NN, preferred_element_type=F32) * scale

        @pl.when(jj == 2 * nq - 1)
        def _():
            dq_ref[...] = dq_s[...]
            dl_ref[...] = jnp.broadcast_to(dl_s[...], dl_ref.shape)

    qspec = pl.BlockSpec((tq, HEAD), lambda b, h, i, jj: (b * nq + i, h))
    kspec = pl.BlockSpec((tq, HEAD), lambda b, h, i, jj: (b * nq + jnp.minimum(i, key_block(jj)), h))
    vspec = pl.BlockSpec((tq, HEAD), lambda b, h, i, jj: (b * nq + jnp.minimum(i, key_block(jj)), v_blk0 + h))
    cq = pl.BlockSpec((None, None, 1, tq), lambda b, h, i, jj: (b, h, 0, i))
    ck = pl.BlockSpec((None, None, 1, tq), lambda b, h, i, jj: (b, h, 0, jnp.minimum(i, key_block(jj))))
    return pl.pallas_call(
        body, name=name, grid=(bsz, nh, nq, 2 * nq), in_specs=[qspec, kspec, vspec, qspec, qspec, cq, ck],
        out_specs=[qspec, qspec], out_shape=[jax.ShapeDtypeStruct((t, nh * HEAD), F32)] * 2,
        scratch_shapes=[pltpu.VMEM((tq, HEAD), F32), pltpu.VMEM((tq, 1), F32)],
        compiler_params=_cparams(("parallel", "parallel", "parallel", "arbitrary")),
    )(qn, kn, proj, do, lse, c_rows, c_rows)


def _fox_bwd_kv(qn, kn, proj, v_blk0, do, lse, delta, c_rows, bsz, seq, nh, name):
    t = qn.shape[0]
    tq = _pick(seq, 512, LANES)
    nq = seq // tq
    scale = HEAD ** -0.5

    def body(q_ref, k_ref, v_ref, do_ref, lse_ref, dl_ref, cq_ref, ck_ref, dk_ref, dv_ref, dc_ref, dk_s, dv_s, dc_s):
        j, i = pl.program_id(2), pl.program_id(3)

        @pl.when(i == 0)
        def _():
            dk_s[...] = jnp.zeros_like(dk_s)
            dv_s[...] = jnp.zeros_like(dv_s)
            dc_s[...] = jnp.zeros_like(dc_s)

        @pl.when(i >= j)
        def _():
            p, dp = _fox_p_dp(q_ref[...], k_ref[...], v_ref[...], do_ref[...], lse_ref[...], cq_ref[...], ck_ref[...], j == i, scale)
            ds = p * (dp - jnp.max(dl_ref[...], axis=1, keepdims=True))
            dv_s[...] += lax.dot_general(p.astype(BF16), do_ref[...].astype(BF16), TN, preferred_element_type=F32)
            dk_s[...] += lax.dot_general(ds.astype(BF16), q_ref[...].astype(BF16), TN, preferred_element_type=F32) * scale
            dc_s[...] -= jnp.sum(ds, axis=0, keepdims=True)

        @pl.when(i == nq - 1)
        def _():
            dk_ref[...] = dk_s[...]
            dv_ref[...] = dv_s[...]
            dc_ref[...] = dc_s[...]

    kspec = pl.BlockSpec((tq, HEAD), lambda b, h, j, i: (b * nq + j, h))
    vspec = pl.BlockSpec((tq, HEAD), lambda b, h, j, i: (b * nq + j, v_blk0 + h))
    qspec = pl.BlockSpec((tq, HEAD), lambda b, h, j, i: (b * nq + jnp.maximum(i, j), h))
    cq = pl.BlockSpec((None, None, 1, tq), lambda b, h, j, i: (b, h, 0, jnp.maximum(i, j)))
    ck = pl.BlockSpec((None, None, 1, tq), lambda b, h, j, i: (b, h, 0, j))
    return pl.pallas_call(
        body, name=name, grid=(bsz, nh, nq, nq), in_specs=[qspec, kspec, vspec, qspec, qspec, qspec, cq, ck],
        out_specs=[kspec, kspec, ck],
        out_shape=[jax.ShapeDtypeStruct((t, nh * HEAD), F32)] * 2 + [jax.ShapeDtypeStruct(c_rows.shape, F32)],
        scratch_shapes=[pltpu.VMEM((tq, HEAD), F32), pltpu.VMEM((tq, HEAD), F32), pltpu.VMEM((1, tq), F32)],
        compiler_params=_cparams(("parallel", "parallel", "parallel", "arbitrary")),
    )(qn, kn, proj, do, lse, delta, c_rows, c_rows)


@jax.custom_vjp
def _unit_lower_inverse(a_mat):
    c = a_mat.shape[-1]
    inv = _eye(c) - a_mat
    pw = _bdot(a_mat, a_mat, BNN)
    n_sq = max(1, (c - 1).bit_length() - 1)
    for it in range(n_sq):
        inv = inv + _bdot(inv, pw, BNN)
        if it < n_sq - 1:
            pw = _bdot(pw, pw, BNN)
    return inv


def _unit_lower_inverse_fwd(a_mat):
    inv = _unit_lower_inverse(a_mat)
    return inv, inv


def _unit_lower_inverse_bwd(inv, d_inv):
    return (-_bdot(_bdot(inv, d_inv, BTN), inv, BNT),)


_unit_lower_inverse.defvjp(_unit_lower_inverse_fwd, _unit_lower_inverse_bwd)


def _gdn_chunk(qp, kp, vp, b_row, a_row, alog, dtb, state):
    hv = vp.shape[0]
    c = qp.shape[1]
    qc, kc, vc = _silu(qp), _silu(kp), _silu(vp)
    qh = qc * lax.rsqrt(jnp.sum(qc * qc, -1, keepdims=True) + RMS_EPS) * (HEAD ** -0.5)
    kh = kc * lax.rsqrt(jnp.sum(kc * kc, -1, keepdims=True) + RMS_EPS)
    q = jnp.stack([qh[h // 2] for h in range(hv)])
    k = jnp.stack([kh[h // 2] for h in range(hv)])
    beta_row = jax.nn.sigmoid(b_row)
    g_row = -jnp.exp(alog) * _softplus(a_row + dtb)
    ri = lax.broadcasted_iota(jnp.int32, (c, c), 0)
    ci = lax.broadcasted_iota(jnp.int32, (c, c), 1)
    eye = (ri == ci).astype(F32)
    tri = ri >= ci
    beta_col = jnp.sum(eye * beta_row, axis=2, keepdims=True)
    g_col = jnp.sum(eye * g_row, axis=2, keepdims=True)
    gc_col = jnp.sum(tri.astype(F32) * g_row, axis=2, keepdims=True)
    gc_row = jnp.sum(g_col * (ri <= ci).astype(F32), axis=1, keepdims=True)
    decay = jnp.where(tri, jnp.exp(jnp.where(tri, gc_col - gc_row, 0.0)), 0.0)
    kb = k * beta_col
    a_mat = jnp.where(ri > ci, _bdot(kb, k, BNT) * decay, 0.0)
    egc = jnp.exp(gc_col)
    inv = _unit_lower_inverse(a_mat)
    u = _bdot(inv, vc * beta_col, BNN)
    w = _bdot(inv, kb * egc, BNN)
    attn = _bdot(q, k, BNT) * decay
    g_last = jnp.sum(g_row, axis=2, keepdims=True)
    v_new = u - _bdot(w, state, BNN)
    o = _bdot(q * egc, state, BNN) + _bdot(attn, v_new, BNN)
    new_state = state * jnp.exp(g_last) + _bdot(k * jnp.exp(g_last - gc_col), v_new, BTN)
    return o, new_state


def _gdn_group(nvh):
    return 4 if nvh % 4 == 0 else 2


def _gdn_specs(nch, nkh, hb, rev):
    def n_of(n):
        return nch - 1 - n if rev else n

    hk = hb // 2
    per_k = pl.BlockSpec((GDN_CHUNK, hk * HEAD), lambda g, b, n: (b * nch + n_of(n), g))
    q = per_k
    k = pl.BlockSpec((GDN_CHUNK, hk * HEAD), lambda g, b, n: (b * nch + n_of(n), nkh // hk + g))
    v = pl.BlockSpec((GDN_CHUNK, hb * HEAD), lambda g, b, n: (b * nch + n_of(n), 2 * nkh // hb + g))
    per_v = pl.BlockSpec((GDN_CHUNK, hb * HEAD), lambda g, b, n: (b * nch + n_of(n), g))
    row = pl.BlockSpec((hb, None, None, 1, GDN_CHUNK), lambda g, b, n: (g, b, n_of(n), 0, 0))
    sc = pl.BlockSpec((hb, 1, 1), lambda g, b, n: (g, 0, 0))
    st = pl.BlockSpec((hb, None, None, HEAD, HEAD), lambda g, b, n: (g, b, n_of(n), 0, 0))
    return q, k, v, per_k, per_v, row, sc, st


def _stack_heads(ref, n):
    return jnp.stack([ref[:, h * HEAD:(h + 1) * HEAD] for h in range(n)])


def _gdn_fwd(conv, b_rows, a_rows, alog, dtb, bsz, seq, nvh, name):
    t = conv.shape[0]
    nch = seq // GDN_CHUNK
    nkh = nvh // 2
    hb = _gdn_group(nvh)
    q, k, v, _, per_v, row, sc, st = _gdn_specs(nch, nkh, hb, False)

    def body(q_ref, k_ref, v_ref, b_ref, a_ref, al_ref, dt_ref, o_ref, st_ref, state):
        @pl.when(pl.program_id(2) == 0)
        def _():
            state[...] = jnp.zeros_like(state)

        st_ref[...] = state[...]
        o, new_state = _gdn_chunk(_stack_heads(q_ref, hb // 2), _stack_heads(k_ref, hb // 2), _stack_heads(v_ref, hb),
                                  b_ref[...], a_ref[...], al_ref[...], dt_ref[...], state[...])
        for h in range(hb):
            o_ref[:, h * HEAD:(h + 1) * HEAD] = o[h]
        state[...] = new_state

    return pl.pallas_call(
        body, name=name, grid=(nvh // hb, bsz, nch), in_specs=[q, k, v, row, row, sc, sc], out_specs=[per_v, st],
        out_shape=[jax.ShapeDtypeStruct((t, nvh * HEAD), F32), jax.ShapeDtypeStruct((nvh, bsz, nch, HEAD, HEAD), F32)],
        scratch_shapes=[pltpu.VMEM((hb, HEAD, HEAD), F32)],
        compiler_params=_cparams(("parallel", "parallel", "arbitrary")),
    )(conv, conv, conv, b_rows, a_rows, alog, dtb)


def _gdn_bwd(conv, b_rows, a_rows, alog, dtb, states, do, bsz, seq, nvh, name):
    t = conv.shape[0]
    nch = seq // GDN_CHUNK
    nkh = nvh // 2
    hb = _gdn_group(nvh)
    q, k, v, per_k, per_v, row, sc, st = _gdn_specs(nch, nkh, hb, True)

    def body(q_ref, k_ref, v_ref, b_ref, a_ref, al_ref, dt_ref, st_ref, do_ref,
             dq_ref, dk_ref, dv_ref, db_ref, da_ref, dal_ref, ddt_ref, dstate):
        b, n = pl.program_id(1), pl.program_id(2)

        @pl.when(n == 0)
        def _():
            dstate[...] = jnp.zeros_like(dstate)

        _, vjp = jax.vjp(_gdn_chunk, _stack_heads(q_ref, hb // 2), _stack_heads(k_ref, hb // 2), _stack_heads(v_ref, hb),
                         b_ref[...], a_ref[...], al_ref[...], dt_ref[...], st_ref[...])
        dq, dk, dv, db, da, dal, ddt, dst = vjp((_stack_heads(do_ref, hb), dstate[...]))
        for h in range(hb // 2):
            dq_ref[:, h * HEAD:(h + 1) * HEAD] = dq[h]
            dk_ref[:, h * HEAD:(h + 1) * HEAD] = dk[h]
        for h in range(hb):
            dv_ref[:, h * HEAD:(h + 1) * HEAD] = dv[h]
        db_ref[...] = db
        da_ref[...] = da
        dstate[...] = dst
        start = jnp.logical_and(b == 0, n == 0)

        @pl.when(start)
        def _():
            dal_ref[...] = dal
            ddt_ref[...] = ddt

        @pl.when(jnp.logical_not(start))
        def _():
            dal_ref[...] += dal
            ddt_ref[...] += ddt

    f = lambda *s: jax.ShapeDtypeStruct(s, F32)
    return pl.pallas_call(
        body, name=name, grid=(nvh // hb, bsz, nch), in_specs=[q, k, v, row, row, sc, sc, st, per_v],
        out_specs=[per_k, per_k, per_v, row, row, sc, sc],
        out_shape=[f(t, nkh * HEAD), f(t, nkh * HEAD), f(t, nvh * HEAD), f(*b_rows.shape), f(*a_rows.shape), f(nvh, 1, 1), f(nvh, 1, 1)],
        scratch_shapes=[pltpu.VMEM((hb, HEAD, HEAD), F32)],
        compiler_params=_cparams(("arbitrary", "arbitrary", "arbitrary")),
    )(conv, conv, conv, b_rows, a_rows, alog, dtb, states, do)


ANY = pl.BlockSpec(memory_space=pl.ANY)
FLIPS = (2, 1, 3)


def _place():
    x, y, c = lax.axis_index("x"), lax.axis_index("y"), lax.axis_index("c")
    chips = [(1 - x, y), (x, 1 - y), (1 - x, 1 - y)]
    return x, y, c, chips


def _chip_index():
    return 2 * lax.axis_index("x") + lax.axis_index("y")


def _core_index():
    return lax.axis_index("c")


def _wide(w):
    return w if (w <= 4096 or w % LANES) else _pick(w, 2048, LANES)


def _rows(start, size, mult=16):
    return pl.ds(pl.multiple_of(start, mult), size)


def _cast_window(w_stack, layer, cls, name):
    _, r, w = w_stack.shape
    if cls == "U":
        tr = _pick(r, 256, 16)
        grid = (r // tr, 1)
        in_spec = pl.BlockSpec((None, tr, w), lambda i, j: (layer, i, 0))
        out_spec = pl.BlockSpec((None, tr, w), lambda i, j: (_chip_index(), i, 0))
        out_shape = (N_CHIPS, r, w)
    else:
        tr, tc = _pick(r, 512, 16), _wide(w)
        nbr, nbc = r // tr, w // tc
        grid = (nbr, nbc)
        in_spec = pl.BlockSpec((None, tr, tc), lambda i, j: (layer, i, j))
        if cls == "C":
            out_spec = pl.BlockSpec((tr, tc), lambda i, j: (i, _chip_index() * nbc + j))
            out_shape = (r, N_CHIPS * w)
        else:
            out_spec = pl.BlockSpec((tr, tc), lambda i, j: (_chip_index() * nbr + i, j))
            out_shape = (N_CHIPS * r, w)

    def body(x_ref, o_ref):
        o_ref[...] = x_ref[...].astype(o_ref.dtype)

    return pl.pallas_call(body, name=name, grid=grid, in_specs=[in_spec], out_specs=out_spec,
                          out_shape=jax.ShapeDtypeStruct(out_shape, BF16), compiler_params=_cparams(("parallel", "parallel")))(w_stack)


def _halved(buf, cls):
    return {"C": buf.shape[0], "U": buf.shape[1], "R": buf.shape[0] // N_CHIPS}[cls]


def _part(buf, cls, chip, start, size):
    if cls == "C":
        w = buf.shape[1] // N_CHIPS
        return buf.at[_rows(start, size), pl.ds(chip * w, w)]
    if cls == "U":
        return buf.at[chip, _rows(start, size), :]
    r = buf.shape[0] // N_CHIPS
    return buf.at[_rows(chip * r + start, size), :]


def _gather_layer(bufs, classes, name):
    n = len(bufs)
    job = _gather_job(bufs, classes)

    def body(*refs):
        outs = refs[n:2 * n]
        job["start"](refs[:n], outs, refs[2 * n], refs[2 * n + 1], True)
        job["finish"](refs[:n], outs, refs[2 * n], refs[2 * n + 1], True)

    return pl.pallas_call(
        body, name=name, in_specs=[ANY] * n, out_specs=[ANY] * n, out_shape=job["out_shapes"],
        input_output_aliases={t: t for t in range(n)},
        scratch_shapes=[pltpu.SemaphoreType.DMA((job["n_sems"],)), pltpu.SemaphoreType.DMA((job["n_sems"],))],
    )(*bufs)


def _on_chip(when, fn):
    x, y, _, _ = _place()
    me = 2 * x + y
    for s in range(N_CHIPS):
        cond = (me == s) if when is True else jnp.logical_and(when, me == s)
        pl.when(cond)(functools.partial(fn, s))


def _gather_job(bufs, classes):
    n = len(bufs)

    def copy(outs, send_sems, recv_sems, t, k, chip, start, to):
        size = _halved(outs[t], classes[t]) // 2
        part = _part(outs[t], classes[t], chip, start, size)
        return pltpu.make_async_remote_copy(src_ref=part, dst_ref=part, send_sem=send_sems.at[6 * t + k],
                                            recv_sem=recv_sems.at[6 * t + k], device_id=to, device_id_type=MESH)

    def halves(outs):
        return [_halved(outs[t], classes[t]) // 2 for t in range(n)]

    def start(ins, outs, send_sems, recv_sems, when):
        x, y, c, chips = _place()

        def run(s_me):
            for t, half in enumerate(halves(outs)):
                for k, (cx, cy) in enumerate(chips):
                    copy(outs, send_sems, recv_sems, t, k, s_me, c * half, (cx, cy, c)).start()
        _on_chip(when, run)

    def finish(ins, outs, send_sems, recv_sems, when):
        x, y, c, chips = _place()

        def run(s_me):
            passed = []
            for t, half in enumerate(halves(outs)):
                for k in range(3):
                    copy(outs, send_sems, recv_sems, t, k, s_me ^ FLIPS[k], c * half, (x, y, c)).wait_recv()
                    fwd = copy(outs, send_sems, recv_sems, t, 3 + k, s_me ^ FLIPS[k], c * half, (x, y, 1 - c))
                    fwd.start()
                    passed.append(fwd)
            for t, half in enumerate(halves(outs)):
                for k in range(3):
                    copy(outs, send_sems, recv_sems, t, 3 + k, s_me ^ FLIPS[k], (1 - c) * half, (x, y, c)).wait_recv()
            for t, half in enumerate(halves(outs)):
                for k, (cx, cy) in enumerate(chips):
                    copy(outs, send_sems, recv_sems, t, k, s_me, c * half, (cx, cy, c)).wait_send()
            for fwd in passed:
                fwd.wait_send()
        _on_chip(when, run)

    return dict(ins=list(bufs), alias=True, n_sems=6 * n, start=start, finish=finish,
                out_shapes=[jax.ShapeDtypeStruct(b.shape, b.dtype) for b in bufs])


def _assemble(slots, width, name):
    _, r, w = slots.shape
    tr = _pick(r, 256, 16)

    def body(s_ref, o_ref):
        parts = [s_ref[s] for s in range(N_CHIPS)]
        if width > N_CHIPS * w:
            parts.append(jnp.zeros((tr, width - N_CHIPS * w), slots.dtype))
        o_ref[...] = jnp.concatenate(parts, axis=1)

    return pl.pallas_call(
        body, name=name, grid=(r // tr,), in_specs=[pl.BlockSpec((N_CHIPS, tr, w), lambda i: (0, i, 0))],
        out_specs=pl.BlockSpec((tr, width), lambda i: (i, 0)), out_shape=jax.ShapeDtypeStruct((r, width), slots.dtype),
        compiler_params=_cparams(("parallel",)),
    )(slots)


def _rs_pair(dws, classes, name):
    n = len(dws)

    def shape_of(d, cls):
        return (N_CHIPS, d.shape[1] // 2, d.shape[2]) if cls == "R" else (d.shape[0] // 2, d.shape[1])

    def body(*refs):
        ins, outs = refs[:n], refs[n:2 * n]
        send_sems, recv_sems = refs[2 * n], refs[2 * n + 1]
        x, y, c, _ = _place()
        cps = []
        for t in range(n):
            if classes[t] == "R":
                h = ins[t].shape[1] // 2
                src = ins[t].at[:, _rows((1 - c) * h, h), :]
            else:
                h = ins[t].shape[0] // 2
                src = ins[t].at[_rows((1 - c) * h, h), :]
            cp = pltpu.make_async_remote_copy(src_ref=src, dst_ref=outs[t], send_sem=send_sems.at[t], recv_sem=recv_sems.at[t],
                                              device_id=(x, y, 1 - c), device_id_type=MESH)
            cp.start()
            cps.append(cp)
        for cp in cps:
            cp.wait()

    return pl.pallas_call(
        body, name=name, in_specs=[ANY] * n, out_specs=[ANY] * n,
        out_shape=[jax.ShapeDtypeStruct(shape_of(d, cls), d.dtype) for d, cls in zip(dws, classes)],
        scratch_shapes=[pltpu.SemaphoreType.DMA((n,)), pltpu.SemaphoreType.DMA((n,))],
    )(*dws)


def _rs_add(dw, got, cls, w, name):
    if cls == "R":
        _, r, _ = dw.shape
        h = r // 2
        th, tc = _pick(h, 256, 16), _wide(w)
        nbh = h // th
        grid = (N_CHIPS, nbh, w // tc)
        in_specs = [pl.BlockSpec((None, th, tc), lambda s, i, j: (s, _core_index() * nbh + i, j)),
                    pl.BlockSpec((None, th, tc), lambda s, i, j: (s, i, j))]
        out_spec = pl.BlockSpec((None, th, tc), lambda s, i, j: (s, i, j))
        sem = ("parallel", "parallel", "parallel")

        def body(a_ref, b_ref, o_ref):
            o_ref[...] = (a_ref[...].astype(F32) + b_ref[...].astype(F32)).astype(o_ref.dtype)
    elif cls == "C":
        r = dw.shape[0]
        h = r // 2
        th, tc = _pick(h, 256, 16), _wide(w)
        nbh, nbc = h // th, w // tc
        grid = (N_CHIPS, nbh, nbc)
        in_specs = [pl.BlockSpec((th, tc), lambda s, i, j: (_core_index() * nbh + i, s * nbc + j)),
                    pl.BlockSpec((th, tc), lambda s, i, j: (i, s * nbc + j))]
        out_spec = pl.BlockSpec((None, th, tc), lambda s, i, j: (s, i, j))
        sem = ("parallel", "parallel", "parallel")

        def body(a_ref, b_ref, o_ref):
            o_ref[...] = (a_ref[...].astype(F32) + b_ref[...].astype(F32)).astype(o_ref.dtype)
    else:
        r, wp = dw.shape
        h = r // 2
        th = _pick(h, 64, 16)
        nbh = h // th
        grid = (nbh,)
        in_specs = [pl.BlockSpec((th, wp), lambda i: (_core_index() * nbh + i, 0)), pl.BlockSpec((th, wp), lambda i: (i, 0))]
        out_spec = pl.BlockSpec((N_CHIPS, th, w), lambda i: (0, i, 0))
        sem = ("parallel",)

        def body(a_ref, b_ref, o_ref):
            tot = a_ref[...].astype(F32) + b_ref[...].astype(F32)
            for s in range(N_CHIPS):
                o_ref[s] = tot[:, s * w:(s + 1) * w].astype(o_ref.dtype)

    return pl.pallas_call(body, name=name, grid=grid, in_specs=in_specs, out_specs=out_spec,
                          out_shape=jax.ShapeDtypeStruct((N_CHIPS, h, w), BF16), compiler_params=_cparams(sem))(dw, got)


def _rs_chip(zs, name):
    n = len(zs)
    job = _chip_job(zs)

    def body(*refs):
        job["start"](refs[:n], refs[n:2 * n], refs[2 * n], refs[2 * n + 1], True)
        job["finish"](refs[:n], refs[n:2 * n], refs[2 * n], refs[2 * n + 1], True)

    return pl.pallas_call(
        body, name=name, in_specs=[ANY] * n, out_specs=[ANY] * n, out_shape=job["out_shapes"],
        scratch_shapes=[pltpu.SemaphoreType.DMA((job["n_sems"],)), pltpu.SemaphoreType.DMA((job["n_sems"],))],
    )(*zs)


def _chip_job(zs):
    n = len(zs)

    def copies(ins, outs, send_sems, recv_sems):
        x, y, c, chips = _place()
        return [pltpu.make_async_remote_copy(src_ref=ins[t].at[2 * cx + cy], dst_ref=outs[t].at[k], send_sem=send_sems.at[3 * t + k],
                                             recv_sem=recv_sems.at[3 * t + k], device_id=(cx, cy, c), device_id_type=MESH)
                for t in range(n) for k, (cx, cy) in enumerate(chips)]

    def start(ins, outs, send_sems, recv_sems, when):
        def run():
            for cp in copies(ins, outs, send_sems, recv_sems):
                cp.start()
        run() if when is True else pl.when(when)(run)

    def finish(ins, outs, send_sems, recv_sems, when):
        def run():
            for cp in copies(ins, outs, send_sems, recv_sems):
                cp.wait()
        run() if when is True else pl.when(when)(run)

    return dict(ins=list(zs), alias=False, n_sems=3 * n, start=start, finish=finish,
                out_shapes=[jax.ShapeDtypeStruct((3,) + z.shape[1:], z.dtype) for z in zs])


def _rs_sum(z, parts, name):
    _, h, w = z.shape
    th = _pick(h, 256, 16)
    tc = _wide(w)
    nbh = h // th

    def body(z_ref, k_ref, o_ref):
        acc = z_ref[...].astype(F32)
        for k in range(3):
            acc = acc + k_ref[k].astype(F32)
        o_ref[...] = acc

    return pl.pallas_call(
        body, name=name, grid=(nbh, w // tc),
        in_specs=[pl.BlockSpec((None, th, tc), lambda i, j: (_chip_index(), i, j)), pl.BlockSpec((3, th, tc), lambda i, j: (0, i, j))],
        out_specs=pl.BlockSpec((th, tc), lambda i, j: (_core_index() * nbh + i, j)),
        out_shape=jax.ShapeDtypeStruct((2 * h, w), F32), compiler_params=_cparams(("parallel", "parallel")))(z, parts)


def _rs_join(bufs, name):
    n = len(bufs)

    def body(*refs):
        outs = refs[n:2 * n]
        send_sems, recv_sems = refs[2 * n], refs[2 * n + 1]
        x, y, c, _ = _place()
        cps = []
        for t in range(n):
            h = outs[t].shape[0] // 2
            mine = outs[t].at[_rows(c * h, h), :]
            cp = pltpu.make_async_remote_copy(src_ref=mine, dst_ref=mine, send_sem=send_sems.at[t], recv_sem=recv_sems.at[t],
                                              device_id=(x, y, 1 - c), device_id_type=MESH)
            cp.start()
            cps.append(cp)
        for t in range(n):
            h = outs[t].shape[0] // 2
            other = outs[t].at[_rows((1 - c) * h, h), :]
            pltpu.make_async_remote_copy(src_ref=other, dst_ref=other, send_sem=send_sems.at[t], recv_sem=recv_sems.at[t],
                                         device_id=(x, y, c), device_id_type=MESH).wait_recv()
        for cp in cps:
            cp.wait_send()

    return pl.pallas_call(
        body, name=name, in_specs=[ANY] * n, out_specs=[ANY] * n,
        out_shape=[jax.ShapeDtypeStruct(b.shape, b.dtype) for b in bufs],
        input_output_aliases={t: t for t in range(n)},
        scratch_shapes=[pltpu.SemaphoreType.DMA((n,)), pltpu.SemaphoreType.DMA((n,))],
    )(*bufs)


def _rs_chip_sums(dws, classes, widths, tag):
    gots = _rs_pair(dws, classes, f"rs_pair_{tag}")
    return [_rs_add(d, g, cls, w, f"rs_add_{tag}_{t}") for t, (d, g, cls, w) in enumerate(zip(dws, gots, classes, widths))]


def _rs_finish(zs, parts, tag):
    halves = [_rs_sum(z, p, f"rs_sum_{tag}_{t}") for t, (z, p) in enumerate(zip(zs, parts))]
    return _rs_join(halves, f"rs_join_{tag}")


def _reduce_scatter_layer(dws, classes, widths, tag):
    zs = _rs_chip_sums(dws, classes, widths, tag)
    return _rs_finish(zs, _rs_chip(zs, f"rs_chip_{tag}"), tag)


def _gather_shards(mine, name):
    r, w = mine.shape
    rh = r // 2

    def body(mine_ref, out_ref, send_sems, recv_sems, local_sem):
        x, y, c, chips = _place()
        me = 2 * x + y
        half = _rows(c * rh, rh)
        other = _rows((1 - c) * rh, rh)

        def copy(k, src, chip, rows, to):
            return pltpu.make_async_remote_copy(src_ref=src, dst_ref=out_ref.at[chip, rows], send_sem=send_sems.at[k],
                                                recv_sem=recv_sems.at[k], device_id=to, device_id_type=MESH)

        local = pltpu.make_async_copy(mine_ref, out_ref.at[me], local_sem)
        local.start()
        sends = [copy(k, mine_ref.at[half], me, half, (cx, cy, c)) for k, (cx, cy) in enumerate(chips)]
        for s in sends:
            s.start()
        passed = []
        for k, (cx, cy) in enumerate(chips):
            chip = 2 * cx + cy
            copy(k, mine_ref.at[half], chip, half, (x, y, c)).wait_recv()
            fwd = copy(3 + k, out_ref.at[chip, half], chip, half, (x, y, 1 - c))
            fwd.start()
            passed.append(fwd)
        for k, (cx, cy) in enumerate(chips):
            copy(3 + k, mine_ref.at[half], 2 * cx + cy, other, (x, y, c)).wait_recv()
        for s in sends + passed:
            s.wait_send()
        local.wait()

    return pl.pallas_call(
        body, name=name, in_specs=[ANY], out_specs=ANY, out_shape=jax.ShapeDtypeStruct((N_CHIPS, r, w), mine.dtype),
        scratch_shapes=[pltpu.SemaphoreType.DMA((6,)), pltpu.SemaphoreType.DMA((6,)), pltpu.SemaphoreType.DMA],
    )(mine)


def _sum_slots(slots, name):
    n, r, w = slots.shape
    tr = _pick(r, 256, SUBLANES)

    def body(s_ref, o_ref):
        acc = s_ref[0]
        for k in range(1, n):
            acc = acc + s_ref[k]
        o_ref[...] = acc

    return pl.pallas_call(
        body, name=name, grid=(r // tr,), in_specs=[pl.BlockSpec((n, tr, w), lambda i: (0, i, 0))],
        out_specs=pl.BlockSpec((tr, w), lambda i: (i, 0)), out_shape=jax.ShapeDtypeStruct((r, w), F32),
        compiler_params=_cparams(("parallel",)),
    )(slots)


def _all_to_all_sum(flat, name):
    r, w = flat.shape

    def body(f_ref, out_ref, send_sems, recv_sems, local_sem):
        x, y, c, _ = _place()
        me = 4 * x + 2 * y + c
        local = pltpu.make_async_copy(f_ref, out_ref.at[me], local_sem)
        local.start()
        sends = []
        for k in range(1, 8):
            peer = (x ^ (k >> 2), y ^ ((k >> 1) & 1), c ^ (k & 1))
            s = pltpu.make_async_remote_copy(src_ref=f_ref, dst_ref=out_ref.at[me], send_sem=send_sems.at[k - 1],
                                             recv_sem=recv_sems.at[k - 1], device_id=peer, device_id_type=MESH)
            s.start()
            sends.append(s)
        for k in range(1, 8):
            peer_slot = 4 * (x ^ (k >> 2)) + 2 * (y ^ ((k >> 1) & 1)) + (c ^ (k & 1))
            pltpu.make_async_remote_copy(src_ref=f_ref, dst_ref=out_ref.at[peer_slot], send_sem=send_sems.at[k - 1],
                                         recv_sem=recv_sems.at[k - 1], device_id=(x, y, c), device_id_type=MESH).wait_recv()
        for s in sends:
            s.wait_send()
        local.wait()

    slots = pl.pallas_call(
        body, name=name, in_specs=[ANY], out_specs=ANY, out_shape=jax.ShapeDtypeStruct((8, r, w), flat.dtype),
        scratch_shapes=[pltpu.SemaphoreType.DMA((7,)), pltpu.SemaphoreType.DMA((7,)), pltpu.SemaphoreType.DMA],
    )(flat)
    return _sum_slots(slots, name + "_sum")


def _pack(pieces, lead):
    flat = []
    n_lead = len(lead)
    for p in pieces:
        f = p.reshape(*lead, -1)
        pad = (-f.shape[-1]) % PACK_COLS
        if pad:
            f = jnp.pad(f, [(0, 0)] * n_lead + [(0, pad)])
        flat.append(f)
    f = jnp.concatenate(flat, axis=-1) if len(flat) > 1 else flat[0]
    pad = (-f.shape[-1]) % (32 * PACK_COLS)
    if pad:
        f = jnp.pad(f, [(0, 0)] * n_lead + [(0, pad)])
    return f.reshape(*lead, -1, PACK_COLS)


def _unpack(buf, shapes, lead):
    f = buf.reshape(*lead, -1)
    out, off = [], 0
    for shp in shapes:
        n = 1
        for s in shp:
            n *= s
        out.append(f[..., off:off + n].reshape(*lead, *shp))
        off += n + ((-n) % PACK_COLS)
    return out


def _join(g, axis):
    g = jnp.moveaxis(g, 0, axis)
    return g.reshape(*g.shape[:axis], g.shape[axis] * g.shape[axis + 1], *g.shape[axis + 2:])


def _split(full, axis):
    shp = full.shape
    g = full.reshape(*shp[:axis], N_CHIPS, shp[axis] // N_CHIPS, *shp[axis + 1:])
    return jnp.moveaxis(g, axis, 0)


def _norm_bwd(xs, dres, dhs, gain, name):
    def fn(x, dr, *rest):
        dh = rest[0]
        for d in rest[1:-1]:
            dh = dh + d
        _, vjp = jax.vjp(_rms, x, rest[-1])
        dx, dg = vjp(dh.astype(F32))
        return dr + dx, dg
    return _rowmap(fn, [xs, dres] + list(dhs), [gain], name, n_acc=1)


def _mm(a, b, kind, dtype, name, job):
    if job is None:
        return _matmul(a, b, kind, dtype, name), None
    return _matmul(a, b, kind, dtype, name, job=job)


def _ffn_fwd(x1, h2, w, seq, tag, jobs=(None, None, None)):
    gpre, j0 = _mm(h2, w["ffn_w_gate"], "nn", F32, f"ffn_gate_{tag}", jobs[0])
    up, j1 = _mm(h2, w["ffn_w_up"], "nn", BF16, f"ffn_up_{tag}", jobs[1])
    gate = _conv_fwd(gpre, gpre.shape[1], w["ffn_conv_w"], w["ffn_conv_b"].reshape(1, -1), seq, f"ffn_conv_{tag}")
    act, = _rowmap(lambda g, u: (_silu(g) * u).astype(BF16), [gate, up], [], f"ffn_act_{tag}", tc=_pick(gate.shape[1], 1024, LANES))
    wd = w["ffn_w_down"] if "ffn_w_down" in w else j0[0]
    f, j2 = _mm(act, wd, "nn", F32, f"ffn_down_{tag}", jobs[2])
    return f, (gpre, up, gate, act), [j0, j1, j2]


def _ffn_bwd(dx2, h2, saved, wg, wu, conv_w, wd, seq, tag, jobs=(None,) * 5):
    gpre, up, gate, act = saved
    da, j0 = _mm(dx2, wd, "nt", BF16, f"ffn_dact_{tag}", jobs[0])
    d_wd, j1 = _mm(act, dx2, "tn", BF16,f"ffn_dwd_{tag}", jobs[1])

    def act_bwd(g, u, d):
        _, vjp = jax.vjp(lambda g_, u_: _silu(g_) * u_, g, u)
        return vjp(d.astype(F32))
    dgate, dup = _rowmap(act_bwd, [gate, up, da], [], f"ffn_dactfn_{tag}", tc=_pick(gate.shape[1], 1024, LANES))
    dgpre, d_cw, d_cb = _conv_bwd(dgate, gpre, gpre.shape[1], conv_w, seq, True, f"ffn_dconv_{tag}", dx_dtype=BF16)
    dh_a, j2 = _mm(dgpre, wg, "nt", F32, f"ffn_dh_gate_{tag}", jobs[2])
    dh_b, j3 = _mm(dup, wu, "nt", F32, f"ffn_dh_up_{tag}", jobs[3])
    d_wg, j4 = _mm(h2, dgpre, "tn", BF16,f"ffn_dwg_{tag}", jobs[4])
    d_wu = _matmul(h2, dup, "tn", BF16,f"ffn_dwu_{tag}")
    grads = dict(ffn_w_gate=d_wg, ffn_w_up=d_wu, ffn_conv_w=d_cw, ffn_conv_b=d_cb.reshape(-1), ffn_w_down=d_wd)
    return (dh_a, dh_b), grads, [j0, j1, j2, j3, j4]


def _gmlp_fwd(h, w, tag, jobs=(None, None)):
    a = w["a_w_out"].shape[0]
    p, j0 = _mm(h, w["a_w_in"], "nn", F32, f"a_in_{tag}", jobs[0])
    b_in, vnorm = w["a_b_in"].reshape(1, -1), w["a_v_norm"].reshape(1, -1)

    def fn(p_, b_, g_):
        hh = _gelu_tanh(p_ + b_)
        return hh[:, :a], _rms(hh[:, a:], g_)
    u, vn = _rowmap(fn, [p], [b_in, vnorm], f"a_gelu_{tag}")
    b_col = w["a_b_s"][:, :, None]
    y = _sgu_fwd(vn, u, w["a_w_s"], b_col, f"a_sgu_{tag}")
    m, j1 = _mm(y, w["a_w_out"], "nn", F32, f"a_out_{tag}", jobs[1])
    return m, (p, u, vn, y, fn, b_in, vnorm, b_col), [j0, j1]


def _gmlp_bwd(dm, h, saved, w, tag):
    p, u, vn, y, fn, b_in, vnorm, b_col = saved
    dy = _matmul(dm, w["a_w_out"], "nt", F32, f"a_dy_{tag}")
    d_wout = _matmul(y, dm, "tn", BF16,f"a_dwout_{tag}")
    dvn, du, d_ws, d_bcol = _sgu_bwd(vn, u, dy, w["a_w_s"], b_col, f"a_dsgu_{tag}")

    def bwd(p_, du_, dvn_, b_, g_):
        _, vjp = jax.vjp(fn, p_, b_, g_)
        return vjp((du_, dvn_))
    dp, d_bin, d_vnorm = _rowmap(bwd, [p, du, dvn], [b_in, vnorm], f"a_dgelu_{tag}", n_acc=2)
    dh = _matmul(dp, w["a_w_in"], "nt", F32, f"a_dh_{tag}")
    d_win = _matmul(h, dp, "tn", BF16,f"a_dwin_{tag}")
    return (dh,), dict(a_w_in=d_win, a_b_in=d_bin.reshape(-1), a_v_norm=d_vnorm.reshape(-1), a_w_s=d_ws, a_b_s=d_bcol[:, :, 0], a_w_out=d_wout)


def _fox_fwd_mixer(h, w, bsz, seq, tag, jobs=(None, None)):
    d = h.shape[1]
    nh = d // HEAD
    win = w["b_w_in"]
    wp = win.shape[1]
    proj, j0 = _mm(h, win, "nn", F32, f"b_in_{tag}", jobs[0])
    gq, gk = w["b_q_norm"].reshape(1, HEAD), w["b_k_norm"].reshape(1, HEAD)
    bf = jnp.pad(w["b_b_f"].reshape(1, nh), ((0, 0), (0, LANES - nh)))

    def prep(pq, pk, pfl, gq_, gk_, bf_):
        qn = jnp.concatenate([_rms(x, gq_) for x in _heads(pq)], axis=1)
        kn = jnp.concatenate([_rms(x, gk_) for x in _heads(pk)], axis=1)
        return qn, kn, -_softplus(-(pfl + bf_))
    wins = [_win(proj, d, 0), _win(proj, d, 1), _win(proj, LANES, 4 * d // LANES)]

    def prep_fwd(pq, pk, pfl, gq_, gk_, bf_):
        qn, kn, lf = prep(pq, pk, pfl, gq_, gk_, bf_)
        return qn.astype(BF16), kn.astype(BF16), lf
    qn, kn, lf = _rowmap(prep_fwd, wins, [gq, gk, bf], f"b_prep_{tag}")
    cs = _cumsum_rows(lf, seq, False, f"b_cumsum_{tag}")
    c_rows = cs[:, :nh].reshape(bsz, seq, nh).transpose(0, 2, 1)[:, :, None, :]
    o, lse = _fox_fwd(qn, kn, proj, 2 * nh, c_rows, bsz, seq, nh, f"b_attn_{tag}")
    og = _win(proj, d, 3)
    y, = _rowmap(lambda o_, g_: (o_ * jax.nn.sigmoid(g_)).astype(BF16), [o, og], [], f"b_gate_{tag}")
    m, j1 = _mm(y, w["b_w_out"], "nn", F32, f"b_out_{tag}", jobs[1])
    return m, (proj, qn, kn, c_rows, o, lse, y, prep, wins, (gq, gk, bf), wp), [j0, j1]


def _fox_bwd_mixer(dm, h, saved, w, bsz, seq, tag):
    proj, qn, kn, c_rows, o, lse, y, prep, wins, (gq, gk, bf), wp = saved
    d = h.shape[1]
    nh = d // HEAD
    dy = _matmul(dm, w["b_w_out"], "nt", F32, f"b_dy_{tag}")
    d_wout = _matmul(y, dm, "tn", BF16,f"b_dwout_{tag}")
    og = _win(proj, d, 3)

    def gate_bwd(o_, g_, dy_):
        _, vjp = jax.vjp(lambda a, b: a * jax.nn.sigmoid(b), o_, g_)
        return vjp(dy_)
    do, dog = _rowmap(gate_bwd, [o, og, dy], [], f"b_dgate_{tag}")
    dqn, delta = _fox_bwd_q(qn, kn, proj, 2 * nh, do, lse, c_rows, bsz, seq, nh, f"b_dattn_q_{tag}")
    dkn, dv, dc_rows = _fox_bwd_kv(qn, kn, proj, 2 * nh, do, lse, delta, c_rows, bsz, seq, nh, f"b_dattn_kv_{tag}")
    dc = dc_rows[:, :, 0, :].transpose(0, 2, 1).reshape(bsz * seq, nh)
    dc = jnp.pad(dc, ((0, 0), (0, LANES - nh)))
    dlf = _cumsum_rows(dc, seq, True, f"b_dcumsum_{tag}")
    extra = wp - (4 * d + LANES)

    def prep_bwd(pq, pk, pfl, dqn_, dkn_, dv_, dog_, dlf_, gq_, gk_, bf_):
        _, vjp = jax.vjp(prep, pq, pk, pfl, gq_, gk_, bf_)
        dpq, dpk, dpfl, dgq, dgk, dbf = vjp((dqn_, dkn_, dlf_))
        parts = [dpq, dpk, dv_, dog_, dpfl]
        if extra:
            parts.append(jnp.zeros((pq.shape[0], extra), F32))
        return jnp.concatenate(parts, axis=1), dgq, dgk, dbf
    dproj, d_gq, d_gk, d_bf = _rowmap(prep_bwd, wins + [dqn, dkn, dv, dog, dlf], [gq, gk, bf], f"b_dprep_{tag}", n_acc=3)
    dh = _matmul(dproj, w["b_w_in"], "nt", F32, f"b_dh_{tag}")
    d_win = _matmul(h, dproj, "tn", BF16,f"b_dwin_{tag}")
    return (dh,), dict(b_w_in=d_win, b_b_f=d_bf[0, :nh], b_q_norm=d_gq.reshape(-1), b_k_norm=d_gk.reshape(-1), b_w_out=d_wout)


def _gdn_fwd_mixer(h, w, bsz, seq, tag, jobs=(None, None)):
    d = h.shape[1]
    nkh = d // HEAD
    nvh = 2 * nkh
    dqkv = (2 * nkh + nvh) * HEAD
    dz = nvh * HEAD
    nch = seq // GDN_CHUNK
    proj, j0 = _mm(h, w["c_w_in"], "nn", F32, f"c_in_{tag}", jobs[0])
    conv = _conv_fwd(proj, dqkv, w["c_conv_w"], None, seq, f"c_conv_{tag}")

    def rows_of(cols):
        return cols.reshape(bsz, nch, GDN_CHUNK, nvh).transpose(3, 0, 1, 2)[:, :, :, None, :]
    b_rows = rows_of(proj[:, dqkv + dz:dqkv + dz + nvh])
    a_rows = rows_of(proj[:, dqkv + dz + nvh:dqkv + dz + 2 * nvh])
    alog, dtb = w["c_a_log"].reshape(nvh, 1, 1), w["c_dt_bias"].reshape(nvh, 1, 1)
    o, states = _gdn_fwd(conv, b_rows, a_rows, alog, dtb, bsz, seq, nvh, f"c_core_{tag}")
    gn = w["c_out_norm"].reshape(1, HEAD)
    zwin = _win(proj, dz, dqkv // dz)

    def outfn(o_, z_, g_):
        return jnp.concatenate([_rms(a, g_) * _silu(b) for a, b in zip(_heads(o_), _heads(z_))], axis=1)
    y, = _rowmap(lambda o_, z_, g_: outfn(o_, z_, g_).astype(BF16), [o, zwin], [gn], f"c_outnorm_{tag}")
    m, j1 = _mm(y, w["c_w_out"], "nn", F32, f"c_out_{tag}", jobs[1])
    return m, (proj, conv, b_rows, a_rows, alog, dtb, o, states, y, gn, zwin, outfn), [j0, j1]


def _gdn_bwd_mixer(dm, h, saved, w, bsz, seq, tag):
    proj, conv, b_rows, a_rows, alog, dtb, o, states, y, gn, zwin, outfn = saved
    d = h.shape[1]
    nkh = d // HEAD
    nvh = 2 * nkh
    dk_, dv_ = nkh * HEAD, nvh * HEAD
    dqkv = 2 * dk_ + dv_
    dz = dv_
    wp = proj.shape[1]
    dy = _matmul(dm, w["c_w_out"], "nt", F32, f"c_dy_{tag}")
    d_wout = _matmul(y, dm, "tn", BF16,f"c_dwout_{tag}")

    def out_bwd(o_, z_, dy_, g_):
        _, vjp = jax.vjp(outfn, o_, z_, g_)
        return vjp(dy_)
    do, dzz, d_gn = _rowmap(out_bwd, [o, zwin, dy], [gn], f"c_doutnorm_{tag}", n_acc=1)
    dq, dk, dv, db_rows, da_rows, d_alog, d_dtb = _gdn_bwd(conv, b_rows, a_rows, alog, dtb, states, do, bsz, seq, nvh, f"c_dcore_{tag}")
    cw = w["c_conv_w"]
    dq_pre, d_cwq = _conv_bwd(dq, proj, dk_, cw[:, :dk_], seq, False, f"c_dconv_q_{tag}", xcol=0)
    dk_pre, d_cwk = _conv_bwd(dk, proj, dk_, cw[:, dk_:2 * dk_], seq, False, f"c_dconv_k_{tag}", xcol=dk_)
    dv_pre, d_cwv = _conv_bwd(dv, proj, dv_, cw[:, 2 * dk_:], seq, False, f"c_dconv_v_{tag}", xcol=2 * dk_)
    d_cw = jnp.concatenate([d_cwq, d_cwk, d_cwv], axis=1)

    def cols_of(rows):
        return rows[:, :, :, 0, :].transpose(1, 2, 3, 0).reshape(bsz * seq, nvh)
    dba = jnp.concatenate([cols_of(db_rows), cols_of(da_rows)], axis=1)
    dba = jnp.pad(dba, ((0, 0), (0, wp - dqkv - dz - 2 * nvh)))
    dproj, = _rowmap(lambda *parts: jnp.concatenate(parts, axis=1), [dq_pre, dk_pre, dv_pre, dzz, dba], [], f"c_dproj_{tag}")
    dh = _matmul(dproj, w["c_w_in"], "nt", F32, f"c_dh_{tag}")
    d_win = _matmul(h, dproj, "tn", BF16,f"c_dwin_{tag}")
    return (dh,), dict(c_w_in=d_win, c_conv_w=d_cw, c_a_log=d_alog.reshape(-1), c_dt_bias=d_dtb.reshape(-1),
                       c_out_norm=d_gn.reshape(-1), c_w_out=d_wout)


_MIXER_FWD = (lambda h, w, bsz, seq, tag, jobs: _gmlp_fwd(h, w, tag, jobs), _fox_fwd_mixer, _gdn_fwd_mixer)
_MIXER_BWD = (lambda dm, h, s, w, bsz, seq, tag: _gmlp_bwd(dm, h, s, w, tag), _fox_bwd_mixer, _gdn_bwd_mixer)


class _NoPlan:
    def __init__(self, layers):
        self.layers = layers

    def weights(self, i):
        return self.layers[i]

    def fwd_jobs(self, i):
        return (None,) * 5

    def fwd_done(self, i, first, outs):
        pass

    def bwd_jobs(self, i):
        return (None,) * 5

    def bwd_done(self, i, outs):
        pass

    def grads_ready(self, i, grads):
        pass


def _local_step(x, target, plan, depth, bsz, seq):
    t, d = x.shape
    saved = []
    m_prev = None
    xin = x
    for i in range(depth):
        w = plan.weights(i)
        tag = f"l{i}"
        g_mix, g_ffn = w["norm_mix"].reshape(1, d), w["norm_ffn"].reshape(1, d)
        if i == 0:
            h, = _rowmap(lambda x_, g_: _rms(x_, g_).astype(BF16), [xin], [g_mix], f"norm_mix_{tag}")
            xl = xin
        else:
            xl, h = _rowmap(lambda x_, m_, g_: (x_ + m_, _rms(x_ + m_, g_).astype(BF16)), [xin, m_prev], [g_mix], f"norm_mix_{tag}")
        fjobs = plan.fwd_jobs(i)
        m, msaved, mouts = _MIXER_FWD[i % 3](h, w, bsz, seq, tag, fjobs[:2])
        plan.fwd_done(i, 0, mouts)
        x1, h2 = _rowmap(lambda x_, m_, g_: (x_ + m_, _rms(x_ + m_, g_).astype(BF16)), [xl, m], [g_ffn], f"norm_ffn_{tag}")
        f, fsaved, jouts = _ffn_fwd(x1, h2, w, seq, tag, fjobs[2:])
        plan.fwd_done(i, 2, jouts)
        saved.append((xl, h, msaved, x1, h2, fsaved))
        xin, m_prev = x1, f

    def loss_fn(x_, f_, tg_):
        e = x_ + f_ - tg_
        return e * (1.0 / d), jnp.full((1, LANES), (0.5 / d) * jnp.sum(e * e), F32)
    dx, loss_acc = _rowmap(loss_fn, [xin, m_prev, target], [], "loss", n_acc=1)
    loss = loss_acc[0, 0]

    grads = [None] * depth
    for i in reversed(range(depth)):
        w = plan.weights(i)
        tag = f"l{i}"
        xl, h, msaved, x1, h2, fsaved = saved[i]
        g_mix, g_ffn = w["norm_mix"].reshape(1, d), w["norm_ffn"].reshape(1, d)
        dhs, gw, jouts = _ffn_bwd(dx, h2, fsaved, w["ffn_w_gate"], w["ffn_w_up"], w["ffn_conv_w"], w["ffn_w_down"], seq, tag, plan.bwd_jobs(i))
        plan.bwd_done(i, jouts)
        dx1, d_gffn = _norm_bwd(x1, dx, dhs, g_ffn, f"dnorm_ffn_{tag}")
        dhs, gm = _MIXER_BWD[i % 3](dx1, h, msaved, w, bsz, seq, tag)
        dx, d_gmix = _norm_bwd(xl, dx1, dhs, g_mix, f"dnorm_mix_{tag}")
        gw.update(gm)
        gw["norm_mix"], gw["norm_ffn"] = d_gmix.reshape(-1), d_gffn.reshape(-1)
        grads[i] = gw
        plan.grads_ready(i, gw)
    return loss, dx, grads


def _adamw_math(w_, g_, m_, v_):
    m_new = ADAM_B1 * m_ + (1.0 - ADAM_B1) * g_
    v_new = ADAM_B2 * v_ + (1.0 - ADAM_B2) * (g_ * g_)
    m_hat = m_new / (1.0 - ADAM_B1 ** ADAM_STEP)
    v_hat = v_new / (1.0 - ADAM_B2 ** ADAM_STEP)
    delta = -ADAM_LR * (m_hat / (jnp.sqrt(v_hat) + ADAM_EPS) + ADAM_WD * w_)
    return delta, m_new, v_new


def _adamw(w, g, m, v, name):
    shape = w.shape
    if w.ndim == 1:
        w, g, m, v = (a.reshape(1, -1) for a in (w, g, m, v))
    return [o.reshape(shape) for o in _elementwise(_adamw_math, [w, g, m, v], 3, name)]


def _adamw_layers(w, gs, m, v, name):
    nl, r, c = w.shape
    tr = _pick(r, max(SUBLANES, (1 << 19) // c // SUBLANES * SUBLANES), SUBLANES)

    def body(*refs):
        w_ref, m_ref, v_ref = refs[:3]
        g_refs = refs[3:3 + nl]
        go_ref, d_ref, mo_ref, vo_ref = refs[3 + nl:]
        layer = pl.program_id(0)
        for k in range(nl):
            @pl.when(layer == k)
            def _(k=k):
                g = g_refs[k][...]
                delta, m_new, v_new = _adamw_math(w_ref[...], g, m_ref[...], v_ref[...])
                go_ref[...] = g
                d_ref[...] = delta
                mo_ref[...] = m_new
                vo_ref[...] = v_new

    st = pl.BlockSpec((None, tr, c), lambda l, i: (l, i, 0))
    g_specs = [pl.BlockSpec((tr, c), functools.partial(lambda l, i, k: (jnp.where(l == k, i, 0), 0), k=k)) for k in range(nl)]
    return pl.pallas_call(
        body, name=name, grid=(nl, r // tr), in_specs=[st, st, st] + g_specs, out_specs=[st] * 4,
        out_shape=[jax.ShapeDtypeStruct(w.shape, F32)] * 4, compiler_params=_cparams(("parallel", "parallel")),
    )(w, m, v, *gs)


WEIGHTS = ['norm_mix', 'norm_ffn', 'ffn_w_gate', 'ffn_w_up', 'ffn_conv_w', 'ffn_conv_b', 'ffn_w_down', 'a_w_in', 'a_b_in', 'a_v_norm',
           'a_w_s', 'a_b_s', 'a_w_out', 'b_w_in', 'b_b_f', 'b_q_norm', 'b_k_norm', 'b_w_out', 'c_w_in', 'c_conv_w', 'c_a_log',
           'c_dt_bias', 'c_out_norm', 'c_w_out']
BIG = {'ffn_w_gate': 1, 'ffn_w_up': 1, 'ffn_w_down': 0, 'a_w_in': 1, 'a_w_out': 0, 'b_w_in': 1, 'b_w_out': 0, 'c_w_in': 1, 'c_w_out': 0}
SMALL_SHARDED = {'ffn_conv_w': 1, 'a_b_in': 0, 'a_v_norm': 0, 'c_conv_w': 1}
MIXER_NAMES = (('a_w_in', 'a_b_in', 'a_v_norm', 'a_w_s', 'a_b_s', 'a_w_out'), ('b_w_in', 'b_b_f', 'b_q_norm', 'b_k_norm', 'b_w_out'),
               ('c_w_in', 'c_conv_w', 'c_a_log', 'c_dt_bias', 'c_out_norm', 'c_w_out'))
FFN_NAMES = ('norm_mix', 'norm_ffn', 'ffn_w_gate', 'ffn_w_up', 'ffn_conv_w', 'ffn_conv_b', 'ffn_w_down')


def _layer_entries(depth):
    out = []
    for i in range(depth):
        kind, j = i % 3, i // 3
        out.append([(n, i) for n in FFN_NAMES] + [(n, j) for n in MIXER_NAMES[kind]])
    return out


def _layout(name, shard_shape):
    if BIG[name] == 0:
        return "R"
    return "C" if shard_shape[-1] % LANES == 0 else "U"


JOB_GROUPS = ((3,), (4,), (0,), (1,), (2,))


class _Plan:
    def __init__(self, params, entries, small_full):
        self.params, self.entries, self.small_full = params, entries, small_full
        self.depth = len(entries)
        self.big = [[(n, j, _layout(n, params[n].shape[1:])) for n, j in ent if n in BIG] for ent in entries]
        self.layers = [None] * self.depth
        self.zs = [None] * self.depth
        self.total = {}
        for i, ent in enumerate(entries):
            self.layers[i] = {n: (small_full[(n, j)] if n in SMALL_SHARDED else params[n][j]) for n, j in ent if n not in BIG}
        mixer = [self._cast(0, t) for t in (3, 4)]
        mixer = _gather_layer(mixer, [self.big[0][t][2] for t in (3, 4)], "gather_mixer_l0")
        self._install(0, 3, mixer[0])
        self._install(0, 4, mixer[1])

    def _cast(self, i, t):
        n, j, cls = self.big[i][t]
        return _cast_window(self.params[n], j, cls, f"cast_{n}_l{i}")

    def _install(self, i, t, buf):
        n, j, cls = self.big[i][t]
        self.layers[i][n] = _assemble(buf, _pad_cols(N_CHIPS * buf.shape[2]), f"assemble_{n}_l{i}") if cls == "U" else buf

    def weights(self, i):
        return self.layers[i]

    def fwd_jobs(self, i):
        todo = [(i, 0), (i, 1), (i, 2)] + ([(i + 1, 3), (i + 1, 4)] if i + 1 < self.depth else [])
        self.fwd_todo = todo
        jobs = [_gather_job([self._cast(li, t)], [self.big[li][t][2]]) for li, t in todo]
        return jobs + [None] * (5 - len(jobs))

    def fwd_done(self, i, first, outs):
        for (li, t), got in zip(self.fwd_todo[first:], outs):
            if got is not None:
                self._install(li, t, got[0])

    def grads_ready(self, i, grads):
        dws, classes, widths = [], [], []
        for n, j, cls in self.big[i]:
            shard = self.params[n].shape[1:]
            dws.append(grads[n].reshape(N_CHIPS, shard[0], shard[1]) if cls == "R" else grads[n])
            classes.append(cls)
            widths.append(shard[1])
        self.zs[i] = _rs_chip_sums(dws, classes, widths, f"l{i}")
        if i == 0:
            self._finish(0, _rs_chip(self.zs[0], "rs_chip_l0"))

    def bwd_jobs(self, i):
        if i + 1 >= self.depth:
            return (None,) * len(JOB_GROUPS)
        return [_chip_job([self.zs[i + 1][t] for t in grp]) for grp in JOB_GROUPS]

    def bwd_done(self, i, outs):
        if i + 1 < self.depth:
            parts = [None] * len(self.big[i + 1])
            for grp, got in zip(JOB_GROUPS, outs):
                for t, p in zip(grp, got):
                    parts[t] = p
            self._finish(i + 1, parts)

    def _finish(self, i, parts):
        for (n, j, _), red in zip(self.big[i], _rs_finish(self.zs[i], parts, f"l{i}")):
            self.total[(n, j)] = red


def _train_step(x, target, params, moments_m, moments_v):
    bsz, seq, d = x.shape
    depth = params['norm_mix'].shape[0]
    entries = _layer_entries(depth)

    small_list = [(n, j) for n in SMALL_SHARDED for j in range(params[n].shape[0])]
    small_buf = _gather_shards(_pack([params[n][j] for n, j in small_list], ()), "gather_small")
    small_full = {}
    for (n, j), g in zip(small_list, _unpack(small_buf, [params[n][j].shape for n, j in small_list], (N_CHIPS,))):
        small_full[(n, j)] = _join(g, SMALL_SHARDED[n])

    plan = _Plan(params, entries, small_full)
    loss_local, dx, grads = _local_step(x.reshape(bsz * seq, d), target.reshape(bsz * seq, d), plan, depth, bsz, seq)
    loss = lax.psum(loss_local, ("x", "y", "c"))

    total = plan.total
    layer_of = {(n, j): i for i, ent in enumerate(entries) for n, j in ent}
    packed = _pack([_split(grads[layer_of[(n, j)]][n], SMALL_SHARDED[n]) for n, j in small_list], (N_CHIPS,))
    red = _reduce_scatter_layer([packed], ["R"], [PACK_COLS], "small")[0]
    for (n, j), g in zip(small_list, _unpack(red, [params[n][j].shape for n, j in small_list], ())):
        total[(n, j)] = g
    repl = [(n, j) for n in WEIGHTS if n not in BIG and n not in SMALL_SHARDED for j in range(params[n].shape[0])]
    flat = jnp.concatenate([grads[layer_of[k]][k[0]].reshape(-1) for k in repl])
    n_flat = flat.shape[0]
    flat = jnp.pad(flat, (0, (-n_flat) % (SUBLANES * LANES))).reshape(-1, LANES)
    flat = _all_to_all_sum(flat, "allreduce_small").reshape(-1)
    off = 0
    for k in repl:
        shp = params[k[0]][k[1]].shape
        n = 1
        for s in shp:
            n *= s
        total[k] = flat[off:off + n].reshape(shp)
        off += n

    grad_w, delta_w, new_m, new_v = {}, {}, {}, {}
    for n in WEIGHTS:
        nl = params[n].shape[0]
        if n in BIG:
            grad_w[n], delta_w[n], new_m[n], new_v[n] = _adamw_layers(params[n], [total[(n, j)] for j in range(nl)], moments_m[n], moments_v[n], f"adamw_{n}")
        else:
            g = jnp.stack([total[(n, j)] for j in range(nl)])
            grad_w[n] = g
            delta_w[n], new_m[n], new_v[n] = _adamw(params[n], g, moments_m[n], moments_v[n], f"adamw_{n}")
    return (loss, dx.reshape(bsz, seq, d), *[grad_w[n] for n in WEIGHTS], *[delta_w[n] for n in WEIGHTS],
            *[new_m[n] for n in WEIGHTS], *[new_v[n] for n in WEIGHTS])


def kernel(x, norm_mix, norm_ffn, ffn_w_gate, ffn_w_up, ffn_conv_w, ffn_conv_b, ffn_w_down, a_w_in, a_b_in, a_v_norm, a_w_s, a_b_s, a_w_out, b_w_in, b_b_f, b_q_norm, b_k_norm, b_w_out, c_w_in, c_conv_w, c_a_log, c_dt_bias, c_out_norm, c_w_out, loss_target, m_norm_mix, m_norm_ffn, m_ffn_w_gate, m_ffn_w_up, m_ffn_conv_w, m_ffn_conv_b, m_ffn_w_down, m_a_w_in, m_a_b_in, m_a_v_norm, m_a_w_s, m_a_b_s, m_a_w_out, m_b_w_in, m_b_b_f, m_b_q_norm, m_b_k_norm, m_b_w_out, m_c_w_in, m_c_conv_w, m_c_a_log, m_c_dt_bias, m_c_out_norm, m_c_w_out, v_norm_mix, v_norm_ffn, v_ffn_w_gate, v_ffn_w_up, v_ffn_conv_w, v_ffn_conv_b, v_ffn_w_down, v_a_w_in, v_a_b_in, v_a_v_norm, v_a_w_s, v_a_b_s, v_a_w_out, v_b_w_in, v_b_b_f, v_b_q_norm, v_b_k_norm, v_b_w_out, v_c_w_in, v_c_conv_w, v_c_a_log, v_c_dt_bias, v_c_out_norm, v_c_w_out):
    given = dict(locals())
    params = {n: given[n] for n in WEIGHTS}
    moments_m = {n: given["m_" + n] for n in WEIGHTS}
    moments_v = {n: given["v_" + n] for n in WEIGHTS}
    return _train_step(x, loss_target, params, moments_m, moments_v)
```

```python
import functools

import jax
import jax.numpy as jnp
from jax import lax
from jax.experimental import pallas as pl
from jax.experimental.pallas import tpu as pltpu

F32 = jnp.float32
BF16 = jnp.bfloat16
HI = lax.Precision.HIGHEST
MESH = pl.DeviceIdType.MESH

RMS_EPS = 1e-6
ADAM_LR, ADAM_B1, ADAM_B2, ADAM_EPS, ADAM_WD, ADAM_STEP = 0.001, 0.9, 0.999, 1e-08, 0.01, 10
A_CHUNK, HEAD, GDN_CHUNK = 128, 128, 64
LANES, SUBLANES = 128, 8
PACK_COLS = 1024
N_CHIPS = 4
VMEM_LIMIT = 56 * 1024 * 1024
ROWMAP_BUDGET = 20 * 1024 * 1024

NN = (((1,), (0,)), ((), ()))
NT = (((1,), (1,)), ((), ()))
TN = (((0,), (0,)), ((), ()))
BNN = (((2,), (1,)), ((0,), (0,)))
BNT = (((2,), (2,)), ((0,), (0,)))
BTN = (((1,), (1,)), ((0,), (0,)))


def _pick(n, cap, mult):
    if n <= cap:
        return n
    best = None
    for d in range(mult, cap + 1, mult):
        if n % d == 0:
            best = d
    if best is None:
        raise ValueError(f"no tile for {n} (cap {cap}, multiple of {mult})")
    return best


def _pad_cols(n):
    j = -(-n // LANES)
    while not (j <= 8 or any(j % d == 0 for d in (4, 5, 6, 7, 8))):
        j += 1
    return j * LANES


def _cparams(sem):
    return pltpu.CompilerParams(dimension_semantics=sem, vmem_limit_bytes=VMEM_LIMIT)


def _matmul(a, b, kind, out_dtype, name, job=None):
    if kind == "nn":
        (m, k), (k2, n) = a.shape, b.shape
    elif kind == "nt":
        (m, k), (n, k2) = a.shape, b.shape
    else:
        (k, m), (k2, n) = a.shape, b.shape
    assert k == k2, (name, a.shape, b.shape)
    tm, tn, tk = _pick(m, 1024, LANES), _pick(n, 1024, LANES), _pick(k, 2048, LANES)
    ni, nj, nk = m // tm, n // tn, k // tk
    dims = {"nn": NN, "nt": NT, "tn": TN}[kind]
    a_spec = pl.BlockSpec((tk, tm), lambda i, j, kk: (kk, i)) if kind == "tn" else pl.BlockSpec((tm, tk), lambda i, j, kk: (i, kk))
    b_spec = pl.BlockSpec((tn, tk), lambda i, j, kk: (j, kk)) if kind == "nt" else pl.BlockSpec((tk, tn), lambda i, j, kk: (kk, j))
    n_jin, n_jout = (len(job["ins"]), len(job["out_shapes"])) if job else (0, 0)

    def body(a_ref, b_ref, *rest):
        jins, o_ref, jouts = rest[:n_jin], rest[n_jin], rest[n_jin + 1:n_jin + 1 + n_jout]
        scratch = rest[n_jin + 1 + n_jout:]
        i, j, kk = pl.program_id(0), pl.program_id(1), pl.program_id(2)
        if job:
            first = jnp.logical_and(jnp.logical_and(i == 0, j == 0), kk == 0)
            job["start"](jins, jouts, scratch[-2], scratch[-1], first)
        prod = lax.dot_general(a_ref[...].astype(BF16), b_ref[...].astype(BF16), dims, preferred_element_type=F32)
        if nk == 1:
            o_ref[...] = prod.astype(o_ref.dtype)
        else:
            acc_ref = scratch[0]

            @pl.when(kk == 0)
            def _():
                acc_ref[...] = prod

            @pl.when(kk > 0)
            def _():
                acc_ref[...] += prod

            @pl.when(kk == nk - 1)
            def _():
                o_ref[...] = acc_ref[...].astype(o_ref.dtype)
        if job:
            last = jnp.logical_and(jnp.logical_and(i == ni - 1, j == nj - 1), kk == nk - 1)
            job["finish"](jins, jouts, scratch[-2], scratch[-1], last)

    scratch_shapes = [pltpu.VMEM((tm, tn), F32)] if nk > 1 else []
    out_specs = pl.BlockSpec((tm, tn), lambda i, j, kk: (i, j))
    out_shape = jax.ShapeDtypeStruct((m, n), out_dtype)
    if not job:
        return pl.pallas_call(
            body, name=name, grid=(ni, nj, nk), in_specs=[a_spec, b_spec], out_specs=out_specs, out_shape=out_shape,
            scratch_shapes=scratch_shapes, compiler_params=_cparams(("parallel", "parallel", "arbitrary")),
        )(a, b)
    scratch_shapes += [pltpu.SemaphoreType.DMA((job["n_sems"],)), pltpu.SemaphoreType.DMA((job["n_sems"],))]
    res = pl.pallas_call(
        body, name=name, grid=(ni, nj, nk), in_specs=[a_spec, b_spec] + [ANY] * n_jin,
        out_specs=[out_specs] + [ANY] * n_jout, out_shape=[out_shape] + list(job["out_shapes"]),
        input_output_aliases={2 + t: 1 + t for t in range(n_jin)} if job["alias"] else {},
        scratch_shapes=scratch_shapes, compiler_params=_cparams(("arbitrary", "arbitrary", "arbitrary")),
    )(a, b, *job["ins"])
    return res[0], list(res[1:])


def _win(arr, width=None, blk=0):
    return (arr, arr.shape[1] if width is None else width, blk)


def _rowmap(fn, rows, params, name, n_acc=0, tc=None, col_params=()):
    rows = [r if isinstance(r, tuple) else _win(r) for r in rows]
    t = rows[0][0].shape[0]
    widths = [tc if tc is not None else w for (_, w, _) in rows]

    def blocks_for(tr):
        rb = [jax.ShapeDtypeStruct((tr, w), a.dtype) for (a, _, _), w in zip(rows, widths)]
        pb = [jax.ShapeDtypeStruct((p.shape[0], tc) if (i in col_params) else p.shape, p.dtype) for i, p in enumerate(params)]
        return rb, pb

    rb, pb = blocks_for(SUBLANES * 2)
    outs = jax.eval_shape(fn, *rb, *pb)
    outs = list(outs) if isinstance(outs, (tuple, list)) else [outs]
    n_row = len(outs) - n_acc
    row_bytes = sum(w * a.dtype.itemsize for (a, _, _), w in zip(rows, widths)) + sum(o.shape[1] * o.dtype.itemsize for o in outs[:n_row])
    tr = 16
    while tr * 2 <= 512 and t % (tr * 2) == 0 and (tr * 2) * row_bytes * 5 <= ROWMAP_BUDGET:
        tr *= 2
    rb, pb = blocks_for(tr)
    outs = jax.eval_shape(fn, *rb, *pb)
    outs = list(outs) if isinstance(outs, (tuple, list)) else [outs]
    n_in = len(rows) + len(params)

    if tc is None:
        grid = (t // tr,)
        row_axis = 0
        in_specs = [pl.BlockSpec((tr, w), functools.partial(lambda i, b: (i, b), b=blk)) for (_, w, blk) in rows]
        in_specs += [pl.BlockSpec(p.shape, functools.partial(lambda i, nd: (0,) * nd, nd=p.ndim)) for p in params]
        out_specs = [pl.BlockSpec((tr, o.shape[1]), lambda i: (i, 0)) for o in outs[:n_row]]
        out_specs += [pl.BlockSpec(o.shape, functools.partial(lambda i, nd: (0,) * nd, nd=len(o.shape))) for o in outs[n_row:]]
        out_shape = [jax.ShapeDtypeStruct((t, o.shape[1]), o.dtype) for o in outs[:n_row]]
        out_shape += [jax.ShapeDtypeStruct(o.shape, o.dtype) for o in outs[n_row:]]
        sem = ("arbitrary",) if n_acc else ("parallel",)
    else:
        wtot = rows[0][1]
        grid = (wtot // tc, t // tr)
        row_axis = 1
        in_specs = [pl.BlockSpec((tr, tc), functools.partial(lambda j, i, b: (i, j + b), b=blk)) for (_, _, blk) in rows]
        for i, p in enumerate(params):
            if i in col_params:
                in_specs.append(pl.BlockSpec((p.shape[0], tc), lambda j, i: (0, j)))
            else:
                in_specs.append(pl.BlockSpec(p.shape, functools.partial(lambda j, i, nd: (0,) * nd, nd=p.ndim)))
        out_specs = [pl.BlockSpec((tr, tc), lambda j, i: (i, j)) for _ in outs[:n_row]]
        out_specs += [pl.BlockSpec((o.shape[0], tc), lambda j, i: (0, j)) for o in outs[n_row:]]
        out_shape = [jax.ShapeDtypeStruct((t, wtot), o.dtype) for o in outs[:n_row]]
        out_shape += [jax.ShapeDtypeStruct((o.shape[0], wtot), o.dtype) for o in outs[n_row:]]
        sem = ("parallel", "arbitrary") if n_acc else ("parallel", "parallel")

    def body(*refs):
        ins, ors = refs[:n_in], refs[n_in:]
        res = fn(*[r[...] for r in ins])
        res = list(res) if isinstance(res, (tuple, list)) else [res]
        for o, r in zip(ors[:n_row], res[:n_row]):
            o[...] = r.astype(o.dtype)
        if n_acc:
            i = pl.program_id(row_axis)
            for o, r in zip(ors[n_row:], res[n_row:]):
                @pl.when(i == 0)
                def _(o=o, r=r):
                    o[...] = r.astype(o.dtype)

                @pl.when(i > 0)
                def _(o=o, r=r):
                    o[...] += r.astype(o.dtype)

    res = pl.pallas_call(
        body, name=name, grid=grid, in_specs=in_specs, out_specs=out_specs, out_shape=out_shape,
        compiler_params=_cparams(sem),
    )(*[a for (a, _, _) in rows], *params)
    return res


def _elementwise(fn, arrays, n_out, name):
    shape = arrays[0].shape
    cols = shape[-1]
    rws = 1
    for s in shape[:-1]:
        rws *= s
    arrs = [a.reshape(rws, cols) for a in arrays]
    per_row = cols * 4 * (len(arrays) + n_out) * 3
    tr = rws
    if rws * per_row > ROWMAP_BUDGET:
        tr = _pick(rws, max(SUBLANES, ROWMAP_BUDGET // per_row), SUBLANES)

    def body(*refs):
        res = fn(*[r[...] for r in refs[:len(arrs)]])
        for o, r in zip(refs[len(arrs):], res):
            o[...] = r

    spec = pl.BlockSpec((tr, cols), lambda i: (i, 0))
    outs = pl.pallas_call(
        body, name=name, grid=(rws // tr,), in_specs=[spec] * len(arrs), out_specs=[spec] * n_out,
        out_shape=[jax.ShapeDtypeStruct((rws, cols), F32)] * n_out, compiler_params=_cparams(("parallel",)),
    )(*arrs)
    return [o.reshape(shape) for o in outs]


def _rms(x, g):
    return x * lax.rsqrt(jnp.mean(x * x, axis=-1, keepdims=True) + RMS_EPS) * g


def _silu(x):
    return x * jax.nn.sigmoid(x)


def _softplus(x):
    return jnp.maximum(x, 0.0) + jnp.log(1.0 + jnp.exp(-jnp.abs(x)))


def _gelu_tanh(x):
    return 0.5 * x * (1.0 + jnp.tanh(0.7978845608028654 * (x + 0.044715 * (x * x * x))))


def _heads(x):
    return [x[:, h * HEAD:(h + 1) * HEAD] for h in range(x.shape[1] // HEAD)]


def _dot(a, b, dims=NN):
    return lax.dot_general(a, b, dims, precision=HI, preferred_element_type=F32)


def _bdot(a, b, dims):
    return lax.dot_general(a.astype(BF16), b.astype(BF16), dims, preferred_element_type=F32)


def _eye(n):
    return (lax.broadcasted_iota(jnp.int32, (n, n), 0) == lax.broadcasted_iota(jnp.int32, (n, n), 1)).astype(F32)


def _tri(n):
    return lax.broadcasted_iota(jnp.int32, (n, n), 0) >= lax.broadcasted_iota(jnp.int32, (n, n), 1)


def _row_to_col(row):
    return jnp.sum(_eye(row.shape[1]) * row, axis=1, keepdims=True)


def _conv_tiles(w, seq):
    return _pick(seq, 512, SUBLANES), _pick(w, 512, LANES)


def _conv_fwd(x, width, w, bias, seq, name):
    t = x.shape[0]
    kk = w.shape[0]
    tr, tc = _conv_tiles(width, seq)
    hb = tr // SUBLANES

    def body(*refs):
        if bias is None:
            x_ref, h_ref, w_ref, o_ref = refs
        else:
            x_ref, h_ref, w_ref, b_ref, o_ref = refs
        i = pl.program_id(1)
        first = (i * tr) % seq == 0
        halo = jnp.where(first, 0.0, h_ref[...])
        xe = jnp.concatenate([halo, x_ref[...]], axis=0)
        wv = w_ref[...]
        acc = xe[SUBLANES:, :] * wv[kk - 1:kk, :]
        for s in range(1, kk):
            acc = acc + pltpu.roll(xe, s, 0)[SUBLANES:, :] * wv[kk - 1 - s:kk - s, :]
        if bias is not None:
            acc = acc + b_ref[...]
        o_ref[...] = acc

    in_specs = [pl.BlockSpec((tr, tc), lambda j, i: (i, j)),
                pl.BlockSpec((SUBLANES, tc), lambda j, i: (jnp.maximum(i * hb - 1, 0), j)),
                pl.BlockSpec((kk, tc), lambda j, i: (0, j))]
    ops = [x, x, w]
    if bias is not None:
        in_specs.append(pl.BlockSpec((1, tc), lambda j, i: (0, j)))
        ops.append(bias)
    return pl.pallas_call(
        body, name=name, grid=(width // tc, t // tr), in_specs=in_specs,
        out_specs=pl.BlockSpec((tr, tc), lambda j, i: (i, j)), out_shape=jax.ShapeDtypeStruct((t, width), F32),
        compiler_params=_cparams(("parallel", "parallel")),
    )(*ops)


def _conv_bwd(dy, x, width, w, seq, with_bias, name, xcol=0, dx_dtype=F32):
    t = x.shape[0]
    kk = w.shape[0]
    tr, tc = _conv_tiles(width, seq)
    hb = tr // SUBLANES
    n_halo_blocks = t // SUBLANES
    assert xcol % tc == 0
    xb = xcol // tc

    def body(dy_ref, dyn_ref, x_ref, xh_ref, w_ref, dx_ref, dw_ref, *rest):
        i = pl.program_id(1)
        first = (i * tr) % seq == 0
        last = ((i + 1) * tr) % seq == 0
        dyc = dy_ref[...]
        dye = jnp.concatenate([dyc, jnp.where(last, 0.0, dyn_ref[...])], axis=0)
        xe = jnp.concatenate([jnp.where(first, 0.0, xh_ref[...]), x_ref[...]], axis=0)
        wv = w_ref[...]
        dx = dyc * wv[kk - 1:kk, :]
        dws = [None] * kk
        dws[kk - 1] = jnp.sum(dyc * xe[SUBLANES:, :], axis=0, keepdims=True)
        for s in range(1, kk):
            dx = dx + pltpu.roll(dye, tr + SUBLANES - s, 0)[:tr, :] * wv[kk - 1 - s:kk - s, :]
            dws[kk - 1 - s] = jnp.sum(dyc * pltpu.roll(xe, s, 0)[SUBLANES:, :], axis=0, keepdims=True)
        dx_ref[...] = dx.astype(dx_ref.dtype)

        @pl.when(i == 0)
        def _():
            for j in range(kk):
                dw_ref[j:j + 1, :] = dws[j]
            if with_bias:
                rest[0][...] = jnp.sum(dyc, axis=0, keepdims=True)

        @pl.when(i > 0)
        def _():
            for j in range(kk):
                dw_ref[j:j + 1, :] += dws[j]
            if with_bias:
                rest[0][...] += jnp.sum(dyc, axis=0, keepdims=True)

    cur = pl.BlockSpec((tr, tc), lambda j, i: (i, j))
    in_specs = [cur, pl.BlockSpec((SUBLANES, tc), lambda j, i: (jnp.minimum((i + 1) * hb, n_halo_blocks - 1), j)),
                pl.BlockSpec((tr, tc), lambda j, i: (i, j + xb)),
                pl.BlockSpec((SUBLANES, tc), lambda j, i: (jnp.maximum(i * hb - 1, 0), j + xb)),
                pl.BlockSpec((kk, tc), lambda j, i: (0, j))]
    out_specs = [cur, pl.BlockSpec((kk, tc), lambda j, i: (0, j))]
    out_shape = [jax.ShapeDtypeStruct((t, width), dx_dtype), jax.ShapeDtypeStruct((kk, width), F32)]
    if with_bias:
        out_specs.append(pl.BlockSpec((1, tc), lambda j, i: (0, j)))
        out_shape.append(jax.ShapeDtypeStruct((1, width), F32))
    return pl.pallas_call(
        body, name=name, grid=(width // tc, t // tr), in_specs=in_specs, out_specs=out_specs, out_shape=out_shape,
        compiler_params=_cparams(("parallel", "arbitrary")),
    )(dy, dy, x, x, w)


def _cumsum_rows(x, seq, reverse, name):
    t, w = x.shape
    tb = _pick(seq, 256, SUBLANES)
    nb = seq // tb

    def pos(b, i):
        return (b * nb + (nb - 1 - i if reverse else i), 0)

    def body(x_ref, o_ref, carry):
        i = pl.program_id(1)

        @pl.when(i == 0)
        def _():
            carry[...] = jnp.zeros_like(carry)

        blk = x_ref[...]
        r = lax.broadcasted_iota(jnp.int32, (tb, tb), 0)
        c = lax.broadcasted_iota(jnp.int32, (tb, tb), 1)
        m = ((r <= c) if reverse else (r >= c)).astype(F32)
        o_ref[...] = _dot(m, blk) + carry[...]
        carry[...] += jnp.sum(blk, axis=0, keepdims=True)

    return pl.pallas_call(
        body, name=name, grid=(t // seq, nb), in_specs=[pl.BlockSpec((tb, w), pos)], out_specs=pl.BlockSpec((tb, w), pos),
        out_shape=jax.ShapeDtypeStruct((t, w), F32), scratch_shapes=[pltpu.VMEM((1, w), F32)],
        compiler_params=_cparams(("parallel", "arbitrary")),
    )(x)


def _sgu_fwd(vn, u, w_s, b_col, name):
    t, a = vn.shape
    g = a // HEAD

    def body(v_ref, u_ref, w_ref, b_ref, y_ref):
        tri = _tri(A_CHUNK)
        for gi in range(g):
            sl = slice(gi * HEAD, (gi + 1) * HEAD)
            wc = jnp.where(tri, w_ref[gi], 0.0)
            sv = _bdot(wc, v_ref[:, sl], NN) + b_ref[gi]
            y_ref[:, sl] = (u_ref[:, sl] * sv).astype(y_ref.dtype)

    blk = pl.BlockSpec((A_CHUNK, a), lambda i: (i, 0))
    return pl.pallas_call(
        body, name=name, grid=(t // A_CHUNK,),
        in_specs=[blk, blk, pl.BlockSpec(w_s.shape, lambda i: (0, 0, 0)), pl.BlockSpec(b_col.shape, lambda i: (0, 0, 0))],
        out_specs=blk, out_shape=jax.ShapeDtypeStruct((t, a), BF16), compiler_params=_cparams(("parallel",)),
    )(vn, u, w_s, b_col)


def _sgu_bwd(vn, u, dy, w_s, b_col, name):
    t, a = vn.shape
    g = a // HEAD

    def body(v_ref, u_ref, dy_ref, w_ref, b_ref, dv_ref, du_ref, dw_ref, db_ref):
        i = pl.program_id(0)
        tri = _tri(A_CHUNK)
        for gi in range(g):
            sl = slice(gi * HEAD, (gi + 1) * HEAD)
            wc = jnp.where(tri, w_ref[gi], 0.0)
            v = v_ref[:, sl]
            sv = _bdot(wc, v, NN) + b_ref[gi]
            dyb = dy_ref[:, sl]
            du_ref[:, sl] = dyb * sv
            dsv = dyb * u_ref[:, sl]
            dv_ref[:, sl] = _bdot(wc, dsv, TN)
            dw = jnp.where(tri, _bdot(dsv, v, NT), 0.0)
            db = jnp.sum(dsv, axis=1, keepdims=True)

            @pl.when(i == 0)
            def _(gi=gi, dw=dw, db=db):
                dw_ref[gi] = dw
                db_ref[gi] = db

            @pl.when(i > 0)
            def _(gi=gi, dw=dw, db=db):
                dw_ref[gi] += dw
                db_ref[gi] += db

    blk = pl.BlockSpec((A_CHUNK, a), lambda i: (i, 0))
    wsp = pl.BlockSpec(w_s.shape, lambda i: (0, 0, 0))
    bsp = pl.BlockSpec(b_col.shape, lambda i: (0, 0, 0))
    return pl.pallas_call(
        body, name=name, grid=(t // A_CHUNK,), in_specs=[blk, blk, blk, wsp, bsp], out_specs=[blk, blk, wsp, bsp],
        out_shape=[jax.ShapeDtypeStruct((t, a), F32), jax.ShapeDtypeStruct((t, a), F32),
                   jax.ShapeDtypeStruct(w_s.shape, F32), jax.ShapeDtypeStruct(b_col.shape, F32)],
        compiler_params=_cparams(("arbitrary",)),
    )(vn, u, dy, w_s, b_col)


def _fox_scores(q, k, cq_row, ck_row, diag, scale):
    s = lax.dot_general(q.astype(BF16), k.astype(BF16), NT, preferred_element_type=F32) * scale
    s = s + _row_to_col(cq_row) - ck_row
    mask = jnp.logical_or(jnp.logical_not(diag), _tri(q.shape[0]))
    return s, mask


def _fox_fwd(qn, kn, proj, v_blk0, c_rows, bsz, seq, nh, name):
    t = qn.shape[0]
    tq = _pick(seq, 512, LANES)
    nq = seq // tq
    scale = HEAD ** -0.5

    def body(q_ref, k_ref, v_ref, cq_ref, ck_ref, o_ref, lse_ref, m_s, l_s, acc_s):
        i, j = pl.program_id(2), pl.program_id(3)

        @pl.when(j == 0)
        def _():
            m_s[...] = jnp.full_like(m_s, -jnp.inf)
            l_s[...] = jnp.zeros_like(l_s)
            acc_s[...] = jnp.zeros_like(acc_s)

        @pl.when(j <= i)
        def _():
            s, mask = _fox_scores(q_ref[...], k_ref[...], cq_ref[...], ck_ref[...], j == i, scale)
            s = jnp.where(mask, s, -jnp.inf)
            m_new = jnp.maximum(m_s[...], jnp.max(s, axis=1, keepdims=True))
            p = jnp.exp(s - m_new)
            alpha = jnp.exp(m_s[...] - m_new)
            l_s[...] = alpha * l_s[...] + jnp.sum(p, axis=1, keepdims=True)
            acc_s[...] = alpha * acc_s[...] + lax.dot_general(p.astype(BF16), v_ref[...].astype(BF16), NN, preferred_element_type=F32)
            m_s[...] = m_new

        @pl.when(j == nq - 1)
        def _():
            o_ref[...] = acc_s[...] / l_s[...]
            lse_ref[...] = jnp.broadcast_to(m_s[...] + jnp.log(l_s[...]), lse_ref.shape)

    qspec = pl.BlockSpec((tq, HEAD), lambda b, h, i, j: (b * nq + i, h))
    kspec = pl.BlockSpec((tq, HEAD), lambda b, h, i, j: (b * nq + jnp.minimum(i, j), h))
    vspec = pl.BlockSpec((tq, HEAD), lambda b, h, i, j: (b * nq + jnp.minimum(i, j), v_blk0 + h))
    cq = pl.BlockSpec((None, None, 1, tq), lambda b, h, i, j: (b, h, 0, i))
    ck = pl.BlockSpec((None, None, 1, tq), lambda b, h, i, j: (b, h, 0, jnp.minimum(i, j)))
    return pl.pallas_call(
        body, name=name, grid=(bsz, nh, nq, nq), in_specs=[qspec, kspec, vspec, cq, ck], out_specs=[qspec, qspec],
        out_shape=[jax.ShapeDtypeStruct((t, nh * HEAD), F32)] * 2,
        scratch_shapes=[pltpu.VMEM((tq, 1), F32), pltpu.VMEM((tq, 1), F32), pltpu.VMEM((tq, HEAD), F32)],
        compiler_params=_cparams(("parallel", "parallel", "parallel", "arbitrary")),
    )(qn, kn, proj, c_rows, c_rows)


def _fox_p_dp(q, k, v, do, lse, cq_row, ck_row, diag, scale):
    s, mask = _fox_scores(q, k, cq_row, ck_row, diag, scale)
    p = jnp.where(mask, jnp.exp(s - jnp.max(lse, axis=1, keepdims=True)), 0.0)
    dp = lax.dot_general(do.astype(BF16), v.astype(BF16), NT, preferred_element_type=F32)
    return p, dp


def _fox_bwd_q(qn, kn, proj, v_blk0, do, lse, c_rows, bsz, seq, nh, name):
    t = qn.shape[0]
    tq = _pick(seq, 512, LANES)
    nq = seq // tq
    scale = HEAD ** -0.5

    def key_block(jj):
        return jnp.where(jj >= nq, jj - nq, jj)

    def body(q_ref, k_ref, v_ref, do_ref, lse_ref, cq_ref, ck_ref, dq_ref, dl_ref, dq_s, dl_s):
        i, jj = pl.program_id(2), pl.program_id(3)
        j = key_block(jj)

        @pl.when(jj == 0)
        def _():
            dq_s[...] = jnp.zeros_like(dq_s)
            dl_s[...] = jnp.zeros_like(dl_s)

        @pl.when(j <= i)
        def _():
            p, dp = _fox_p_dp(q_ref[...], k_ref[...], v_ref[...], do_ref[...], lse_ref[...], cq_ref[...], ck_ref[...], j == i, scale)

            @pl.when(jj < nq)
            def _():
                dl_s[...] += jnp.sum(p * dp, axis=1, keepdims=True)

            @pl.when(jj >= nq)
            def _():
                ds = p * (dp - dl_s[...])
                dq_s[...] += lax.dot_general(ds.astype(BF16), k_ref[...].astype(BF16), NN, preferred_element_type=F32) * scale

        @pl.when(jj == 2 * nq - 1)
        def _():
            dq_ref[...] = dq_s[...]
            dl_ref[...] = jnp.broadcast_to(dl_s[...], dl_ref.shape)

    qspec = pl.BlockSpec((tq, HEAD), lambda b, h, i, jj: (b * nq + i, h))
    kspec = pl.BlockSpec((tq, HEAD), lambda b, h, i, jj: (b * nq + jnp.minimum(i, key_block(jj)), h))
    vspec = pl.BlockSpec((tq, HEAD), lambda b, h, i, jj: (b * nq + jnp.minimum(i, key_block(jj)), v_blk0 + h))
    cq = pl.BlockSpec((None, None, 1, tq), lambda b, h, i, jj: (b, h, 0, i))
    ck = pl.BlockSpec((None, None, 1, tq), lambda b, h, i, jj: (b, h, 0, jnp.minimum(i, key_block(jj))))
    return pl.pallas_call(
        body, name=name, grid=(bsz, nh, nq, 2 * nq), in_specs=[qspec, kspec, vspec, qspec, qspec, cq, ck],
        out_specs=[qspec, qspec], out_shape=[jax.ShapeDtypeStruct((t, nh * HEAD), F32)] * 2,
        scratch_shapes=[pltpu.VMEM((tq, HEAD), F32), pltpu.VMEM((tq, 1), F32)],
        compiler_params=_cparams(("parallel", "parallel", "parallel", "arbitrary")),
    )(qn, kn, proj, do, lse, c_rows, c_rows)


def _fox_bwd_kv(qn, kn, proj, v_blk0, do, lse, delta, c_rows, bsz, seq, nh, name):
    t = qn.shape[0]
    tq = _pick(seq, 512, LANES)
    nq = seq // tq
    scale = HEAD ** -0.5

    def body(q_ref, k_ref, v_ref, do_ref, lse_ref, dl_ref, cq_ref, ck_ref, dk_ref, dv_ref, dc_ref, dk_s, dv_s, dc_s):
        j, i = pl.program_id(2), pl.program_id(3)

        @pl.when(i == 0)
        def _():
            dk_s[...] = jnp.zeros_like(dk_s)
            dv_s[...] = jnp.zeros_like(dv_s)
            dc_s[...] = jnp.zeros_like(dc_s)

        @pl.when(i >= j)
        def _():
            p, dp = _fox_p_dp(q_ref[...], k_ref[...], v_ref[...], do_ref[...], lse_ref[...], cq_ref[...], ck_ref[...], j == i, scale)
            ds = p * (dp - jnp.max(dl_ref[...], axis=1, keepdims=True))
            dv_s[...] += lax.dot_general(p.astype(BF16), do_ref[...].astype(BF16), TN, preferred_element_type=F32)
            dk_s[...] += lax.dot_general(ds.astype(BF16), q_ref[...].astype(BF16), TN, preferred_element_type=F32) * scale
            dc_s[...] -= jnp.sum(ds, axis=0, keepdims=True)

        @pl.when(i == nq - 1)
        def _():
            dk_ref[...] = dk_s[...]
            dv_ref[...] = dv_s[...]
            dc_ref[...] = dc_s[...]

    kspec = pl.BlockSpec((tq, HEAD), lambda b, h, j, i: (b * nq + j, h))
    vspec = pl.BlockSpec((tq, HEAD), lambda b, h, j, i: (b * nq + j, v_blk0 + h))
    qspec = pl.BlockSpec((tq, HEAD), lambda b, h, j, i: (b * nq + jnp.maximum(i, j), h))
    cq = pl.BlockSpec((None, None, 1, tq), lambda b, h, j, i: (b, h, 0, jnp.maximum(i, j)))
    ck = pl.BlockSpec((None, None, 1, tq), lambda b, h, j, i: (b, h, 0, j))
    return pl.pallas_call(
        body, name=name, grid=(bsz, nh, nq, nq), in_specs=[qspec, kspec, vspec, qspec, qspec, qspec, cq, ck],
        out_specs=[kspec, kspec, ck],
        out_shape=[jax.ShapeDtypeStruct((t, nh * HEAD), F32)] * 2 + [jax.ShapeDtypeStruct(c_rows.shape, F32)],
        scratch_shapes=[pltpu.VMEM((tq, HEAD), F32), pltpu.VMEM((tq, HEAD), F32), pltpu.VMEM((1, tq), F32)],
        compiler_params=_cparams(("parallel", "parallel", "parallel", "arbitrary")),
    )(qn, kn, proj, do, lse, delta, c_rows, c_rows)


@jax.custom_vjp
def _unit_lower_inverse(a_mat):
    c = a_mat.shape[-1]
    inv = _eye(c) - a_mat
    pw = _bdot(a_mat, a_mat, BNN)
    n_sq = max(1, (c - 1).bit_length() - 1)
    for it in range(n_sq):
        inv = inv + _bdot(inv, pw, BNN)
        if it < n_sq - 1:
            pw = _bdot(pw, pw, BNN)
    return inv


def _unit_lower_inverse_fwd(a_mat):
    inv = _unit_lower_inverse(a_mat)
    return inv, inv


def _unit_lower_inverse_bwd(inv, d_inv):
    return (-_bdot(_bdot(inv, d_inv, BTN), inv, BNT),)


_unit_lower_inverse.defvjp(_unit_lower_inverse_fwd, _unit_lower_inverse_bwd)


def _gdn_chunk(qp, kp, vp, b_row, a_row, alog, dtb, state):
    hv = vp.shape[0]
    c = qp.shape[1]
    qc, kc, vc = _silu(qp), _silu(kp), _silu(vp)
    qh = qc * lax.rsqrt(jnp.sum(qc * qc, -1, keepdims=True) + RMS_EPS) * (HEAD ** -0.5)
    kh = kc * lax.rsqrt(jnp.sum(kc * kc, -1, keepdims=True) + RMS_EPS)
    q = jnp.stack([qh[h // 2] for h in range(hv)])
    k = jnp.stack([kh[h // 2] for h in range(hv)])
    beta_row = jax.nn.sigmoid(b_row)
    g_row = -jnp.exp(alog) * _softplus(a_row + dtb)
    ri = lax.broadcasted_iota(jnp.int32, (c, c), 0)
    ci = lax.broadcasted_iota(jnp.int32, (c, c), 1)
    eye = (ri == ci).astype(F32)
    tri = ri >= ci
    beta_col = jnp.sum(eye * beta_row, axis=2, keepdims=True)
    g_col = jnp.sum(eye * g_row, axis=2, keepdims=True)
    gc_col = jnp.sum(tri.astype(F32) * g_row, axis=2, keepdims=True)
    gc_row = jnp.sum(g_col * (ri <= ci).astype(F32), axis=1, keepdims=True)
    decay = jnp.where(tri, jnp.exp(jnp.where(tri, gc_col - gc_row, 0.0)), 0.0)
    kb = k * beta_col
    a_mat = jnp.where(ri > ci, _bdot(kb, k, BNT) * decay, 0.0)
    egc = jnp.exp(gc_col)
    inv = _unit_lower_inverse(a_mat)
    u = _bdot(inv, vc * beta_col, BNN)
    w = _bdot(inv, kb * egc, BNN)
    attn = _bdot(q, k, BNT) * decay
    g_last = jnp.sum(g_row, axis=2, keepdims=True)
    v_new = u - _bdot(w, state, BNN)
    o = _bdot(q * egc, state, BNN) + _bdot(attn, v_new, BNN)
    new_state = state * jnp.exp(g_last) + _bdot(k * jnp.exp(g_last - gc_col), v_new, BTN)
    return o, new_state


def _gdn_group(nvh):
    return 8 if nvh % 8 == 0 else (4 if nvh % 4 == 0 else 2)


def _gdn_specs(nch, nkh, hb, rev):
    def n_of(n):
        return nch - 1 - n if rev else n

    hk = hb // 2
    per_k = pl.BlockSpec((GDN_CHUNK, hk * HEAD), lambda g, b, n: (b * nch + n_of(n), g))
    q = per_k
    k = pl.BlockSpec((GDN_CHUNK, hk * HEAD), lambda g, b, n: (b * nch + n_of(n), nkh // hk + g))
    v = pl.BlockSpec((GDN_CHUNK, hb * HEAD), lambda g, b, n: (b * nch + n_of(n), 2 * nkh // hb + g))
    per_v = pl.BlockSpec((GDN_CHUNK, hb * HEAD), lambda g, b, n: (b * nch + n_of(n), g))
    row = pl.BlockSpec((hb, None, None, 1, GDN_CHUNK), lambda g, b, n: (g, b, n_of(n), 0, 0))
    sc = pl.BlockSpec((hb, 1, 1), lambda g, b, n: (g, 0, 0))
    st = pl.BlockSpec((hb, None, None, HEAD, HEAD), lambda g, b, n: (g, b, n_of(n), 0, 0))
    return q, k, v, per_k, per_v, row, sc, st


def _stack_heads(ref, n):
    return jnp.stack([ref[:, h * HEAD:(h + 1) * HEAD] for h in range(n)])


def _gdn_fwd(conv, b_rows, a_rows, alog, dtb, bsz, seq, nvh, name):
    t = conv.shape[0]
    nch = seq // GDN_CHUNK
    nkh = nvh // 2
    hb = _gdn_group(nvh)
    q, k, v, _, per_v, row, sc, st = _gdn_specs(nch, nkh, hb, False)

    def body(q_ref, k_ref, v_ref, b_ref, a_ref, al_ref, dt_ref, o_ref, st_ref, state):
        @pl.when(pl.program_id(2) == 0)
        def _():
            state[...] = jnp.zeros_like(state)

        st_ref[...] = state[...]
        o, new_state = _gdn_chunk(_stack_heads(q_ref, hb // 2), _stack_heads(k_ref, hb // 2), _stack_heads(v_ref, hb),
                                  b_ref[...], a_ref[...], al_ref[...], dt_ref[...], state[...])
        for h in range(hb):
            o_ref[:, h * HEAD:(h + 1) * HEAD] = o[h]
        state[...] = new_state

    return pl.pallas_call(
        body, name=name, grid=(nvh // hb, bsz, nch), in_specs=[q, k, v, row, row, sc, sc], out_specs=[per_v, st],
        out_shape=[jax.ShapeDtypeStruct((t, nvh * HEAD), F32), jax.ShapeDtypeStruct((nvh, bsz, nch, HEAD, HEAD), F32)],
        scratch_shapes=[pltpu.VMEM((hb, HEAD, HEAD), F32)],
        compiler_params=_cparams(("parallel", "parallel", "arbitrary")),
    )(conv, conv, conv, b_rows, a_rows, alog, dtb)


def _gdn_bwd(conv, b_rows, a_rows, alog, dtb, states, do, bsz, seq, nvh, name):
    t = conv.shape[0]
    nch = seq // GDN_CHUNK
    nkh = nvh // 2
    hb = _gdn_group(nvh)
    q, k, v, per_k, per_v, row, sc, st = _gdn_specs(nch, nkh, hb, True)

    def body(q_ref, k_ref, v_ref, b_ref, a_ref, al_ref, dt_ref, st_ref, do_ref,
             dq_ref, dk_ref, dv_ref, db_ref, da_ref, dal_ref, ddt_ref, dstate):
        b, n = pl.program_id(1), pl.program_id(2)

        @pl.when(n == 0)
        def _():
            dstate[...] = jnp.zeros_like(dstate)

        _, vjp = jax.vjp(_gdn_chunk, _stack_heads(q_ref, hb // 2), _stack_heads(k_ref, hb // 2), _stack_heads(v_ref, hb),
                         b_ref[...], a_ref[...], al_ref[...], dt_ref[...], st_ref[...])
        dq, dk, dv, db, da, dal, ddt, dst = vjp((_stack_heads(do_ref, hb), dstate[...]))
        for h in range(hb // 2):
            dq_ref[:, h * HEAD:(h + 1) * HEAD] = dq[h]
            dk_ref[:, h * HEAD:(h + 1) * HEAD] = dk[h]
        for h in range(hb):
            dv_ref[:, h * HEAD:(h + 1) * HEAD] = dv[h]
        db_ref[...] = db
        da_ref[...] = da
        dstate[...] = dst
        start = jnp.logical_and(b == 0, n == 0)

        @pl.when(start)
        def _():
            dal_ref[...] = dal
            ddt_ref[...] = ddt

        @pl.when(jnp.logical_not(start))
        def _():
            dal_ref[...] += dal
            ddt_ref[...] += ddt

    f = lambda *s: jax.ShapeDtypeStruct(s, F32)
    return pl.pallas_call(
        body, name=name, grid=(nvh // hb, bsz, nch), in_specs=[q, k, v, row, row, sc, sc, st, per_v],
        out_specs=[per_k, per_k, per_v, row, row, sc, sc],
        out_shape=[f(t, nkh * HEAD), f(t, nkh * HEAD), f(t, nvh * HEAD), f(*b_rows.shape), f(*a_rows.shape), f(nvh, 1, 1), f(nvh, 1, 1)],
        scratch_shapes=[pltpu.VMEM((hb, HEAD, HEAD), F32)],
        compiler_params=_cparams(("arbitrary", "arbitrary", "arbitrary")),
    )(conv, conv, conv, b_rows, a_rows, alog, dtb, states, do)


ANY = pl.BlockSpec(memory_space=pl.ANY)
FLIPS = (2, 1, 3)


def _place():
    x, y, c = lax.axis_index("x"), lax.axis_index("y"), lax.axis_index("c")
    chips = [(1 - x, y), (x, 1 - y), (1 - x, 1 - y)]
    return x, y, c, chips


def _chip_index():
    return 2 * lax.axis_index("x") + lax.axis_index("y")


def _core_index():
    return lax.axis_index("c")


def _wide(w):
    return w if (w <= 4096 or w % LANES) else _pick(w, 2048, LANES)


def _rows(start, size, mult=16):
    return pl.ds(pl.multiple_of(start, mult), size)


def _cast_window(w_stack, layer, cls, name):
    _, r, w = w_stack.shape
    if cls == "U":
        tr = _pick(r, 256, 16)
        grid = (r // tr, 1)
        in_spec = pl.BlockSpec((None, tr, w), lambda i, j: (layer, i, 0))
        out_spec = pl.BlockSpec((None, tr, w), lambda i, j: (_chip_index(), i, 0))
        out_shape = (N_CHIPS, r, w)
    else:
        tr, tc = _pick(r, 512, 16), _wide(w)
        nbr, nbc = r // tr, w // tc
        grid = (nbr, nbc)
        in_spec = pl.BlockSpec((None, tr, tc), lambda i, j: (layer, i, j))
        if cls == "C":
            out_spec = pl.BlockSpec((tr, tc), lambda i, j: (i, _chip_index() * nbc + j))
            out_shape = (r, N_CHIPS * w)
        else:
            out_spec = pl.BlockSpec((tr, tc), lambda i, j: (_chip_index() * nbr + i, j))
            out_shape = (N_CHIPS * r, w)

    def body(x_ref, o_ref):
        o_ref[...] = x_ref[...].astype(o_ref.dtype)

    return pl.pallas_call(body, name=name, grid=grid, in_specs=[in_spec], out_specs=out_spec,
                          out_shape=jax.ShapeDtypeStruct(out_shape, BF16), compiler_params=_cparams(("parallel", "parallel")))(w_stack)


def _halved(buf, cls):
    return {"C": buf.shape[0], "U": buf.shape[1], "R": buf.shape[0] // N_CHIPS}[cls]


def _part(buf, cls, chip, start, size):
    if cls == "C":
        w = buf.shape[1] // N_CHIPS
        return buf.at[_rows(start, size), pl.ds(chip * w, w)]
    if cls == "U":
        return buf.at[chip, _rows(start, size), :]
    r = buf.shape[0] // N_CHIPS
    return buf.at[_rows(chip * r + start, size), :]


def _gather_layer(bufs, classes, name):
    n = len(bufs)
    job = _gather_job(bufs, classes)

    def body(*refs):
        outs = refs[n:2 * n]
        job["start"](refs[:n], outs, refs[2 * n], refs[2 * n + 1], True)
        job["finish"](refs[:n], outs, refs[2 * n], refs[2 * n + 1], True)

    return pl.pallas_call(
        body, name=name, in_specs=[ANY] * n, out_specs=[ANY] * n, out_shape=job["out_shapes"],
        input_output_aliases={t: t for t in range(n)},
        scratch_shapes=[pltpu.SemaphoreType.DMA((job["n_sems"],)), pltpu.SemaphoreType.DMA((job["n_sems"],))],
    )(*bufs)


def _on_chip(when, fn):
    x, y, _, _ = _place()
    me = 2 * x + y
    for s in range(N_CHIPS):
        cond = (me == s) if when is True else jnp.logical_and(when, me == s)
        pl.when(cond)(functools.partial(fn, s))


def _gather_job(bufs, classes):
    n = len(bufs)

    def copy(outs, send_sems, recv_sems, t, k, chip, start, to):
        size = _halved(outs[t], classes[t]) // 2
        part = _part(outs[t], classes[t], chip, start, size)
        return pltpu.make_async_remote_copy(src_ref=part, dst_ref=part, send_sem=send_sems.at[6 * t + k],
                                            recv_sem=recv_sems.at[6 * t + k], device_id=to, device_id_type=MESH)

    def halves(outs):
        return [_halved(outs[t], classes[t]) // 2 for t in range(n)]

    def start(ins, outs, send_sems, recv_sems, when):
        x, y, c, chips = _place()

        def run(s_me):
            for t, half in enumerate(halves(outs)):
                for k, (cx, cy) in enumerate(chips):
                    copy(outs, send_sems, recv_sems, t, k, s_me, c * half, (cx, cy, c)).start()
        _on_chip(when, run)

    def finish(ins, outs, send_sems, recv_sems, when):
        x, y, c, chips = _place()

        def run(s_me):
            passed = []
            for t, half in enumerate(halves(outs)):
                for k in range(3):
                    copy(outs, send_sems, recv_sems, t, k, s_me ^ FLIPS[k], c * half, (x, y, c)).wait_recv()
                    fwd = copy(outs, send_sems, recv_sems, t, 3 + k, s_me ^ FLIPS[k], c * half, (x, y, 1 - c))
                    fwd.start()
                    passed.append(fwd)
            for t, half in enumerate(halves(outs)):
                for k in range(3):
                    copy(outs, send_sems, recv_sems, t, 3 + k, s_me ^ FLIPS[k], (1 - c) * half, (x, y, c)).wait_recv()
            for t, half in enumerate(halves(outs)):
                for k, (cx, cy) in enumerate(chips):
                    copy(outs, send_sems, recv_sems, t, k, s_me, c * half, (cx, cy, c)).wait_send()
            for fwd in passed:
                fwd.wait_send()
        _on_chip(when, run)

    return dict(ins=list(bufs), alias=True, n_sems=6 * n, start=start, finish=finish,
                out_shapes=[jax.ShapeDtypeStruct(b.shape, b.dtype) for b in bufs])


def _assemble(slots, width, name):
    _, r, w = slots.shape
    tr = _pick(r, 256, 16)

    def body(s_ref, o_ref):
        parts = [s_ref[s] for s in range(N_CHIPS)]
        if width > N_CHIPS * w:
            parts.append(jnp.zeros((tr, width - N_CHIPS * w), slots.dtype))
        o_ref[...] = jnp.concatenate(parts, axis=1)

    return pl.pallas_call(
        body, name=name, grid=(r // tr,), in_specs=[pl.BlockSpec((N_CHIPS, tr, w), lambda i: (0, i, 0))],
        out_specs=pl.BlockSpec((tr, width), lambda i: (i, 0)), out_shape=jax.ShapeDtypeStruct((r, width), slots.dtype),
        compiler_params=_cparams(("parallel",)),
    )(slots)


def _rs_pair(dws, classes, name):
    n = len(dws)

    def shape_of(d, cls):
        return (N_CHIPS, d.shape[1] // 2, d.shape[2]) if cls == "R" else (d.shape[0] // 2, d.shape[1])

    def body(*refs):
        ins, outs = refs[:n], refs[n:2 * n]
        send_sems, recv_sems = refs[2 * n], refs[2 * n + 1]
        x, y, c, _ = _place()
        cps = []
        for t in range(n):
            if classes[t] == "R":
                h = ins[t].shape[1] // 2
                src = ins[t].at[:, _rows((1 - c) * h, h), :]
            else:
                h = ins[t].shape[0] // 2
                src = ins[t].at[_rows((1 - c) * h, h), :]
            cp = pltpu.make_async_remote_copy(src_ref=src, dst_ref=outs[t], send_sem=send_sems.at[t], recv_sem=recv_sems.at[t],
                                              device_id=(x, y, 1 - c), device_id_type=MESH)
            cp.start()
            cps.append(cp)
        for cp in cps:
            cp.wait()

    return pl.pallas_call(
        body, name=name, in_specs=[ANY] * n, out_specs=[ANY] * n,
        out_shape=[jax.ShapeDtypeStruct(shape_of(d, cls), d.dtype) for d, cls in zip(dws, classes)],
        scratch_shapes=[pltpu.SemaphoreType.DMA((n,)), pltpu.SemaphoreType.DMA((n,))],
    )(*dws)


def _rs_add(dw, got, cls, w, name):
    if cls == "R":
        _, r, _ = dw.shape
        h = r // 2
        th, tc = _pick(h, 256, 16), _wide(w)
        nbh = h // th
        grid = (N_CHIPS, nbh, w // tc)
        in_specs = [pl.BlockSpec((None, th, tc), lambda s, i, j: (s, _core_index() * nbh + i, j)),
                    pl.BlockSpec((None, th, tc), lambda s, i, j: (s, i, j))]
        out_spec = pl.BlockSpec((None, th, tc), lambda s, i, j: (s, i, j))
        sem = ("parallel", "parallel", "parallel")

        def body(a_ref, b_ref, o_ref):
            o_ref[...] = (a_ref[...].astype(F32) + b_ref[...].astype(F32)).astype(o_ref.dtype)
    elif cls == "C":
        r = dw.shape[0]
        h = r // 2
        th, tc = _pick(h, 256, 16), _wide(w)
        nbh, nbc = h // th, w // tc
        grid = (N_CHIPS, nbh, nbc)
        in_specs = [pl.BlockSpec((th, tc), lambda s, i, j: (_core_index() * nbh + i, s * nbc + j)),
                    pl.BlockSpec((th, tc), lambda s, i, j: (i, s * nbc + j))]
        out_spec = pl.BlockSpec((None, th, tc), lambda s, i, j: (s, i, j))
        sem = ("parallel", "parallel", "parallel")

        def body(a_ref, b_ref, o_ref):
            o_ref[...] = (a_ref[...].astype(F32) + b_ref[...].astype(F32)).astype(o_ref.dtype)
    else:
        r, wp = dw.shape
        h = r // 2
        th = _pick(h, 64, 16)
        nbh = h // th
        grid = (nbh,)
        in_specs = [pl.BlockSpec((th, wp), lambda i: (_core_index() * nbh + i, 0)), pl.BlockSpec((th, wp), lambda i: (i, 0))]
        out_spec = pl.BlockSpec((N_CHIPS, th, w), lambda i: (0, i, 0))
        sem = ("parallel",)

        def body(a_ref, b_ref, o_ref):
            tot = a_ref[...].astype(F32) + b_ref[...].astype(F32)
            for s in range(N_CHIPS):
                o_ref[s] = tot[:, s * w:(s + 1) * w].astype(o_ref.dtype)

    return pl.pallas_call(body, name=name, grid=grid, in_specs=in_specs, out_specs=out_spec,
                          out_shape=jax.ShapeDtypeStruct((N_CHIPS, h, w), BF16), compiler_params=_cparams(sem))(dw, got)


def _rs_chip(zs, name):
    n = len(zs)
    job = _chip_job(zs)

    def body(*refs):
        job["start"](refs[:n], refs[n:2 * n], refs[2 * n], refs[2 * n + 1], True)
        job["finish"](refs[:n], refs[n:2 * n], refs[2 * n], refs[2 * n + 1], True)

    return pl.pallas_call(
        body, name=name, in_specs=[ANY] * n, out_specs=[ANY] * n, out_shape=job["out_shapes"],
        scratch_shapes=[pltpu.SemaphoreType.DMA((job["n_sems"],)), pltpu.SemaphoreType.DMA((job["n_sems"],))],
    )(*zs)


def _chip_job(zs):
    n = len(zs)

    def copies(ins, outs, send_sems, recv_sems):
        x, y, c, chips = _place()
        return [pltpu.make_async_remote_copy(src_ref=ins[t].at[2 * cx + cy], dst_ref=outs[t].at[k], send_sem=send_sems.at[3 * t + k],
                                             recv_sem=recv_sems.at[3 * t + k], device_id=(cx, cy, c), device_id_type=MESH)
                for t in range(n) for k, (cx, cy) in enumerate(chips)]

    def start(ins, outs, send_sems, recv_sems, when):
        def run():
            for cp in copies(ins, outs, send_sems, recv_sems):
                cp.start()
        run() if when is True else pl.when(when)(run)

    def finish(ins, outs, send_sems, recv_sems, when):
        def run():
            for cp in copies(ins, outs, send_sems, recv_sems):
                cp.wait()
        run() if when is True else pl.when(when)(run)

    return dict(ins=list(zs), alias=False, n_sems=3 * n, start=start, finish=finish,
                out_shapes=[jax.ShapeDtypeStruct((3,) + z.shape[1:], z.dtype) for z in zs])


def _rs_sum(z, parts, name):
    _, h, w = z.shape
    th = _pick(h, 256, 16)
    tc = _wide(w)
    nbh = h // th

    def body(z_ref, k_ref, o_ref):
        acc = z_ref[...].astype(F32)
        for k in range(3):
            acc = acc + k_ref[k].astype(F32)
        o_ref[...] = acc

    return pl.pallas_call(
        body, name=name, grid=(nbh, w // tc),
        in_specs=[pl.BlockSpec((None, th, tc), lambda i, j: (_chip_index(), i, j)), pl.BlockSpec((3, th, tc), lambda i, j: (0, i, j))],
        out_specs=pl.BlockSpec((th, tc), lambda i, j: (_core_index() * nbh + i, j)),
        out_shape=jax.ShapeDtypeStruct((2 * h, w), F32), compiler_params=_cparams(("parallel", "parallel")))(z, parts)


def _rs_join(bufs, name):
    n = len(bufs)

    def body(*refs):
        outs = refs[n:2 * n]
        send_sems, recv_sems = refs[2 * n], refs[2 * n + 1]
        x, y, c, _ = _place()
        cps = []
        for t in range(n):
            h = outs[t].shape[0] // 2
            mine = outs[t].at[_rows(c * h, h), :]
            cp = pltpu.make_async_remote_copy(src_ref=mine, dst_ref=mine, send_sem=send_sems.at[t], recv_sem=recv_sems.at[t],
                                              device_id=(x, y, 1 - c), device_id_type=MESH)
            cp.start()
            cps.append(cp)
        for t in range(n):
            h = outs[t].shape[0] // 2
            other = outs[t].at[_rows((1 - c) * h, h), :]
            pltpu.make_async_remote_copy(src_ref=other, dst_ref=other, send_sem=send_sems.at[t], recv_sem=recv_sems.at[t],
                                         device_id=(x, y, c), device_id_type=MESH).wait_recv()
        for cp in cps:
            cp.wait_send()

    return pl.pallas_call(
        body, name=name, in_specs=[ANY] * n, out_specs=[ANY] * n,
        out_shape=[jax.ShapeDtypeStruct(b.shape, b.dtype) for b in bufs],
        input_output_aliases={t: t for t in range(n)},
        scratch_shapes=[pltpu.SemaphoreType.DMA((n,)), pltpu.SemaphoreType.DMA((n,))],
    )(*bufs)


def _rs_chip_sums(dws, classes, widths, tag):
    gots = _rs_pair(dws, classes, f"rs_pair_{tag}")
    return [_rs_add(d, g, cls, w, f"rs_add_{tag}_{t}") for t, (d, g, cls, w) in enumerate(zip(dws, gots, classes, widths))]


def _rs_finish(zs, parts, tag):
    halves = [_rs_sum(z, p, f"rs_sum_{tag}_{t}") for t, (z, p) in enumerate(zip(zs, parts))]
    return _rs_join(halves, f"rs_join_{tag}")


def _reduce_scatter_layer(dws, classes, widths, tag):
    zs = _rs_chip_sums(dws, classes, widths, tag)
    return _rs_finish(zs, _rs_chip(zs, f"rs_chip_{tag}"), tag)


def _gather_shards(mine, name):
    r, w = mine.shape
    rh = r // 2

    def body(mine_ref, out_ref, send_sems, recv_sems, local_sem):
        x, y, c, chips = _place()
        me = 2 * x + y
        half = _rows(c * rh, rh)
        other = _rows((1 - c) * rh, rh)

        def copy(k, src, chip, rows, to):
            return pltpu.make_async_remote_copy(src_ref=src, dst_ref=out_ref.at[chip, rows], send_sem=send_sems.at[k],
                                                recv_sem=recv_sems.at[k], device_id=to, device_id_type=MESH)

        local = pltpu.make_async_copy(mine_ref, out_ref.at[me], local_sem)
        local.start()
        sends = [copy(k, mine_ref.at[half], me, half, (cx, cy, c)) for k, (cx, cy) in enumerate(chips)]
        for s in sends:
            s.start()
        passed = []
        for k, (cx, cy) in enumerate(chips):
            chip = 2 * cx + cy
            copy(k, mine_ref.at[half], chip, half, (x, y, c)).wait_recv()
            fwd = copy(3 + k, out_ref.at[chip, half], chip, half, (x, y, 1 - c))
            fwd.start()
            passed.append(fwd)
        for k, (cx, cy) in enumerate(chips):
            copy(3 + k, mine_ref.at[half], 2 * cx + cy, other, (x, y, c)).wait_recv()
        for s in sends + passed:
            s.wait_send()
        local.wait()

    return pl.pallas_call(
        body, name=name, in_specs=[ANY], out_specs=ANY, out_shape=jax.ShapeDtypeStruct((N_CHIPS, r, w), mine.dtype),
        scratch_shapes=[pltpu.SemaphoreType.DMA((6,)), pltpu.SemaphoreType.DMA((6,)), pltpu.SemaphoreType.DMA],
    )(mine)


def _sum_slots(slots, name):
    n, r, w = slots.shape
    tr = _pick(r, 256, SUBLANES)

    def body(s_ref, o_ref):
        acc = s_ref[0]
        for k in range(1, n):
            acc = acc + s_ref[k]
        o_ref[...] = acc

    return pl.pallas_call(
        body, name=name, grid=(r // tr,), in_specs=[pl.BlockSpec((n, tr, w), lambda i: (0, i, 0))],
        out_specs=pl.BlockSpec((tr, w), lambda i: (i, 0)), out_shape=jax.ShapeDtypeStruct((r, w), F32),
        compiler_params=_cparams(("parallel",)),
    )(slots)


def _all_to_all_sum(flat, name):
    r, w = flat.shape

    def body(f_ref, out_ref, send_sems, recv_sems, local_sem):
        x, y, c, _ = _place()
        me = 4 * x + 2 * y + c
        local = pltpu.make_async_copy(f_ref, out_ref.at[me], local_sem)
        local.start()
        sends = []
        for k in range(1, 8):
            peer = (x ^ (k >> 2), y ^ ((k >> 1) & 1), c ^ (k & 1))
            s = pltpu.make_async_remote_copy(src_ref=f_ref, dst_ref=out_ref.at[me], send_sem=send_sems.at[k - 1],
                                             recv_sem=recv_sems.at[k - 1], device_id=peer, device_id_type=MESH)
            s.start()
            sends.append(s)
        for k in range(1, 8):
            peer_slot = 4 * (x ^ (k >> 2)) + 2 * (y ^ ((k >> 1) & 1)) + (c ^ (k & 1))
            pltpu.make_async_remote_copy(src_ref=f_ref, dst_ref=out_ref.at[peer_slot], send_sem=send_sems.at[k - 1],
                                         recv_sem=recv_sems.at[k - 1], device_id=(x, y, c), device_id_type=MESH).wait_recv()
        for s in sends:
            s.wait_send()
        local.wait()

    slots = pl.pallas_call(
        body, name=name, in_specs=[ANY], out_specs=ANY, out_shape=jax.ShapeDtypeStruct((8, r, w), flat.dtype),
        scratch_shapes=[pltpu.SemaphoreType.DMA((7,)), pltpu.SemaphoreType.DMA((7,)), pltpu.SemaphoreType.DMA],
    )(flat)
    return _sum_slots(slots, name + "_sum")


def _pack(pieces, lead):
    flat = []
    n_lead = len(lead)
    for p in pieces:
        f = p.reshape(*lead, -1)
        pad = (-f.shape[-1]) % PACK_COLS
        if pad:
            f = jnp.pad(f, [(0, 0)] * n_lead + [(0, pad)])
        flat.append(f)
    f = jnp.concatenate(flat, axis=-1) if len(flat) > 1 else flat[0]
    pad = (-f.shape[-1]) % (32 * PACK_COLS)
    if pad:
        f = jnp.pad(f, [(0, 0)] * n_lead + [(0, pad)])
    return f.reshape(*lead, -1, PACK_COLS)


def _unpack(buf, shapes, lead):
    f = buf.reshape(*lead, -1)
    out, off = [], 0
    for shp in shapes:
        n = 1
        for s in shp:
            n *= s
        out.append(f[..., off:off + n].reshape(*lead, *shp))
        off += n + ((-n) % PACK_COLS)
    return out


def _join(g, axis):
    g = jnp.moveaxis(g, 0, axis)
    return g.reshape(*g.shape[:axis], g.shape[axis] * g.shape[axis + 1], *g.shape[axis + 2:])


def _split(full, axis):
    shp = full.shape
    g = full.reshape(*shp[:axis], N_CHIPS, shp[axis] // N_CHIPS, *shp[axis + 1:])
    return jnp.moveaxis(g, axis, 0)


def _norm_bwd(xs, dres, dhs, gain, name):
    def fn(x, dr, *rest):
        dh = rest[0]
        for d in rest[1:-1]:
            dh = dh + d
        _, vjp = jax.vjp(_rms, x, rest[-1])
        dx, dg = vjp(dh.astype(F32))
        return dr + dx, dg
    return _rowmap(fn, [xs, dres] + list(dhs), [gain], name, n_acc=1)


def _mm(a, b, kind, dtype, name, job):
    if job is None:
        return _matmul(a, b, kind, dtype, name), None
    return _matmul(a, b, kind, dtype, name, job=job)


def _ffn_fwd(x1, h2, w, seq, tag, jobs=(None, None, None)):
    gpre, j0 = _mm(h2, w["ffn_w_gate"], "nn", F32, f"ffn_gate_{tag}", jobs[0])
    up, j1 = _mm(h2, w["ffn_w_up"], "nn", BF16, f"ffn_up_{tag}", jobs[1])
    gate = _conv_fwd(gpre, gpre.shape[1], w["ffn_conv_w"], w["ffn_conv_b"].reshape(1, -1), seq, f"ffn_conv_{tag}")
    act, = _rowmap(lambda g, u: (_silu(g) * u).astype(BF16), [gate, up], [], f"ffn_act_{tag}", tc=_pick(gate.shape[1], 1024, LANES))
    wd = w["ffn_w_down"] if "ffn_w_down" in w else j0[0]
    f, j2 = _mm(act, wd, "nn", F32, f"ffn_down_{tag}", jobs[2])
    return f, (gpre, up, gate, act), [j0, j1, j2]


def _ffn_bwd(dx2, h2, saved, wg, wu, conv_w, wd, seq, tag, jobs=(None,) * 5):
    gpre, up, gate, act = saved
    da, j0 = _mm(dx2, wd, "nt", BF16, f"ffn_dact_{tag}", jobs[0])
    d_wd, j1 = _mm(act, dx2, "tn", BF16,f"ffn_dwd_{tag}", jobs[1])

    def act_bwd(g, u, d):
        _, vjp = jax.vjp(lambda g_, u_: _silu(g_) * u_, g, u)
        return vjp(d.astype(F32))
    dgate, dup = _rowmap(act_bwd, [gate, up, da], [], f"ffn_dactfn_{tag}", tc=_pick(gate.shape[1], 1024, LANES))
    dgpre, d_cw, d_cb = _conv_bwd(dgate, gpre, gpre.shape[1], conv_w, seq, True, f"ffn_dconv_{tag}", dx_dtype=BF16)
    dh_a, j2 = _mm(dgpre, wg, "nt", F32, f"ffn_dh_gate_{tag}", jobs[2])
    dh_b, j3 = _mm(dup, wu, "nt", F32, f"ffn_dh_up_{tag}", jobs[3])
    d_wg, j4 = _mm(h2, dgpre, "tn", BF16,f"ffn_dwg_{tag}", jobs[4])
    d_wu = _matmul(h2, dup, "tn", BF16,f"ffn_dwu_{tag}")
    grads = dict(ffn_w_gate=d_wg, ffn_w_up=d_wu, ffn_conv_w=d_cw, ffn_conv_b=d_cb.reshape(-1), ffn_w_down=d_wd)
    return (dh_a, dh_b), grads, [j0, j1, j2, j3, j4]


def _gmlp_fwd(h, w, tag, jobs=(None, None)):
    a = w["a_w_out"].shape[0]
    p, j0 = _mm(h, w["a_w_in"], "nn", F32, f"a_in_{tag}", jobs[0])
    b_in, vnorm = w["a_b_in"].reshape(1, -1), w["a_v_norm"].reshape(1, -1)

    def fn(p_, b_, g_):
        hh = _gelu_tanh(p_ + b_)
        return hh[:, :a], _rms(hh[:, a:], g_)
    u, vn = _rowmap(fn, [p], [b_in, vnorm], f"a_gelu_{tag}")
    b_col = w["a_b_s"][:, :, None]
    y = _sgu_fwd(vn, u, w["a_w_s"], b_col, f"a_sgu_{tag}")
    m, j1 = _mm(y, w["a_w_out"], "nn", F32, f"a_out_{tag}", jobs[1])
    return m, (p, u, vn, y, fn, b_in, vnorm, b_col), [j0, j1]


def _gmlp_bwd(dm, h, saved, w, tag):
    p, u, vn, y, fn, b_in, vnorm, b_col = saved
    dy = _matmul(dm, w["a_w_out"], "nt", F32, f"a_dy_{tag}")
    d_wout = _matmul(y, dm, "tn", BF16,f"a_dwout_{tag}")
    dvn, du, d_ws, d_bcol = _sgu_bwd(vn, u, dy, w["a_w_s"], b_col, f"a_dsgu_{tag}")

    def bwd(p_, du_, dvn_, b_, g_):
        _, vjp = jax.vjp(fn, p_, b_, g_)
        return vjp((du_, dvn_))
    dp, d_bin, d_vnorm = _rowmap(bwd, [p, du, dvn], [b_in, vnorm], f"a_dgelu_{tag}", n_acc=2)
    dh = _matmul(dp, w["a_w_in"], "nt", F32, f"a_dh_{tag}")
    d_win = _matmul(h, dp, "tn", BF16,f"a_dwin_{tag}")
    return (dh,), dict(a_w_in=d_win, a_b_in=d_bin.reshape(-1), a_v_norm=d_vnorm.reshape(-1), a_w_s=d_ws, a_b_s=d_bcol[:, :, 0], a_w_out=d_wout)


def _fox_fwd_mixer(h, w, bsz, seq, tag, jobs=(None, None)):
    d = h.shape[1]
    nh = d // HEAD
    win = w["b_w_in"]
    wp = win.shape[1]
    proj, j0 = _mm(h, win, "nn", F32, f"b_in_{tag}", jobs[0])
    gq, gk = w["b_q_norm"].reshape(1, HEAD), w["b_k_norm"].reshape(1, HEAD)
    bf = jnp.pad(w["b_b_f"].reshape(1, nh), ((0, 0), (0, LANES - nh)))

    def prep(pq, pk, pfl, gq_, gk_, bf_):
        qn = jnp.concatenate([_rms(x, gq_) for x in _heads(pq)], axis=1)
        kn = jnp.concatenate([_rms(x, gk_) for x in _heads(pk)], axis=1)
        return qn, kn, -_softplus(-(pfl + bf_))
    wins = [_win(proj, d, 0), _win(proj, d, 1), _win(proj, LANES, 4 * d // LANES)]

    def prep_fwd(pq, pk, pfl, gq_, gk_, bf_):
        qn, kn, lf = prep(pq, pk, pfl, gq_, gk_, bf_)
        return qn.astype(BF16), kn.astype(BF16), lf
    qn, kn, lf = _rowmap(prep_fwd, wins, [gq, gk, bf], f"b_prep_{tag}")
    cs = _cumsum_rows(lf, seq, False, f"b_cumsum_{tag}")
    c_rows = cs[:, :nh].reshape(bsz, seq, nh).transpose(0, 2, 1)[:, :, None, :]
    o, lse = _fox_fwd(qn, kn, proj, 2 * nh, c_rows, bsz, seq, nh, f"b_attn_{tag}")
    og = _win(proj, d, 3)
    y, = _rowmap(lambda o_, g_: (o_ * jax.nn.sigmoid(g_)).astype(BF16), [o, og], [], f"b_gate_{tag}")
    m, j1 = _mm(y, w["b_w_out"], "nn", F32, f"b_out_{tag}", jobs[1])
    return m, (proj, qn, kn, c_rows, o, lse, y, prep, wins, (gq, gk, bf), wp), [j0, j1]


def _fox_bwd_mixer(dm, h, saved, w, bsz, seq, tag):
    proj, qn, kn, c_rows, o, lse, y, prep, wins, (gq, gk, bf), wp = saved
    d = h.shape[1]
    nh = d // HEAD
    dy = _matmul(dm, w["b_w_out"], "nt", F32, f"b_dy_{tag}")
    d_wout = _matmul(y, dm, "tn", BF16,f"b_dwout_{tag}")
    og = _win(proj, d, 3)

    def gate_bwd(o_, g_, dy_):
        _, vjp = jax.vjp(lambda a, b: a * jax.nn.sigmoid(b), o_, g_)
        return vjp(dy_)
    do, dog = _rowmap(gate_bwd, [o, og, dy], [], f"b_dgate_{tag}")
    dqn, delta = _fox_bwd_q(qn, kn, proj, 2 * nh, do, lse, c_rows, bsz, seq, nh, f"b_dattn_q_{tag}")
    dkn, dv, dc_rows = _fox_bwd_kv(qn, kn, proj, 2 * nh, do, lse, delta, c_rows, bsz, seq, nh, f"b_dattn_kv_{tag}")
    dc = dc_rows[:, :, 0, :].transpose(0, 2, 1).reshape(bsz * seq, nh)
    dc = jnp.pad(dc, ((0, 0), (0, LANES - nh)))
    dlf = _cumsum_rows(dc, seq, True, f"b_dcumsum_{tag}")
    extra = wp - (4 * d + LANES)

    def prep_bwd(pq, pk, pfl, dqn_, dkn_, dv_, dog_, dlf_, gq_, gk_, bf_):
        _, vjp = jax.vjp(prep, pq, pk, pfl, gq_, gk_, bf_)
        dpq, dpk, dpfl, dgq, dgk, dbf = vjp((dqn_, dkn_, dlf_))
        parts = [dpq, dpk, dv_, dog_, dpfl]
        if extra:
            parts.append(jnp.zeros((pq.shape[0], extra), F32))
        return jnp.concatenate(parts, axis=1), dgq, dgk, dbf
    dproj, d_gq, d_gk, d_bf = _rowmap(prep_bwd, wins + [dqn, dkn, dv, dog, dlf], [gq, gk, bf], f"b_dprep_{tag}", n_acc=3)
    dh = _matmul(dproj, w["b_w_in"], "nt", F32, f"b_dh_{tag}")
    d_win = _matmul(h, dproj, "tn", BF16,f"b_dwin_{tag}")
    return (dh,), dict(b_w_in=d_win, b_b_f=d_bf[0, :nh], b_q_norm=d_gq.reshape(-1), b_k_norm=d_gk.reshape(-1), b_w_out=d_wout)


def _gdn_fwd_mixer(h, w, bsz, seq, tag, jobs=(None, None)):
    d = h.shape[1]
    nkh = d // HEAD
    nvh = 2 * nkh
    dqkv = (2 * nkh + nvh) * HEAD
    dz = nvh * HEAD
    nch = seq // GDN_CHUNK
    proj, j0 = _mm(h, w["c_w_in"], "nn", F32, f"c_in_{tag}", jobs[0])
    conv = _conv_fwd(proj, dqkv, w["c_conv_w"], None, seq, f"c_conv_{tag}")

    def rows_of(cols):
        return cols.reshape(bsz, nch, GDN_CHUNK, nvh).transpose(3, 0, 1, 2)[:, :, :, None, :]
    b_rows = rows_of(proj[:, dqkv + dz:dqkv + dz + nvh])
    a_rows = rows_of(proj[:, dqkv + dz + nvh:dqkv + dz + 2 * nvh])
    alog, dtb = w["c_a_log"].reshape(nvh, 1, 1), w["c_dt_bias"].reshape(nvh, 1, 1)
    o, states = _gdn_fwd(conv, b_rows, a_rows, alog, dtb, bsz, seq, nvh, f"c_core_{tag}")
    gn = w["c_out_norm"].reshape(1, HEAD)
    zwin = _win(proj, dz, dqkv // dz)

    def outfn(o_, z_, g_):
        return jnp.concatenate([_rms(a, g_) * _silu(b) for a, b in zip(_heads(o_), _heads(z_))], axis=1)
    y, = _rowmap(lambda o_, z_, g_: outfn(o_, z_, g_).astype(BF16), [o, zwin], [gn], f"c_outnorm_{tag}")
    m, j1 = _mm(y, w["c_w_out"], "nn", F32, f"c_out_{tag}", jobs[1])
    return m, (proj, conv, b_rows, a_rows, alog, dtb, o, states, y, gn, zwin, outfn), [j0, j1]


def _gdn_bwd_mixer(dm, h, saved, w, bsz, seq, tag):
    proj, conv, b_rows, a_rows, alog, dtb, o, states, y, gn, zwin, outfn = saved
    d = h.shape[1]
    nkh = d // HEAD
    nvh = 2 * nkh
    dk_, dv_ = nkh * HEAD, nvh * HEAD
    dqkv = 2 * dk_ + dv_
    dz = dv_
    wp = proj.shape[1]
    dy = _matmul(dm, w["c_w_out"], "nt", F32, f"c_dy_{tag}")
    d_wout = _matmul(y, dm, "tn", BF16,f"c_dwout_{tag}")

    def out_bwd(o_, z_, dy_, g_):
        _, vjp = jax.vjp(outfn, o_, z_, g_)
        return vjp(dy_)
    do, dzz, d_gn = _rowmap(out_bwd, [o, zwin, dy], [gn], f"c_doutnorm_{tag}", n_acc=1)
    dq, dk, dv, db_rows, da_rows, d_alog, d_dtb = _gdn_bwd(conv, b_rows, a_rows, alog, dtb, states, do, bsz, seq, nvh, f"c_dcore_{tag}")
    cw = w["c_conv_w"]
    dq_pre, d_cwq = _conv_bwd(dq, proj, dk_, cw[:, :dk_], seq, False, f"c_dconv_q_{tag}", xcol=0)
    dk_pre, d_cwk = _conv_bwd(dk, proj, dk_, cw[:, dk_:2 * dk_], seq, False, f"c_dconv_k_{tag}", xcol=dk_)
    dv_pre, d_cwv = _conv_bwd(dv, proj, dv_, cw[:, 2 * dk_:], seq, False, f"c_dconv_v_{tag}", xcol=2 * dk_)
    d_cw = jnp.concatenate([d_cwq, d_cwk, d_cwv], axis=1)

    def cols_of(rows):
        return rows[:, :, :, 0, :].transpose(1, 2, 3, 0).reshape(bsz * seq, nvh)
    dba = jnp.concatenate([cols_of(db_rows), cols_of(da_rows)], axis=1)
    dba = jnp.pad(dba, ((0, 0), (0, wp - dqkv - dz - 2 * nvh)))
    dproj, = _rowmap(lambda *parts: jnp.concatenate(parts, axis=1), [dq_pre, dk_pre, dv_pre, dzz, dba], [], f"c_dproj_{tag}")
    dh = _matmul(dproj, w["c_w_in"], "nt", F32, f"c_dh_{tag}")
    d_win = _matmul(h, dproj, "tn", BF16,f"c_dwin_{tag}")
    return (dh,), dict(c_w_in=d_win, c_conv_w=d_cw, c_a_log=d_alog.reshape(-1), c_dt_bias=d_dtb.reshape(-1),
                       c_out_norm=d_gn.reshape(-1), c_w_out=d_wout)


_MIXER_FWD = (lambda h, w, bsz, seq, tag, jobs: _gmlp_fwd(h, w, tag, jobs), _fox_fwd_mixer, _gdn_fwd_mixer)
_MIXER_BWD = (lambda dm, h, s, w, bsz, seq, tag: _gmlp_bwd(dm, h, s, w, tag), _fox_bwd_mixer, _gdn_bwd_mixer)


class _NoPlan:
    def __init__(self, layers):
        self.layers = layers

    def weights(self, i):
        return self.layers[i]

    def fwd_jobs(self, i):
        return (None,) * 5

    def fwd_done(self, i, first, outs):
        pass

    def bwd_jobs(self, i):
        return (None,) * 5

    def bwd_done(self, i, outs):
        pass

    def grads_ready(self, i, grads):
        pass


def _local_step(x, target, plan, depth, bsz, seq):
    t, d = x.shape
    saved = []
    m_prev = None
    xin = x
    for i in range(depth):
        w = plan.weights(i)
        tag = f"l{i}"
        g_mix, g_ffn = w["norm_mix"].reshape(1, d), w["norm_ffn"].reshape(1, d)
        if i == 0:
            h, = _rowmap(lambda x_, g_: _rms(x_, g_).astype(BF16), [xin], [g_mix], f"norm_mix_{tag}")
            xl = xin
        else:
            xl, h = _rowmap(lambda x_, m_, g_: (x_ + m_, _rms(x_ + m_, g_).astype(BF16)), [xin, m_prev], [g_mix], f"norm_mix_{tag}")
        fjobs = plan.fwd_jobs(i)
        m, msaved, mouts = _MIXER_FWD[i % 3](h, w, bsz, seq, tag, fjobs[:2])
        plan.fwd_done(i, 0, mouts)
        x1, h2 = _rowmap(lambda x_, m_, g_: (x_ + m_, _rms(x_ + m_, g_).astype(BF16)), [xl, m], [g_ffn], f"norm_ffn_{tag}")
        f, fsaved, jouts = _ffn_fwd(x1, h2, w, seq, tag, fjobs[2:])
        plan.fwd_done(i, 2, jouts)
        saved.append((xl, h, msaved, x1, h2, fsaved))
        xin, m_prev = x1, f

    def loss_fn(x_, f_, tg_):
        e = x_ + f_ - tg_
        return e * (1.0 / d), jnp.full((1, LANES), (0.5 / d) * jnp.sum(e * e), F32)
    dx, loss_acc = _rowmap(loss_fn, [xin, m_prev, target], [], "loss", n_acc=1)
    loss = loss_acc[0, 0]

    grads = [None] * depth
    for i in reversed(range(depth)):
        w = plan.weights(i)
        tag = f"l{i}"
        xl, h, msaved, x1, h2, fsaved = saved[i]
        g_mix, g_ffn = w["norm_mix"].reshape(1, d), w["norm_ffn"].reshape(1, d)
        dhs, gw, jouts = _ffn_bwd(dx, h2, fsaved, w["ffn_w_gate"], w["ffn_w_up"], w["ffn_conv_w"], w["ffn_w_down"], seq, tag, plan.bwd_jobs(i))
        plan.bwd_done(i, jouts)
        dx1, d_gffn = _norm_bwd(x1, dx, dhs, g_ffn, f"dnorm_ffn_{tag}")
        dhs, gm = _MIXER_BWD[i % 3](dx1, h, msaved, w, bsz, seq, tag)
        dx, d_gmix = _norm_bwd(xl, dx1, dhs, g_mix, f"dnorm_mix_{tag}")
        gw.update(gm)
        gw["norm_mix"], gw["norm_ffn"] = d_gmix.reshape(-1), d_gffn.reshape(-1)
        grads[i] = gw
        plan.grads_ready(i, gw)
    return loss, dx, grads


def _adamw_math(w_, g_, m_, v_):
    m_new = ADAM_B1 * m_ + (1.0 - ADAM_B1) * g_
    v_new = ADAM_B2 * v_ + (1.0 - ADAM_B2) * (g_ * g_)
    m_hat = m_new / (1.0 - ADAM_B1 ** ADAM_STEP)
    v_hat = v_new / (1.0 - ADAM_B2 ** ADAM_STEP)
    delta = -ADAM_LR * (m_hat / (jnp.sqrt(v_hat) + ADAM_EPS) + ADAM_WD * w_)
    return delta, m_new, v_new


def _adamw(w, g, m, v, name):
    shape = w.shape
    if w.ndim == 1:
        w, g, m, v = (a.reshape(1, -1) for a in (w, g, m, v))
    return [o.reshape(shape) for o in _elementwise(_adamw_math, [w, g, m, v], 3, name)]


def _adamw_layers(w, gs, m, v, name):
    nl, r, c = w.shape
    tr = _pick(r, max(SUBLANES, (1 << 19) // c // SUBLANES * SUBLANES), SUBLANES)

    def body(*refs):
        w_ref, m_ref, v_ref = refs[:3]
        g_refs = refs[3:3 + nl]
        go_ref, d_ref, mo_ref, vo_ref = refs[3 + nl:]
        layer = pl.program_id(0)
        for k in range(nl):
            @pl.when(layer == k)
            def _(k=k):
                g = g_refs[k][...]
                delta, m_new, v_new = _adamw_math(w_ref[...], g, m_ref[...], v_ref[...])
                go_ref[...] = g
                d_ref[...] = delta
                mo_ref[...] = m_new
                vo_ref[...] = v_new

    st = pl.BlockSpec((None, tr, c), lambda l, i: (l, i, 0))
    g_specs = [pl.BlockSpec((tr, c), functools.partial(lambda l, i, k: (jnp.where(l == k, i, 0), 0), k=k)) for k in range(nl)]
    return pl.pallas_call(
        body, name=name, grid=(nl, r // tr), in_specs=[st, st, st] + g_specs, out_specs=[st] * 4,
        out_shape=[jax.ShapeDtypeStruct(w.shape, F32)] * 4, compiler_params=_cparams(("parallel", "parallel")),
    )(w, m, v, *gs)


WEIGHTS = ['norm_mix', 'norm_ffn', 'ffn_w_gate', 'ffn_w_up', 'ffn_conv_w', 'ffn_conv_b', 'ffn_w_down', 'a_w_in', 'a_b_in', 'a_v_norm',
           'a_w_s', 'a_b_s', 'a_w_out', 'b_w_in', 'b_b_f', 'b_q_norm', 'b_k_norm', 'b_w_out', 'c_w_in', 'c_conv_w', 'c_a_log',
           'c_dt_bias', 'c_out_norm', 'c_w_out']
BIG = {'ffn_w_gate': 1, 'ffn_w_up': 1, 'ffn_w_down': 0, 'a_w_in': 1, 'a_w_out': 0, 'b_w_in': 1, 'b_w_out': 0, 'c_w_in': 1, 'c_w_out': 0}
SMALL_SHARDED = {'ffn_conv_w': 1, 'a_b_in': 0, 'a_v_norm': 0, 'c_conv_w': 1}
MIXER_NAMES = (('a_w_in', 'a_b_in', 'a_v_norm', 'a_w_s', 'a_b_s', 'a_w_out'), ('b_w_in', 'b_b_f', 'b_q_norm', 'b_k_norm', 'b_w_out'),
               ('c_w_in', 'c_conv_w', 'c_a_log', 'c_dt_bias', 'c_out_norm', 'c_w_out'))
FFN_NAMES = ('norm_mix', 'norm_ffn', 'ffn_w_gate', 'ffn_w_up', 'ffn_conv_w', 'ffn_conv_b', 'ffn_w_down')


def _layer_entries(depth):
    out = []
    for i in range(depth):
        kind, j = i % 3, i // 3
        out.append([(n, i) for n in FFN_NAMES] + [(n, j) for n in MIXER_NAMES[kind]])
    return out


def _layout(name, shard_shape):
    if BIG[name] == 0:
        return "R"
    return "C" if shard_shape[-1] % LANES == 0 else "U"


JOB_GROUPS = ((3,), (4,), (0,), (1,), (2,))


class _Plan:
    def __init__(self, params, entries, small_full):
        self.params, self.entries, self.small_full = params, entries, small_full
        self.depth = len(entries)
        self.big = [[(n, j, _layout(n, params[n].shape[1:])) for n, j in ent if n in BIG] for ent in entries]
        self.layers = [None] * self.depth
        self.zs = [None] * self.depth
        self.total = {}
        for i, ent in enumerate(entries):
            self.layers[i] = {n: (small_full[(n, j)] if n in SMALL_SHARDED else params[n][j]) for n, j in ent if n not in BIG}
        mixer = [self._cast(0, t) for t in (3, 4)]
        mixer = _gather_layer(mixer, [self.big[0][t][2] for t in (3, 4)], "gather_mixer_l0")
        self._install(0, 3, mixer[0])
        self._install(0, 4, mixer[1])

    def _cast(self, i, t):
        n, j, cls = self.big[i][t]
        return _cast_window(self.params[n], j, cls, f"cast_{n}_l{i}")

    def _install(self, i, t, buf):
        n, j, cls = self.big[i][t]
        self.layers[i][n] = _assemble(buf, _pad_cols(N_CHIPS * buf.shape[2]), f"assemble_{n}_l{i}") if cls == "U" else buf

    def weights(self, i):
        return self.layers[i]

    def fwd_jobs(self, i):
        todo = [(i, 0), (i, 1), (i, 2)] + ([(i + 1, 3), (i + 1, 4)] if i + 1 < self.depth else [])
        self.fwd_todo = todo
        jobs = [_gather_job([self._cast(li, t)], [self.big[li][t][2]]) for li, t in todo]
        return jobs + [None] * (5 - len(jobs))

    def fwd_done(self, i, first, outs):
        for (li, t), got in zip(self.fwd_todo[first:], outs):
            if got is not None:
                self._install(li, t, got[0])

    def grads_ready(self, i, grads):
        dws, classes, widths = [], [], []
        for n, j, cls in self.big[i]:
            shard = self.params[n].shape[1:]
            dws.append(grads[n].reshape(N_CHIPS, shard[0], shard[1]) if cls == "R" else grads[n])
            classes.append(cls)
            widths.append(shard[1])
        self.zs[i] = _rs_chip_sums(dws, classes, widths, f"l{i}")
        if i == 0:
            self._finish(0, _rs_chip(self.zs[0], "rs_chip_l0"))

    def bwd_jobs(self, i):
        if i + 1 >= self.depth:
            return (None,) * len(JOB_GROUPS)
        return [_chip_job([self.zs[i + 1][t] for t in grp]) for grp in JOB_GROUPS]

    def bwd_done(self, i, outs):
        if i + 1 < self.depth:
            parts = [None] * len(self.big[i + 1])
            for grp, got in zip(JOB_GROUPS, outs):
                for t, p in zip(grp, got):
                    parts[t] = p
            self._finish(i + 1, parts)

    def _finish(self, i, parts):
        for (n, j, _), red in zip(self.big[i], _rs_finish(self.zs[i], parts, f"l{i}")):
            self.total[(n, j)] = red


def _train_step(x, target, params, moments_m, moments_v):
    bsz, seq, d = x.shape
    depth = params['norm_mix'].shape[0]
    entries = _layer_entries(depth)

    small_list = [(n, j) for n in SMALL_SHARDED for j in range(params[n].shape[0])]
    small_buf = _gather_shards(_pack([params[n][j] for n, j in small_list], ()), "gather_small")
    small_full = {}
    for (n, j), g in zip(small_list, _unpack(small_buf, [params[n][j].shape for n, j in small_list], (N_CHIPS,))):
        small_full[(n, j)] = _join(g, SMALL_SHARDED[n])

    plan = _Plan(params, entries, small_full)
    loss_local, dx, grads = _local_step(x.reshape(bsz * seq, d), target.reshape(bsz * seq, d), plan, depth, bsz, seq)
    loss = lax.psum(loss_local, ("x", "y", "c"))

    total = plan.total
    layer_of = {(n, j): i for i, ent in enumerate(entries) for n, j in ent}
    packed = _pack([_split(grads[layer_of[(n, j)]][n], SMALL_SHARDED[n]) for n, j in small_list], (N_CHIPS,))
    red = _reduce_scatter_layer([packed], ["R"], [PACK_COLS], "small")[0]
    for (n, j), g in zip(small_list, _unpack(red, [params[n][j].shape for n, j in small_list], ())):
        total[(n, j)] = g
    repl = [(n, j) for n in WEIGHTS if n not in BIG and n not in SMALL_SHARDED for j in range(params[n].shape[0])]
    flat = jnp.concatenate([grads[layer_of[k]][k[0]].reshape(-1) for k in repl])
    n_flat = flat.shape[0]
    flat = jnp.pad(flat, (0, (-n_flat) % (SUBLANES * LANES))).reshape(-1, LANES)
    flat = _all_to_all_sum(flat, "allreduce_small").reshape(-1)
    off = 0
    for k in repl:
        shp = params[k[0]][k[1]].shape
        n = 1
        for s in shp:
            n *= s
        total[k] = flat[off:off + n].reshape(shp)
        off += n

    grad_w, delta_w, new_m, new_v = {}, {}, {}, {}
    for n in WEIGHTS:
        nl = params[n].shape[0]
        if n in BIG:
            grad_w[n], delta_w[n], new_m[n], new_v[n] = _adamw_layers(params[n], [total[(n, j)] for j in range(nl)], moments_m[n], moments_v[n], f"adamw_{n}")
        else:
            g = jnp.stack([total[(n, j)] for j in range(nl)])
            grad_w[n] = g
            delta_w[n], new_m[n], new_v[n] = _adamw(params[n], g, moments_m[n], moments_v[n], f"adamw_{n}")
    return (loss, dx.reshape(bsz, seq, d), *[grad_w[n] for n in WEIGHTS], *[delta_w[n] for n in WEIGHTS],
            *[new_m[n] for n in WEIGHTS], *[new_v[n] for n in WEIGHTS])


def kernel(x, norm_mix, norm_ffn, ffn_w_gate, ffn_w_up, ffn_conv_w, ffn_conv_b, ffn_w_down, a_w_in, a_b_in, a_v_norm, a_w_s, a_b_s, a_w_out, b_w_in, b_b_f, b_q_norm, b_k_norm, b_w_out, c_w_in, c_conv_w, c_a_log, c_dt_bias, c_out_norm, c_w_out, loss_target, m_norm_mix, m_norm_ffn, m_ffn_w_gate, m_ffn_w_up, m_ffn_conv_w, m_ffn_conv_b, m_ffn_w_down, m_a_w_in, m_a_b_in, m_a_v_norm, m_a_w_s, m_a_b_s, m_a_w_out, m_b_w_in, m_b_b_f, m_b_q_norm, m_b_k_norm, m_b_w_out, m_c_w_in, m_c_conv_w, m_c_a_log, m_c_dt_bias, m_c_out_norm, m_c_w_out, v_norm_mix, v_norm_ffn, v_ffn_w_gate, v_ffn_w_up, v_ffn_conv_w, v_ffn_conv_b, v_ffn_w_down, v_a_w_in, v_a_b_in, v_a_v_norm, v_a_w_s, v_a_b_s, v_a_w_out, v_b_w_in, v_b_b_f, v_b_q_norm, v_b_k_norm, v_b_w_out, v_c_w_in, v_c_conv_w, v_c_a_log, v_c_dt_bias, v_c_out_norm, v_c_w_out):
    given = dict(locals())
    params = {n: given[n] for n in WEIGHTS}
    moments_m = {n: given["m_" + n] for n in WEIGHTS}
    moments_v = {n: given["v_" + n] for n in WEIGHTS}
    return _train_step(x, loss_target, params, moments_m, moments_v)
```

```python
import functools

import jax
import jax.numpy as jnp
from jax import lax
from jax.experimental import pallas as pl
from jax.experimental.pallas import tpu as pltpu

F32 = jnp.float32
BF16 = jnp.bfloat16
HI = lax.Precision.HIGHEST
MESH = pl.DeviceIdType.MESH

RMS_EPS = 1e-6
ADAM_LR, ADAM_B1, ADAM_B2, ADAM_EPS, ADAM_WD, ADAM_STEP = 0.001, 0.9, 0.999, 1e-08, 0.01, 10
A_CHUNK, HEAD, GDN_CHUNK = 128, 128, 64
LANES, SUBLANES = 128, 8
PACK_COLS = 1024
N_CHIPS = 4
VMEM_LIMIT = 56 * 1024 * 1024
ROWMAP_BUDGET = 20 * 1024 * 1024

NN = (((1,), (0,)), ((), ()))
NT = (((1,), (1,)), ((), ()))
TN = (((0,), (0,)), ((), ()))
BNN = (((2,), (1,)), ((0,), (0,)))
BNT = (((2,), (2,)), ((0,), (0,)))
BTN = (((1,), (1,)), ((0,), (0,)))


def _pick(n, cap, mult):
    if n <= cap:
        return n
    best = None
    for d in range(mult, cap + 1, mult):
        if n % d == 0:
            best = d
    if best is None:
        raise ValueError(f"no tile for {n} (cap {cap}, multiple of {mult})")
    return best


def _pad_cols(n):
    j = -(-n // LANES)
    while not (j <= 8 or any(j % d == 0 for d in (4, 5, 6, 7, 8))):
        j += 1
    return j * LANES


def _cparams(sem):
    return pltpu.CompilerParams(dimension_semantics=sem, vmem_limit_bytes=VMEM_LIMIT)


def _matmul(a, b, kind, out_dtype, name, job=None):
    if kind == "nn":
        (m, k), (k2, n) = a.shape, b.shape
    elif kind == "nt":
        (m, k), (n, k2) = a.shape, b.shape
    else:
        (k, m), (k2, n) = a.shape, b.shape
    assert k == k2, (name, a.shape, b.shape)
    tm, tn, tk = _pick(m, 1024, LANES), _pick(n, 1024, LANES), _pick(k, 2048, LANES)
    ni, nj, nk = m // tm, n // tn, k // tk
    dims = {"nn": NN, "nt": NT, "tn": TN}[kind]
    a_spec = pl.BlockSpec((tk, tm), lambda i, j, kk: (kk, i)) if kind == "tn" else pl.BlockSpec((tm, tk), lambda i, j, kk: (i, kk))
    b_spec = pl.BlockSpec((tn, tk), lambda i, j, kk: (j, kk)) if kind == "nt" else pl.BlockSpec((tk, tn), lambda i, j, kk: (kk, j))
    n_jin, n_jout = (len(job["ins"]), len(job["out_shapes"])) if job else (0, 0)

    def body(a_ref, b_ref, *rest):
        jins, o_ref, jouts = rest[:n_jin], rest[n_jin], rest[n_jin + 1:n_jin + 1 + n_jout]
        scratch = rest[n_jin + 1 + n_jout:]
        i, j, kk = pl.program_id(0), pl.program_id(1), pl.program_id(2)
        if job:
            first = jnp.logical_and(jnp.logical_and(i == 0, j == 0), kk == 0)
            job["start"](jins, jouts, scratch[-2], scratch[-1], first)
        prod = lax.dot_general(a_ref[...].astype(BF16), b_ref[...].astype(BF16), dims, preferred_element_type=F32)
        if nk == 1:
            o_ref[...] = prod.astype(o_ref.dtype)
        else:
            acc_ref = scratch[0]

            @pl.when(kk == 0)
            def _():
                acc_ref[...] = prod

            @pl.when(kk > 0)
            def _():
                acc_ref[...] += prod

            @pl.when(kk == nk - 1)
            def _():
                o_ref[...] = acc_ref[...].astype(o_ref.dtype)
        if job:
            last = jnp.logical_and(jnp.logical_and(i == ni - 1, j == nj - 1), kk == nk - 1)
            job["finish"](jins, jouts, scratch[-2], scratch[-1], last)

    scratch_shapes = [pltpu.VMEM((tm, tn), F32)] if nk > 1 else []
    out_specs = pl.BlockSpec((tm, tn), lambda i, j, kk: (i, j))
    out_shape = jax.ShapeDtypeStruct((m, n), out_dtype)
    if not job:
        return pl.pallas_call(
            body, name=name, grid=(ni, nj, nk), in_specs=[a_spec, b_spec], out_specs=out_specs, out_shape=out_shape,
            scratch_shapes=scratch_shapes, compiler_params=_cparams(("parallel", "parallel", "arbitrary")),
        )(a, b)
    scratch_shapes += [pltpu.SemaphoreType.DMA((job["n_sems"],)), pltpu.SemaphoreType.DMA((job["n_sems"],))]
    res = pl.pallas_call(
        body, name=name, grid=(ni, nj, nk), in_specs=[a_spec, b_spec] + [ANY] * n_jin,
        out_specs=[out_specs] + [ANY] * n_jout, out_shape=[out_shape] + list(job["out_shapes"]),
        input_output_aliases={2 + t: 1 + t for t in range(n_jin)} if job["alias"] else {},
        scratch_shapes=scratch_shapes, compiler_params=_cparams(("arbitrary", "arbitrary", "arbitrary")),
    )(a, b, *job["ins"])
    return res[0], list(res[1:])


def _win(arr, width=None, blk=0):
    return (arr, arr.shape[1] if width is None else width, blk)


def _rowmap(fn, rows, params, name, n_acc=0, tc=None, col_params=()):
    rows = [r if isinstance(r, tuple) else _win(r) for r in rows]
    t = rows[0][0].shape[0]
    widths = [tc if tc is not None else w for (_, w, _) in rows]

    def blocks_for(tr):
        rb = [jax.ShapeDtypeStruct((tr, w), a.dtype) for (a, _, _), w in zip(rows, widths)]
        pb = [jax.ShapeDtypeStruct((p.shape[0], tc) if (i in col_params) else p.shape, p.dtype) for i, p in enumerate(params)]
        return rb, pb

    rb, pb = blocks_for(SUBLANES * 2)
    outs = jax.eval_shape(fn, *rb, *pb)
    outs = list(outs) if isinstance(outs, (tuple, list)) else [outs]
    n_row = len(outs) - n_acc
    row_bytes = sum(w * a.dtype.itemsize for (a, _, _), w in zip(rows, widths)) + sum(o.shape[1] * o.dtype.itemsize for o in outs[:n_row])
    tr = 16
    while tr * 2 <= 512 and t % (tr * 2) == 0 and (tr * 2) * row_bytes * 5 <= ROWMAP_BUDGET:
        tr *= 2
    rb, pb = blocks_for(tr)
    outs = jax.eval_shape(fn, *rb, *pb)
    outs = list(outs) if isinstance(outs, (tuple, list)) else [outs]
    n_in = len(rows) + len(params)

    if tc is None:
        grid = (t // tr,)
        row_axis = 0
        in_specs = [pl.BlockSpec((tr, w), functools.partial(lambda i, b: (i, b), b=blk)) for (_, w, blk) in rows]
        in_specs += [pl.BlockSpec(p.shape, functools.partial(lambda i, nd: (0,) * nd, nd=p.ndim)) for p in params]
        out_specs = [pl.BlockSpec((tr, o.shape[1]), lambda i: (i, 0)) for o in outs[:n_row]]
        out_specs += [pl.BlockSpec(o.shape, functools.partial(lambda i, nd: (0,) * nd, nd=len(o.shape))) for o in outs[n_row:]]
        out_shape = [jax.ShapeDtypeStruct((t, o.shape[1]), o.dtype) for o in outs[:n_row]]
        out_shape += [jax.ShapeDtypeStruct(o.shape, o.dtype) for o in outs[n_row:]]
        sem = ("arbitrary",) if n_acc else ("parallel",)
    else:
        wtot = rows[0][1]
        grid = (wtot // tc, t // tr)
        row_axis = 1
        in_specs = [pl.BlockSpec((tr, tc), functools.partial(lambda j, i, b: (i, j + b), b=blk)) for (_, _, blk) in rows]
        for i, p in enumerate(params):
            if i in col_params:
                in_specs.append(pl.BlockSpec((p.shape[0], tc), lambda j, i: (0, j)))
            else:
                in_specs.append(pl.BlockSpec(p.shape, functools.partial(lambda j, i, nd: (0,) * nd, nd=p.ndim)))
        out_specs = [pl.BlockSpec((tr, tc), lambda j, i: (i, j)) for _ in outs[:n_row]]
        out_specs += [pl.BlockSpec((o.shape[0], tc), lambda j, i: (0, j)) for o in outs[n_row:]]
        out_shape = [jax.ShapeDtypeStruct((t, wtot), o.dtype) for o in outs[:n_row]]
        out_shape += [jax.ShapeDtypeStruct((o.shape[0], wtot), o.dtype) for o in outs[n_row:]]
        sem = ("parallel", "arbitrary") if n_acc else ("parallel", "parallel")

    def body(*refs):
        ins, ors = refs[:n_in], refs[n_in:]
        res = fn(*[r[...] for r in ins])
        res = list(res) if isinstance(res, (tuple, list)) else [res]
        for o, r in zip(ors[:n_row], res[:n_row]):
            o[...] = r.astype(o.dtype)
        if n_acc:
            i = pl.program_id(row_axis)
            for o, r in zip(ors[n_row:], res[n_row:]):
                @pl.when(i == 0)
                def _(o=o, r=r):
                    o[...] = r.astype(o.dtype)

                @pl.when(i > 0)
                def _(o=o, r=r):
                    o[...] += r.astype(o.dtype)

    res = pl.pallas_call(
        body, name=name, grid=grid, in_specs=in_specs, out_specs=out_specs, out_shape=out_shape,
        compiler_params=_cparams(sem),
    )(*[a for (a, _, _) in rows], *params)
    return res


def _elementwise(fn, arrays, n_out, name):
    shape = arrays[0].shape
    cols = shape[-1]
    rws = 1
    for s in shape[:-1]:
        rws *= s
    arrs = [a.reshape(rws, cols) for a in arrays]
    per_row = cols * 4 * (len(arrays) + n_out) * 3
    tr = rws
    if rws * per_row > ROWMAP_BUDGET:
        tr = _pick(rws, max(SUBLANES, ROWMAP_BUDGET // per_row), SUBLANES)

    def body(*refs):
        res = fn(*[r[...] for r in refs[:len(arrs)]])
        for o, r in zip(refs[len(arrs):], res):
            o[...] = r

    spec = pl.BlockSpec((tr, cols), lambda i: (i, 0))
    outs = pl.pallas_call(
        body, name=name, grid=(rws // tr,), in_specs=[spec] * len(arrs), out_specs=[spec] * n_out,
        out_shape=[jax.ShapeDtypeStruct((rws, cols), F32)] * n_out, compiler_params=_cparams(("parallel",)),
    )(*arrs)
    return [o.reshape(shape) for o in outs]


def _rms(x, g):
    return x * lax.rsqrt(jnp.mean(x * x, axis=-1, keepdims=True) + RMS_EPS) * g


def _silu(x):
    return x * jax.nn.sigmoid(x)


def _softplus(x):
    return jnp.maximum(x, 0.0) + jnp.log(1.0 + jnp.exp(-jnp.abs(x)))


def _gelu_tanh(x):
    return 0.5 * x * (1.0 + jnp.tanh(0.7978845608028654 * (x + 0.044715 * (x * x * x))))


def _heads(x):
    return [x[:, h * HEAD:(h + 1) * HEAD] for h in range(x.shape[1] // HEAD)]


def _dot(a, b, dims=NN):
    return lax.dot_general(a, b, dims, precision=HI, preferred_element_type=F32)


def _bdot(a, b, dims):
    return lax.dot_general(a.astype(BF16), b.astype(BF16), dims, preferred_element_type=F32)


def _eye(n):
    return (lax.broadcasted_iota(jnp.int32, (n, n), 0) == lax.broadcasted_iota(jnp.int32, (n, n), 1)).astype(F32)


def _tri(n):
    return lax.broadcasted_iota(jnp.int32, (n, n), 0) >= lax.broadcasted_iota(jnp.int32, (n, n), 1)


def _row_to_col(row):
    return jnp.sum(_eye(row.shape[1]) * row, axis=1, keepdims=True)


def _conv_tiles(w, seq):
    return _pick(seq, 512, SUBLANES), _pick(w, 512, LANES)


def _conv_fwd(x, width, w, bias, seq, name, gated=None):
    t = x.shape[0]
    kk = w.shape[0]
    tr, tc = _conv_tiles(width, seq)
    hb = tr // SUBLANES

    def body(*refs):
        refs = list(refs)
        x_ref, h_ref, w_ref = refs[:3]
        b_ref = refs[3] if bias is not None else None
        o_ref = refs[3 + (bias is not None) + (gated is not None)]
        i = pl.program_id(1)
        first = (i * tr) % seq == 0
        halo = jnp.where(first, 0.0, h_ref[...])
        xe = jnp.concatenate([halo, x_ref[...]], axis=0)
        wv = w_ref[...]
        acc = xe[SUBLANES:, :] * wv[kk - 1:kk, :]
        for s in range(1, kk):
            acc = acc + pltpu.roll(xe, s, 0)[SUBLANES:, :] * wv[kk - 1 - s:kk - s, :]
        if bias is not None:
            acc = acc + b_ref[...]
        o_ref[...] = acc
        if gated is not None:
            refs[-1][...] = (_silu(acc) * refs[3 + (bias is not None)][...]).astype(BF16)

    cur = pl.BlockSpec((tr, tc), lambda j, i: (i, j))
    in_specs = [cur, pl.BlockSpec((SUBLANES, tc), lambda j, i: (jnp.maximum(i * hb - 1, 0), j)),
                pl.BlockSpec((kk, tc), lambda j, i: (0, j))]
    ops = [x, x, w]
    if bias is not None:
        in_specs.append(pl.BlockSpec((1, tc), lambda j, i: (0, j)))
        ops.append(bias)
    out_specs, out_shape = cur, jax.ShapeDtypeStruct((t, width), F32)
    if gated is not None:
        in_specs.append(cur)
        ops.append(gated)
        out_specs, out_shape = [cur, cur], [out_shape, jax.ShapeDtypeStruct((t, width), BF16)]
    return pl.pallas_call(
        body, name=name, grid=(width // tc, t // tr), in_specs=in_specs, out_specs=out_specs, out_shape=out_shape,
        compiler_params=_cparams(("parallel", "parallel")),
    )(*ops)


def _conv_bwd(dy, x, width, w, seq, with_bias, name, xcol=0, dx_dtype=F32):
    t = x.shape[0]
    kk = w.shape[0]
    tr, tc = _conv_tiles(width, seq)
    hb = tr // SUBLANES
    n_halo_blocks = t // SUBLANES
    assert xcol % tc == 0
    xb = xcol // tc

    def body(dy_ref, dyn_ref, x_ref, xh_ref, w_ref, dx_ref, dw_ref, *rest):
        i = pl.program_id(1)
        first = (i * tr) % seq == 0
        last = ((i + 1) * tr) % seq == 0
        dyc = dy_ref[...]
        dye = jnp.concatenate([dyc, jnp.where(last, 0.0, dyn_ref[...])], axis=0)
        xe = jnp.concatenate([jnp.where(first, 0.0, xh_ref[...]), x_ref[...]], axis=0)
        wv = w_ref[...]
        dx = dyc * wv[kk - 1:kk, :]
        dws = [None] * kk
        dws[kk - 1] = jnp.sum(dyc * xe[SUBLANES:, :], axis=0, keepdims=True)
        for s in range(1, kk):
            dx = dx + pltpu.roll(dye, tr + SUBLANES - s, 0)[:tr, :] * wv[kk - 1 - s:kk - s, :]
            dws[kk - 1 - s] = jnp.sum(dyc * pltpu.roll(xe, s, 0)[SUBLANES:, :], axis=0, keepdims=True)
        dx_ref[...] = dx.astype(dx_ref.dtype)

        @pl.when(i == 0)
        def _():
            for j in range(kk):
                dw_ref[j:j + 1, :] = dws[j]
            if with_bias:
                rest[0][...] = jnp.sum(dyc, axis=0, keepdims=True)

        @pl.when(i > 0)
        def _():
            for j in range(kk):
                dw_ref[j:j + 1, :] += dws[j]
            if with_bias:
                rest[0][...] += jnp.sum(dyc, axis=0, keepdims=True)

    cur = pl.BlockSpec((tr, tc), lambda j, i: (i, j))
    in_specs = [cur, pl.BlockSpec((SUBLANES, tc), lambda j, i: (jnp.minimum((i + 1) * hb, n_halo_blocks - 1), j)),
                pl.BlockSpec((tr, tc), lambda j, i: (i, j + xb)),
                pl.BlockSpec((SUBLANES, tc), lambda j, i: (jnp.maximum(i * hb - 1, 0), j + xb)),
                pl.BlockSpec((kk, tc), lambda j, i: (0, j))]
    out_specs = [cur, pl.BlockSpec((kk, tc), lambda j, i: (0, j))]
    out_shape = [jax.ShapeDtypeStruct((t, width), dx_dtype), jax.ShapeDtypeStruct((kk, width), F32)]
    if with_bias:
        out_specs.append(pl.BlockSpec((1, tc), lambda j, i: (0, j)))
        out_shape.append(jax.ShapeDtypeStruct((1, width), F32))
    return pl.pallas_call(
        body, name=name, grid=(width // tc, t // tr), in_specs=in_specs, out_specs=out_specs, out_shape=out_shape,
        compiler_params=_cparams(("parallel", "arbitrary")),
    )(dy, dy, x, x, w)


def _cumsum_rows(x, seq, reverse, name):
    t, w = x.shape
    tb = _pick(seq, 256, SUBLANES)
    nb = seq // tb

    def pos(b, i):
        return (b * nb + (nb - 1 - i if reverse else i), 0)

    def body(x_ref, o_ref, carry):
        i = pl.program_id(1)

        @pl.when(i == 0)
        def _():
            carry[...] = jnp.zeros_like(carry)

        blk = x_ref[...]
        r = lax.broadcasted_iota(jnp.int32, (tb, tb), 0)
        c = lax.broadcasted_iota(jnp.int32, (tb, tb), 1)
        m = ((r <= c) if reverse else (r >= c)).astype(F32)
        o_ref[...] = _dot(m, blk) + carry[...]
        carry[...] += jnp.sum(blk, axis=0, keepdims=True)

    return pl.pallas_call(
        body, name=name, grid=(t // seq, nb), in_specs=[pl.BlockSpec((tb, w), pos)], out_specs=pl.BlockSpec((tb, w), pos),
        out_shape=jax.ShapeDtypeStruct((t, w), F32), scratch_shapes=[pltpu.VMEM((1, w), F32)],
        compiler_params=_cparams(("parallel", "arbitrary")),
    )(x)


def _sgu_fwd(vn, u, w_s, b_col, name):
    t, a = vn.shape
    g = a // HEAD

    def body(v_ref, u_ref, w_ref, b_ref, y_ref):
        tri = _tri(A_CHUNK)
        for gi in range(g):
            sl = slice(gi * HEAD, (gi + 1) * HEAD)
            wc = jnp.where(tri, w_ref[gi], 0.0)
            sv = _bdot(wc, v_ref[:, sl], NN) + b_ref[gi]
            y_ref[:, sl] = (u_ref[:, sl] * sv).astype(y_ref.dtype)

    blk = pl.BlockSpec((A_CHUNK, a), lambda i: (i, 0))
    return pl.pallas_call(
        body, name=name, grid=(t // A_CHUNK,),
        in_specs=[blk, blk, pl.BlockSpec(w_s.shape, lambda i: (0, 0, 0)), pl.BlockSpec(b_col.shape, lambda i: (0, 0, 0))],
        out_specs=blk, out_shape=jax.ShapeDtypeStruct((t, a), BF16), compiler_params=_cparams(("parallel",)),
    )(vn, u, w_s, b_col)


def _sgu_bwd(vn, u, dy, w_s, b_col, name):
    t, a = vn.shape
    g = a // HEAD

    def body(v_ref, u_ref, dy_ref, w_ref, b_ref, dv_ref, du_ref, dw_ref, db_ref):
        i = pl.program_id(0)
        tri = _tri(A_CHUNK)
        for gi in range(g):
            sl = slice(gi * HEAD, (gi + 1) * HEAD)
            wc = jnp.where(tri, w_ref[gi], 0.0)
            v = v_ref[:, sl]
            sv = _bdot(wc, v, NN) + b_ref[gi]
            dyb = dy_ref[:, sl]
            du_ref[:, sl] = dyb * sv
            dsv = dyb * u_ref[:, sl]
            dv_ref[:, sl] = _bdot(wc, dsv, TN)
            dw = jnp.where(tri, _bdot(dsv, v, NT), 0.0)
            db = jnp.sum(dsv, axis=1, keepdims=True)

            @pl.when(i == 0)
            def _(gi=gi, dw=dw, db=db):
                dw_ref[gi] = dw
                db_ref[gi] = db

            @pl.when(i > 0)
            def _(gi=gi, dw=dw, db=db):
                dw_ref[gi] += dw
                db_ref[gi] += db

    blk = pl.BlockSpec((A_CHUNK, a), lambda i: (i, 0))
    wsp = pl.BlockSpec(w_s.shape, lambda i: (0, 0, 0))
    bsp = pl.BlockSpec(b_col.shape, lambda i: (0, 0, 0))
    return pl.pallas_call(
        body, name=name, grid=(t // A_CHUNK,), in_specs=[blk, blk, blk, wsp, bsp], out_specs=[blk, blk, wsp, bsp],
        out_shape=[jax.ShapeDtypeStruct((t, a), F32), jax.ShapeDtypeStruct((t, a), F32),
                   jax.ShapeDtypeStruct(w_s.shape, F32), jax.ShapeDtypeStruct(b_col.shape, F32)],
        compiler_params=_cparams(("arbitrary",)),
    )(vn, u, dy, w_s, b_col)


def _fox_scores(q, k, cq_row, ck_row, diag, scale):
    s = lax.dot_general(q.astype(BF16), k.astype(BF16), NT, preferred_element_type=F32) * scale
    s = s + _row_to_col(cq_row) - ck_row
    mask = jnp.logical_or(jnp.logical_not(diag), _tri(q.shape[0]))
    return s, mask


def _fox_fwd(qn, kn, proj, v_blk0, c_rows, bsz, seq, nh, name):
    t = qn.shape[0]
    tq = _pick(seq, 512, LANES)
    nq = seq // tq
    scale = HEAD ** -0.5

    def body(q_ref, k_ref, v_ref, cq_ref, ck_ref, o_ref, lse_ref, m_s, l_s, acc_s):
        i, j = pl.program_id(2), pl.program_id(3)

        @pl.when(j == 0)
        def _():
            m_s[...] = jnp.full_like(m_s, -jnp.inf)
            l_s[...] = jnp.zeros_like(l_s)
            acc_s[...] = jnp.zeros_like(acc_s)

        @pl.when(j <= i)
        def _():
            s, mask = _fox_scores(q_ref[...], k_ref[...], cq_ref[...], ck_ref[...], j == i, scale)
            s = jnp.where(mask, s, -jnp.inf)
            m_new = jnp.maximum(m_s[...], jnp.max(s, axis=1, keepdims=True))
            p = jnp.exp(s - m_new)
            alpha = jnp.exp(m_s[...] - m_new)
            l_s[...] = alpha * l_s[...] + jnp.sum(p, axis=1, keepdims=True)
            acc_s[...] = alpha * acc_s[...] + lax.dot_general(p.astype(BF16), v_ref[...].astype(BF16), NN, preferred_element_type=F32)
            m_s[...] = m_new

        @pl.when(j == nq - 1)
        def _():
            o_ref[...] = acc_s[...] / l_s[...]
            lse_ref[...] = jnp.broadcast_to(m_s[...] + jnp.log(l_s[...]), lse_ref.shape)

    qspec = pl.BlockSpec((tq, HEAD), lambda b, h, i, j: (b * nq + i, h))
    kspec = pl.BlockSpec((tq, HEAD), lambda b, h, i, j: (b * nq + jnp.minimum(i, j), h))
    vspec = pl.BlockSpec((tq, HEAD), lambda b, h, i, j: (b * nq + jnp.minimum(i, j), v_blk0 + h))
    cq = pl.BlockSpec((None, None, 1, tq), lambda b, h, i, j: (b, h, 0, i))
    ck = pl.BlockSpec((None, None, 1, tq), lambda b, h, i, j: (b, h, 0, jnp.minimum(i, j)))
    return pl.pallas_call(
        body, name=name, grid=(bsz, nh, nq, nq), in_specs=[qspec, kspec, vspec, cq, ck], out_specs=[qspec, qspec],
        out_shape=[jax.ShapeDtypeStruct((t, nh * HEAD), F32)] * 2,
        scratch_shapes=[pltpu.VMEM((tq, 1), F32), pltpu.VMEM((tq, 1), F32), pltpu.VMEM((tq, HEAD), F32)],
        compiler_params=_cparams(("parallel", "parallel", "parallel", "arbitrary")),
    )(qn, kn, proj, c_rows, c_rows)


def _fox_p_dp(q, k, v, do, lse, cq_row, ck_row, diag, scale):
    s, mask = _fox_scores(q, k, cq_row, ck_row, diag, scale)
    p = jnp.where(mask, jnp.exp(s - jnp.max(lse, axis=1, keepdims=True)), 0.0)
    dp = lax.dot_general(do.astype(BF16), v.astype(BF16), NT, preferred_element_type=F32)
    return p, dp


def _fox_bwd_q(qn, kn, proj, v_blk0, do, lse, c_rows, bsz, seq, nh, name):
    t = qn.shape[0]
    tq = _pick(seq, 512, LANES)
    nq = seq // tq
    scale = HEAD ** -0.5

    def key_block(jj):
        return jnp.where(jj >= nq, jj - nq, jj)

    def body(q_ref, k_ref, v_ref, do_ref, lse_ref, cq_ref, ck_ref, dq_ref, dl_ref, dq_s, dl_s):
        i, jj = pl.program_id(2), pl.program_id(3)
        j = key_block(jj)

        @pl.when(jj == 0)
        def _():
            dq_s[...] = jnp.zeros_like(dq_s)
            dl_s[...] = jnp.zeros_like(dl_s)

        @pl.when(j <= i)
        def _():
            p, dp = _fox_p_dp(q_ref[...], k_ref[...], v_ref[...], do_ref[...], lse_ref[...], cq_ref[...], ck_ref[...], j == i, scale)

            @pl.when(jj < nq)
            def _():
                dl_s[...] += jnp.sum(p * dp, axis=1, keepdims=True)

            @pl.when(jj >= nq)
            def _():
                ds = p * (dp - dl_s[...])
                dq_s[...] += lax.dot_general(ds.astype(BF16), k_ref[...].astype(BF16), NN, preferred_element_type=F32) * scale

        @pl.when(jj == 2 * nq - 1)
        def _():
            dq_ref[...] = dq_s[...]
            dl_ref[...] = jnp.broadcast_to(dl_s[...], dl_ref.shape)

    qspec = pl.BlockSpec((tq, HEAD), lambda b, h, i, jj: (b * nq + i, h))
    kspec = pl.BlockSpec((tq, HEAD), lambda b, h, i, jj: (b * nq + jnp.minimum(i, key_block(jj)), h))
    vspec = pl.BlockSpec((tq, HEAD), lambda b, h, i, jj: (b * nq + jnp.minimum(i, key_block(jj)), v_blk0 + h))
    cq = pl.BlockSpec((None, None, 1, tq), lambda b, h, i, jj: (b, h, 0, i))
    ck = pl.BlockSpec((None, None, 1, tq), lambda b, h, i, jj: (b, h, 0, jnp.minimum(i, key_block(jj))))
    return pl.pallas_call(
        body, name=name, grid=(bsz, nh, nq, 2 * nq), in_specs=[qspec, kspec, vspec, qspec, qspec, cq, ck],
        out_specs=[qspec, qspec], out_shape=[jax.ShapeDtypeStruct((t, nh * HEAD), F32)] * 2,
        scratch_shapes=[pltpu.VMEM((tq, HEAD), F32), pltpu.VMEM((tq, 1), F32)],
        compiler_params=_cparams(("parallel", "parallel", "parallel", "arbitrary")),
    )(qn, kn, proj, do, lse, c_rows, c_rows)


def _fox_bwd_kv(qn, kn, proj, v_blk0, do, lse, delta, c_rows, bsz, seq, nh, name):
    t = qn.shape[0]
    tq = _pick(seq, 512, LANES)
    nq = seq // tq
    scale = HEAD ** -0.5

    def body(q_ref, k_ref, v_ref, do_ref, lse_ref, dl_ref, cq_ref, ck_ref, dk_ref, dv_ref, dc_ref, dk_s, dv_s, dc_s):
        j, i = pl.program_id(2), pl.program_id(3)

        @pl.when(i == 0)
        def _():
            dk_s[...] = jnp.zeros_like(dk_s)
            dv_s[...] = jnp.zeros_like(dv_s)
            dc_s[...] = jnp.zeros_like(dc_s)

        @pl.when(i >= j)
        def _():
            p, dp = _fox_p_dp(q_ref[...], k_ref[...], v_ref[...], do_ref[...], lse_ref[...], cq_ref[...], ck_ref[...], j == i, scale)
            ds = p * (dp - jnp.max(dl_ref[...], axis=1, keepdims=True))
            dv_s[...] += lax.dot_general(p.astype(BF16), do_ref[...].astype(BF16), TN, preferred_element_type=F32)
            dk_s[...] += lax.dot_general(ds.astype(BF16), q_ref[...].astype(BF16), TN, preferred_element_type=F32) * scale
            dc_s[...] -= jnp.sum(ds, axis=0, keepdims=True)

        @pl.when(i == nq - 1)
        def _():
            dk_ref[...] = dk_s[...]
            dv_ref[...] = dv_s[...]
            dc_ref[...] = dc_s[...]

    kspec = pl.BlockSpec((tq, HEAD), lambda b, h, j, i: (b * nq + j, h))
    vspec = pl.BlockSpec((tq, HEAD), lambda b, h, j, i: (b * nq + j, v_blk0 + h))
    qspec = pl.BlockSpec((tq, HEAD), lambda b, h, j, i: (b * nq + jnp.maximum(i, j), h))
    cq = pl.BlockSpec((None, None, 1, tq), lambda b, h, j, i: (b, h, 0, jnp.maximum(i, j)))
    ck = pl.BlockSpec((None, None, 1, tq), lambda b, h, j, i: (b, h, 0, j))
    return pl.pallas_call(
        body, name=name, grid=(bsz, nh, nq, nq), in_specs=[qspec, kspec, vspec, qspec, qspec, qspec, cq, ck],
        out_specs=[kspec, kspec, ck],
        out_shape=[jax.ShapeDtypeStruct((t, nh * HEAD), F32)] * 2 + [jax.ShapeDtypeStruct(c_rows.shape, F32)],
        scratch_shapes=[pltpu.VMEM((tq, HEAD), F32), pltpu.VMEM((tq, HEAD), F32), pltpu.VMEM((1, tq), F32)],
        compiler_params=_cparams(("parallel", "parallel", "parallel", "arbitrary")),
    )(qn, kn, proj, do, lse, delta, c_rows, c_rows)


@jax.custom_vjp
def _unit_lower_inverse(a_mat):
    c = a_mat.shape[-1]
    inv = _eye(c) - a_mat
    pw = _bdot(a_mat, a_mat, BNN)
    n_sq = max(1, (c - 1).bit_length() - 1)
    for it in range(n_sq):
        inv = inv + _bdot(inv, pw, BNN)
        if it < n_sq - 1:
            pw = _bdot(pw, pw, BNN)
    return inv


def _unit_lower_inverse_fwd(a_mat):
    inv = _unit_lower_inverse(a_mat)
    return inv, inv


def _unit_lower_inverse_bwd(inv, d_inv):
    return (-_bdot(_bdot(inv, d_inv, BTN), inv, BNT),)


_unit_lower_inverse.defvjp(_unit_lower_inverse_fwd, _unit_lower_inverse_bwd)


def _gdn_chunk(qp, kp, vp, b_row, a_row, alog, dtb, state):
    hv = vp.shape[0]
    c = qp.shape[1]
    qc, kc, vc = _silu(qp), _silu(kp), _silu(vp)
    qh = qc * lax.rsqrt(jnp.sum(qc * qc, -1, keepdims=True) + RMS_EPS) * (HEAD ** -0.5)
    kh = kc * lax.rsqrt(jnp.sum(kc * kc, -1, keepdims=True) + RMS_EPS)
    q = jnp.stack([qh[h // 2] for h in range(hv)])
    k = jnp.stack([kh[h // 2] for h in range(hv)])
    beta_row = jax.nn.sigmoid(b_row)
    g_row = -jnp.exp(alog) * _softplus(a_row + dtb)
    ri = lax.broadcasted_iota(jnp.int32, (c, c), 0)
    ci = lax.broadcasted_iota(jnp.int32, (c, c), 1)
    eye = (ri == ci).astype(F32)
    tri = ri >= ci
    beta_col = jnp.sum(eye * beta_row, axis=2, keepdims=True)
    g_col = jnp.sum(eye * g_row, axis=2, keepdims=True)
    gc_col = jnp.sum(tri.astype(F32) * g_row, axis=2, keepdims=True)
    gc_row = jnp.sum(g_col * (ri <= ci).astype(F32), axis=1, keepdims=True)
    decay = jnp.where(tri, jnp.exp(jnp.where(tri, gc_col - gc_row, 0.0)), 0.0)
    kb = k * beta_col
    a_mat = jnp.where(ri > ci, _bdot(kb, k, BNT) * decay, 0.0)
    egc = jnp.exp(gc_col)
    inv = _unit_lower_inverse(a_mat)
    u = _bdot(inv, vc * beta_col, BNN)
    w = _bdot(inv, kb * egc, BNN)
    attn = _bdot(q, k, BNT) * decay
    g_last = jnp.sum(g_row, axis=2, keepdims=True)
    v_new = u - _bdot(w, state, BNN)
    o = _bdot(q * egc, state, BNN) + _bdot(attn, v_new, BNN)
    new_state = state * jnp.exp(g_last) + _bdot(k * jnp.exp(g_last - gc_col), v_new, BTN)
    return o, new_state


def _gdn_group(nvh):
    return 8 if nvh % 8 == 0 else (4 if nvh % 4 == 0 else 2)


def _gdn_specs(nch, nkh, hb, rev):
    def n_of(n):
        return nch - 1 - n if rev else n

    hk = hb // 2
    per_k = pl.BlockSpec((GDN_CHUNK, hk * HEAD), lambda g, b, n: (b * nch + n_of(n), g))
    q = per_k
    k = pl.BlockSpec((GDN_CHUNK, hk * HEAD), lambda g, b, n: (b * nch + n_of(n), nkh // hk + g))
    v = pl.BlockSpec((GDN_CHUNK, hb * HEAD), lambda g, b, n: (b * nch + n_of(n), 2 * nkh // hb + g))
    per_v = pl.BlockSpec((GDN_CHUNK, hb * HEAD), lambda g, b, n: (b * nch + n_of(n), g))
    row = pl.BlockSpec((hb, None, None, 1, GDN_CHUNK), lambda g, b, n: (g, b, n_of(n), 0, 0))
    sc = pl.BlockSpec((hb, 1, 1), lambda g, b, n: (g, 0, 0))
    st = pl.BlockSpec((hb, None, None, HEAD, HEAD), lambda g, b, n: (g, b, n_of(n), 0, 0))
    return q, k, v, per_k, per_v, row, sc, st


def _stack_heads(ref, n):
    return jnp.stack([ref[:, h * HEAD:(h + 1) * HEAD] for h in range(n)])


def _gdn_fwd(conv, b_rows, a_rows, alog, dtb, bsz, seq, nvh, name):
    t = conv.shape[0]
    nch = seq // GDN_CHUNK
    nkh = nvh // 2
    hb = _gdn_group(nvh)
    q, k, v, _, per_v, row, sc, st = _gdn_specs(nch, nkh, hb, False)

    def body(q_ref, k_ref, v_ref, b_ref, a_ref, al_ref, dt_ref, o_ref, st_ref, state):
        @pl.when(pl.program_id(2) == 0)
        def _():
            state[...] = jnp.zeros_like(state)

        st_ref[...] = state[...]
        o, new_state = _gdn_chunk(_stack_heads(q_ref, hb // 2), _stack_heads(k_ref, hb // 2), _stack_heads(v_ref, hb),
                                  b_ref[...], a_ref[...], al_ref[...], dt_ref[...], state[...])
        for h in range(hb):
            o_ref[:, h * HEAD:(h + 1) * HEAD] = o[h]
        state[...] = new_state

    return pl.pallas_call(
        body, name=name, grid=(nvh // hb, bsz, nch), in_specs=[q, k, v, row, row, sc, sc], out_specs=[per_v, st],
        out_shape=[jax.ShapeDtypeStruct((t, nvh * HEAD), F32), jax.ShapeDtypeStruct((nvh, bsz, nch, HEAD, HEAD), F32)],
        scratch_shapes=[pltpu.VMEM((hb, HEAD, HEAD), F32)],
        compiler_params=_cparams(("parallel", "parallel", "arbitrary")),
    )(conv, conv, conv, b_rows, a_rows, alog, dtb)


def _gdn_bwd(conv, b_rows, a_rows, alog, dtb, states, do, bsz, seq, nvh, name):
    t = conv.shape[0]
    nch = seq // GDN_CHUNK
    nkh = nvh // 2
    hb = _gdn_group(nvh)
    q, k, v, per_k, per_v, row, sc, st = _gdn_specs(nch, nkh, hb, True)

    def body(q_ref, k_ref, v_ref, b_ref, a_ref, al_ref, dt_ref, st_ref, do_ref,
             dq_ref, dk_ref, dv_ref, db_ref, da_ref, dal_ref, ddt_ref, dstate):
        b, n = pl.program_id(1), pl.program_id(2)

        @pl.when(n == 0)
        def _():
            dstate[...] = jnp.zeros_like(dstate)

        _, vjp = jax.vjp(_gdn_chunk, _stack_heads(q_ref, hb // 2), _stack_heads(k_ref, hb // 2), _stack_heads(v_ref, hb),
                         b_ref[...], a_ref[...], al_ref[...], dt_ref[...], st_ref[...])
        dq, dk, dv, db, da, dal, ddt, dst = vjp((_stack_heads(do_ref, hb), dstate[...]))
        for h in range(hb // 2):
            dq_ref[:, h * HEAD:(h + 1) * HEAD] = dq[h]
            dk_ref[:, h * HEAD:(h + 1) * HEAD] = dk[h]
        for h in range(hb):
            dv_ref[:, h * HEAD:(h + 1) * HEAD] = dv[h]
        db_ref[...] = db
        da_ref[...] = da
        dstate[...] = dst
        start = jnp.logical_and(b == 0, n == 0)

        @pl.when(start)
        def _():
            dal_ref[...] = dal
            ddt_ref[...] = ddt

        @pl.when(jnp.logical_not(start))
        def _():
            dal_ref[...] += dal
            ddt_ref[...] += ddt

    f = lambda *s: jax.ShapeDtypeStruct(s, F32)
    return pl.pallas_call(
        body, name=name, grid=(nvh // hb, bsz, nch), in_specs=[q, k, v, row, row, sc, sc, st, per_v],
        out_specs=[per_k, per_k, per_v, row, row, sc, sc],
        out_shape=[f(t, nkh * HEAD), f(t, nkh * HEAD), f(t, nvh * HEAD), f(*b_rows.shape), f(*a_rows.shape), f(nvh, 1, 1), f(nvh, 1, 1)],
        scratch_shapes=[pltpu.VMEM((hb, HEAD, HEAD), F32)],
        compiler_params=_cparams(("arbitrary", "arbitrary", "arbitrary")),
    )(conv, conv, conv, b_rows, a_rows, alog, dtb, states, do)


ANY = pl.BlockSpec(memory_space=pl.ANY)
FLIPS = (2, 1, 3)


def _place():
    x, y, c = lax.axis_index("x"), lax.axis_index("y"), lax.axis_index("c")
    chips = [(1 - x, y), (x, 1 - y), (1 - x, 1 - y)]
    return x, y, c, chips


def _chip_index():
    return 2 * lax.axis_index("x") + lax.axis_index("y")


def _core_index():
    return lax.axis_index("c")


def _wide(w):
    return w if (w <= 4096 or w % LANES) else _pick(w, 2048, LANES)


def _rows(start, size, mult=16):
    return pl.ds(pl.multiple_of(start, mult), size)


def _cast_window(w_stack, layer, cls, name):
    _, r, w = w_stack.shape
    if cls == "U":
        tr = _pick(r, 256, 16)
        grid = (r // tr, 1)
        in_spec = pl.BlockSpec((None, tr, w), lambda i, j: (layer, i, 0))
        out_spec = pl.BlockSpec((None, tr, w), lambda i, j: (_chip_index(), i, 0))
        out_shape = (N_CHIPS, r, w)
    else:
        tr, tc = _pick(r, 512, 16), _wide(w)
        nbr, nbc = r // tr, w // tc
        grid = (nbr, nbc)
        in_spec = pl.BlockSpec((None, tr, tc), lambda i, j: (layer, i, j))
        if cls == "C":
            out_spec = pl.BlockSpec((tr, tc), lambda i, j: (i, _chip_index() * nbc + j))
            out_shape = (r, N_CHIPS * w)
        else:
            out_spec = pl.BlockSpec((tr, tc), lambda i, j: (_chip_index() * nbr + i, j))
            out_shape = (N_CHIPS * r, w)

    def body(x_ref, o_ref):
        o_ref[...] = x_ref[...].astype(o_ref.dtype)

    return pl.pallas_call(body, name=name, grid=grid, in_specs=[in_spec], out_specs=out_spec,
                          out_shape=jax.ShapeDtypeStruct(out_shape, BF16), compiler_params=_cparams(("parallel", "parallel")))(w_stack)


def _halved(buf, cls):
    return {"C": buf.shape[0], "U": buf.shape[1], "R": buf.shape[0] // N_CHIPS}[cls]


def _part(buf, cls, chip, start, size):
    if cls == "C":
        w = buf.shape[1] // N_CHIPS
        return buf.at[_rows(start, size), pl.ds(chip * w, w)]
    if cls == "U":
        return buf.at[chip, _rows(start, size), :]
    r = buf.shape[0] // N_CHIPS
    return buf.at[_rows(chip * r + start, size), :]


def _gather_layer(bufs, classes, name):
    n = len(bufs)
    job = _gather_job(bufs, classes)

    def body(*refs):
        outs = refs[n:2 * n]
        job["start"](refs[:n], outs, refs[2 * n], refs[2 * n + 1], True)
        job["finish"](refs[:n], outs, refs[2 * n], refs[2 * n + 1], True)

    return pl.pallas_call(
        body, name=name, in_specs=[ANY] * n, out_specs=[ANY] * n, out_shape=job["out_shapes"],
        input_output_aliases={t: t for t in range(n)},
        scratch_shapes=[pltpu.SemaphoreType.DMA((job["n_sems"],)), pltpu.SemaphoreType.DMA((job["n_sems"],))],
    )(*bufs)


def _on_chip(when, fn):
    x, y, _, _ = _place()
    me = 2 * x + y
    for s in range(N_CHIPS):
        cond = (me == s) if when is True else jnp.logical_and(when, me == s)
        pl.when(cond)(functools.partial(fn, s))


def _gather_job(bufs, classes):
    n = len(bufs)

    def copy(outs, send_sems, recv_sems, t, k, chip, start, to):
        size = _halved(outs[t], classes[t]) // 2
        part = _part(outs[t], classes[t], chip, start, size)
        return pltpu.make_async_remote_copy(src_ref=part, dst_ref=part, send_sem=send_sems.at[6 * t + k],
                                            recv_sem=recv_sems.at[6 * t + k], device_id=to, device_id_type=MESH)

    def halves(outs):
        return [_halved(outs[t], classes[t]) // 2 for t in range(n)]

    def start(ins, outs, send_sems, recv_sems, when):
        x, y, c, chips = _place()

        def run(s_me):
            for t, half in enumerate(halves(outs)):
                for k, (cx, cy) in enumerate(chips):
                    copy(outs, send_sems, recv_sems, t, k, s_me, c * half, (cx, cy, c)).start()
        _on_chip(when, run)

    def finish(ins, outs, send_sems, recv_sems, when):
        x, y, c, chips = _place()

        def run(s_me):
            passed = []
            for t, half in enumerate(halves(outs)):
                for k in range(3):
                    copy(outs, send_sems, recv_sems, t, k, s_me ^ FLIPS[k], c * half, (x, y, c)).wait_recv()
                    fwd = copy(outs, send_sems, recv_sems, t, 3 + k, s_me ^ FLIPS[k], c * half, (x, y, 1 - c))
                    fwd.start()
                    passed.append(fwd)
            for t, half in enumerate(halves(outs)):
                for k in range(3):
                    copy(outs, send_sems, recv_sems, t, 3 + k, s_me ^ FLIPS[k], (1 - c) * half, (x, y, c)).wait_recv()
            for t, half in enumerate(halves(outs)):
                for k, (cx, cy) in enumerate(chips):
                    copy(outs, send_sems, recv_sems, t, k, s_me, c * half, (cx, cy, c)).wait_send()
            for fwd in passed:
                fwd.wait_send()
        _on_chip(when, run)

    return dict(ins=list(bufs), alias=True, n_sems=6 * n, start=start, finish=finish,
                out_shapes=[jax.ShapeDtypeStruct(b.shape, b.dtype) for b in bufs])


def _assemble(slots, width, name):
    _, r, w = slots.shape
    tr = _pick(r, 256, 16)

    def body(s_ref, o_ref):
        parts = [s_ref[s] for s in range(N_CHIPS)]
        if width > N_CHIPS * w:
            parts.append(jnp.zeros((tr, width - N_CHIPS * w), slots.dtype))
        o_ref[...] = jnp.concatenate(parts, axis=1)

    return pl.pallas_call(
        body, name=name, grid=(r // tr,), in_specs=[pl.BlockSpec((N_CHIPS, tr, w), lambda i: (0, i, 0))],
        out_specs=pl.BlockSpec((tr, width), lambda i: (i, 0)), out_shape=jax.ShapeDtypeStruct((r, width), slots.dtype),
        compiler_params=_cparams(("parallel",)),
    )(slots)


def _rs_pair(dws, classes, name):
    n = len(dws)

    def shape_of(d, cls):
        return (N_CHIPS, d.shape[1] // 2, d.shape[2]) if cls == "R" else (d.shape[0] // 2, d.shape[1])

    def body(*refs):
        ins, outs = refs[:n], refs[n:2 * n]
        send_sems, recv_sems = refs[2 * n], refs[2 * n + 1]
        x, y, c, _ = _place()
        cps = []
        for t in range(n):
            if classes[t] == "R":
                h = ins[t].shape[1] // 2
                src = ins[t].at[:, _rows((1 - c) * h, h), :]
            else:
                h = ins[t].shape[0] // 2
                src = ins[t].at[_rows((1 - c) * h, h), :]
            cp = pltpu.make_async_remote_copy(src_ref=src, dst_ref=outs[t], send_sem=send_sems.at[t], recv_sem=recv_sems.at[t],
                                              device_id=(x, y, 1 - c), device_id_type=MESH)
            cp.start()
            cps.append(cp)
        for cp in cps:
            cp.wait()

    return pl.pallas_call(
        body, name=name, in_specs=[ANY] * n, out_specs=[ANY] * n,
        out_shape=[jax.ShapeDtypeStruct(shape_of(d, cls), d.dtype) for d, cls in zip(dws, classes)],
        scratch_shapes=[pltpu.SemaphoreType.DMA((n,)), pltpu.SemaphoreType.DMA((n,))],
    )(*dws)


def _rs_add(dw, got, cls, w, name):
    if cls == "R":
        _, r, _ = dw.shape
        h = r // 2
        th, tc = _pick(h, 256, 16), _wide(w)
        nbh = h // th
        grid = (N_CHIPS, nbh, w // tc)
        in_specs = [pl.BlockSpec((None, th, tc), lambda s, i, j: (s, _core_index() * nbh + i, j)),
                    pl.BlockSpec((None, th, tc), lambda s, i, j: (s, i, j))]
        out_spec = pl.BlockSpec((None, th, tc), lambda s, i, j: (s, i, j))
        sem = ("parallel", "parallel", "parallel")

        def body(a_ref, b_ref, o_ref):
            o_ref[...] = (a_ref[...].astype(F32) + b_ref[...].astype(F32)).astype(o_ref.dtype)
    elif cls == "C":
        r = dw.shape[0]
        h = r // 2
        th, tc = _pick(h, 256, 16), _wide(w)
        nbh, nbc = h // th, w // tc
        grid = (N_CHIPS, nbh, nbc)
        in_specs = [pl.BlockSpec((th, tc), lambda s, i, j: (_core_index() * nbh + i, s * nbc + j)),
                    pl.BlockSpec((th, tc), lambda s, i, j: (i, s * nbc + j))]
        out_spec = pl.BlockSpec((None, th, tc), lambda s, i, j: (s, i, j))
        sem = ("parallel", "parallel", "parallel")

        def body(a_ref, b_ref, o_ref):
            o_ref[...] = (a_ref[...].astype(F32) + b_ref[...].astype(F32)).astype(o_ref.dtype)
    else:
        r, wp = dw.shape
        h = r // 2
        th = _pick(h, 64, 16)
        nbh = h // th
        grid = (nbh,)
        in_specs = [pl.BlockSpec((th, wp), lambda i: (_core_index() * nbh + i, 0)), pl.BlockSpec((th, wp), lambda i: (i, 0))]
        out_spec = pl.BlockSpec((N_CHIPS, th, w), lambda i: (0, i, 0))
        sem = ("parallel",)

        def body(a_ref, b_ref, o_ref):
            tot = a_ref[...].astype(F32) + b_ref[...].astype(F32)
            for s in range(N_CHIPS):
                o_ref[s] = tot[:, s * w:(s + 1) * w].astype(o_ref.dtype)

    return pl.pallas_call(body, name=name, grid=grid, in_specs=in_specs, out_specs=out_spec,
                          out_shape=jax.ShapeDtypeStruct((N_CHIPS, h, w), BF16), compiler_params=_cparams(sem))(dw, got)


def _rs_chip(zs, name):
    n = len(zs)
    job = _chip_job(zs)

    def body(*refs):
        job["start"](refs[:n], refs[n:2 * n], refs[2 * n], refs[2 * n + 1], True)
        job["finish"](refs[:n], refs[n:2 * n], refs[2 * n], refs[2 * n + 1], True)

    return pl.pallas_call(
        body, name=name, in_specs=[ANY] * n, out_specs=[ANY] * n, out_shape=job["out_shapes"],
        scratch_shapes=[pltpu.SemaphoreType.DMA((job["n_sems"],)), pltpu.SemaphoreType.DMA((job["n_sems"],))],
    )(*zs)


def _chip_job(zs):
    n = len(zs)

    def copies(ins, outs, send_sems, recv_sems):
        x, y, c, chips = _place()
        return [pltpu.make_async_remote_copy(src_ref=ins[t].at[2 * cx + cy], dst_ref=outs[t].at[k], send_sem=send_sems.at[3 * t + k],
                                             recv_sem=recv_sems.at[3 * t + k], device_id=(cx, cy, c), device_id_type=MESH)
                for t in range(n) for k, (cx, cy) in enumerate(chips)]

    def start(ins, outs, send_sems, recv_sems, when):
        def run():
            for cp in copies(ins, outs, send_sems, recv_sems):
                cp.start()
        run() if when is True else pl.when(when)(run)

    def finish(ins, outs, send_sems, recv_sems, when):
        def run():
            for cp in copies(ins, outs, send_sems, recv_sems):
                cp.wait()
        run() if when is True else pl.when(when)(run)

    return dict(ins=list(zs), alias=False, n_sems=3 * n, start=start, finish=finish,
                out_shapes=[jax.ShapeDtypeStruct((3,) + z.shape[1:], z.dtype) for z in zs])


def _rs_sum(z, parts, name):
    _, h, w = z.shape
    th = _pick(h, 256, 16)
    tc = _wide(w)
    nbh = h // th

    def body(z_ref, k_ref, o_ref):
        acc = z_ref[...].astype(F32)
        for k in range(3):
            acc = acc + k_ref[k].astype(F32)
        o_ref[...] = acc

    return pl.pallas_call(
        body, name=name, grid=(nbh, w // tc),
        in_specs=[pl.BlockSpec((None, th, tc), lambda i, j: (_chip_index(), i, j)), pl.BlockSpec((3, th, tc), lambda i, j: (0, i, j))],
        out_specs=pl.BlockSpec((th, tc), lambda i, j: (_core_index() * nbh + i, j)),
        out_shape=jax.ShapeDtypeStruct((2 * h, w), F32), compiler_params=_cparams(("parallel", "parallel")))(z, parts)


def _rs_join(bufs, name):
    n = len(bufs)

    def body(*refs):
        outs = refs[n:2 * n]
        send_sems, recv_sems = refs[2 * n], refs[2 * n + 1]
        x, y, c, _ = _place()
        cps = []
        for t in range(n):
            h = outs[t].shape[0] // 2
            mine = outs[t].at[_rows(c * h, h), :]
            cp = pltpu.make_async_remote_copy(src_ref=mine, dst_ref=mine, send_sem=send_sems.at[t], recv_sem=recv_sems.at[t],
                                              device_id=(x, y, 1 - c), device_id_type=MESH)
            cp.start()
            cps.append(cp)
        for t in range(n):
            h = outs[t].shape[0] // 2
            other = outs[t].at[_rows((1 - c) * h, h), :]
            pltpu.make_async_remote_copy(src_ref=other, dst_ref=other, send_sem=send_sems.at[t], recv_sem=recv_sems.at[t],
                                         device_id=(x, y, c), device_id_type=MESH).wait_recv()
        for cp in cps:
            cp.wait_send()

    return pl.pallas_call(
        body, name=name, in_specs=[ANY] * n, out_specs=[ANY] * n,
        out_shape=[jax.ShapeDtypeStruct(b.shape, b.dtype) for b in bufs],
        input_output_aliases={t: t for t in range(n)},
        scratch_shapes=[pltpu.SemaphoreType.DMA((n,)), pltpu.SemaphoreType.DMA((n,))],
    )(*bufs)


def _rs_chip_sums(dws, classes, widths, tag):
    gots = _rs_pair(dws, classes, f"rs_pair_{tag}")
    return [_rs_add(d, g, cls, w, f"rs_add_{tag}_{t}") for t, (d, g, cls, w) in enumerate(zip(dws, gots, classes, widths))]


def _rs_finish(zs, parts, tag):
    halves = [_rs_sum(z, p, f"rs_sum_{tag}_{t}") for t, (z, p) in enumerate(zip(zs, parts))]
    return _rs_join(halves, f"rs_join_{tag}")


def _reduce_scatter_layer(dws, classes, widths, tag):
    zs = _rs_chip_sums(dws, classes, widths, tag)
    return _rs_finish(zs, _rs_chip(zs, f"rs_chip_{tag}"), tag)


def _gather_shards(mine, name):
    r, w = mine.shape
    rh = r // 2

    def body(mine_ref, out_ref, send_sems, recv_sems, local_sem):
        x, y, c, chips = _place()
        me = 2 * x + y
        half = _rows(c * rh, rh)
        other = _rows((1 - c) * rh, rh)

        def copy(k, src, chip, rows, to):
            return pltpu.make_async_remote_copy(src_ref=src, dst_ref=out_ref.at[chip, rows], send_sem=send_sems.at[k],
                                                recv_sem=recv_sems.at[k], device_id=to, device_id_type=MESH)

        local = pltpu.make_async_copy(mine_ref, out_ref.at[me], local_sem)
        local.start()
        sends = [copy(k, mine_ref.at[half], me, half, (cx, cy, c)) for k, (cx, cy) in enumerate(chips)]
        for s in sends:
            s.start()
        passed = []
        for k, (cx, cy) in enumerate(chips):
            chip = 2 * cx + cy
            copy(k, mine_ref.at[half], chip, half, (x, y, c)).wait_recv()
            fwd = copy(3 + k, out_ref.at[chip, half], chip, half, (x, y, 1 - c))
            fwd.start()
            passed.append(fwd)
        for k, (cx, cy) in enumerate(chips):
            copy(3 + k, mine_ref.at[half], 2 * cx + cy, other, (x, y, c)).wait_recv()
        for s in sends + passed:
            s.wait_send()
        local.wait()

    return pl.pallas_call(
        body, name=name, in_specs=[ANY], out_specs=ANY, out_shape=jax.ShapeDtypeStruct((N_CHIPS, r, w), mine.dtype),
        scratch_shapes=[pltpu.SemaphoreType.DMA((6,)), pltpu.SemaphoreType.DMA((6,)), pltpu.SemaphoreType.DMA],
    )(mine)


def _sum_slots(slots, name):
    n, r, w = slots.shape
    tr = _pick(r, 256, SUBLANES)

    def body(s_ref, o_ref):
        acc = s_ref[0]
        for k in range(1, n):
            acc = acc + s_ref[k]
        o_ref[...] = acc

    return pl.pallas_call(
        body, name=name, grid=(r // tr,), in_specs=[pl.BlockSpec((n, tr, w), lambda i: (0, i, 0))],
        out_specs=pl.BlockSpec((tr, w), lambda i: (i, 0)), out_shape=jax.ShapeDtypeStruct((r, w), F32),
        compiler_params=_cparams(("parallel",)),
    )(slots)


def _all_to_all_sum(flat, name):
    r, w = flat.shape

    def body(f_ref, out_ref, send_sems, recv_sems, local_sem):
        x, y, c, _ = _place()
        me = 4 * x + 2 * y + c
        local = pltpu.make_async_copy(f_ref, out_ref.at[me], local_sem)
        local.start()
        sends = []
        for k in range(1, 8):
            peer = (x ^ (k >> 2), y ^ ((k >> 1) & 1), c ^ (k & 1))
            s = pltpu.make_async_remote_copy(src_ref=f_ref, dst_ref=out_ref.at[me], send_sem=send_sems.at[k - 1],
                                             recv_sem=recv_sems.at[k - 1], device_id=peer, device_id_type=MESH)
            s.start()
            sends.append(s)
        for k in range(1, 8):
            peer_slot = 4 * (x ^ (k >> 2)) + 2 * (y ^ ((k >> 1) & 1)) + (c ^ (k & 1))
            pltpu.make_async_remote_copy(src_ref=f_ref, dst_ref=out_ref.at[peer_slot], send_sem=send_sems.at[k - 1],
                                         recv_sem=recv_sems.at[k - 1], device_id=(x, y, c), device_id_type=MESH).wait_recv()
        for s in sends:
            s.wait_send()
        local.wait()

    slots = pl.pallas_call(
        body, name=name, in_specs=[ANY], out_specs=ANY, out_shape=jax.ShapeDtypeStruct((8, r, w), flat.dtype),
        scratch_shapes=[pltpu.SemaphoreType.DMA((7,)), pltpu.SemaphoreType.DMA((7,)), pltpu.SemaphoreType.DMA],
    )(flat)
    return _sum_slots(slots, name + "_sum")


def _pack(pieces, lead):
    flat = []
    n_lead = len(lead)
    for p in pieces:
        f = p.reshape(*lead, -1)
        pad = (-f.shape[-1]) % PACK_COLS
        if pad:
            f = jnp.pad(f, [(0, 0)] * n_lead + [(0, pad)])
        flat.append(f)
    f = jnp.concatenate(flat, axis=-1) if len(flat) > 1 else flat[0]
    pad = (-f.shape[-1]) % (32 * PACK_COLS)
    if pad:
        f = jnp.pad(f, [(0, 0)] * n_lead + [(0, pad)])
    return f.reshape(*lead, -1, PACK_COLS)


def _unpack(buf, shapes, lead):
    f = buf.reshape(*lead, -1)
    out, off = [], 0
    for shp in shapes:
        n = 1
        for s in shp:
            n *= s
        out.append(f[..., off:off + n].reshape(*lead, *shp))
        off += n + ((-n) % PACK_COLS)
    return out


def _join(g, axis):
    g = jnp.moveaxis(g, 0, axis)
    return g.reshape(*g.shape[:axis], g.shape[axis] * g.shape[axis + 1], *g.shape[axis + 2:])


def _split(full, axis):
    shp = full.shape
    g = full.reshape(*shp[:axis], N_CHIPS, shp[axis] // N_CHIPS, *shp[axis + 1:])
    return jnp.moveaxis(g, axis, 0)


def _norm_bwd(xs, dres, dhs, gain, name):
    def fn(x, dr, *rest):
        dh = rest[0]
        for d in rest[1:-1]:
            dh = dh + d
        _, vjp = jax.vjp(_rms, x, rest[-1])
        dx, dg = vjp(dh.astype(F32))
        return dr + dx, dg
    return _rowmap(fn, [xs, dres] + list(dhs), [gain], name, n_acc=1)


def _mm(a, b, kind, dtype, name, job):
    if job is None:
        return _matmul(a, b, kind, dtype, name), None
    return _matmul(a, b, kind, dtype, name, job=job)


def _ffn_fwd(x1, h2, w, seq, tag, jobs=(None, None, None)):
    gpre, j0 = _mm(h2, w["ffn_w_gate"], "nn", F32, f"ffn_gate_{tag}", jobs[0])
    up, j1 = _mm(h2, w["ffn_w_up"], "nn", BF16, f"ffn_up_{tag}", jobs[1])
    gate, act = _conv_fwd(gpre, gpre.shape[1], w["ffn_conv_w"], w["ffn_conv_b"].reshape(1, -1), seq, f"ffn_conv_{tag}", gated=up)
    wd = w["ffn_w_down"] if "ffn_w_down" in w else j0[0]
    f, j2 = _mm(act, wd, "nn", F32, f"ffn_down_{tag}", jobs[2])
    return f, (gpre, up, gate, act), [j0, j1, j2]


def _ffn_bwd(dx2, h2, saved, wg, wu, conv_w, wd, seq, tag, jobs=(None,) * 5):
    gpre, up, gate, act = saved
    da, j0 = _mm(dx2, wd, "nt", BF16, f"ffn_dact_{tag}", jobs[0])
    d_wd, j1 = _mm(act, dx2, "tn", BF16,f"ffn_dwd_{tag}", jobs[1])

    def act_bwd(g, u, d):
        _, vjp = jax.vjp(lambda g_, u_: _silu(g_) * u_, g, u)
        return vjp(d.astype(F32))
    dgate, dup = _rowmap(act_bwd, [gate, up, da], [], f"ffn_dactfn_{tag}", tc=_pick(gate.shape[1], 1024, LANES))
    dgpre, d_cw, d_cb = _conv_bwd(dgate, gpre, gpre.shape[1], conv_w, seq, True, f"ffn_dconv_{tag}", dx_dtype=BF16)
    dh_a, j2 = _mm(dgpre, wg, "nt", F32, f"ffn_dh_gate_{tag}", jobs[2])
    dh_b, j3 = _mm(dup, wu, "nt", F32, f"ffn_dh_up_{tag}", jobs[3])
    d_wg, j4 = _mm(h2, dgpre, "tn", BF16,f"ffn_dwg_{tag}", jobs[4])
    d_wu = _matmul(h2, dup, "tn", BF16,f"ffn_dwu_{tag}")
    grads = dict(ffn_w_gate=d_wg, ffn_w_up=d_wu, ffn_conv_w=d_cw, ffn_conv_b=d_cb.reshape(-1), ffn_w_down=d_wd)
    return (dh_a, dh_b), grads, [j0, j1, j2, j3, j4]


def _gmlp_fwd(h, w, tag, jobs=(None, None)):
    a = w["a_w_out"].shape[0]
    p, j0 = _mm(h, w["a_w_in"], "nn", F32, f"a_in_{tag}", jobs[0])
    b_in, vnorm = w["a_b_in"].reshape(1, -1), w["a_v_norm"].reshape(1, -1)

    def fn(p_, b_, g_):
        hh = _gelu_tanh(p_ + b_)
        return hh[:, :a], _rms(hh[:, a:], g_)
    u, vn = _rowmap(fn, [p], [b_in, vnorm], f"a_gelu_{tag}")
    b_col = w["a_b_s"][:, :, None]
    y = _sgu_fwd(vn, u, w["a_w_s"], b_col, f"a_sgu_{tag}")
    m, j1 = _mm(y, w["a_w_out"], "nn", F32, f"a_out_{tag}", jobs[1])
    return m, (p, u, vn, y, fn, b_in, vnorm, b_col), [j0, j1]


def _gmlp_bwd(dm, h, saved, w, tag):
    p, u, vn, y, fn, b_in, vnorm, b_col = saved
    dy = _matmul(dm, w["a_w_out"], "nt", F32, f"a_dy_{tag}")
    d_wout = _matmul(y, dm, "tn", BF16,f"a_dwout_{tag}")
    dvn, du, d_ws, d_bcol = _sgu_bwd(vn, u, dy, w["a_w_s"], b_col, f"a_dsgu_{tag}")

    def bwd(p_, du_, dvn_, b_, g_):
        _, vjp = jax.vjp(fn, p_, b_, g_)
        return vjp((du_, dvn_))
    dp, d_bin, d_vnorm = _rowmap(bwd, [p, du, dvn], [b_in, vnorm], f"a_dgelu_{tag}", n_acc=2)
    dh = _matmul(dp, w["a_w_in"], "nt", F32, f"a_dh_{tag}")
    d_win = _matmul(h, dp, "tn", BF16,f"a_dwin_{tag}")
    return (dh,), dict(a_w_in=d_win, a_b_in=d_bin.reshape(-1), a_v_norm=d_vnorm.reshape(-1), a_w_s=d_ws, a_b_s=d_bcol[:, :, 0], a_w_out=d_wout)


def _fox_fwd_mixer(h, w, bsz, seq, tag, jobs=(None, None)):
    d = h.shape[1]
    nh = d // HEAD
    win = w["b_w_in"]
    wp = win.shape[1]
    proj, j0 = _mm(h, win, "nn", F32, f"b_in_{tag}", jobs[0])
    gq, gk = w["b_q_norm"].reshape(1, HEAD), w["b_k_norm"].reshape(1, HEAD)
    bf = jnp.pad(w["b_b_f"].reshape(1, nh), ((0, 0), (0, LANES - nh)))

    def prep(pq, pk, pfl, gq_, gk_, bf_):
        qn = jnp.concatenate([_rms(x, gq_) for x in _heads(pq)], axis=1)
        kn = jnp.concatenate([_rms(x, gk_) for x in _heads(pk)], axis=1)
        return qn, kn, -_softplus(-(pfl + bf_))
    wins = [_win(proj, d, 0), _win(proj, d, 1), _win(proj, LANES, 4 * d // LANES)]

    def prep_fwd(pq, pk, pfl, gq_, gk_, bf_):
        qn, kn, lf = prep(pq, pk, pfl, gq_, gk_, bf_)
        return qn.astype(BF16), kn.astype(BF16), lf
    qn, kn, lf = _rowmap(prep_fwd, wins, [gq, gk, bf], f"b_prep_{tag}")
    cs = _cumsum_rows(lf, seq, False, f"b_cumsum_{tag}")
    c_rows = cs[:, :nh].reshape(bsz, seq, nh).transpose(0, 2, 1)[:, :, None, :]
    o, lse = _fox_fwd(qn, kn, proj, 2 * nh, c_rows, bsz, seq, nh, f"b_attn_{tag}")
    og = _win(proj, d, 3)
    y, = _rowmap(lambda o_, g_: (o_ * jax.nn.sigmoid(g_)).astype(BF16), [o, og], [], f"b_gate_{tag}")
    m, j1 = _mm(y, w["b_w_out"], "nn", F32, f"b_out_{tag}", jobs[1])
    return m, (proj, qn, kn, c_rows, o, lse, y, prep, wins, (gq, gk, bf), wp), [j0, j1]


def _fox_bwd_mixer(dm, h, saved, w, bsz, seq, tag):
    proj, qn, kn, c_rows, o, lse, y, prep, wins, (gq, gk, bf), wp = saved
    d = h.shape[1]
    nh = d // HEAD
    dy = _matmul(dm, w["b_w_out"], "nt", F32, f"b_dy_{tag}")
    d_wout = _matmul(y, dm, "tn", BF16,f"b_dwout_{tag}")
    og = _win(proj, d, 3)

    def gate_bwd(o_, g_, dy_):
        _, vjp = jax.vjp(lambda a, b: a * jax.nn.sigmoid(b), o_, g_)
        return vjp(dy_)
    do, dog = _rowmap(gate_bwd, [o, og, dy], [], f"b_dgate_{tag}")
    dqn, delta = _fox_bwd_q(qn, kn, proj, 2 * nh, do, lse, c_rows, bsz, seq, nh, f"b_dattn_q_{tag}")
    dkn, dv, dc_rows = _fox_bwd_kv(qn, kn, proj, 2 * nh, do, lse, delta, c_rows, bsz, seq, nh, f"b_dattn_kv_{tag}")
    dc = dc_rows[:, :, 0, :].transpose(0, 2, 1).reshape(bsz * seq, nh)
    dc = jnp.pad(dc, ((0, 0), (0, LANES - nh)))
    dlf = _cumsum_rows(dc, seq, True, f"b_dcumsum_{tag}")
    extra = wp - (4 * d + LANES)

    def prep_bwd(pq, pk, pfl, dqn_, dkn_, dv_, dog_, dlf_, gq_, gk_, bf_):
        _, vjp = jax.vjp(prep, pq, pk, pfl, gq_, gk_, bf_)
        dpq, dpk, dpfl, dgq, dgk, dbf = vjp((dqn_, dkn_, dlf_))
        parts = [dpq, dpk, dv_, dog_, dpfl]
        if extra:
            parts.append(jnp.zeros((pq.shape[0], extra), F32))
        return jnp.concatenate(parts, axis=1), dgq, dgk, dbf
    dproj, d_gq, d_gk, d_bf = _rowmap(prep_bwd, wins + [dqn, dkn, dv, dog, dlf], [gq, gk, bf], f"b_dprep_{tag}", n_acc=3)
    dh = _matmul(dproj, w["b_w_in"], "nt", F32, f"b_dh_{tag}")
    d_win = _matmul(h, dproj, "tn", BF16,f"b_dwin_{tag}")
    return (dh,), dict(b_w_in=d_win, b_b_f=d_bf[0, :nh], b_q_norm=d_gq.reshape(-1), b_k_norm=d_gk.reshape(-1), b_w_out=d_wout)


def _gdn_fwd_mixer(h, w, bsz, seq, tag, jobs=(None, None)):
    d = h.shape[1]
    nkh = d // HEAD
    nvh = 2 * nkh
    dqkv = (2 * nkh + nvh) * HEAD
    dz = nvh * HEAD
    nch = seq // GDN_CHUNK
    proj, j0 = _mm(h, w["c_w_in"], "nn", F32, f"c_in_{tag}", jobs[0])
    conv = _conv_fwd(proj, dqkv, w["c_conv_w"], None, seq, f"c_conv_{tag}")

    def rows_of(cols):
        return cols.reshape(bsz, nch, GDN_CHUNK, nvh).transpose(3, 0, 1, 2)[:, :, :, None, :]
    b_rows = rows_of(proj[:, dqkv + dz:dqkv + dz + nvh])
    a_rows = rows_of(proj[:, dqkv + dz + nvh:dqkv + dz + 2 * nvh])
    alog, dtb = w["c_a_log"].reshape(nvh, 1, 1), w["c_dt_bias"].reshape(nvh, 1, 1)
    o, states = _gdn_fwd(conv, b_rows, a_rows, alog, dtb, bsz, seq, nvh, f"c_core_{tag}")
    gn = w["c_out_norm"].reshape(1, HEAD)
    zwin = _win(proj, dz, dqkv // dz)

    def outfn(o_, z_, g_):
        return jnp.concatenate([_rms(a, g_) * _silu(b) for a, b in zip(_heads(o_), _heads(z_))], axis=1)
    y, = _rowmap(lambda o_, z_, g_: outfn(o_, z_, g_).astype(BF16), [o, zwin], [gn], f"c_outnorm_{tag}")
    m, j1 = _mm(y, w["c_w_out"], "nn", F32, f"c_out_{tag}", jobs[1])
    return m, (proj, conv, b_rows, a_rows, alog, dtb, o, states, y, gn, zwin, outfn), [j0, j1]


def _gdn_bwd_mixer(dm, h, saved, w, bsz, seq, tag):
    proj, conv, b_rows, a_rows, alog, dtb, o, states, y, gn, zwin, outfn = saved
    d = h.shape[1]
    nkh = d // HEAD
    nvh = 2 * nkh
    dk_, dv_ = nkh * HEAD, nvh * HEAD
    dqkv = 2 * dk_ + dv_
    dz = dv_
    wp = proj.shape[1]
    dy = _matmul(dm, w["c_w_out"], "nt", F32, f"c_dy_{tag}")
    d_wout = _matmul(y, dm, "tn", BF16,f"c_dwout_{tag}")

    def out_bwd(o_, z_, dy_, g_):
        _, vjp = jax.vjp(outfn, o_, z_, g_)
        return vjp(dy_)
    do, dzz, d_gn = _rowmap(out_bwd, [o, zwin, dy], [gn], f"c_doutnorm_{tag}", n_acc=1)
    dq, dk, dv, db_rows, da_rows, d_alog, d_dtb = _gdn_bwd(conv, b_rows, a_rows, alog, dtb, states, do, bsz, seq, nvh, f"c_dcore_{tag}")
    cw = w["c_conv_w"]
    dq_pre, d_cwq = _conv_bwd(dq, proj, dk_, cw[:, :dk_], seq, False, f"c_dconv_q_{tag}", xcol=0)
    dk_pre, d_cwk = _conv_bwd(dk, proj, dk_, cw[:, dk_:2 * dk_], seq, False, f"c_dconv_k_{tag}", xcol=dk_)
    dv_pre, d_cwv = _conv_bwd(dv, proj, dv_, cw[:, 2 * dk_:], seq, False, f"c_dconv_v_{tag}", xcol=2 * dk_)
    d_cw = jnp.concatenate([d_cwq, d_cwk, d_cwv], axis=1)

    def cols_of(rows):
        return rows[:, :, :, 0, :].transpose(1, 2, 3, 0).reshape(bsz * seq, nvh)
    dba = jnp.concatenate([cols_of(db_rows), cols_of(da_rows)], axis=1)
    dba = jnp.pad(dba, ((0, 0), (0, wp - dqkv - dz - 2 * nvh)))
    dproj, = _rowmap(lambda *parts: jnp.concatenate(parts, axis=1), [dq_pre, dk_pre, dv_pre, dzz, dba], [], f"c_dproj_{tag}")
    dh = _matmul(dproj, w["c_w_in"], "nt", F32, f"c_dh_{tag}")
    d_win = _matmul(h, dproj, "tn", BF16,f"c_dwin_{tag}")
    return (dh,), dict(c_w_in=d_win, c_conv_w=d_cw, c_a_log=d_alog.reshape(-1), c_dt_bias=d_dtb.reshape(-1),
                       c_out_norm=d_gn.reshape(-1), c_w_out=d_wout)


_MIXER_FWD = (lambda h, w, bsz, seq, tag, jobs: _gmlp_fwd(h, w, tag, jobs), _fox_fwd_mixer, _gdn_fwd_mixer)
_MIXER_BWD = (lambda dm, h, s, w, bsz, seq, tag: _gmlp_bwd(dm, h, s, w, tag), _fox_bwd_mixer, _gdn_bwd_mixer)


class _NoPlan:
    def __init__(self, layers):
        self.layers = layers

    def weights(self, i):
        return self.layers[i]

    def fwd_jobs(self, i):
        return (None,) * 5

    def fwd_done(self, i, first, outs):
        pass

    def bwd_jobs(self, i):
        return (None,) * 5

    def bwd_done(self, i, outs):
        pass

    def grads_ready(self, i, grads):
        pass


def _local_step(x, target, plan, depth, bsz, seq):
    t, d = x.shape
    saved = []
    m_prev = None
    xin = x
    for i in range(depth):
        w = plan.weights(i)
        tag = f"l{i}"
        g_mix, g_ffn = w["norm_mix"].reshape(1, d), w["norm_ffn"].reshape(1, d)
        if i == 0:
            h, = _rowmap(lambda x_, g_: _rms(x_, g_).astype(BF16), [xin], [g_mix], f"norm_mix_{tag}")
            xl = xin
        else:
            xl, h = _rowmap(lambda x_, m_, g_: (x_ + m_, _rms(x_ + m_, g_).astype(BF16)), [xin, m_prev], [g_mix], f"norm_mix_{tag}")
        fjobs = plan.fwd_jobs(i)
        m, msaved, mouts = _MIXER_FWD[i % 3](h, w, bsz, seq, tag, fjobs[:2])
        plan.fwd_done(i, 0, mouts)
        x1, h2 = _rowmap(lambda x_, m_, g_: (x_ + m_, _rms(x_ + m_, g_).astype(BF16)), [xl, m], [g_ffn], f"norm_ffn_{tag}")
        f, fsaved, jouts = _ffn_fwd(x1, h2, w, seq, tag, fjobs[2:])
        plan.fwd_done(i, 2, jouts)
        saved.append((xl, h, msaved, x1, h2, fsaved))
        xin, m_prev = x1, f

    def loss_fn(x_, f_, tg_):
        e = x_ + f_ - tg_
        return e * (1.0 / d), jnp.full((1, LANES), (0.5 / d) * jnp.sum(e * e), F32)
    dx, loss_acc = _rowmap(loss_fn, [xin, m_prev, target], [], "loss", n_acc=1)
    loss = loss_acc[0, 0]

    grads = [None] * depth
    for i in reversed(range(depth)):
        w = plan.weights(i)
        tag = f"l{i}"
        xl, h, msaved, x1, h2, fsaved = saved[i]
        g_mix, g_ffn = w["norm_mix"].reshape(1, d), w["norm_ffn"].reshape(1, d)
        dhs, gw, jouts = _ffn_bwd(dx, h2, fsaved, w["ffn_w_gate"], w["ffn_w_up"], w["ffn_conv_w"], w["ffn_w_down"], seq, tag, plan.bwd_jobs(i))
        plan.bwd_done(i, jouts)
        dx1, d_gffn = _norm_bwd(x1, dx, dhs, g_ffn, f"dnorm_ffn_{tag}")
        dhs, gm = _MIXER_BWD[i % 3](dx1, h, msaved, w, bsz, seq, tag)
        dx, d_gmix = _norm_bwd(xl, dx1, dhs, g_mix, f"dnorm_mix_{tag}")
        gw.update(gm)
        gw["norm_mix"], gw["norm_ffn"] = d_gmix.reshape(-1), d_gffn.reshape(-1)
        grads[i] = gw
        plan.grads_ready(i, gw)
    return loss, dx, grads


def _adamw_math(w_, g_, m_, v_):
    m_new = ADAM_B1 * m_ + (1.0 - ADAM_B1) * g_
    v_new = ADAM_B2 * v_ + (1.0 - ADAM_B2) * (g_ * g_)
    m_hat = m_new / (1.0 - ADAM_B1 ** ADAM_STEP)
    v_hat = v_new / (1.0 - ADAM_B2 ** ADAM_STEP)
    delta = -ADAM_LR * (m_hat / (jnp.sqrt(v_hat) + ADAM_EPS) + ADAM_WD * w_)
    return delta, m_new, v_new


def _adamw(w, g, m, v, name):
    shape = w.shape
    if w.ndim == 1:
        w, g, m, v = (a.reshape(1, -1) for a in (w, g, m, v))
    return [o.reshape(shape) for o in _elementwise(_adamw_math, [w, g, m, v], 3, name)]


def _adamw_layers(w, gs, m, v, name):
    nl, r, c = w.shape
    tr = _pick(r, max(SUBLANES, (1 << 19) // c // SUBLANES * SUBLANES), SUBLANES)

    def body(*refs):
        w_ref, m_ref, v_ref = refs[:3]
        g_refs = refs[3:3 + nl]
        go_ref, d_ref, mo_ref, vo_ref = refs[3 + nl:]
        layer = pl.program_id(0)
        for k in range(nl):
            @pl.when(layer == k)
            def _(k=k):
                g = g_refs[k][...]
                delta, m_new, v_new = _adamw_math(w_ref[...], g, m_ref[...], v_ref[...])
                go_ref[...] = g
                d_ref[...] = delta
                mo_ref[...] = m_new
                vo_ref[...] = v_new

    st = pl.BlockSpec((None, tr, c), lambda l, i: (l, i, 0))
    g_specs = [pl.BlockSpec((tr, c), functools.partial(lambda l, i, k: (jnp.where(l == k, i, 0), 0), k=k)) for k in range(nl)]
    return pl.pallas_call(
        body, name=name, grid=(nl, r // tr), in_specs=[st, st, st] + g_specs, out_specs=[st] * 4,
        out_shape=[jax.ShapeDtypeStruct(w.shape, F32)] * 4, compiler_params=_cparams(("parallel", "parallel")),
    )(w, m, v, *gs)


WEIGHTS = ['norm_mix', 'norm_ffn', 'ffn_w_gate', 'ffn_w_up', 'ffn_conv_w', 'ffn_conv_b', 'ffn_w_down', 'a_w_in', 'a_b_in', 'a_v_norm',
           'a_w_s', 'a_b_s', 'a_w_out', 'b_w_in', 'b_b_f', 'b_q_norm', 'b_k_norm', 'b_w_out', 'c_w_in', 'c_conv_w', 'c_a_log',
           'c_dt_bias', 'c_out_norm', 'c_w_out']
BIG = {'ffn_w_gate': 1, 'ffn_w_up': 1, 'ffn_w_down': 0, 'a_w_in': 1, 'a_w_out': 0, 'b_w_in': 1, 'b_w_out': 0, 'c_w_in': 1, 'c_w_out': 0}
SMALL_SHARDED = {'ffn_conv_w': 1, 'a_b_in': 0, 'a_v_norm': 0, 'c_conv_w': 1}
MIXER_NAMES = (('a_w_in', 'a_b_in', 'a_v_norm', 'a_w_s', 'a_b_s', 'a_w_out'), ('b_w_in', 'b_b_f', 'b_q_norm', 'b_k_norm', 'b_w_out'),
               ('c_w_in', 'c_conv_w', 'c_a_log', 'c_dt_bias', 'c_out_norm', 'c_w_out'))
FFN_NAMES = ('norm_mix', 'norm_ffn', 'ffn_w_gate', 'ffn_w_up', 'ffn_conv_w', 'ffn_conv_b', 'ffn_w_down')


def _layer_entries(depth):
    out = []
    for i in range(depth):
        kind, j = i % 3, i // 3
        out.append([(n, i) for n in FFN_NAMES] + [(n, j) for n in MIXER_NAMES[kind]])
    return out


def _layout(name, shard_shape):
    if BIG[name] == 0:
        return "R"
    return "C" if shard_shape[-1] % LANES == 0 else "U"


JOB_GROUPS = ((3,), (4,), (0,), (1,), (2,))


class _Plan:
    def __init__(self, params, entries, small_full):
        self.params, self.entries, self.small_full = params, entries, small_full
        self.depth = len(entries)
        self.big = [[(n, j, _layout(n, params[n].shape[1:])) for n, j in ent if n in BIG] for ent in entries]
        self.layers = [None] * self.depth
        self.zs = [None] * self.depth
        self.total = {}
        for i, ent in enumerate(entries):
            self.layers[i] = {n: (small_full[(n, j)] if n in SMALL_SHARDED else params[n][j]) for n, j in ent if n not in BIG}
        mixer = [self._cast(0, t) for t in (3, 4)]
        mixer = _gather_layer(mixer, [self.big[0][t][2] for t in (3, 4)], "gather_mixer_l0")
        self._install(0, 3, mixer[0])
        self._install(0, 4, mixer[1])

    def _cast(self, i, t):
        n, j, cls = self.big[i][t]
        return _cast_window(self.params[n], j, cls, f"cast_{n}_l{i}")

    def _install(self, i, t, buf):
        n, j, cls = self.big[i][t]
        self.layers[i][n] = _assemble(buf, _pad_cols(N_CHIPS * buf.shape[2]), f"assemble_{n}_l{i}") if cls == "U" else buf

    def weights(self, i):
        return self.layers[i]

    def fwd_jobs(self, i):
        todo = [(i, 0), (i, 1), (i, 2)] + ([(i + 1, 3), (i + 1, 4)] if i + 1 < self.depth else [])
        self.fwd_todo = todo
        jobs = [_gather_job([self._cast(li, t)], [self.big[li][t][2]]) for li, t in todo]
        return jobs + [None] * (5 - len(jobs))

    def fwd_done(self, i, first, outs):
        for (li, t), got in zip(self.fwd_todo[first:], outs):
            if got is not None:
                self._install(li, t, got[0])

    def grads_ready(self, i, grads):
        dws, classes, widths = [], [], []
        for n, j, cls in self.big[i]:
            shard = self.params[n].shape[1:]
            dws.append(grads[n].reshape(N_CHIPS, shard[0], shard[1]) if cls == "R" else grads[n])
            classes.append(cls)
            widths.append(shard[1])
        self.zs[i] = _rs_chip_sums(dws, classes, widths, f"l{i}")
        if i == 0:
            self._finish(0, _rs_chip(self.zs[0], "rs_chip_l0"))

    def bwd_jobs(self, i):
        if i + 1 >= self.depth:
            return (None,) * len(JOB_GROUPS)
        return [_chip_job([self.zs[i + 1][t] for t in grp]) for grp in JOB_GROUPS]

    def bwd_done(self, i, outs):
        if i + 1 < self.depth:
            parts = [None] * len(self.big[i + 1])
            for grp, got in zip(JOB_GROUPS, outs):
                for t, p in zip(grp, got):
                    parts[t] = p
            self._finish(i + 1, parts)

    def _finish(self, i, parts):
        for (n, j, _), red in zip(self.big[i], _rs_finish(self.zs[i], parts, f"l{i}")):
            self.total[(n, j)] = red


def _train_step(x, target, params, moments_m, moments_v):
    bsz, seq, d = x.shape
    depth = params['norm_mix'].shape[0]
    entries = _layer_entries(depth)

    small_list = [(n, j) for n in SMALL_SHARDED for j in range(params[n].shape[0])]
    small_buf = _gather_shards(_pack([params[n][j] for n, j in small_list], ()), "gather_small")
    small_full = {}
    for (n, j), g in zip(small_list, _unpack(small_buf, [params[n][j].shape for n, j in small_list], (N_CHIPS,))):
        small_full[(n, j)] = _join(g, SMALL_SHARDED[n])

    plan = _Plan(params, entries, small_full)
    loss_local, dx, grads = _local_step(x.reshape(bsz * seq, d), target.reshape(bsz * seq, d), plan, depth, bsz, seq)
    loss = lax.psum(loss_local, ("x", "y", "c"))

    total = plan.total
    layer_of = {(n, j): i for i, ent in enumerate(entries) for n, j in ent}
    packed = _pack([_split(grads[layer_of[(n, j)]][n], SMALL_SHARDED[n]) for n, j in small_list], (N_CHIPS,))
    red = _reduce_scatter_layer([packed], ["R"], [PACK_COLS], "small")[0]
    for (n, j), g in zip(small_list, _unpack(red, [params[n][j].shape for n, j in small_list], ())):
        total[(n, j)] = g
    repl = [(n, j) for n in WEIGHTS if n not in BIG and n not in SMALL_SHARDED for j in range(params[n].shape[0])]
    flat = jnp.concatenate([grads[layer_of[k]][k[0]].reshape(-1) for k in repl])
    n_flat = flat.shape[0]
    flat = jnp.pad(flat, (0, (-n_flat) % (SUBLANES * LANES))).reshape(-1, LANES)
    flat = _all_to_all_sum(flat, "allreduce_small").reshape(-1)
    off = 0
    for k in repl:
        shp = params[k[0]][k[1]].shape
        n = 1
        for s in shp:
            n *= s
        total[k] = flat[off:off + n].reshape(shp)
        off += n

    grad_w, delta_w, new_m, new_v = {}, {}, {}, {}
    for n in WEIGHTS:
        nl = params[n].shape[0]
        if n in BIG:
            grad_w[n], delta_w[n], new_m[n], new_v[n] = _adamw_layers(params[n], [total[(n, j)] for j in range(nl)], moments_m[n], moments_v[n], f"adamw_{n}")
        else:
            g = jnp.stack([total[(n, j)] for j in range(nl)])
            grad_w[n] = g
            delta_w[n], new_m[n], new_v[n] = _adamw(params[n], g, moments_m[n], moments_v[n], f"adamw_{n}")
    return (loss, dx.reshape(bsz, seq, d), *[grad_w[n] for n in WEIGHTS], *[delta_w[n] for n in WEIGHTS],
            *[new_m[n] for n in WEIGHTS], *[new_v[n] for n in WEIGHTS])


def kernel(x, norm_mix, norm_ffn, ffn_w_gate, ffn_w_up, ffn_conv_w, ffn_conv_b, ffn_w_down, a_w_in, a_b_in, a_v_norm, a_w_s, a_b_s, a_w_out, b_w_in, b_b_f, b_q_norm, b_k_norm, b_w_out, c_w_in, c_conv_w, c_a_log, c_dt_bias, c_out_norm, c_w_out, loss_target, m_norm_mix, m_norm_ffn, m_ffn_w_gate, m_ffn_w_up, m_ffn_conv_w, m_ffn_conv_b, m_ffn_w_down, m_a_w_in, m_a_b_in, m_a_v_norm, m_a_w_s, m_a_b_s, m_a_w_out, m_b_w_in, m_b_b_f, m_b_q_norm, m_b_k_norm, m_b_w_out, m_c_w_in, m_c_conv_w, m_c_a_log, m_c_dt_bias, m_c_out_norm, m_c_w_out, v_norm_mix, v_norm_ffn, v_ffn_w_gate, v_ffn_w_up, v_ffn_conv_w, v_ffn_conv_b, v_ffn_w_down, v_a_w_in, v_a_b_in, v_a_v_norm, v_a_w_s, v_a_b_s, v_a_w_out, v_b_w_in, v_b_b_f, v_b_q_norm, v_b_k_norm, v_b_w_out, v_c_w_in, v_c_conv_w, v_c_a_log, v_c_dt_bias, v_c_out_norm, v_c_w_out):
    given = dict(locals())
    params = {n: given[n] for n in WEIGHTS}
    moments_m = {n: given["m_" + n] for n in WEIGHTS}
    moments_v = {n: given["v_" + n] for n in WEIGHTS}
    return _train_step(x, loss_target, params, moments_m, moments_v)
```
